```python
import math
import jax, jax.numpy as jnp
from jax import lax
import numpy as np


D_MODEL = 1024
BATCH = 8
SEQ = 2048
DEPTH = 2
DEC_BATCH = 128
DEC_SEQ = 8
PAST_LEN = 16384
PAGE_SIZE = 128

RET_HEADS = 8
RET_DK = 64
RET_DV = 64
RET_W = RET_HEADS * RET_DV
RET_CHUNK = 128
ROPE_BASE = 10000.0
SSM_W = 512
SSM_GC = 16
SSM_G = SSM_W // SSM_GC
SSM_P = 64
DT_MIN = 1e-3
DT_MAX = 1e-1
CONV_W = 512
CONV_K = 3
PROJ_SIZES = (RET_HEADS * RET_DK, RET_HEADS * RET_DK, RET_W, RET_W, SSM_W, CONV_W, CONV_W, CONV_W, D_MODEL, D_MODEL, D_MODEL)
PROJ_W = sum(PROJ_SIZES)
PEER_HEADS = 8
PEER_DQ = 256
PEER_NKEYS = 128
PEER_TOPK = 16
PEER_NEXP = PEER_NKEYS ** 2
PEER_BLOCK = 256
EPS = 1e-6

kernel_name = "hybrid_retention_s5_shortconv_peer_step"


def rmsnorm(x, g):
    xf = x.astype(jnp.float32)
    y = xf * lax.rsqrt(jnp.mean(xf * xf, axis=-1, keepdims=True) + EPS)
    return (y * g.astype(jnp.float32)).astype(x.dtype)


def rotary(x, pos):
    half = x.shape[-1] // 2
    freqs = ROPE_BASE ** (-jnp.arange(half, dtype=jnp.float32) / half)
    ang = pos[:, None] * freqs[None, :]
    cos = jnp.cos(ang)[None, :, None, :]
    sin = jnp.sin(ang)[None, :, None, :]
    x1, x2 = x[..., :half], x[..., half:]
    return jnp.concatenate([x1 * cos - x2 * sin, x1 * sin + x2 * cos], axis=-1)


def retention_branch(q, k, v, g, s0, pos, gn, w_out):
    B, S, _ = q.shape
    f32 = jnp.float32
    qf = rotary(q.astype(f32).reshape(B, S, RET_HEADS, RET_DK), pos)
    kf = rotary(k.astype(f32).reshape(B, S, RET_HEADS, RET_DK), pos) * (RET_DK ** -0.5)
    vf = v.astype(f32).reshape(B, S, RET_HEADS, RET_DV)
    lg = jnp.log1p(-jnp.exp2(-5.0 - jnp.arange(RET_HEADS, dtype=f32)))
    C = math.gcd(S, RET_CHUNK)
    n = S // C
    idx = jnp.arange(C, dtype=f32)
    diff = idx[:, None] - idx[None, :]
    dmask = jnp.where(diff[None] >= 0, jnp.exp(jnp.maximum(diff, 0.0)[None] * lg[:, None, None]), 0.0)
    q_dec = jnp.exp((idx[:, None] + 1.0) * lg[None, :])
    k_dec = jnp.exp((C - 1.0 - idx)[:, None] * lg[None, :])
    c_dec = jnp.exp(C * lg)

    def to_chunks(t):
        return t.reshape(B, n, C, RET_HEADS, t.shape[-1]).transpose(1, 0, 2, 3, 4)

    def body(st, xs):
        qc, kc, vc = xs
        scores = jnp.einsum('bihk,bjhk->bhij', qc, kc) * dmask[None]
        inner = jnp.einsum('bhij,bjhv->bihv', scores, vc)
        cross = jnp.einsum('bihk,bhkv->bihv', qc * q_dec[None, :, :, None], st)
        st = st * c_dec[None, :, None, None] + jnp.einsum('bjhk,bjhv->bhkv', kc * k_dec[None, :, :, None], vc)
        return st, inner + cross

    s_fin, outs = lax.scan(body, s0.astype(f32), (to_chunks(qf), to_chunks(kf), to_chunks(vf)))
    o = outs.transpose(1, 0, 2, 3, 4).reshape(B, S, RET_HEADS, RET_DV)
    mu = jnp.mean(o, axis=-1, keepdims=True)
    var = jnp.mean(jnp.square(o - mu), axis=-1, keepdims=True)
    o = ((o - mu) * lax.rsqrt(var + EPS)).reshape(B, S, RET_W) * gn.astype(f32)
    out = (jax.nn.silu(g.astype(f32)) * o).astype(q.dtype) @ w_out
    return out, s_fin


def _cplx_combine(e1, e2):
    ar1, ai1, br1, bi1 = e1
    ar2, ai2, br2, bi2 = e2
    return (ar2 * ar1 - ai2 * ai1,
            ar2 * ai1 + ai2 * ar1,
            ar2 * br1 - ai2 * bi1 + br2,
            ar2 * bi1 + ai2 * br1 + bi2)


def s5_branch(u, h_re, h_im, a_re, a_im, b_re, b_im, c_re, c_im, d, log_dt, w_ga, w_gb):
    B, S, _ = u.shape
    f32 = jnp.float32
    uf = u.astype(f32).reshape(B, S, SSM_G, SSM_GC)
    ar, ai = a_re.astype(f32), a_im.astype(f32)
    dt = jnp.exp(log_dt.astype(f32))[:, None]
    dar, dai = dt * ar, dt * ai
    mag = jnp.exp(dar)
    abar_re, abar_im = mag * jnp.cos(dai), mag * jnp.sin(dai)
    den = ar * ar + ai * ai
    nr, ni = abar_re - 1.0, abar_im
    f_re = (nr * ar + ni * ai) / den
    f_im = (ni * ar - nr * ai) / den
    br, bi = b_re.astype(f32), b_im.astype(f32)
    bbar_re = f_re[..., None] * br - f_im[..., None] * bi
    bbar_im = f_re[..., None] * bi + f_im[..., None] * br
    bu_re = jnp.einsum('bsgc,gpc->bsgp', uf, bbar_re)
    bu_im = jnp.einsum('bsgc,gpc->bsgp', uf, bbar_im)
    a_seq_re = jnp.broadcast_to(abar_re, bu_re.shape)
    a_seq_im = jnp.broadcast_to(abar_im, bu_im.shape)
    _, _, x_re, x_im = lax.associative_scan(_cplx_combine, (a_seq_re, a_seq_im, bu_re, bu_im), axis=1)
    t1 = jnp.arange(1, S + 1, dtype=f32)[:, None, None]
    pmag = jnp.exp(t1 * dar[None])
    pang = t1 * dai[None]
    ap_re, ap_im = pmag * jnp.cos(pang), pmag * jnp.sin(pang)
    h_re = h_re.astype(f32)[:, None]
    h_im = h_im.astype(f32)[:, None]
    x_re = x_re + ap_re[None] * h_re - ap_im[None] * h_im
    x_im = x_im + ap_re[None] * h_im + ap_im[None] * h_re
    y = (jnp.einsum('bsgp,gcp->bsgc', x_re, c_re.astype(f32))
         - jnp.einsum('bsgp,gcp->bsgc', x_im, c_im.astype(f32))).reshape(B, S, SSM_W)
    y = y + d.astype(f32) * u.astype(f32)
    y = jax.nn.gelu(y).astype(u.dtype)
    out = (y @ w_ga) * jax.nn.sigmoid(y @ w_gb)
    return out.astype(u.dtype), x_re[:, -1], x_im[:, -1]


def conv_branch(bg, cg, hc, buf, conv_w, conv_b, w_out):
    S = hc.shape[1]
    z = cg * hc
    zp = jnp.concatenate([buf.astype(z.dtype), z], axis=1)
    y = conv_b
    for j in range(CONV_K):
        y = y + conv_w[j] * zp[:, j:j + S]
    out = (bg * y) @ w_out
    return out, zp[:, S:]


def peer_ffn(x, wq, k1, k2, u_tab, v_tab):
    B, S, D = x.shape
    T = B * S
    nb = -(-T // PEER_BLOCK)
    xt = jnp.pad(x.reshape(T, D), ((0, nb * PEER_BLOCK - T), (0, 0))).reshape(nb, PEER_BLOCK, D)
    half = PEER_DQ // 2
    f32 = jnp.float32

    def block(xb):
        q = (xb @ wq).reshape(PEER_BLOCK, PEER_HEADS, PEER_DQ).astype(f32)
        s1 = jnp.einsum('thd,hnd->thn', q[..., :half], k1.astype(f32))
        s2 = jnp.einsum('thd,hnd->thn', q[..., half:], k2.astype(f32))
        v1, i1 = lax.top_k(s1, PEER_TOPK)
        v2, i2 = lax.top_k(s2, PEER_TOPK)
        cand = (v1[..., :, None] + v2[..., None, :]).reshape(PEER_BLOCK, PEER_HEADS, PEER_TOPK * PEER_TOPK)
        sc, ci = lax.top_k(cand, PEER_TOPK)
        e1 = jnp.take_along_axis(i1, ci // PEER_TOPK, axis=-1)
        e2 = jnp.take_along_axis(i2, ci % PEER_TOPK, axis=-1)
        idx = e1 * PEER_NKEYS + e2
        gate = jax.nn.softmax(sc, axis=-1)
        ug = jnp.take(u_tab, idx, axis=0)
        act = jax.nn.gelu(jnp.einsum('thkd,td->thk', ug, xb).astype(f32))
        vg = jnp.take(v_tab, idx, axis=0)
        return jnp.einsum('thk,thkd->td', (gate * act).astype(xb.dtype), vg)

    y = lax.map(block, xt).reshape(nb * PEER_BLOCK, D)[:T]
    return y.reshape(B, S, D)


def layer(x, s_ret, s_re, s_im, s_conv, pos, p):
    h = rmsnorm(x, p['norm_mix'])
    z = h @ p['w_in']
    splits = [int(c) for c in np.cumsum(PROJ_SIZES)[:-1]]
    q, k, v, g, u, bg, cg, hc, ga, gb, gc = jnp.split(z, splits, axis=-1)
    oa, s_ret_new = retention_branch(q, k, v, g, s_ret, pos, p['ret_norm'], p['w_ret_out'])
    ob, re_new, im_new = s5_branch(u, s_re, s_im, p['ssm_a_re'], p['ssm_a_im'], p['ssm_b_re'], p['ssm_b_im'],
                                   p['ssm_c_re'], p['ssm_c_im'], p['ssm_d'], p['ssm_log_dt'],
                                   p['w_glu_a'], p['w_glu_b'])
    oc, conv_new = conv_branch(bg, cg, hc, s_conv, p['conv_w'], p['conv_b'], p['w_conv_out'])
    merged = jax.nn.sigmoid(ga) * oa.astype(x.dtype) + jax.nn.sigmoid(gb) * ob + jax.nn.sigmoid(gc) * oc
    x = x + merged @ p['w_mix_out']
    x = x + peer_ffn(rmsnorm(x, p['norm_ffn']), p['peer_wq'], p['peer_k1'], p['peer_k2'], p['peer_u'], p['peer_v'])
    return x, s_ret_new, re_new, im_new, conv_new


def setup_inputs(seed: int = 0) -> dict:
    key = jax.random.key(seed)
    ks = jax.random.split(key, 32)
    f32 = jnp.float32

    def nrm(k, shape, scale):
        return jax.random.normal(k, shape, f32) * scale

    L = DEPTH
    log_dt = jax.random.uniform(ks[13], (L, SSM_G), f32, math.log(DT_MIN), math.log(DT_MAX))
    a_im = jnp.pi * jnp.arange(SSM_P, dtype=f32)[None, None, :] + nrm(ks[9], (L, SSM_G, SSM_P), 0.01)
    return {
        'x_prompt': nrm(ks[0], (BATCH, SEQ, D_MODEL), 1.0),
        'x_sample': nrm(ks[1], (DEC_BATCH, DEC_SEQ, D_MODEL), 1.0),
        'state_ret': nrm(ks[2], (L, DEC_BATCH, RET_HEADS, RET_DK, RET_DV), 0.5),
        'state_ssm_re': nrm(ks[3], (L, DEC_BATCH, SSM_G, SSM_P), 0.5),
        'state_ssm_im': nrm(ks[4], (L, DEC_BATCH, SSM_G, SSM_P), 0.5),
        'state_conv': nrm(ks[5], (L, DEC_BATCH, CONV_K - 1, CONV_W), 1.0),
        'norm_mix': 1.0 + nrm(ks[6], (L, D_MODEL), 0.02),
        'w_in': nrm(ks[7], (L, D_MODEL, PROJ_W), D_MODEL ** -0.5),
        'ret_norm': 1.0 + nrm(ks[8], (L, RET_W), 0.02),
        'w_ret_out': nrm(ks[10], (L, RET_W, D_MODEL), RET_W ** -0.5),
        'ssm_a_re': -0.5 + nrm(ks[11], (L, SSM_G, SSM_P), 0.01),
        'ssm_a_im': a_im,
        'ssm_b_re': nrm(ks[12], (L, SSM_G, SSM_P, SSM_GC), (2 * SSM_GC) ** -0.5),
        'ssm_b_im': nrm(ks[14], (L, SSM_G, SSM_P, SSM_GC), (2 * SSM_GC) ** -0.5),
        'ssm_c_re': nrm(ks[15], (L, SSM_G, SSM_GC, SSM_P), SSM_P ** -0.5),
        'ssm_c_im': nrm(ks[16], (L, SSM_G, SSM_GC, SSM_P), SSM_P ** -0.5),
        'ssm_d': nrm(ks[17], (L, SSM_W), 0.5),
        'ssm_log_dt': log_dt,
        'w_glu_a': nrm(ks[18], (L, SSM_W, D_MODEL), SSM_W ** -0.5),
        'w_glu_b': nrm(ks[19], (L, SSM_W, D_MODEL), SSM_W ** -0.5),
        'conv_w': nrm(ks[20], (L, CONV_K, CONV_W), 0.5),
        'conv_b': nrm(ks[21], (L, CONV_W), 0.02),
        'w_conv_out': nrm(ks[22], (L, CONV_W, D_MODEL), CONV_W ** -0.5),
        'w_mix_out': nrm(ks[23], (L, D_MODEL, D_MODEL), D_MODEL ** -0.5),
        'norm_ffn': 1.0 + nrm(ks[24], (L, D_MODEL), 0.02),
        'peer_wq': nrm(ks[25], (L, D_MODEL, PEER_HEADS * PEER_DQ), D_MODEL ** -0.5),
        'peer_k1': nrm(ks[26], (L, PEER_HEADS, PEER_NKEYS, PEER_DQ // 2), (PEER_DQ // 2) ** -0.5),
        'peer_k2': nrm(ks[27], (L, PEER_HEADS, PEER_NKEYS, PEER_DQ // 2), (PEER_DQ // 2) ** -0.5),
        'peer_u': nrm(ks[28], (L, PEER_NEXP, D_MODEL), D_MODEL ** -0.5),
        'peer_v': nrm(ks[29], (L, PEER_NEXP, D_MODEL), 0.3),
        'norm_final': 1.0 + nrm(ks[30], (D_MODEL,), 0.02),
    }


def reference(x_prompt, x_sample, state_ret, state_ssm_re, state_ssm_im, state_conv,
              norm_mix, w_in, ret_norm, w_ret_out, ssm_a_re, ssm_a_im, ssm_b_re, ssm_b_im,
              ssm_c_re, ssm_c_im, ssm_d, ssm_log_dt, w_glu_a, w_glu_b, conv_w, conv_b,
              w_conv_out, w_mix_out, norm_ffn, peer_wq, peer_k1, peer_k2, peer_u, peer_v, norm_final):
    f32 = jnp.float32
    Bp, Sp, _ = x_prompt.shape
    Bs, Ss, _ = x_sample.shape
    pos_p = jnp.arange(Sp, dtype=f32)
    pos_s = PAST_LEN + jnp.arange(Ss, dtype=f32)
    xp, xs = x_prompt, x_sample
    ret_p, ret_s, re_p, re_s, im_p, im_s, cv_p, cv_s = [], [], [], [], [], [], [], []
    for l in range(DEPTH):
        p = dict(norm_mix=norm_mix[l], w_in=w_in[l], ret_norm=ret_norm[l], w_ret_out=w_ret_out[l],
                 ssm_a_re=ssm_a_re[l], ssm_a_im=ssm_a_im[l], ssm_b_re=ssm_b_re[l], ssm_b_im=ssm_b_im[l],
                 ssm_c_re=ssm_c_re[l], ssm_c_im=ssm_c_im[l], ssm_d=ssm_d[l], ssm_log_dt=ssm_log_dt[l],
                 w_glu_a=w_glu_a[l], w_glu_b=w_glu_b[l], conv_w=conv_w[l], conv_b=conv_b[l],
                 w_conv_out=w_conv_out[l], w_mix_out=w_mix_out[l], norm_ffn=norm_ffn[l],
                 peer_wq=peer_wq[l], peer_k1=peer_k1[l], peer_k2=peer_k2[l], peer_u=peer_u[l], peer_v=peer_v[l])
        xp, sr, sre, sim, scv = layer(xp,
                                      jnp.zeros((Bp, RET_HEADS, RET_DK, RET_DV), f32),
                                      jnp.zeros((Bp, SSM_G, SSM_P), f32),
                                      jnp.zeros((Bp, SSM_G, SSM_P), f32),
                                      jnp.zeros((Bp, CONV_K - 1, CONV_W), x_prompt.dtype),
                                      pos_p, p)
        ret_p.append(sr); re_p.append(sre); im_p.append(sim); cv_p.append(scv)
        xs, sr, sre, sim, scv = layer(xs, state_ret[l], state_ssm_re[l], state_ssm_im[l], state_conv[l], pos_s, p)
        ret_s.append(sr); re_s.append(sre); im_s.append(sim); cv_s.append(scv)
    y_prompt = rmsnorm(xp, norm_final)
    y_sample = rmsnorm(xs, norm_final)
    return (y_prompt, y_sample,
            jnp.stack(ret_p), jnp.stack(ret_s),
            jnp.stack(re_p), jnp.stack(re_s),
            jnp.stack(im_p), jnp.stack(im_s),
            jnp.stack(cv_p), jnp.stack(cv_s))
```

```python
import functools
import math

import jax
import jax.numpy as jnp
from jax import lax
from jax.experimental import pallas as pl
from jax.experimental.pallas import tpu as pltpu

F32 = jnp.float32
BF16 = jnp.bfloat16

D_MODEL = 1024
DEPTH = 2
PAST_LEN = 16384
RET_HEADS = 8
RET_DK = 64
RET_W = 512
RET_CHUNK = 128
ROPE_BASE = 10000.0
SSM_W = 512
SSM_GC = 16
SSM_G = 32
SSM_P = 64
SSM_N = SSM_G * SSM_P
SSM_SLABS = 4
SSM_SLAB_N = SSM_N // SSM_SLABS
CONV_W = 512
CONV_K = 3
PROJ_W = 7168
PEER_HEADS = 8
PEER_DQ = 256
PEER_NKEYS = 128
PEER_TOPK = 16
PEER_NEXP = PEER_NKEYS ** 2
PEER_SLOTS = PEER_HEADS * PEER_TOPK
EPS = 1e-6

ROWS = 1024
LANES = 128
SUBLANES = 8
MIB = 1024 * 1024

_CANDS = [(a, b) for a in range(PEER_TOPK) for b in range(PEER_TOPK) if (a + 1) * (b + 1) <= PEER_TOPK]


def _params(sem, vmem_mib):
    return pltpu.CompilerParams(dimension_semantics=sem, vmem_limit_bytes=vmem_mib * MIB)


def _rms(x, g):
    return x * lax.rsqrt(jnp.mean(x * x, axis=-1, keepdims=True) + EPS) * g


def _dot(a, b):
    return jnp.dot(a, b, preferred_element_type=F32)


def _dot_nt(a, b):
    return lax.dot_general(a, b, (((1,), (1,)), ((), ())), preferred_element_type=F32)


def _inproj_kernel(x_ref, g_ref, w_ref, z_ref, h_scr):
    @pl.when(pl.program_id(1) == 0)
    def _():
        h_scr[...] = _rms(x_ref[...], g_ref[...]).astype(BF16)

    z_ref[...] = _dot(h_scr[...], w_ref[...])


def _inproj(x, g, w):
    t = x.shape[0]
    nb = 1024
    return pl.pallas_call(
        _inproj_kernel,
        grid=(t // ROWS, PROJ_W // nb),
        in_specs=[pl.BlockSpec((ROWS, D_MODEL), lambda i, j: (i, 0)),
                  pl.BlockSpec((1, D_MODEL), lambda i, j: (0, 0)),
                  pl.BlockSpec((D_MODEL, nb), lambda i, j: (0, j))],
        out_specs=pl.BlockSpec((ROWS, nb), lambda i, j: (i, j)),
        out_shape=jax.ShapeDtypeStruct((t, PROJ_W), F32),
        scratch_shapes=[pltpu.VMEM((ROWS, D_MODEL), BF16)],
        compiler_params=_params(("parallel", "arbitrary"), 40),
        name="inproj",
    )(x, g, w)


def _ret_kernel(batch, bblk, z_ref, cos_ref, sa_ref, sb_ref, qdec_ref, kdec_ref, cdec_ref, gn_ref, s0_ref,
                o_ref, s_ref, qd_scr, kd_scr, v_scr, mask_scr, oacc_scr):
    bb = pl.program_id(0)
    c = pl.program_id(1)
    steps = ROWS // batch
    nslab = RET_W // LANES

    def head_view(ref, h):
        return ref[h // 2, :, (h % 2) * RET_DK:(h % 2 + 1) * RET_DK]

    @pl.when((bb == 0) & (c == 0))
    def _():
        r = lax.broadcasted_iota(jnp.int32, (ROWS, ROWS), 0)
        cc = lax.broadcasted_iota(jnp.int32, (ROWS, ROWS), 1)
        same = (r & (batch - 1)) == (cc & (batch - 1))
        mask_scr[...] = (same & (r >= cc)).astype(F32)

    @pl.when(c == 0)
    def _():
        s_ref[...] = s0_ref[...]

    @pl.when(bb == 0)
    def _():
        cos, sa, sb = cos_ref[...], sa_ref[...], sb_ref[...]

        def rot(x):
            return x * cos + pltpu.roll(x, 32, 1) * sa + pltpu.roll(x, 96, 1) * sb

        for s in range(nslab):
            cols = slice(s * LANES, (s + 1) * LANES)
            qd_scr[s] = rot(z_ref[:, cols]) * qdec_ref[:, cols]
            kd_scr[s] = rot(z_ref[:, RET_W + s * LANES:RET_W + (s + 1) * LANES]) * kdec_ref[:, cols]
            v_scr[s] = z_ref[:, 2 * RET_W + s * LANES:2 * RET_W + (s + 1) * LANES]
        for s in range(nslab):
            outs = []
            for h in (2 * s, 2 * s + 1):
                qh = head_view(qd_scr, h).astype(BF16)
                kh = head_view(kd_scr, h).astype(BF16)
                vh = head_view(v_scr, h).astype(BF16)
                p = (_dot_nt(qh, kh) * mask_scr[...]).astype(BF16)
                outs.append(_dot(p, vh))
            oacc_scr[s] = jnp.concatenate(outs, axis=1)

    def per_seq(bl, carry):
        b = bb * bblk + bl
        rows = pl.ds(b, steps, stride=batch)
        for s in range(nslab):
            qb = qd_scr[s, rows, :]
            kb = kd_scr[s, rows, :]
            vb = v_scr[s, rows, :]
            cross = []
            for hh in range(2):
                h = 2 * s + hh
                hc = slice(hh * RET_DK, (hh + 1) * RET_DK)
                st = s_ref[bl, h]
                cross.append(_dot(qb[:, hc].astype(BF16), st.astype(BF16)))
                upd = lax.dot_general(kb[:, hc].astype(BF16), vb[:, hc].astype(BF16),
                                      (((0,), (0,)), ((), ())), preferred_element_type=F32)
                s_ref[bl, h] = (st + upd) * cdec_ref[:, h * RET_DK:(h + 1) * RET_DK]
            oacc_scr[s, rows, :] = oacc_scr[s, rows, :] + jnp.concatenate(cross, axis=1)
        return carry

    lax.fori_loop(0, bblk, per_seq, 0)

    @pl.when(bb == pl.num_programs(0) - 1)
    def _():
        normed = []
        for h in range(RET_HEADS):
            oh = head_view(oacc_scr, h)
            mu = jnp.mean(oh, axis=-1, keepdims=True)
            dlt = oh - mu
            var = jnp.mean(dlt * dlt, axis=-1, keepdims=True)
            normed.append(dlt * lax.rsqrt(var + EPS))
        o = jnp.concatenate(normed, axis=1) * gn_ref[...]
        o_ref[...] = jax.nn.silu(z_ref[:, 3 * RET_W:4 * RET_W]) * o


def _retention(z, row_blk0, nblk, batch, bblk, tabs, gn, s0):
    cos, sa, sb, qdec, kdec, cdec = tabs
    nbb = batch // bblk
    st_spec = pl.BlockSpec((bblk, RET_HEADS, RET_DK, RET_DK), lambda bb, c: (bb, 0, 0, 0))
    const = lambda bb, c: (0, 0)
    return pl.pallas_call(
        functools.partial(_ret_kernel, batch, bblk),
        grid=(nbb, nblk),
        in_specs=[pl.BlockSpec((ROWS, 4 * RET_W), lambda bb, c: (row_blk0 + c, 0)),
                  pl.BlockSpec((ROWS, LANES), lambda bb, c: (c, 0)),
                  pl.BlockSpec((ROWS, LANES), lambda bb, c: (c, 0)),
                  pl.BlockSpec((ROWS, LANES), lambda bb, c: (c, 0)),
                  pl.BlockSpec((ROWS, RET_W), const),
                  pl.BlockSpec((ROWS, RET_W), const),
                  pl.BlockSpec((1, RET_W), const),
                  pl.BlockSpec((1, RET_W), const),
                  st_spec],
        out_specs=[pl.BlockSpec((ROWS, RET_W), lambda bb, c: (c, 0)), st_spec],
        out_shape=[jax.ShapeDtypeStruct((nblk * ROWS, RET_W), F32),
                   jax.ShapeDtypeStruct((batch, RET_HEADS, RET_DK, RET_DK), F32)],
        scratch_shapes=[pltpu.VMEM((RET_W // LANES, ROWS, LANES), F32)] * 3
        + [pltpu.VMEM((ROWS, ROWS), F32), pltpu.VMEM((RET_W // LANES, ROWS, LANES), F32)],
        compiler_params=_params(("arbitrary", "arbitrary"), 56),
        name="retention",
    )(z, cos, sa, sb, qdec, kdec, cdec, gn, s0)


def _retention_tables(pos, batch):
    half = RET_DK // 2
    freqs = ROPE_BASE ** (-jnp.arange(half, dtype=F32) / half)
    ang = pos[:, None] * freqs[None, :]
    cos, sin = jnp.cos(ang), jnp.sin(ang)
    zero = jnp.zeros_like(sin)
    reps = LANES // RET_DK
    cos_t = jnp.tile(jnp.concatenate([cos, cos], axis=1), (1, reps))
    sa_t = jnp.tile(jnp.concatenate([zero, sin], axis=1), (1, reps))
    sb_t = jnp.tile(jnp.concatenate([-sin, zero], axis=1), (1, reps))
    lg = jnp.log1p(-jnp.exp2(-5.0 - jnp.arange(RET_HEADS, dtype=F32)))
    steps = ROWS // batch
    i1 = (jnp.arange(ROWS) // batch).astype(F32) + 1.0
    qdec = jnp.repeat(jnp.exp(i1[:, None] * lg[None, :]), RET_DK, axis=1)
    kdec = jnp.repeat(jnp.exp(-i1[:, None] * lg[None, :]), RET_DK, axis=1) * (RET_DK ** -0.5)
    cdec = jnp.repeat(jnp.exp(steps * lg), RET_DK)[None, :]
    return cos_t, sa_t, sb_t, qdec, kdec, cdec


def _s5_disc_kernel(are_ref, aim_ref, ldt_ref, bre_ref, bim_ref, abre_ref, abim_ref, bbre_ref, bbim_ref):
    ar, ai = are_ref[...], aim_ref[...]
    dt = jnp.exp(ldt_ref[...])
    dar, dai = dt * ar, dt * ai
    mag = jnp.exp(dar)
    abar_re, abar_im = mag * jnp.cos(dai), mag * jnp.sin(dai)
    den = ar * ar + ai * ai
    nr, ni = abar_re - 1.0, abar_im
    f_re = (nr * ar + ni * ai) / den
    f_im = (ni * ar - nr * ai) / den
    abre_ref[...] = abar_re
    abim_ref[...] = abar_im
    br, bi = bre_ref[...], bim_ref[...]
    bbre_ref[...] = f_re[:, None, :] * br - f_im[:, None, :] * bi
    bbim_ref[...] = f_re[:, None, :] * bi + f_im[:, None, :] * br


def _s5_discretise(a_re, a_im, log_dt, b_re_t, b_im_t):
    lg = a_re.shape[0]
    small = jax.ShapeDtypeStruct((lg, SSM_P), F32)
    big = jax.ShapeDtypeStruct((lg, SSM_GC, SSM_P), F32)
    return pl.pallas_call(_s5_disc_kernel, out_shape=[small, small, big, big], name="s5_disc")(
        a_re, a_im, log_dt, b_re_t, b_im_t)


def _s5_kernel(batch, u_ref, bmat_ref, cmat_ref, are_ref, aim_ref, d_ref, h0re_ref, h0im_ref,
               y_ref, xre_ref, xim_ref, x_scr):
    c = pl.program_id(0)
    steps = ROWS // batch
    half = SSM_SLAB_N

    @pl.when(c == 0)
    def _():
        xre_ref[...] = h0re_ref[...]
        xim_ref[...] = h0im_ref[...]

    u = u_ref[...]
    ub = u.astype(BF16)
    for s in range(SSM_SLABS):
        x_scr[:, 2 * half * s:2 * half * (s + 1)] = _dot(ub[:, s * LANES:(s + 1) * LANES], bmat_ref[s])

    for s in range(SSM_SLABS):
        re0 = 2 * half * s
        im0 = re0 + half
        sc = slice(half * s, half * (s + 1))
        ar = jnp.broadcast_to(are_ref[:, sc], (SUBLANES, half))
        ai = jnp.broadcast_to(aim_ref[:, sc], (SUBLANES, half))

        def row_tile(rt, carry, re0=re0, im0=im0, sc=sc, ar=ar, ai=ai):
            r0 = pl.multiple_of(rt * SUBLANES, SUBLANES)

            def step(t, x):
                xr, xi = x
                row = pl.multiple_of(t * batch + r0, SUBLANES)
                nr = ar * xr - ai * xi + x_scr[pl.ds(row, SUBLANES), re0:re0 + half]
                ni = ar * xi + ai * xr + x_scr[pl.ds(row, SUBLANES), im0:im0 + half]
                x_scr[pl.ds(row, SUBLANES), re0:re0 + half] = nr
                x_scr[pl.ds(row, SUBLANES), im0:im0 + half] = ni
                return nr, ni

            init = (xre_ref[pl.ds(r0, SUBLANES), sc], xim_ref[pl.ds(r0, SUBLANES), sc])
            xr, xi = lax.fori_loop(0, steps, step, init, unroll=8)
            xre_ref[pl.ds(r0, SUBLANES), sc] = xr
            xim_ref[pl.ds(r0, SUBLANES), sc] = xi
            return carry

        lax.fori_loop(0, batch // SUBLANES, row_tile, 0)

    ys = [_dot(x_scr[:, 2 * half * s:2 * half * (s + 1)].astype(BF16), cmat_ref[s]) for s in range(SSM_SLABS)]
    y = jnp.concatenate(ys, axis=1) + d_ref[...] * u
    y_ref[...] = jax.nn.gelu(y)


def _s5(z, row_blk0, nblk, batch, bmat, cmat, abre, abim, d, h0re, h0im):
    const2 = lambda c: (0, 0)
    const3 = lambda c: (0, 0, 0)
    st = pl.BlockSpec((batch, SSM_N), const2)
    return pl.pallas_call(
        functools.partial(_s5_kernel, batch),
        grid=(nblk,),
        in_specs=[pl.BlockSpec((ROWS, SSM_W), lambda c: (row_blk0 + c, 4)),
                  pl.BlockSpec((SSM_SLABS, LANES, 2 * SSM_SLAB_N), const3),
                  pl.BlockSpec((SSM_SLABS, 2 * SSM_SLAB_N, LANES), const3),
                  pl.BlockSpec((1, SSM_N), const2),
                  pl.BlockSpec((1, SSM_N), const2),
                  pl.BlockSpec((1, SSM_W), const2),
                  st, st],
        out_specs=[pl.BlockSpec((ROWS, SSM_W), lambda c: (c, 0)), st, st],
        out_shape=[jax.ShapeDtypeStruct((nblk * ROWS, SSM_W), F32),
                   jax.ShapeDtypeStruct((batch, SSM_N), F32),
                   jax.ShapeDtypeStruct((batch, SSM_N), F32)],
        scratch_shapes=[pltpu.VMEM((ROWS, 2 * SSM_N), F32)],
        compiler_params=_params(("arbitrary",), 48),
        name="s5",
    )(z, bmat, cmat, abre, abim, d, h0re, h0im)


def _block_diag_slabs(w, rows_inner):
    gps = SSM_G // SSM_SLABS
    eye = jnp.eye(gps, dtype=w.dtype)
    w4 = w.reshape(SSM_SLABS, gps, w.shape[1], w.shape[2])
    out = w4[:, :, :, None, :] * eye[None, :, None, :, None]
    return out.reshape(SSM_SLABS, gps * w.shape[1], gps * w.shape[2])


def _conv_kernel(batch, bg_ref, cg_ref, hc_ref, buf0_ref, w_ref, b_ref, o_ref, buf_ref, zp_scr):
    c = pl.program_id(0)
    pad = (CONV_K - 1) * batch

    @pl.when(c == 0)
    def _():
        zp_scr[0:pad, :] = buf0_ref[...]

    zc = cg_ref[...] * hc_ref[...]
    zp_scr[pad:pad + ROWS, :] = zc
    y = b_ref[...]
    for j in range(CONV_K):
        y = y + w_ref[j:j + 1, :] * zp_scr[j * batch:j * batch + ROWS, :]
    o_ref[...] = bg_ref[...] * y
    tail = zp_scr[ROWS:ROWS + pad, :]
    buf_ref[...] = tail
    zp_scr[0:pad, :] = tail


def _conv(z, row_blk0, nblk, batch, buf0, w, b):
    pad = (CONV_K - 1) * batch
    const = lambda c: (0, 0)
    return pl.pallas_call(
        functools.partial(_conv_kernel, batch),
        grid=(nblk,),
        in_specs=[pl.BlockSpec((ROWS, CONV_W), lambda c: (row_blk0 + c, 5)),
                  pl.BlockSpec((ROWS, CONV_W), lambda c: (row_blk0 + c, 6)),
                  pl.BlockSpec((ROWS, CONV_W), lambda c: (row_blk0 + c, 7)),
                  pl.BlockSpec((pad, CONV_W), const),
                  pl.BlockSpec((CONV_K, CONV_W), const),
                  pl.BlockSpec((1, CONV_W), const)],
        out_specs=[pl.BlockSpec((ROWS, CONV_W), lambda c: (c, 0)), pl.BlockSpec((pad, CONV_W), const)],
        out_shape=[jax.ShapeDtypeStruct((nblk * ROWS, CONV_W), F32),
                   jax.ShapeDtypeStruct((pad, CONV_W), F32)],
        scratch_shapes=[pltpu.VMEM((ROWS + pad, CONV_W), F32)],
        compiler_params=_params(("arbitrary",), 32),
        name="conv",
    )(z, z, z, buf0, w, b)


def _merge_kernel(x_ref, oa_ref, ys_ref, oc_ref, ga_ref, gb_ref, gc_ref, wr_ref, wa_ref, wb_ref, wc_ref,
                  wm_ref, gf_ref, x1_ref, xn_ref):
    oa = _dot(oa_ref[...].astype(BF16), wr_ref[...])
    ysb = ys_ref[...].astype(BF16)
    ob = _dot(ysb, wa_ref[...]) * jax.nn.sigmoid(_dot(ysb, wb_ref[...]))
    oc = _dot(oc_ref[...].astype(BF16), wc_ref[...])
    merged = (jax.nn.sigmoid(ga_ref[...]) * oa + jax.nn.sigmoid(gb_ref[...]) * ob
              + jax.nn.sigmoid(gc_ref[...]) * oc)
    x1 = x_ref[...] + _dot(merged.astype(BF16), wm_ref[...])
    x1_ref[...] = x1
    xn_ref[...] = _rms(x1, gf_ref[...]).astype(BF16)


def _merge(x, z, oa, ys, oc, wr, wa, wb, wc, wm, gf):
    t = x.shape[0]
    rb = 512
    row = lambda w: pl.BlockSpec((rb, w), lambda i: (i, 0))
    gate = lambda j: pl.BlockSpec((rb, D_MODEL), lambda i: (i, j))
    wsp = lambda k: pl.BlockSpec((k, D_MODEL), lambda i: (0, 0))
    return pl.pallas_call(
        _merge_kernel,
        grid=(t // rb,),
        in_specs=[row(D_MODEL), row(RET_W), row(SSM_W), row(CONV_W), gate(4), gate(5), gate(6),
                  wsp(RET_W), wsp(SSM_W), wsp(SSM_W), wsp(CONV_W), wsp(D_MODEL), wsp(1)],
        out_specs=[row(D_MODEL), row(D_MODEL)],
        out_shape=[jax.ShapeDtypeStruct((t, D_MODEL), F32), jax.ShapeDtypeStruct((t, D_MODEL), BF16)],
        compiler_params=_params(("parallel",), 48),
        name="merge",
    )(x, oa, ys, oc, z, z, z, wr, wa, wb, wc, wm, gf)


def _top16_of_keys(s_scr, v_scr, i_scr):
    key = lax.broadcasted_iota(jnp.int32, (PEER_NKEYS, SUBLANES, LANES), 0).astype(F32)

    def body(r, carry):
        s = s_scr[...]
        m = jnp.max(s, axis=0)
        idx = jnp.min(jnp.where(s == m[None], key, float(PEER_NKEYS)), axis=0)
        v_scr[r] = m
        i_scr[r] = idx
        s_scr[...] = jnp.where(key == idx[None], -jnp.inf, s)
        return carry

    lax.fori_loop(0, PEER_TOPK, body, 0)


def _select_kernel(tb, xn_ref, wq_ref, k1_ref, k2_ref, e1_ref, e2_ref, g_ref,
                   s_scr, v1_scr, i1_scr, v2_scr, i2_scr, cand_scr, sc_scr, se1_scr, se2_scr):
    q = _dot(xn_ref[...], wq_ref[...]).astype(BF16)
    hq = PEER_HEADS * PEER_DQ // 2
    s1 = _dot_nt(k1_ref[...], q[:, :hq])
    s2 = _dot_nt(k2_ref[...], q[:, hq:])
    ncand = len(_CANDS)
    for lt in range(tb // LANES):
        lanes = slice(lt * LANES, (lt + 1) * LANES)
        s_scr[...] = s1[:, lanes].reshape(PEER_NKEYS, SUBLANES, LANES)
        _top16_of_keys(s_scr, v1_scr, i1_scr)
        s_scr[...] = s2[:, lanes].reshape(PEER_NKEYS, SUBLANES, LANES)
        _top16_of_keys(s_scr, v2_scr, i2_scr)
        for n, (a, b) in enumerate(_CANDS):
            cand_scr[n] = v1_scr[a] + v2_scr[b]

        def body(r, carry):
            cs = [cand_scr[n] for n in range(ncand)]
            m = functools.reduce(jnp.maximum, cs)
            big = float(PEER_TOPK * PEER_TOPK)
            flat = functools.reduce(
                jnp.minimum, [jnp.where(cs[n] == m, float(a * PEER_TOPK + b), big) for n, (a, b) in enumerate(_CANDS)])
            for n, (a, b) in enumerate(_CANDS):
                cand_scr[n] = jnp.where(flat == float(a * PEER_TOPK + b), -jnp.inf, cs[n])
            fa = jnp.floor(flat * (1.0 / PEER_TOPK))
            fb = flat - fa * PEER_TOPK
            e1 = functools.reduce(jnp.add, [jnp.where(fa == float(a), i1_scr[a], 0.0) for a in range(PEER_TOPK)])
            e2 = functools.reduce(jnp.add, [jnp.where(fb == float(b), i2_scr[b], 0.0) for b in range(PEER_TOPK)])
            sc_scr[r] = m
            se1_scr[r] = e1
            se2_scr[r] = e2
            return carry

        lax.fori_loop(0, PEER_TOPK, body, 0)
        sc = sc_scr[...]
        ex = jnp.exp(sc - jnp.max(sc, axis=0, keepdims=True))
        gate = ex / jnp.sum(ex, axis=0, keepdims=True)
        rows = slice(lt * LANES, (lt + 1) * LANES)
        g_ref[rows, :] = gate.reshape(PEER_SLOTS, LANES).T
        e1_ref[rows, :] = se1_scr[...].reshape(PEER_SLOTS, LANES).T
        e2_ref[rows, :] = se2_scr[...].reshape(PEER_SLOTS, LANES).T


def _peer_select(xn, wq, k1big, k2big):
    t = xn.shape[0]
    tb = 256
    hq = PEER_HEADS * PEER_DQ // 2
    nk = PEER_NKEYS * PEER_HEADS
    const = lambda i: (0, 0)
    row = lambda dt: jax.ShapeDtypeStruct((t, PEER_SLOTS), dt)
    vec = lambda n: pltpu.VMEM((n, SUBLANES, LANES), F32)
    return pl.pallas_call(
        functools.partial(_select_kernel, tb),
        grid=(t // tb,),
        in_specs=[pl.BlockSpec((tb, D_MODEL), lambda i: (i, 0)),
                  pl.BlockSpec((D_MODEL, 2 * hq), const),
                  pl.BlockSpec((nk, hq), const),
                  pl.BlockSpec((nk, hq), const)],
        out_specs=[pl.BlockSpec((tb, PEER_SLOTS), lambda i: (i, 0))] * 3,
        out_shape=[row(F32), row(F32), row(F32)],
        scratch_shapes=[vec(PEER_NKEYS), vec(PEER_TOPK), vec(PEER_TOPK), vec(PEER_TOPK), vec(PEER_TOPK),
                        vec(len(_CANDS)), vec(PEER_TOPK), vec(PEER_TOPK), vec(PEER_TOPK)],
        compiler_params=_params(("parallel",), 40),
        name="peer_select",
    )(xn, wq, k1big, k2big)


def _peer_kernel(tb, eb, stride, xn_ref, e1_ref, e2_ref, g_ref, u_ref, v_ref, x1_ref, out_ref, m_scr):
    e = pl.program_id(1)

    @pl.when(e == 0)
    def _():
        key = lax.broadcasted_iota(jnp.int32, (PEER_NKEYS, PEER_SLOTS), 0).astype(F32)

        def token(t, carry):
            e1 = e1_ref[pl.ds(t, 1), :]
            e2 = e2_ref[pl.ds(t, 1), :]
            g = g_ref[pl.ds(t, 1), :]
            a_t = jnp.where(key == e1, g, 0.0).astype(BF16)
            b_t = jnp.where(key == e2, 1.0, 0.0).astype(BF16)
            m_scr[pl.ds(t, PEER_NKEYS, stride=stride), :] = _dot_nt(a_t, b_t)
            return carry

        lax.fori_loop(0, tb, token, 0)
        out_ref[...] = x1_ref[...]

    act = jax.nn.gelu(_dot_nt(xn_ref[...], u_ref[...]))
    nk1 = eb // PEER_NKEYS
    gates = [m_scr[pl.ds(pl.multiple_of((e * nk1 + i) * stride, SUBLANES), tb), :] for i in range(nk1)]
    w = (act * jnp.concatenate(gates, axis=1)).astype(BF16)
    out_ref[...] += _dot(w, v_ref[...])


def _peer_dense(xn, e1, e2, g, u, v, x1):
    t = xn.shape[0]
    tb, eb = 512, 512
    stride = tb + SUBLANES
    tok = lambda w: pl.BlockSpec((tb, w), lambda i, e: (i, 0))
    tab = pl.BlockSpec((eb, D_MODEL), lambda i, e: (e, 0))
    return pl.pallas_call(
        functools.partial(_peer_kernel, tb, eb, stride),
        grid=(t // tb, PEER_NEXP // eb),
        in_specs=[tok(D_MODEL), tok(PEER_SLOTS), tok(PEER_SLOTS), tok(PEER_SLOTS), tab, tab, tok(D_MODEL)],
        out_specs=tok(D_MODEL),
        out_shape=jax.ShapeDtypeStruct((t, D_MODEL), F32),
        scratch_shapes=[pltpu.VMEM((PEER_NKEYS * stride, PEER_NKEYS), F32)],
        compiler_params=_params(("parallel", "arbitrary"), 60),
        name="peer_dense",
    )(xn, e1, e2, g, u, v, x1)


def _norm_kernel(x_ref, g_ref, y_ref):
    y_ref[...] = _rms(x_ref[...], g_ref[...])


def _final_norm(x, g):
    t = x.shape[0]
    return pl.pallas_call(
        _norm_kernel,
        grid=(t // ROWS,),
        in_specs=[pl.BlockSpec((ROWS, D_MODEL), lambda i: (i, 0)), pl.BlockSpec((1, D_MODEL), lambda i: (0, 0))],
        out_specs=pl.BlockSpec((ROWS, D_MODEL), lambda i: (i, 0)),
        out_shape=jax.ShapeDtypeStruct((t, D_MODEL), F32),
        compiler_params=_params(("parallel",), 32),
        name="final_norm",
    )(x, g)


def _time_major(x):
    b, s, d = x.shape
    return x.transpose(1, 0, 2).reshape(s * b, d)


def _batch_major(y, b, s):
    return y.reshape(s, b, y.shape[-1]).transpose(1, 0, 2)


def kernel(x_prompt, x_sample, state_ret, state_ssm_re, state_ssm_im, state_conv, norm_mix, w_in, ret_norm, w_ret_out, ssm_a_re, ssm_a_im, ssm_b_re, ssm_b_im, ssm_c_re, ssm_c_im, ssm_d, ssm_log_dt, w_glu_a, w_glu_b, conv_w, conv_b, w_conv_out, w_mix_out, norm_ffn, peer_wq, peer_k1, peer_k2, peer_u, peer_v, norm_final):
    bp, sp, _ = x_prompt.shape
    bs, ss, _ = x_sample.shape
    tp, ts = bp * sp, bs * ss
    depth = w_in.shape[0]
    groups = ((0, tp // ROWS, bp), (tp // ROWS, ts // ROWS, bs))
    assert tp % ROWS == 0 and ts == ROWS and ROWS % bp == 0 and ROWS // bp == math.gcd(sp, RET_CHUNK)

    x = jnp.concatenate([_time_major(x_prompt), _time_major(x_sample)], axis=0)

    pos_p = jnp.repeat(jnp.arange(sp, dtype=F32), bp)
    pos_s = jnp.repeat(PAST_LEN + jnp.arange(ss, dtype=F32), bs)
    ret_tabs = (_retention_tables(pos_p, bp), _retention_tables(pos_s, bs))

    lg = depth * SSM_G
    abre, abim, bbre, bbim = _s5_discretise(
        ssm_a_re.reshape(lg, SSM_P), ssm_a_im.reshape(lg, SSM_P), ssm_log_dt.reshape(lg, 1),
        ssm_b_re.transpose(0, 1, 3, 2).reshape(lg, SSM_GC, SSM_P),
        ssm_b_im.transpose(0, 1, 3, 2).reshape(lg, SSM_GC, SSM_P))

    ret_p, ret_s, re_p, re_s, im_p, im_s, cv_p, cv_s = [], [], [], [], [], [], [], []
    for l in range(depth):
        sl = slice(l * SSM_G, (l + 1) * SSM_G)
        bmat = jnp.concatenate([_block_diag_slabs(bbre[sl], None), _block_diag_slabs(bbim[sl], None)],
                               axis=2).astype(BF16)
        cmat = jnp.concatenate([_block_diag_slabs(ssm_c_re[l].transpose(0, 2, 1), None),
                                _block_diag_slabs(-ssm_c_im[l].transpose(0, 2, 1), None)],
                               axis=1).astype(BF16)
        are_row = abre[sl].reshape(1, SSM_N)
        aim_row = abim[sl].reshape(1, SSM_N)

        z = _inproj(x, norm_mix[l][None, :], w_in[l].astype(BF16))

        oa_l, ys_l, oc_l = [], [], []
        for gi, (blk0, nblk, batch) in enumerate(groups):
            if gi == 0:
                s0 = jnp.zeros((batch, RET_HEADS, RET_DK, RET_DK), F32)
                h0re = jnp.zeros((batch, SSM_N), F32)
                h0im = jnp.zeros((batch, SSM_N), F32)
                buf0 = jnp.zeros(((CONV_K - 1) * batch, CONV_W), F32)
                bblk = batch
            else:
                s0 = state_ret[l]
                h0re = state_ssm_re[l].reshape(batch, SSM_N)
                h0im = state_ssm_im[l].reshape(batch, SSM_N)
                buf0 = state_conv[l].transpose(1, 0, 2).reshape((CONV_K - 1) * batch, CONV_W)
                bblk = 16
            oa, s_new = _retention(z, blk0, nblk, batch, bblk, ret_tabs[gi], ret_norm[l][None, :], s0)
            ys, xre, xim = _s5(z, blk0, nblk, batch, bmat, cmat, are_row, aim_row, ssm_d[l][None, :], h0re, h0im)
            oc, buf = _conv(z, blk0, nblk, batch, buf0, conv_w[l], conv_b[l][None, :])
            oa_l.append(oa); ys_l.append(ys); oc_l.append(oc)
            cv = buf.reshape(CONV_K - 1, batch, CONV_W).transpose(1, 0, 2)
            sre = xre.reshape(batch, SSM_G, SSM_P)
            sim = xim.reshape(batch, SSM_G, SSM_P)
            if gi == 0:
                ret_p.append(s_new); re_p.append(sre); im_p.append(sim); cv_p.append(cv)
            else:
                ret_s.append(s_new); re_s.append(sre); im_s.append(sim); cv_s.append(cv)

        x1, xn = _merge(x, z, jnp.concatenate(oa_l), jnp.concatenate(ys_l), jnp.concatenate(oc_l),
                        w_ret_out[l].astype(BF16), w_glu_a[l].astype(BF16), w_glu_b[l].astype(BF16),
                        w_conv_out[l].astype(BF16), w_mix_out[l].astype(BF16), norm_ffn[l][None, :])

        hq = PEER_DQ // 2
        wq = peer_wq[l].reshape(D_MODEL, PEER_HEADS, 2, hq).transpose(0, 2, 1, 3).reshape(D_MODEL, -1).astype(BF16)
        eye = jnp.eye(PEER_HEADS, dtype=F32)

        def keys_block_diag(k):
            return (k.transpose(1, 0, 2)[:, :, None, :] * eye[None, :, :, None]).reshape(
                PEER_NKEYS * PEER_HEADS, PEER_HEADS * hq).astype(BF16)

        e1, e2, g = _peer_select(xn, wq, keys_block_diag(peer_k1[l]), keys_block_diag(peer_k2[l]))
        x = _peer_dense(xn, e1, e2, g, peer_u[l].astype(BF16), peer_v[l].astype(BF16), x1)

    y = _final_norm(x, norm_final[None, :])
    y_prompt = _batch_major(y[:tp], bp, sp)
    y_sample = _batch_major(y[tp:], bs, ss)
    return (y_prompt, y_sample,
            jnp.stack(ret_p), jnp.stack(ret_s),
            jnp.stack(re_p), jnp.stack(re_s),
            jnp.stack(im_p), jnp.stack(im_s),
            jnp.stack(cv_p), jnp.stack(cv_s))
```

```python
import functools
import math

import jax
import jax.numpy as jnp
from jax import lax
from jax.experimental import pallas as pl
from jax.experimental.pallas import tpu as pltpu

F32 = jnp.float32
BF16 = jnp.bfloat16

D_MODEL = 1024
DEPTH = 2
PAST_LEN = 16384
RET_HEADS = 8
RET_DK = 64
RET_W = 512
RET_CHUNK = 128
ROPE_BASE = 10000.0
SSM_W = 512
SSM_GC = 16
SSM_G = 32
SSM_P = 64
SSM_N = SSM_G * SSM_P
SSM_SLABS = 4
SSM_SLAB_N = SSM_N // SSM_SLABS
CONV_W = 512
CONV_K = 3
PROJ_W = 7168
PEER_HEADS = 8
PEER_DQ = 256
PEER_NKEYS = 128
PEER_TOPK = 16
PEER_NEXP = PEER_NKEYS ** 2
PEER_SLOTS = PEER_HEADS * PEER_TOPK
EPS = 1e-6

ROWS = 1024
LANES = 128
SUBLANES = 8
MIB = 1024 * 1024

_CANDS = [(a, b) for a in range(PEER_TOPK) for b in range(PEER_TOPK) if (a + 1) * (b + 1) <= PEER_TOPK]


def _params(sem, vmem_mib):
    return pltpu.CompilerParams(dimension_semantics=sem, vmem_limit_bytes=vmem_mib * MIB)


def _rms(x, g):
    return x * lax.rsqrt(jnp.mean(x * x, axis=-1, keepdims=True) + EPS) * g


def _dot(a, b):
    return jnp.dot(a, b, preferred_element_type=F32)


def _dot_nt(a, b):
    return lax.dot_general(a, b, (((1,), (1,)), ((), ())), preferred_element_type=F32)


def _inproj_kernel(x_ref, g_ref, w_ref, z_ref, h_scr):
    @pl.when(pl.program_id(1) == 0)
    def _():
        h_scr[...] = _rms(x_ref[...], g_ref[...]).astype(BF16)

    z_ref[...] = _dot(h_scr[...], w_ref[...])


def _inproj(x, g, w):
    t = x.shape[0]
    nb = 1024
    return pl.pallas_call(
        _inproj_kernel,
        grid=(t // ROWS, PROJ_W // nb),
        in_specs=[pl.BlockSpec((ROWS, D_MODEL), lambda i, j: (i, 0)),
                  pl.BlockSpec((1, D_MODEL), lambda i, j: (0, 0)),
                  pl.BlockSpec((D_MODEL, nb), lambda i, j: (0, j))],
        out_specs=pl.BlockSpec((ROWS, nb), lambda i, j: (i, j)),
        out_shape=jax.ShapeDtypeStruct((t, PROJ_W), F32),
        scratch_shapes=[pltpu.VMEM((ROWS, D_MODEL), BF16)],
        compiler_params=_params(("parallel", "arbitrary"), 40),
        name="inproj",
    )(x, g, w)


def _ret_kernel(batch, bblk, z_ref, cos_ref, sa_ref, sb_ref, qdec_ref, kdec_ref, cdec_ref, gn_ref, s0_ref,
                o_ref, s_ref, qd_scr, kd_scr, v_scr, mask_scr, oacc_scr):
    bb = pl.program_id(0)
    c = pl.program_id(1)
    steps = ROWS // batch
    nslab = RET_W // LANES

    def head_view(ref, h):
        return ref[h // 2, :, (h % 2) * RET_DK:(h % 2 + 1) * RET_DK]

    @pl.when((bb == 0) & (c == 0))
    def _():
        r = lax.broadcasted_iota(jnp.int32, (ROWS, ROWS), 0)
        cc = lax.broadcasted_iota(jnp.int32, (ROWS, ROWS), 1)
        same = (r & (batch - 1)) == (cc & (batch - 1))
        mask_scr[...] = (same & (r >= cc)).astype(F32)

    @pl.when(c == 0)
    def _():
        s_ref[...] = s0_ref[...]

    @pl.when(bb == 0)
    def _():
        cos, sa, sb = cos_ref[...], sa_ref[...], sb_ref[...]

        def rot(x):
            return x * cos + pltpu.roll(x, 32, 1) * sa + pltpu.roll(x, 96, 1) * sb

        for s in range(nslab):
            cols = slice(s * LANES, (s + 1) * LANES)
            qd_scr[s] = rot(z_ref[:, cols]) * qdec_ref[:, cols]
            kd_scr[s] = rot(z_ref[:, RET_W + s * LANES:RET_W + (s + 1) * LANES]) * kdec_ref[:, cols]
            v_scr[s] = z_ref[:, 2 * RET_W + s * LANES:2 * RET_W + (s + 1) * LANES]
        for s in range(nslab):
            outs = []
            for h in (2 * s, 2 * s + 1):
                qh = head_view(qd_scr, h).astype(BF16)
                kh = head_view(kd_scr, h).astype(BF16)
                vh = head_view(v_scr, h).astype(BF16)
                p = (_dot_nt(qh, kh) * mask_scr[...]).astype(BF16)
                outs.append(_dot(p, vh))
            oacc_scr[s] = jnp.concatenate(outs, axis=1)

    def per_seq(bl, carry):
        b = bb * bblk + bl
        rows = pl.ds(b, steps, stride=batch)
        for s in range(nslab):
            qb = qd_scr[s, rows, :]
            kb = kd_scr[s, rows, :]
            vb = v_scr[s, rows, :]
            cross = []
            for hh in range(2):
                h = 2 * s + hh
                hc = slice(hh * RET_DK, (hh + 1) * RET_DK)
                st = s_ref[bl, h]
                cross.append(_dot(qb[:, hc].astype(BF16), st.astype(BF16)))
                upd = lax.dot_general(kb[:, hc].astype(BF16), vb[:, hc].astype(BF16),
                                      (((0,), (0,)), ((), ())), preferred_element_type=F32)
                s_ref[bl, h] = (st + upd) * cdec_ref[:, h * RET_DK:(h + 1) * RET_DK]
            oacc_scr[s, rows, :] = oacc_scr[s, rows, :] + jnp.concatenate(cross, axis=1)
        return carry

    lax.fori_loop(0, bblk, per_seq, 0)

    @pl.when(bb == pl.num_programs(0) - 1)
    def _():
        normed = []
        for h in range(RET_HEADS):
            oh = head_view(oacc_scr, h)
            mu = jnp.mean(oh, axis=-1, keepdims=True)
            dlt = oh - mu
            var = jnp.mean(dlt * dlt, axis=-1, keepdims=True)
            normed.append(dlt * lax.rsqrt(var + EPS))
        o = jnp.concatenate(normed, axis=1) * gn_ref[...]
        o_ref[...] = jax.nn.silu(z_ref[:, 3 * RET_W:4 * RET_W]) * o


def _retention(z, row_blk0, nblk, batch, bblk, tabs, gn, s0):
    cos, sa, sb, qdec, kdec, cdec = tabs
    nbb = batch // bblk
    st_spec = pl.BlockSpec((bblk, RET_HEADS, RET_DK, RET_DK), lambda bb, c: (bb, 0, 0, 0))
    const = lambda bb, c: (0, 0)
    return pl.pallas_call(
        functools.partial(_ret_kernel, batch, bblk),
        grid=(nbb, nblk),
        in_specs=[pl.BlockSpec((ROWS, 4 * RET_W), lambda bb, c: (row_blk0 + c, 0)),
                  pl.BlockSpec((ROWS, LANES), lambda bb, c: (c, 0)),
                  pl.BlockSpec((ROWS, LANES), lambda bb, c: (c, 0)),
                  pl.BlockSpec((ROWS, LANES), lambda bb, c: (c, 0)),
                  pl.BlockSpec((ROWS, RET_W), const),
                  pl.BlockSpec((ROWS, RET_W), const),
                  pl.BlockSpec((1, RET_W), const),
                  pl.BlockSpec((1, RET_W), const),
                  st_spec],
        out_specs=[pl.BlockSpec((ROWS, RET_W), lambda bb, c: (c, 0)), st_spec],
        out_shape=[jax.ShapeDtypeStruct((nblk * ROWS, RET_W), F32),
                   jax.ShapeDtypeStruct((batch, RET_HEADS, RET_DK, RET_DK), F32)],
        scratch_shapes=[pltpu.VMEM((RET_W // LANES, ROWS, LANES), F32)] * 3
        + [pltpu.VMEM((ROWS, ROWS), F32), pltpu.VMEM((RET_W // LANES, ROWS, LANES), F32)],
        compiler_params=_params(("arbitrary", "arbitrary"), 56),
        name="retention",
    )(z, cos, sa, sb, qdec, kdec, cdec, gn, s0)


def _retention_tables(pos, batch):
    half = RET_DK // 2
    freqs = ROPE_BASE ** (-jnp.arange(half, dtype=F32) / half)
    ang = pos[:, None] * freqs[None, :]
    cos, sin = jnp.cos(ang), jnp.sin(ang)
    zero = jnp.zeros_like(sin)
    reps = LANES // RET_DK
    cos_t = jnp.tile(jnp.concatenate([cos, cos], axis=1), (1, reps))
    sa_t = jnp.tile(jnp.concatenate([zero, sin], axis=1), (1, reps))
    sb_t = jnp.tile(jnp.concatenate([-sin, zero], axis=1), (1, reps))
    lg = jnp.log1p(-jnp.exp2(-5.0 - jnp.arange(RET_HEADS, dtype=F32)))
    steps = ROWS // batch
    i1 = (jnp.arange(ROWS) // batch).astype(F32) + 1.0
    qdec = jnp.repeat(jnp.exp(i1[:, None] * lg[None, :]), RET_DK, axis=1)
    kdec = jnp.repeat(jnp.exp(-i1[:, None] * lg[None, :]), RET_DK, axis=1) * (RET_DK ** -0.5)
    cdec = jnp.repeat(jnp.exp(steps * lg), RET_DK)[None, :]
    return cos_t, sa_t, sb_t, qdec, kdec, cdec


def _s5_disc_kernel(are_ref, aim_ref, ldt_ref, bre_ref, bim_ref, abre_ref, abim_ref, bbre_ref, bbim_ref):
    ar, ai = are_ref[...], aim_ref[...]
    dt = jnp.exp(ldt_ref[...])
    dar, dai = dt * ar, dt * ai
    mag = jnp.exp(dar)
    abar_re, abar_im = mag * jnp.cos(dai), mag * jnp.sin(dai)
    den = ar * ar + ai * ai
    nr, ni = abar_re - 1.0, abar_im
    f_re = (nr * ar + ni * ai) / den
    f_im = (ni * ar - nr * ai) / den
    abre_ref[...] = abar_re
    abim_ref[...] = abar_im
    br, bi = bre_ref[...], bim_ref[...]
    bbre_ref[...] = f_re[:, None, :] * br - f_im[:, None, :] * bi
    bbim_ref[...] = f_re[:, None, :] * bi + f_im[:, None, :] * br


def _s5_discretise(a_re, a_im, log_dt, b_re_t, b_im_t):
    lg = a_re.shape[0]
    small = jax.ShapeDtypeStruct((lg, SSM_P), F32)
    big = jax.ShapeDtypeStruct((lg, SSM_GC, SSM_P), F32)
    return pl.pallas_call(_s5_disc_kernel, out_shape=[small, small, big, big], name="s5_disc")(
        a_re, a_im, log_dt, b_re_t, b_im_t)


def _s5_kernel(batch, u_ref, bmat_ref, cmat_ref, are_ref, aim_ref, d_ref, h0re_ref, h0im_ref,
               y_ref, xre_ref, xim_ref, x_scr):
    c = pl.program_id(0)
    steps = ROWS // batch
    half = SSM_SLAB_N

    @pl.when(c == 0)
    def _():
        xre_ref[...] = h0re_ref[...]
        xim_ref[...] = h0im_ref[...]

    u = u_ref[...]
    ub = u.astype(BF16)
    for s in range(SSM_SLABS):
        x_scr[:, 2 * half * s:2 * half * (s + 1)] = _dot(ub[:, s * LANES:(s + 1) * LANES], bmat_ref[s])

    for s in range(SSM_SLABS):
        re0 = 2 * half * s
        im0 = re0 + half
        sc = slice(half * s, half * (s + 1))
        ar = jnp.broadcast_to(are_ref[:, sc], (SUBLANES, half))
        ai = jnp.broadcast_to(aim_ref[:, sc], (SUBLANES, half))

        def row_tile(rt, carry, re0=re0, im0=im0, sc=sc, ar=ar, ai=ai):
            r0 = pl.multiple_of(rt * SUBLANES, SUBLANES)

            def step(t, x):
                xr, xi = x
                row = pl.multiple_of(t * batch + r0, SUBLANES)
                nr = ar * xr - ai * xi + x_scr[pl.ds(row, SUBLANES), re0:re0 + half]
                ni = ar * xi + ai * xr + x_scr[pl.ds(row, SUBLANES), im0:im0 + half]
                x_scr[pl.ds(row, SUBLANES), re0:re0 + half] = nr
                x_scr[pl.ds(row, SUBLANES), im0:im0 + half] = ni
                return nr, ni

            init = (xre_ref[pl.ds(r0, SUBLANES), sc], xim_ref[pl.ds(r0, SUBLANES), sc])
            xr, xi = lax.fori_loop(0, steps, step, init, unroll=8)
            xre_ref[pl.ds(r0, SUBLANES), sc] = xr
            xim_ref[pl.ds(r0, SUBLANES), sc] = xi
            return carry

        lax.fori_loop(0, batch // SUBLANES, row_tile, 0)

    ys = [_dot(x_scr[:, 2 * half * s:2 * half * (s + 1)].astype(BF16), cmat_ref[s]) for s in range(SSM_SLABS)]
    y = jnp.concatenate(ys, axis=1) + d_ref[...] * u
    y_ref[...] = jax.nn.gelu(y)


def _s5(z, row_blk0, nblk, batch, bmat, cmat, abre, abim, d, h0re, h0im):
    const2 = lambda c: (0, 0)
    const3 = lambda c: (0, 0, 0)
    st = pl.BlockSpec((batch, SSM_N), const2)
    return pl.pallas_call(
        functools.partial(_s5_kernel, batch),
        grid=(nblk,),
        in_specs=[pl.BlockSpec((ROWS, SSM_W), lambda c: (row_blk0 + c, 4)),
                  pl.BlockSpec((SSM_SLABS, LANES, 2 * SSM_SLAB_N), const3),
                  pl.BlockSpec((SSM_SLABS, 2 * SSM_SLAB_N, LANES), const3),
                  pl.BlockSpec((1, SSM_N), const2),
                  pl.BlockSpec((1, SSM_N), const2),
                  pl.BlockSpec((1, SSM_W), const2),
                  st, st],
        out_specs=[pl.BlockSpec((ROWS, SSM_W), lambda c: (c, 0)), st, st],
        out_shape=[jax.ShapeDtypeStruct((nblk * ROWS, SSM_W), F32),
                   jax.ShapeDtypeStruct((batch, SSM_N), F32),
                   jax.ShapeDtypeStruct((batch, SSM_N), F32)],
        scratch_shapes=[pltpu.VMEM((ROWS, 2 * SSM_N), F32)],
        compiler_params=_params(("arbitrary",), 48),
        name="s5",
    )(z, bmat, cmat, abre, abim, d, h0re, h0im)


def _block_diag_slabs(w, rows_inner):
    gps = SSM_G // SSM_SLABS
    eye = jnp.eye(gps, dtype=w.dtype)
    w4 = w.reshape(SSM_SLABS, gps, w.shape[1], w.shape[2])
    out = w4[:, :, :, None, :] * eye[None, :, None, :, None]
    return out.reshape(SSM_SLABS, gps * w.shape[1], gps * w.shape[2])


def _conv_kernel(batch, bg_ref, cg_ref, hc_ref, buf0_ref, w_ref, b_ref, o_ref, buf_ref, zp_scr):
    c = pl.program_id(0)
    pad = (CONV_K - 1) * batch

    @pl.when(c == 0)
    def _():
        zp_scr[0:pad, :] = buf0_ref[...]

    zc = cg_ref[...] * hc_ref[...]
    zp_scr[pad:pad + ROWS, :] = zc
    y = b_ref[...]
    for j in range(CONV_K):
        y = y + w_ref[j:j + 1, :] * zp_scr[j * batch:j * batch + ROWS, :]
    o_ref[...] = bg_ref[...] * y
    tail = zp_scr[ROWS:ROWS + pad, :]
    buf_ref[...] = tail
    zp_scr[0:pad, :] = tail


def _conv(z, row_blk0, nblk, batch, buf0, w, b):
    pad = (CONV_K - 1) * batch
    const = lambda c: (0, 0)
    return pl.pallas_call(
        functools.partial(_conv_kernel, batch),
        grid=(nblk,),
        in_specs=[pl.BlockSpec((ROWS, CONV_W), lambda c: (row_blk0 + c, 5)),
                  pl.BlockSpec((ROWS, CONV_W), lambda c: (row_blk0 + c, 6)),
                  pl.BlockSpec((ROWS, CONV_W), lambda c: (row_blk0 + c, 7)),
                  pl.BlockSpec((pad, CONV_W), const),
                  pl.BlockSpec((CONV_K, CONV_W), const),
                  pl.BlockSpec((1, CONV_W), const)],
        out_specs=[pl.BlockSpec((ROWS, CONV_W), lambda c: (c, 0)), pl.BlockSpec((pad, CONV_W), const)],
        out_shape=[jax.ShapeDtypeStruct((nblk * ROWS, CONV_W), F32),
                   jax.ShapeDtypeStruct((pad, CONV_W), F32)],
        scratch_shapes=[pltpu.VMEM((ROWS + pad, CONV_W), F32)],
        compiler_params=_params(("arbitrary",), 32),
        name="conv",
    )(z, z, z, buf0, w, b)


def _merge_kernel(x_ref, oa_ref, ys_ref, oc_ref, ga_ref, gb_ref, gc_ref, wr_ref, wa_ref, wb_ref, wc_ref,
                  wm_ref, gf_ref, x1_ref, xn_ref):
    oa = _dot(oa_ref[...].astype(BF16), wr_ref[...])
    ysb = ys_ref[...].astype(BF16)
    ob = _dot(ysb, wa_ref[...]) * jax.nn.sigmoid(_dot(ysb, wb_ref[...]))
    oc = _dot(oc_ref[...].astype(BF16), wc_ref[...])
    merged = (jax.nn.sigmoid(ga_ref[...]) * oa + jax.nn.sigmoid(gb_ref[...]) * ob
              + jax.nn.sigmoid(gc_ref[...]) * oc)
    x1 = x_ref[...] + _dot(merged.astype(BF16), wm_ref[...])
    x1_ref[...] = x1
    xn_ref[...] = _rms(x1, gf_ref[...]).astype(BF16)


def _merge(x, z, oa, ys, oc, wr, wa, wb, wc, wm, gf):
    t = x.shape[0]
    rb = 512
    row = lambda w: pl.BlockSpec((rb, w), lambda i: (i, 0))
    gate = lambda j: pl.BlockSpec((rb, D_MODEL), lambda i: (i, j))
    wsp = lambda k: pl.BlockSpec((k, D_MODEL), lambda i: (0, 0))
    return pl.pallas_call(
        _merge_kernel,
        grid=(t // rb,),
        in_specs=[row(D_MODEL), row(RET_W), row(SSM_W), row(CONV_W), gate(4), gate(5), gate(6),
                  wsp(RET_W), wsp(SSM_W), wsp(SSM_W), wsp(CONV_W), wsp(D_MODEL), wsp(1)],
        out_specs=[row(D_MODEL), row(D_MODEL)],
        out_shape=[jax.ShapeDtypeStruct((t, D_MODEL), F32), jax.ShapeDtypeStruct((t, D_MODEL), BF16)],
        compiler_params=_params(("parallel",), 48),
        name="merge",
    )(x, oa, ys, oc, z, z, z, wr, wa, wb, wc, wm, gf)


def _tree(items, combine):
    while len(items) > 1:
        nxt = [combine(items[i], items[i + 1]) for i in range(0, len(items) - 1, 2)]
        if len(items) % 2:
            nxt.append(items[-1])
        items = nxt
    return items[0]


def _first_max(x, y):
    (vx, ix), (vy, iy) = x, y
    return jnp.maximum(vx, vy), jnp.where(vx >= vy, ix, iy)


def _top16_of_keys(s_scrs, v_scrs, i_scrs):
    def body(r, carry):
        for s_scr, v_scr, i_scr in zip(s_scrs, v_scrs, i_scrs):
            m, idx = _tree([(s_scr[k], float(k)) for k in range(PEER_NKEYS)], _first_max)
            v_scr[r] = m
            i_scr[r] = idx
            for k in range(PEER_NKEYS):
                s_scr[k] = jnp.where(idx == float(k), -jnp.inf, s_scr[k])
        return carry

    lax.fori_loop(0, PEER_TOPK, body, 0)


def _select_kernel(tb, xn_ref, wq_ref, k1_ref, k2_ref, e1_ref, e2_ref, g_ref,
                   s1_scr, s2_scr, v1_scr, i1_scr, v2_scr, i2_scr, cand_scr, sc_scr, se1_scr, se2_scr):
    q = _dot(xn_ref[...], wq_ref[...]).astype(BF16)
    hq = PEER_HEADS * PEER_DQ // 2
    s1 = _dot_nt(k1_ref[...], q[:, :hq])
    s2 = _dot_nt(k2_ref[...], q[:, hq:])
    flats = [float(a * PEER_TOPK + b) for a, b in _CANDS]
    for lt in range(tb // LANES):
        lanes = slice(lt * LANES, (lt + 1) * LANES)
        s1_scr[...] = s1[:, lanes].reshape(PEER_NKEYS, SUBLANES, LANES)
        s2_scr[...] = s2[:, lanes].reshape(PEER_NKEYS, SUBLANES, LANES)
        _top16_of_keys((s1_scr, s2_scr), (v1_scr, v2_scr), (i1_scr, i2_scr))
        for n, (a, b) in enumerate(_CANDS):
            cand_scr[n] = v1_scr[a] + v2_scr[b]

        def body(r, carry):
            m, flat = _tree([(cand_scr[n], f) for n, f in enumerate(flats)], _first_max)
            for n, f in enumerate(flats):
                cand_scr[n] = jnp.where(flat == f, -jnp.inf, cand_scr[n])
            fa = jnp.floor(flat * (1.0 / PEER_TOPK))
            fb = flat - fa * PEER_TOPK
            sc_scr[r] = m
            se1_scr[r] = _tree([jnp.where(fa == float(a), i1_scr[a], 0.0) for a in range(PEER_TOPK)], jnp.add)
            se2_scr[r] = _tree([jnp.where(fb == float(b), i2_scr[b], 0.0) for b in range(PEER_TOPK)], jnp.add)
            return carry

        lax.fori_loop(0, PEER_TOPK, body, 0)
        sc = sc_scr[...]
        ex = jnp.exp(sc - jnp.max(sc, axis=0, keepdims=True))
        gate = ex / jnp.sum(ex, axis=0, keepdims=True)
        rows = slice(lt * LANES, (lt + 1) * LANES)
        g_ref[rows, :] = gate.reshape(PEER_SLOTS, LANES).T
        e1_ref[rows, :] = se1_scr[...].reshape(PEER_SLOTS, LANES).T
        e2_ref[rows, :] = se2_scr[...].reshape(PEER_SLOTS, LANES).T


def _peer_select(xn, wq, k1big, k2big):
    t = xn.shape[0]
    tb = 256
    hq = PEER_HEADS * PEER_DQ // 2
    nk = PEER_NKEYS * PEER_HEADS
    const = lambda i: (0, 0)
    row = lambda dt: jax.ShapeDtypeStruct((t, PEER_SLOTS), dt)
    vec = lambda n: pltpu.VMEM((n, SUBLANES, LANES), F32)
    return pl.pallas_call(
        functools.partial(_select_kernel, tb),
        grid=(t // tb,),
        in_specs=[pl.BlockSpec((tb, D_MODEL), lambda i: (i, 0)),
                  pl.BlockSpec((D_MODEL, 2 * hq), const),
                  pl.BlockSpec((nk, hq), const),
                  pl.BlockSpec((nk, hq), const)],
        out_specs=[pl.BlockSpec((tb, PEER_SLOTS), lambda i: (i, 0))] * 3,
        out_shape=[row(F32), row(F32), row(F32)],
        scratch_shapes=[vec(PEER_NKEYS), vec(PEER_NKEYS), vec(PEER_TOPK), vec(PEER_TOPK), vec(PEER_TOPK), vec(PEER_TOPK),
                        vec(len(_CANDS)), vec(PEER_TOPK), vec(PEER_TOPK), vec(PEER_TOPK)],
        compiler_params=_params(("parallel",), 40),
        name="peer_select",
    )(xn, wq, k1big, k2big)


def _peer_kernel(tb, eb, stride, xn_ref, e1_ref, e2_ref, g_ref, u_ref, v_ref, x1_ref, out_ref, m_scr):
    e = pl.program_id(1)

    @pl.when(e == 0)
    def _():
        out_ref[...] = x1_ref[...]

    @pl.when(e == 0)
    def _():
        key = lax.broadcasted_iota(jnp.int32, (PEER_NKEYS, PEER_SLOTS), 0).astype(F32)

        def token(t, carry):
            e1 = e1_ref[pl.ds(t, 1), :]
            e2 = e2_ref[pl.ds(t, 1), :]
            g = g_ref[pl.ds(t, 1), :]
            a_t = jnp.where(key == e1, g, 0.0).astype(BF16)
            b_t = jnp.where(key == e2, 1.0, 0.0).astype(BF16)
            row0 = pl.multiple_of(t * stride, SUBLANES)
            m_scr[pl.ds(row0, PEER_NKEYS), :] = _dot_nt(a_t, b_t)
            return carry

        lax.fori_loop(0, tb, token, 0, unroll=16)

    act =jax.nn.gelu(_dot_nt(xn_ref[...], u_ref[...]))
    nk1 = eb // PEER_NKEYS
    gates = [m_scr[pl.ds(e * nk1 + i, tb, stride=stride), :] for i in range(nk1)]
    w = (act * jnp.concatenate(gates, axis=1)).astype(BF16)
    out_ref[...] += _dot(w, v_ref[...])


def _peer_dense(xn, e1, e2, g, u, v, x1):
    t = xn.shape[0]
    tb, eb = 512, 512
    stride = PEER_NKEYS + SUBLANES
    tok = lambda w: pl.BlockSpec((tb, w), lambda i, e: (i, 0))
    tab = pl.BlockSpec((eb, D_MODEL), lambda i, e: (e, 0))
    return pl.pallas_call(
        functools.partial(_peer_kernel, tb, eb, stride),
        grid=(t // tb, PEER_NEXP // eb),
        in_specs=[tok(D_MODEL), tok(PEER_SLOTS), tok(PEER_SLOTS), tok(PEER_SLOTS), tab, tab, tok(D_MODEL)],
        out_specs=tok(D_MODEL),
        out_shape=jax.ShapeDtypeStruct((t, D_MODEL), F32),
        scratch_shapes=[pltpu.VMEM((tb * stride, PEER_NKEYS), F32)],
        compiler_params=_params(("parallel", "arbitrary"), 60),
        name="peer_dense",
    )(xn, e1, e2, g, u, v, x1)


def _norm_kernel(x_ref, g_ref, y_ref):
    y_ref[...] = _rms(x_ref[...], g_ref[...])


def _final_norm(x, g):
    t = x.shape[0]
    return pl.pallas_call(
        _norm_kernel,
        grid=(t // ROWS,),
        in_specs=[pl.BlockSpec((ROWS, D_MODEL), lambda i: (i, 0)), pl.BlockSpec((1, D_MODEL), lambda i: (0, 0))],
        out_specs=pl.BlockSpec((ROWS, D_MODEL), lambda i: (i, 0)),
        out_shape=jax.ShapeDtypeStruct((t, D_MODEL), F32),
        compiler_params=_params(("parallel",), 32),
        name="final_norm",
    )(x, g)


def _time_major(x):
    b, s, d = x.shape
    return x.transpose(1, 0, 2).reshape(s * b, d)


def _batch_major(y, b, s):
    return y.reshape(s, b, y.shape[-1]).transpose(1, 0, 2)


def kernel(x_prompt, x_sample, state_ret, state_ssm_re, state_ssm_im, state_conv, norm_mix, w_in, ret_norm, w_ret_out, ssm_a_re, ssm_a_im, ssm_b_re, ssm_b_im, ssm_c_re, ssm_c_im, ssm_d, ssm_log_dt, w_glu_a, w_glu_b, conv_w, conv_b, w_conv_out, w_mix_out, norm_ffn, peer_wq, peer_k1, peer_k2, peer_u, peer_v, norm_final):
    bp, sp, _ = x_prompt.shape
    bs, ss, _ = x_sample.shape
    tp, ts = bp * sp, bs * ss
    depth = w_in.shape[0]
    groups = ((0, tp // ROWS, bp), (tp // ROWS, ts // ROWS, bs))
    assert tp % ROWS == 0 and ts == ROWS and ROWS % bp == 0 and ROWS // bp == math.gcd(sp, RET_CHUNK)

    x = jnp.concatenate([_time_major(x_prompt), _time_major(x_sample)], axis=0)

    pos_p = jnp.repeat(jnp.arange(sp, dtype=F32), bp)
    pos_s = jnp.repeat(PAST_LEN + jnp.arange(ss, dtype=F32), bs)
    ret_tabs = (_retention_tables(pos_p, bp), _retention_tables(pos_s, bs))

    lg = depth * SSM_G
    abre, abim, bbre, bbim = _s5_discretise(
        ssm_a_re.reshape(lg, SSM_P), ssm_a_im.reshape(lg, SSM_P), ssm_log_dt.reshape(lg, 1),
        ssm_b_re.transpose(0, 1, 3, 2).reshape(lg, SSM_GC, SSM_P),
        ssm_b_im.transpose(0, 1, 3, 2).reshape(lg, SSM_GC, SSM_P))

    ret_p, ret_s, re_p, re_s, im_p, im_s, cv_p, cv_s = [], [], [], [], [], [], [], []
    for l in range(depth):
        sl = slice(l * SSM_G, (l + 1) * SSM_G)
        bmat = jnp.concatenate([_block_diag_slabs(bbre[sl], None), _block_diag_slabs(bbim[sl], None)],
                               axis=2).astype(BF16)
        cmat = jnp.concatenate([_block_diag_slabs(ssm_c_re[l].transpose(0, 2, 1), None),
                                _block_diag_slabs(-ssm_c_im[l].transpose(0, 2, 1), None)],
                               axis=1).astype(BF16)
        are_row = abre[sl].reshape(1, SSM_N)
        aim_row = abim[sl].reshape(1, SSM_N)

        z = _inproj(x, norm_mix[l][None, :], w_in[l].astype(BF16))

        oa_l, ys_l, oc_l = [], [], []
        for gi, (blk0, nblk, batch) in enumerate(groups):
            if gi == 0:
                s0 = jnp.zeros((batch, RET_HEADS, RET_DK, RET_DK), F32)
                h0re = jnp.zeros((batch, SSM_N), F32)
                h0im = jnp.zeros((batch, SSM_N), F32)
                buf0 = jnp.zeros(((CONV_K - 1) * batch, CONV_W), F32)
                bblk = batch
            else:
                s0 = state_ret[l]
                h0re = state_ssm_re[l].reshape(batch, SSM_N)
                h0im = state_ssm_im[l].reshape(batch, SSM_N)
                buf0 = state_conv[l].transpose(1, 0, 2).reshape((CONV_K - 1) * batch, CONV_W)
                bblk = 16
            oa, s_new = _retention(z, blk0, nblk, batch, bblk, ret_tabs[gi], ret_norm[l][None, :], s0)
            ys, xre, xim = _s5(z, blk0, nblk, batch, bmat, cmat, are_row, aim_row, ssm_d[l][None, :], h0re, h0im)
            oc, buf = _conv(z, blk0, nblk, batch, buf0, conv_w[l], conv_b[l][None, :])
            oa_l.append(oa); ys_l.append(ys); oc_l.append(oc)
            cv = buf.reshape(CONV_K - 1, batch, CONV_W).transpose(1, 0, 2)
            sre = xre.reshape(batch, SSM_G, SSM_P)
            sim = xim.reshape(batch, SSM_G, SSM_P)
            if gi == 0:
                ret_p.append(s_new); re_p.append(sre); im_p.append(sim); cv_p.append(cv)
            else:
                ret_s.append(s_new); re_s.append(sre); im_s.append(sim); cv_s.append(cv)

        x1, xn = _merge(x, z, jnp.concatenate(oa_l), jnp.concatenate(ys_l), jnp.concatenate(oc_l),
                        w_ret_out[l].astype(BF16), w_glu_a[l].astype(BF16), w_glu_b[l].astype(BF16),
                        w_conv_out[l].astype(BF16), w_mix_out[l].astype(BF16), norm_ffn[l][None, :])

        hq = PEER_DQ // 2
        wq = peer_wq[l].reshape(D_MODEL, PEER_HEADS, 2, hq).transpose(0, 2, 1, 3).reshape(D_MODEL, -1).astype(BF16)
        eye = jnp.eye(PEER_HEADS, dtype=F32)

        def keys_block_diag(k):
            return (k.transpose(1, 0, 2)[:, :, None, :] * eye[None, :, :, None]).reshape(
                PEER_NKEYS * PEER_HEADS, PEER_HEADS * hq).astype(BF16)

        e1, e2, g = _peer_select(xn, wq, keys_block_diag(peer_k1[l]), keys_block_diag(peer_k2[l]))
        x = _peer_dense(xn, e1, e2, g, peer_u[l].astype(BF16), peer_v[l].astype(BF16), x1)

    y = _final_norm(x, norm_final[None, :])
    y_prompt = _batch_major(y[:tp], bp, sp)
    y_sample = _batch_major(y[tp:], bs, ss)
    return (y_prompt, y_sample,
            jnp.stack(ret_p), jnp.stack(ret_s),
            jnp.stack(re_p), jnp.stack(re_s),
            jnp.stack(im_p), jnp.stack(im_s),
            jnp.stack(cv_p), jnp.stack(cv_s))
```

```python
import functools
import math

import jax
import jax.numpy as jnp
from jax import lax
from jax.experimental import pallas as pl
from jax.experimental.pallas import tpu as pltpu

F32 = jnp.float32
BF16 = jnp.bfloat16

D_MODEL = 1024
DEPTH = 2
PAST_LEN = 16384
RET_HEADS = 8
RET_DK = 64
RET_W = 512
RET_CHUNK = 128
ROPE_BASE = 10000.0
SSM_W = 512
SSM_GC = 16
SSM_G = 32
SSM_P = 64
SSM_N = SSM_G * SSM_P
SSM_SLABS = 4
SSM_SLAB_N = SSM_N // SSM_SLABS
CONV_W = 512
CONV_K = 3
PROJ_W = 7168
PEER_HEADS = 8
PEER_DQ = 256
PEER_NKEYS = 128
PEER_TOPK = 16
PEER_NEXP = PEER_NKEYS ** 2
PEER_SLOTS = PEER_HEADS * PEER_TOPK
EPS = 1e-6
GELU_C = math.sqrt(2.0 / math.pi)
GELU_A = 0.044715

ROWS = 1024
LANES = 128
SUBLANES = 8
MXU_DEPTH = 256
MIB = 1024 * 1024

_CANDS = [(a, b) for a in range(PEER_TOPK) for b in range(PEER_TOPK) if (a + 1) * (b + 1) <= PEER_TOPK]


def _params(sem, vmem_mib):
    return pltpu.CompilerParams(dimension_semantics=sem, vmem_limit_bytes=vmem_mib * MIB)


def _rms(x, g):
    return x * lax.rsqrt(jnp.mean(x * x, axis=-1, keepdims=True) + EPS) * g


def _dot(a, b):
    return jnp.dot(a, b, preferred_element_type=F32)


def _dot_nt(a, b):
    return lax.dot_general(a, b, (((1,), (1,)), ((), ())), preferred_element_type=F32)


def _inproj_kernel(x_ref, g_ref, w_ref, z_ref, h_scr):
    @pl.when(pl.program_id(1) == 0)
    def _():
        h_scr[...] = _rms(x_ref[...], g_ref[...]).astype(BF16)

    z_ref[...] = _dot(h_scr[...], w_ref[...])


def _inproj(x, g, w):
    t = x.shape[0]
    nb = 1024
    return pl.pallas_call(
        _inproj_kernel,
        grid=(t // ROWS, PROJ_W // nb),
        in_specs=[pl.BlockSpec((ROWS, D_MODEL), lambda i, j: (i, 0)),
                  pl.BlockSpec((1, D_MODEL), lambda i, j: (0, 0)),
                  pl.BlockSpec((D_MODEL, nb), lambda i, j: (0, j))],
        out_specs=pl.BlockSpec((ROWS, nb), lambda i, j: (i, j)),
        out_shape=jax.ShapeDtypeStruct((t, PROJ_W), F32),
        scratch_shapes=[pltpu.VMEM((ROWS, D_MODEL), BF16)],
        compiler_params=_params(("parallel", "arbitrary"), 40),
        name="inproj",
    )(x, g, w)


def _ret_kernel(batch, bblk, z_ref, cos_ref, sa_ref, sb_ref, qdec_ref, kdec_ref, cdec_ref, gn_ref, s0_ref,
                o_ref, s_ref, qd_scr, kd_scr, v_scr, mask_scr, oacc_scr):
    bb = pl.program_id(0)
    c = pl.program_id(1)
    steps = ROWS // batch
    nslab = RET_W // LANES

    def head_view(ref, h):
        return ref[h // 2, :, (h % 2) * RET_DK:(h % 2 + 1) * RET_DK]

    @pl.when((bb == 0) & (c == 0))
    def _():
        r = lax.broadcasted_iota(jnp.int32, (ROWS, ROWS), 0)
        cc = lax.broadcasted_iota(jnp.int32, (ROWS, ROWS), 1)
        same = (r & (batch - 1)) == (cc & (batch - 1))
        mask_scr[...] = (same & (r >= cc)).astype(F32)

    @pl.when(c == 0)
    def _():
        s_ref[...] = s0_ref[...]

    @pl.when(bb == 0)
    def _():
        cos, sa, sb = cos_ref[...], sa_ref[...], sb_ref[...]

        def rot(x):
            return x * cos + pltpu.roll(x, 32, 1) * sa + pltpu.roll(x, 96, 1) * sb

        for s in range(nslab):
            cols = slice(s * LANES, (s + 1) * LANES)
            qd_scr[s] = rot(z_ref[:, cols]) * qdec_ref[:, cols]
            kd_scr[s] = rot(z_ref[:, RET_W + s * LANES:RET_W + (s + 1) * LANES]) * kdec_ref[:, cols]
            v_scr[s] = z_ref[:, 2 * RET_W + s * LANES:2 * RET_W + (s + 1) * LANES]
        for s in range(nslab):
            outs = []
            for h in (2 * s, 2 * s + 1):
                qh = head_view(qd_scr, h).astype(BF16)
                kh = head_view(kd_scr, h).astype(BF16)
                vh = head_view(v_scr, h).astype(BF16)
                p = (_dot_nt(qh, kh) * mask_scr[...]).astype(BF16)
                outs.append(_dot(p, vh))
            oacc_scr[s] = jnp.concatenate(outs, axis=1)

    def per_seq(bl, carry):
        b = bb * bblk + bl
        rows = pl.ds(b, steps, stride=batch)
        for s in range(nslab):
            qb = qd_scr[s, rows, :]
            kb = kd_scr[s, rows, :]
            vb = v_scr[s, rows, :]
            cross = []
            for hh in range(2):
                h = 2 * s + hh
                hc = slice(hh * RET_DK, (hh + 1) * RET_DK)
                st = s_ref[bl, h]
                cross.append(_dot(qb[:, hc].astype(BF16), st.astype(BF16)))
                upd = lax.dot_general(kb[:, hc].astype(BF16), vb[:, hc].astype(BF16),
                                      (((0,), (0,)), ((), ())), preferred_element_type=F32)
                s_ref[bl, h] = (st + upd) * cdec_ref[:, h * RET_DK:(h + 1) * RET_DK]
            oacc_scr[s, rows, :] = oacc_scr[s, rows, :] + jnp.concatenate(cross, axis=1)
        return carry

    lax.fori_loop(0, bblk, per_seq, 0)

    @pl.when(bb == pl.num_programs(0) - 1)
    def _():
        normed = []
        for h in range(RET_HEADS):
            oh = head_view(oacc_scr, h)
            mu = jnp.mean(oh, axis=-1, keepdims=True)
            dlt = oh - mu
            var = jnp.mean(dlt * dlt, axis=-1, keepdims=True)
            normed.append(dlt * lax.rsqrt(var + EPS))
        o = jnp.concatenate(normed, axis=1) * gn_ref[...]
        o_ref[...] = jax.nn.silu(z_ref[:, 3 * RET_W:4 * RET_W]) * o


def _retention(z, row_blk0, nblk, batch, bblk, tabs, gn, s0):
    cos, sa, sb, qdec, kdec, cdec = tabs
    nbb = batch // bblk
    st_spec = pl.BlockSpec((bblk, RET_HEADS, RET_DK, RET_DK), lambda bb, c: (bb, 0, 0, 0))
    const = lambda bb, c: (0, 0)
    return pl.pallas_call(
        functools.partial(_ret_kernel, batch, bblk),
        grid=(nbb, nblk),
        in_specs=[pl.BlockSpec((ROWS, 4 * RET_W), lambda bb, c: (row_blk0 + c, 0)),
                  pl.BlockSpec((ROWS, LANES), lambda bb, c: (c, 0)),
                  pl.BlockSpec((ROWS, LANES), lambda bb, c: (c, 0)),
                  pl.BlockSpec((ROWS, LANES), lambda bb, c: (c, 0)),
                  pl.BlockSpec((ROWS, RET_W), const),
                  pl.BlockSpec((ROWS, RET_W), const),
                  pl.BlockSpec((1, RET_W), const),
                  pl.BlockSpec((1, RET_W), const),
                  st_spec],
        out_specs=[pl.BlockSpec((ROWS, RET_W), lambda bb, c: (c, 0)), st_spec],
        out_shape=[jax.ShapeDtypeStruct((nblk * ROWS, RET_W), F32),
                   jax.ShapeDtypeStruct((batch, RET_HEADS, RET_DK, RET_DK), F32)],
        scratch_shapes=[pltpu.VMEM((RET_W // LANES, ROWS, LANES), F32)] * 3
        + [pltpu.VMEM((ROWS, ROWS), F32), pltpu.VMEM((RET_W // LANES, ROWS, LANES), F32)],
        compiler_params=_params(("arbitrary", "arbitrary"), 56),
        name="retention",
    )(z, cos, sa, sb, qdec, kdec, cdec, gn, s0)


def _retention_tables(pos, batch):
    half = RET_DK // 2
    freqs = ROPE_BASE ** (-jnp.arange(half, dtype=F32) / half)
    ang = pos[:, None] * freqs[None, :]
    cos, sin = jnp.cos(ang), jnp.sin(ang)
    zero = jnp.zeros_like(sin)
    reps = LANES // RET_DK
    cos_t = jnp.tile(jnp.concatenate([cos, cos], axis=1), (1, reps))
    sa_t = jnp.tile(jnp.concatenate([zero, sin], axis=1), (1, reps))
    sb_t = jnp.tile(jnp.concatenate([-sin, zero], axis=1), (1, reps))
    lg = jnp.log1p(-jnp.exp2(-5.0 - jnp.arange(RET_HEADS, dtype=F32)))
    steps = ROWS // batch
    i1 = (jnp.arange(ROWS) // batch).astype(F32) + 1.0
    qdec = jnp.repeat(jnp.exp(i1[:, None] * lg[None, :]), RET_DK, axis=1)
    kdec = jnp.repeat(jnp.exp(-i1[:, None] * lg[None, :]), RET_DK, axis=1) * (RET_DK ** -0.5)
    cdec = jnp.repeat(jnp.exp(steps * lg), RET_DK)[None, :]
    return cos_t, sa_t, sb_t, qdec, kdec, cdec


def _s5_disc_kernel(are_ref, aim_ref, ldt_ref, bre_ref, bim_ref, abre_ref, abim_ref, bbre_ref, bbim_ref):
    ar, ai = are_ref[...], aim_ref[...]
    dt = jnp.exp(ldt_ref[...])
    dar, dai = dt * ar, dt * ai
    mag = jnp.exp(dar)
    abar_re, abar_im = mag * jnp.cos(dai), mag * jnp.sin(dai)
    den = ar * ar + ai * ai
    nr, ni = abar_re - 1.0, abar_im
    f_re = (nr * ar + ni * ai) / den
    f_im = (ni * ar - nr * ai) / den
    abre_ref[...] = abar_re
    abim_ref[...] = abar_im
    br, bi = bre_ref[...], bim_ref[...]
    bbre_ref[...] = f_re[:, None, :] * br - f_im[:, None, :] * bi
    bbim_ref[...] = f_re[:, None, :] * bi + f_im[:, None, :] * br


def _s5_discretise(a_re, a_im, log_dt, b_re_t, b_im_t):
    lg = a_re.shape[0]
    small = jax.ShapeDtypeStruct((lg, SSM_P), F32)
    big = jax.ShapeDtypeStruct((lg, SSM_GC, SSM_P), F32)
    return pl.pallas_call(_s5_disc_kernel, out_shape=[small, small, big, big], name="s5_disc")(
        a_re, a_im, log_dt, b_re_t, b_im_t)


def _s5_kernel(batch, u_ref, bmat_ref, cmat_ref, are_ref, aim_ref, d_ref, h0re_ref, h0im_ref,
               y_ref, xre_ref, xim_ref, x_scr):
    c = pl.program_id(0)
    steps = ROWS // batch
    half = SSM_SLAB_N

    @pl.when(c == 0)
    def _():
        xre_ref[...] = h0re_ref[...]
        xim_ref[...] = h0im_ref[...]

    u = u_ref[...]
    ub = u.astype(BF16)
    for s in range(SSM_SLABS):
        x_scr[:, 2 * half * s:2 * half * (s + 1)] = _dot(ub[:, s * LANES:(s + 1) * LANES], bmat_ref[s])

    for s in range(SSM_SLABS):
        re0 = 2 * half * s
        im0 = re0 + half
        sc = slice(half * s, half * (s + 1))
        ar = jnp.broadcast_to(are_ref[:, sc], (SUBLANES, half))
        ai = jnp.broadcast_to(aim_ref[:, sc], (SUBLANES, half))

        def row_tile(rt, carry, re0=re0, im0=im0, sc=sc, ar=ar, ai=ai):
            r0 = pl.multiple_of(rt * SUBLANES, SUBLANES)

            def step(t, x):
                xr, xi = x
                row = pl.multiple_of(t * batch + r0, SUBLANES)
                nr = ar * xr - ai * xi + x_scr[pl.ds(row, SUBLANES), re0:re0 + half]
                ni = ar * xi + ai * xr + x_scr[pl.ds(row, SUBLANES), im0:im0 + half]
                x_scr[pl.ds(row, SUBLANES), re0:re0 + half] = nr
                x_scr[pl.ds(row, SUBLANES), im0:im0 + half] = ni
                return nr, ni

            init = (xre_ref[pl.ds(r0, SUBLANES), sc], xim_ref[pl.ds(r0, SUBLANES), sc])
            xr, xi = lax.fori_loop(0, steps, step, init, unroll=8)
            xre_ref[pl.ds(r0, SUBLANES), sc] = xr
            xim_ref[pl.ds(r0, SUBLANES), sc] = xi
            return carry

        lax.fori_loop(0, batch // SUBLANES, row_tile, 0)

    ys = [_dot(x_scr[:, 2 * half * s:2 * half * (s + 1)].astype(BF16), cmat_ref[s]) for s in range(SSM_SLABS)]
    y = jnp.concatenate(ys, axis=1) + d_ref[...] * u
    y_ref[...] = jax.nn.gelu(y)


def _s5(z, row_blk0, nblk, batch, bmat, cmat, abre, abim, d, h0re, h0im):
    const2 = lambda c: (0, 0)
    const3 = lambda c: (0, 0, 0)
    st = pl.BlockSpec((batch, SSM_N), const2)
    return pl.pallas_call(
        functools.partial(_s5_kernel, batch),
        grid=(nblk,),
        in_specs=[pl.BlockSpec((ROWS, SSM_W), lambda c: (row_blk0 + c, 4)),
                  pl.BlockSpec((SSM_SLABS, LANES, 2 * SSM_SLAB_N), const3),
                  pl.BlockSpec((SSM_SLABS, 2 * SSM_SLAB_N, LANES), const3),
                  pl.BlockSpec((1, SSM_N), const2),
                  pl.BlockSpec((1, SSM_N), const2),
                  pl.BlockSpec((1, SSM_W), const2),
                  st, st],
        out_specs=[pl.BlockSpec((ROWS, SSM_W), lambda c: (c, 0)), st, st],
        out_shape=[jax.ShapeDtypeStruct((nblk * ROWS, SSM_W), F32),
                   jax.ShapeDtypeStruct((batch, SSM_N), F32),
                   jax.ShapeDtypeStruct((batch, SSM_N), F32)],
        scratch_shapes=[pltpu.VMEM((ROWS, 2 * SSM_N), F32)],
        compiler_params=_params(("arbitrary",), 48),
        name="s5",
    )(z, bmat, cmat, abre, abim, d, h0re, h0im)


def _block_diag_slabs(w, rows_inner):
    gps = SSM_G // SSM_SLABS
    eye = jnp.eye(gps, dtype=w.dtype)
    w4 = w.reshape(SSM_SLABS, gps, w.shape[1], w.shape[2])
    out = w4[:, :, :, None, :] * eye[None, :, None, :, None]
    return out.reshape(SSM_SLABS, gps * w.shape[1], gps * w.shape[2])


def _conv_kernel(batch, bg_ref, cg_ref, hc_ref, buf0_ref, w_ref, b_ref, o_ref, buf_ref, zp_scr):
    c = pl.program_id(0)
    pad = (CONV_K - 1) * batch

    @pl.when(c == 0)
    def _():
        zp_scr[0:pad, :] = buf0_ref[...]

    zc = cg_ref[...] * hc_ref[...]
    zp_scr[pad:pad + ROWS, :] = zc
    y = b_ref[...]
    for j in range(CONV_K):
        y = y + w_ref[j:j + 1, :] * zp_scr[j * batch:j * batch + ROWS, :]
    o_ref[...] = bg_ref[...] * y
    tail = zp_scr[ROWS:ROWS + pad, :]
    buf_ref[...] = tail
    zp_scr[0:pad, :] = tail


def _conv(z, row_blk0, nblk, batch, buf0, w, b):
    pad = (CONV_K - 1) * batch
    const = lambda c: (0, 0)
    return pl.pallas_call(
        functools.partial(_conv_kernel, batch),
        grid=(nblk,),
        in_specs=[pl.BlockSpec((ROWS, CONV_W), lambda c: (row_blk0 + c, 5)),
                  pl.BlockSpec((ROWS, CONV_W), lambda c: (row_blk0 + c, 6)),
                  pl.BlockSpec((ROWS, CONV_W), lambda c: (row_blk0 + c, 7)),
                  pl.BlockSpec((pad, CONV_W), const),
                  pl.BlockSpec((CONV_K, CONV_W), const),
                  pl.BlockSpec((1, CONV_W), const)],
        out_specs=[pl.BlockSpec((ROWS, CONV_W), lambda c: (c, 0)), pl.BlockSpec((pad, CONV_W), const)],
        out_shape=[jax.ShapeDtypeStruct((nblk * ROWS, CONV_W), F32),
                   jax.ShapeDtypeStruct((pad, CONV_W), F32)],
        scratch_shapes=[pltpu.VMEM((ROWS + pad, CONV_W), F32)],
        compiler_params=_params(("arbitrary",), 32),
        name="conv",
    )(z, z, z, buf0, w, b)


def _merge_kernel(x_ref, oa_ref, ys_ref, oc_ref, ga_ref, gb_ref, gc_ref, wr_ref, wa_ref, wb_ref, wc_ref,
                  wm_ref, gf_ref, x1_ref, xn_ref):
    oa = _dot(oa_ref[...].astype(BF16), wr_ref[...])
    ysb = ys_ref[...].astype(BF16)
    ob = _dot(ysb, wa_ref[...]) * jax.nn.sigmoid(_dot(ysb, wb_ref[...]))
    oc = _dot(oc_ref[...].astype(BF16), wc_ref[...])
    merged = (jax.nn.sigmoid(ga_ref[...]) * oa + jax.nn.sigmoid(gb_ref[...]) * ob
              + jax.nn.sigmoid(gc_ref[...]) * oc)
    x1 = x_ref[...] + _dot(merged.astype(BF16), wm_ref[...])
    x1_ref[...] = x1
    xn_ref[...] = _rms(x1, gf_ref[...]).astype(BF16)


def _merge(x, z, oa, ys, oc, wr, wa, wb, wc, wm, gf):
    t = x.shape[0]
    rb = 512
    row = lambda w: pl.BlockSpec((rb, w), lambda i: (i, 0))
    gate = lambda j: pl.BlockSpec((rb, D_MODEL), lambda i: (i, j))
    wsp = lambda k: pl.BlockSpec((k, D_MODEL), lambda i: (0, 0))
    return pl.pallas_call(
        _merge_kernel,
        grid=(t // rb,),
        in_specs=[row(D_MODEL), row(RET_W), row(SSM_W), row(CONV_W), gate(4), gate(5), gate(6),
                  wsp(RET_W), wsp(SSM_W), wsp(SSM_W), wsp(CONV_W), wsp(D_MODEL), wsp(1)],
        out_specs=[row(D_MODEL), row(D_MODEL)],
        out_shape=[jax.ShapeDtypeStruct((t, D_MODEL), F32), jax.ShapeDtypeStruct((t, D_MODEL), BF16)],
        compiler_params=_params(("parallel",), 48),
        name="merge",
    )(x, oa, ys, oc, z, z, z, wr, wa, wb, wc, wm, gf)


def _tree(items, combine):
    while len(items) > 1:
        nxt = [combine(items[i], items[i + 1]) for i in range(0, len(items) - 1, 2)]
        if len(items) % 2:
            nxt.append(items[-1])
        items = nxt
    return items[0]


def _first_max(x, y):
    (vx, ix), (vy, iy) = x, y
    return jnp.maximum(vx, vy), jnp.where(vx >= vy, ix, iy)


def _top16_of_keys(s_scrs, v_scrs, i_scrs):
    def body(r, carry):
        for s_scr, v_scr, i_scr in zip(s_scrs, v_scrs, i_scrs):
            m, idx = _tree([(s_scr[k], float(k)) for k in range(PEER_NKEYS)], _first_max)
            v_scr[r] = m
            i_scr[r] = idx
            for k in range(PEER_NKEYS):
                s_scr[k] = jnp.where(idx == float(k), -jnp.inf, s_scr[k])
        return carry

    lax.fori_loop(0, PEER_TOPK, body, 0)


def _select_kernel(tb, xn_ref, wq_ref, k1_ref, k2_ref, e1_ref, e2_ref, g_ref,
                   s1_scr, s2_scr, v1_scr, i1_scr, v2_scr, i2_scr, cand_scr, sc_scr, se1_scr, se2_scr):
    q = _dot(xn_ref[...], wq_ref[...]).astype(BF16)
    hq = PEER_HEADS * PEER_DQ // 2
    s1 = _dot_nt(k1_ref[...], q[:, :hq])
    s2 = _dot_nt(k2_ref[...], q[:, hq:])
    flats = [float(a * PEER_TOPK + b) for a, b in _CANDS]
    for lt in range(tb // LANES):
        lanes = slice(lt * LANES, (lt + 1) * LANES)
        s1_scr[...] = s1[:, lanes].reshape(PEER_NKEYS, SUBLANES, LANES)
        s2_scr[...] = s2[:, lanes].reshape(PEER_NKEYS, SUBLANES, LANES)
        _top16_of_keys((s1_scr, s2_scr), (v1_scr, v2_scr), (i1_scr, i2_scr))
        for n, (a, b) in enumerate(_CANDS):
            cand_scr[n] = v1_scr[a] + v2_scr[b]

        def body(r, carry):
            m, flat = _tree([(cand_scr[n], f) for n, f in enumerate(flats)], _first_max)
            for n, f in enumerate(flats):
                cand_scr[n] = jnp.where(flat == f, -jnp.inf, cand_scr[n])
            fa = jnp.floor(flat * (1.0 / PEER_TOPK))
            fb = flat - fa * PEER_TOPK
            sc_scr[r] = m
            se1_scr[r] = _tree([jnp.where(fa == float(a), i1_scr[a], 0.0) for a in range(PEER_TOPK)], jnp.add)
            se2_scr[r] = _tree([jnp.where(fb == float(b), i2_scr[b], 0.0) for b in range(PEER_TOPK)], jnp.add)
            return carry

        lax.fori_loop(0, PEER_TOPK, body, 0)
        sc = sc_scr[...]
        ex = jnp.exp(sc - jnp.max(sc, axis=0, keepdims=True))
        gate = ex / jnp.sum(ex, axis=0, keepdims=True)
        rows = slice(lt * LANES, (lt + 1) * LANES)
        g_ref[rows, :] = gate.reshape(PEER_SLOTS, LANES).T
        e1_ref[rows, :] = se1_scr[...].reshape(PEER_SLOTS, LANES).T
        e2_ref[rows, :] = se2_scr[...].reshape(PEER_SLOTS, LANES).T


def _peer_select(xn, wq, k1big, k2big):
    t = xn.shape[0]
    tb = 256
    hq = PEER_HEADS * PEER_DQ // 2
    nk = PEER_NKEYS * PEER_HEADS
    const = lambda i: (0, 0)
    row = lambda dt: jax.ShapeDtypeStruct((t, PEER_SLOTS), dt)
    vec = lambda n: pltpu.VMEM((n, SUBLANES, LANES), F32)
    return pl.pallas_call(
        functools.partial(_select_kernel, tb),
        grid=(t // tb,),
        in_specs=[pl.BlockSpec((tb, D_MODEL), lambda i: (i, 0)),
                  pl.BlockSpec((D_MODEL, 2 * hq), const),
                  pl.BlockSpec((nk, hq), const),
                  pl.BlockSpec((nk, hq), const)],
        out_specs=[pl.BlockSpec((tb, PEER_SLOTS), lambda i: (i, 0))] * 3,
        out_shape=[row(F32), row(F32), row(F32)],
        scratch_shapes=[vec(PEER_NKEYS), vec(PEER_NKEYS), vec(PEER_TOPK), vec(PEER_TOPK), vec(PEER_TOPK), vec(PEER_TOPK),
                        vec(len(_CANDS)), vec(PEER_TOPK), vec(PEER_TOPK), vec(PEER_TOPK)],
        compiler_params=_params(("parallel",), 40),
        name="peer_select",
    )(xn, wq, k1big, k2big)


def _peer_kernel(tb, eb, stride, xn_ref, e1_ref, e2_ref, g_ref, u_ref, v_ref, x1_ref, out_ref, m_scr):
    e = pl.program_id(1)
    nk1 = eb // PEER_NKEYS

    @pl.when(e == 0)
    def _():
        out_ref[...] = x1_ref[...]

    @pl.when(e == 0)
    def _():
        key = lax.broadcasted_iota(jnp.int32, (PEER_NKEYS, PEER_SLOTS), 0).astype(F32)

        def token(t, carry):
            e1 = e1_ref[pl.ds(t, 1), :]
            e2 = e2_ref[pl.ds(t, 1), :]
            gt = 0.5 * g_ref[pl.ds(t, 1), :]
            a_t = jnp.where(key == e1, gt, 0.0).astype(BF16)
            b_t = jnp.where(key == e2, 1.0, 0.0).astype(BF16)
            m_scr[pl.ds(t, PEER_NKEYS, stride=stride), :] = _dot_nt(a_t, b_t)
            return carry

        lax.fori_loop(0, tb, token, 0, unroll=64)

    s = _dot_nt(xn_ref[...], u_ref[...])
    t = jnp.tanh(s * (GELU_C + (GELU_C * GELU_A) * (s * s)))
    k1 = e * nk1
    gates = [m_scr[pl.ds(pl.multiple_of((k1 + i) * stride, SUBLANES), tb), :] for i in range(nk1)]
    w = ((s + s * t) * jnp.concatenate(gates, axis=1)).astype(BF16)
    out_ref[...] += _dot(w, v_ref[...])


def _peer_dense(xn, e1, e2, g, u, v, x1):
    t = xn.shape[0]
    tb, eb = 512, 1024
    stride = tb + SUBLANES
    once = pl.Buffered(1)
    tok = lambda w: pl.BlockSpec((tb, w), lambda i, e: (i, 0), pipeline_mode=once)
    tab = pl.BlockSpec((eb, D_MODEL), lambda i, e: (e, 0))
    return pl.pallas_call(
        functools.partial(_peer_kernel, tb, eb, stride),
        grid=(t // tb, PEER_NEXP // eb),
        in_specs=[tok(D_MODEL), tok(PEER_SLOTS), tok(PEER_SLOTS), tok(PEER_SLOTS), tab, tab, tok(D_MODEL)],
        out_specs=pl.BlockSpec((tb, D_MODEL), lambda i, e: (i, 0)),
        out_shape=jax.ShapeDtypeStruct((t, D_MODEL), F32),
        scratch_shapes=[pltpu.VMEM((PEER_NKEYS * stride, PEER_NKEYS), F32)],
        compiler_params=_params(("parallel", "arbitrary"), 60),
        name="peer_dense",
    )(xn, e1, e2, g, u, v, x1)


def _norm_kernel(x_ref, g_ref, y_ref):
    y_ref[...] = _rms(x_ref[...], g_ref[...])


def _final_norm(x, g):
    t = x.shape[0]
    return pl.pallas_call(
        _norm_kernel,
        grid=(t // ROWS,),
        in_specs=[pl.BlockSpec((ROWS, D_MODEL), lambda i: (i, 0)), pl.BlockSpec((1, D_MODEL), lambda i: (0, 0))],
        out_specs=pl.BlockSpec((ROWS, D_MODEL), lambda i: (i, 0)),
        out_shape=jax.ShapeDtypeStruct((t, D_MODEL), F32),
        compiler_params=_params(("parallel",), 32),
        name="final_norm",
    )(x, g)


def _time_major(x):
    b, s, d = x.shape
    return x.transpose(1, 0, 2).reshape(s * b, d)


def _batch_major(y, b, s):
    return y.reshape(s, b, y.shape[-1]).transpose(1, 0, 2)


def kernel(x_prompt, x_sample, state_ret, state_ssm_re, state_ssm_im, state_conv, norm_mix, w_in, ret_norm, w_ret_out, ssm_a_re, ssm_a_im, ssm_b_re, ssm_b_im, ssm_c_re, ssm_c_im, ssm_d, ssm_log_dt, w_glu_a, w_glu_b, conv_w, conv_b, w_conv_out, w_mix_out, norm_ffn, peer_wq, peer_k1, peer_k2, peer_u, peer_v, norm_final):
    bp, sp, _ = x_prompt.shape
    bs, ss, _ = x_sample.shape
    tp, ts = bp * sp, bs * ss
    depth = w_in.shape[0]
    groups = ((0, tp // ROWS, bp), (tp // ROWS, ts // ROWS, bs))
    assert tp % ROWS == 0 and ts == ROWS and ROWS % bp == 0 and ROWS // bp == math.gcd(sp, RET_CHUNK)

    x = jnp.concatenate([_time_major(x_prompt), _time_major(x_sample)], axis=0)

    pos_p = jnp.repeat(jnp.arange(sp, dtype=F32), bp)
    pos_s = jnp.repeat(PAST_LEN + jnp.arange(ss, dtype=F32), bs)
    ret_tabs = (_retention_tables(pos_p, bp), _retention_tables(pos_s, bs))

    lg = depth * SSM_G
    abre, abim, bbre, bbim = _s5_discretise(
        ssm_a_re.reshape(lg, SSM_P), ssm_a_im.reshape(lg, SSM_P), ssm_log_dt.reshape(lg, 1),
        ssm_b_re.transpose(0, 1, 3, 2).reshape(lg, SSM_GC, SSM_P),
        ssm_b_im.transpose(0, 1, 3, 2).reshape(lg, SSM_GC, SSM_P))

    ret_p, ret_s, re_p, re_s, im_p, im_s, cv_p, cv_s = [], [], [], [], [], [], [], []
    for l in range(depth):
        sl = slice(l * SSM_G, (l + 1) * SSM_G)
        bmat = jnp.concatenate([_block_diag_slabs(bbre[sl], None), _block_diag_slabs(bbim[sl], None)],
                               axis=2).astype(BF16)
        cmat = jnp.concatenate([_block_diag_slabs(ssm_c_re[l].transpose(0, 2, 1), None),
                                _block_diag_slabs(-ssm_c_im[l].transpose(0, 2, 1), None)],
                               axis=1).astype(BF16)
        are_row = abre[sl].reshape(1, SSM_N)
        aim_row = abim[sl].reshape(1, SSM_N)

        z = _inproj(x, norm_mix[l][None, :], w_in[l].astype(BF16))

        oa_l, ys_l, oc_l = [], [], []
        for gi, (blk0, nblk, batch) in enumerate(groups):
            if gi == 0:
                s0 = jnp.zeros((batch, RET_HEADS, RET_DK, RET_DK), F32)
                h0re = jnp.zeros((batch, SSM_N), F32)
                h0im = jnp.zeros((batch, SSM_N), F32)
                buf0 = jnp.zeros(((CONV_K - 1) * batch, CONV_W), F32)
                bblk = batch
            else:
                s0 = state_ret[l]
                h0re = state_ssm_re[l].reshape(batch, SSM_N)
                h0im = state_ssm_im[l].reshape(batch, SSM_N)
                buf0 = state_conv[l].transpose(1, 0, 2).reshape((CONV_K - 1) * batch, CONV_W)
                bblk = 16
            oa, s_new = _retention(z, blk0, nblk, batch, bblk, ret_tabs[gi], ret_norm[l][None, :], s0)
            ys, xre, xim = _s5(z, blk0, nblk, batch, bmat, cmat, are_row, aim_row, ssm_d[l][None, :], h0re, h0im)
            oc, buf = _conv(z, blk0, nblk, batch, buf0, conv_w[l], conv_b[l][None, :])
            oa_l.append(oa); ys_l.append(ys); oc_l.append(oc)
            cv = buf.reshape(CONV_K - 1, batch, CONV_W).transpose(1, 0, 2)
            sre = xre.reshape(batch, SSM_G, SSM_P)
            sim = xim.reshape(batch, SSM_G, SSM_P)
            if gi == 0:
                ret_p.append(s_new); re_p.append(sre); im_p.append(sim); cv_p.append(cv)
            else:
                ret_s.append(s_new); re_s.append(sre); im_s.append(sim); cv_s.append(cv)

        x1, xn = _merge(x, z, jnp.concatenate(oa_l), jnp.concatenate(ys_l), jnp.concatenate(oc_l),
                        w_ret_out[l].astype(BF16), w_glu_a[l].astype(BF16), w_glu_b[l].astype(BF16),
                        w_conv_out[l].astype(BF16), w_mix_out[l].astype(BF16), norm_ffn[l][None, :])

        hq = PEER_DQ // 2
        wq = peer_wq[l].reshape(D_MODEL, PEER_HEADS, 2, hq).transpose(0, 2, 1, 3).reshape(D_MODEL, -1).astype(BF16)
        eye = jnp.eye(PEER_HEADS, dtype=F32)

        def keys_block_diag(k):
            return (k.transpose(1, 0, 2)[:, :, None, :] * eye[None, :, :, None]).reshape(
                PEER_NKEYS * PEER_HEADS, PEER_HEADS * hq).astype(BF16)

        e1, e2, g = _peer_select(xn, wq, keys_block_diag(peer_k1[l]), keys_block_diag(peer_k2[l]))
        x = _peer_dense(xn, e1, e2, g, peer_u[l].astype(BF16), peer_v[l].astype(BF16), x1)

    y = _final_norm(x, norm_final[None, :])
    y_prompt = _batch_major(y[:tp], bp, sp)
    y_sample = _batch_major(y[tp:], bs, ss)
    return (y_prompt, y_sample,
            jnp.stack(ret_p), jnp.stack(ret_s),
            jnp.stack(re_p), jnp.stack(re_s),
            jnp.stack(im_p), jnp.stack(im_s),
            jnp.stack(cv_p), jnp.stack(cv_s))
```

```python
import functools
import math

import jax
import jax.numpy as jnp
from jax import lax
from jax.experimental import pallas as pl
from jax.experimental.pallas import tpu as pltpu

F32 = jnp.float32
BF16 = jnp.bfloat16

D_MODEL = 1024
DEPTH = 2
PAST_LEN = 16384
RET_HEADS = 8
RET_DK = 64
RET_W = 512
RET_CHUNK = 128
ROPE_BASE = 10000.0
SSM_W = 512
SSM_GC = 16
SSM_G = 32
SSM_P = 64
SSM_N = SSM_G * SSM_P
SSM_SLABS = 4
SSM_SLAB_N = SSM_N // SSM_SLABS
CONV_W = 512
CONV_K = 3
PROJ_W = 7168
PEER_HEADS = 8
PEER_DQ = 256
PEER_NKEYS = 128
PEER_TOPK = 16
PEER_NEXP = PEER_NKEYS ** 2
PEER_SLOTS = PEER_HEADS * PEER_TOPK
EPS = 1e-6
GELU_C = math.sqrt(2.0 / math.pi)
GELU_A = 0.044715

ROWS = 1024
LANES = 128
SUBLANES = 8
MXU_DEPTH = 256
MIB = 1024 * 1024

_CANDS = [(a, b) for a in range(PEER_TOPK) for b in range(PEER_TOPK) if (a + 1) * (b + 1) <= PEER_TOPK]


def _params(sem, vmem_mib):
    return pltpu.CompilerParams(dimension_semantics=sem, vmem_limit_bytes=vmem_mib * MIB)


def _rms(x, g):
    return x * lax.rsqrt(jnp.mean(x * x, axis=-1, keepdims=True) + EPS) * g


def _dot(a, b):
    return jnp.dot(a, b, preferred_element_type=F32)


def _dot_nt(a, b):
    return lax.dot_general(a, b, (((1,), (1,)), ((), ())), preferred_element_type=F32)


def _inproj_kernel(x_ref, g_ref, w_ref, z_ref, h_scr):
    @pl.when(pl.program_id(1) == 0)
    def _():
        h_scr[...] = _rms(x_ref[...], g_ref[...]).astype(BF16)

    z_ref[...] = _dot(h_scr[...], w_ref[...])


def _inproj(x, g, w):
    t = x.shape[0]
    nb = 1024
    return pl.pallas_call(
        _inproj_kernel,
        grid=(t // ROWS, PROJ_W // nb),
        in_specs=[pl.BlockSpec((ROWS, D_MODEL), lambda i, j: (i, 0)),
                  pl.BlockSpec((1, D_MODEL), lambda i, j: (0, 0)),
                  pl.BlockSpec((D_MODEL, nb), lambda i, j: (0, j))],
        out_specs=pl.BlockSpec((ROWS, nb), lambda i, j: (i, j)),
        out_shape=jax.ShapeDtypeStruct((t, PROJ_W), F32),
        scratch_shapes=[pltpu.VMEM((ROWS, D_MODEL), BF16)],
        compiler_params=_params(("parallel", "arbitrary"), 40),
        name="inproj",
    )(x, g, w)


def _ret_kernel(batch, bblk, z_ref, cos_ref, sa_ref, sb_ref, qdec_ref, kdec_ref, cdec_ref, gn_ref, s0_ref,
                o_ref, s_ref, qd_scr, kd_scr, v_scr, mask_scr, oacc_scr):
    bb = pl.program_id(0)
    c = pl.program_id(1)
    steps = ROWS // batch
    nslab = RET_W // LANES

    def head_view(ref, h):
        return ref[h // 2, :, (h % 2) * RET_DK:(h % 2 + 1) * RET_DK]

    @pl.when((bb == 0) & (c == 0))
    def _():
        r = lax.broadcasted_iota(jnp.int32, (ROWS, ROWS), 0)
        cc = lax.broadcasted_iota(jnp.int32, (ROWS, ROWS), 1)
        same = (r & (batch - 1)) == (cc & (batch - 1))
        mask_scr[...] = (same & (r >= cc)).astype(F32)

    @pl.when(c == 0)
    def _():
        s_ref[...] = s0_ref[...]

    @pl.when(bb == 0)
    def _():
        cos, sa, sb = cos_ref[...], sa_ref[...], sb_ref[...]

        def rot(x):
            return x * cos + pltpu.roll(x, 32, 1) * sa + pltpu.roll(x, 96, 1) * sb

        for s in range(nslab):
            cols = slice(s * LANES, (s + 1) * LANES)
            qd_scr[s] = rot(z_ref[:, cols]) * qdec_ref[:, cols]
            kd_scr[s] = rot(z_ref[:, RET_W + s * LANES:RET_W + (s + 1) * LANES]) * kdec_ref[:, cols]
            v_scr[s] = z_ref[:, 2 * RET_W + s * LANES:2 * RET_W + (s + 1) * LANES]
        for s in range(nslab):
            outs = []
            for h in (2 * s, 2 * s + 1):
                qh = head_view(qd_scr, h).astype(BF16)
                kh = head_view(kd_scr, h).astype(BF16)
                vh = head_view(v_scr, h).astype(BF16)
                p = (_dot_nt(qh, kh) * mask_scr[...]).astype(BF16)
                outs.append(_dot(p, vh))
            oacc_scr[s] = jnp.concatenate(outs, axis=1)

    def per_seq(bl, carry):
        b = bb * bblk + bl
        rows = pl.ds(b, steps, stride=batch)
        for s in range(nslab):
            qb = qd_scr[s, rows, :]
            kb = kd_scr[s, rows, :]
            vb = v_scr[s, rows, :]
            cross = []
            for hh in range(2):
                h = 2 * s + hh
                hc = slice(hh * RET_DK, (hh + 1) * RET_DK)
                st = s_ref[bl, h]
                cross.append(_dot(qb[:, hc].astype(BF16), st.astype(BF16)))
                upd = lax.dot_general(kb[:, hc].astype(BF16), vb[:, hc].astype(BF16),
                                      (((0,), (0,)), ((), ())), preferred_element_type=F32)
                s_ref[bl, h] = (st + upd) * cdec_ref[:, h * RET_DK:(h + 1) * RET_DK]
            oacc_scr[s, rows, :] = oacc_scr[s, rows, :] + jnp.concatenate(cross, axis=1)
        return carry

    lax.fori_loop(0, bblk, per_seq, 0)

    @pl.when(bb == pl.num_programs(0) - 1)
    def _():
        normed = []
        for h in range(RET_HEADS):
            oh = head_view(oacc_scr, h)
            mu = jnp.mean(oh, axis=-1, keepdims=True)
            dlt = oh - mu
            var = jnp.mean(dlt * dlt, axis=-1, keepdims=True)
            normed.append(dlt * lax.rsqrt(var + EPS))
        o = jnp.concatenate(normed, axis=1) * gn_ref[...]
        o_ref[...] = jax.nn.silu(z_ref[:, 3 * RET_W:4 * RET_W]) * o


def _retention(z, row_blk0, nblk, batch, bblk, tabs, gn, s0):
    cos, sa, sb, qdec, kdec, cdec = tabs
    nbb = batch // bblk
    st_spec = pl.BlockSpec((bblk, RET_HEADS, RET_DK, RET_DK), lambda bb, c: (bb, 0, 0, 0))
    const = lambda bb, c: (0, 0)
    return pl.pallas_call(
        functools.partial(_ret_kernel, batch, bblk),
        grid=(nbb, nblk),
        in_specs=[pl.BlockSpec((ROWS, 4 * RET_W), lambda bb, c: (row_blk0 + c, 0)),
                  pl.BlockSpec((ROWS, LANES), lambda bb, c: (c, 0)),
                  pl.BlockSpec((ROWS, LANES), lambda bb, c: (c, 0)),
                  pl.BlockSpec((ROWS, LANES), lambda bb, c: (c, 0)),
                  pl.BlockSpec((ROWS, RET_W), const),
                  pl.BlockSpec((ROWS, RET_W), const),
                  pl.BlockSpec((1, RET_W), const),
                  pl.BlockSpec((1, RET_W), const),
                  st_spec],
        out_specs=[pl.BlockSpec((ROWS, RET_W), lambda bb, c: (c, 0)), st_spec],
        out_shape=[jax.ShapeDtypeStruct((nblk * ROWS, RET_W), F32),
                   jax.ShapeDtypeStruct((batch, RET_HEADS, RET_DK, RET_DK), F32)],
        scratch_shapes=[pltpu.VMEM((RET_W // LANES, ROWS, LANES), F32)] * 3
        + [pltpu.VMEM((ROWS, ROWS), F32), pltpu.VMEM((RET_W // LANES, ROWS, LANES), F32)],
        compiler_params=_params(("arbitrary", "arbitrary"), 56),
        name="retention",
    )(z, cos, sa, sb, qdec, kdec, cdec, gn, s0)


def _retention_tables(pos, batch):
    half = RET_DK // 2
    freqs = ROPE_BASE ** (-jnp.arange(half, dtype=F32) / half)
    ang = pos[:, None] * freqs[None, :]
    cos, sin = jnp.cos(ang), jnp.sin(ang)
    zero = jnp.zeros_like(sin)
    reps = LANES // RET_DK
    cos_t = jnp.tile(jnp.concatenate([cos, cos], axis=1), (1, reps))
    sa_t = jnp.tile(jnp.concatenate([zero, sin], axis=1), (1, reps))
    sb_t = jnp.tile(jnp.concatenate([-sin, zero], axis=1), (1, reps))
    lg = jnp.log1p(-jnp.exp2(-5.0 - jnp.arange(RET_HEADS, dtype=F32)))
    steps = ROWS // batch
    i1 = (jnp.arange(ROWS) // batch).astype(F32) + 1.0
    qdec = jnp.repeat(jnp.exp(i1[:, None] * lg[None, :]), RET_DK, axis=1)
    kdec = jnp.repeat(jnp.exp(-i1[:, None] * lg[None, :]), RET_DK, axis=1) * (RET_DK ** -0.5)
    cdec = jnp.repeat(jnp.exp(steps * lg), RET_DK)[None, :]
    return cos_t, sa_t, sb_t, qdec, kdec, cdec


def _s5_disc_kernel(are_ref, aim_ref, ldt_ref, bre_ref, bim_ref, abre_ref, abim_ref, bbre_ref, bbim_ref):
    ar, ai = are_ref[...], aim_ref[...]
    dt = jnp.exp(ldt_ref[...])
    dar, dai = dt * ar, dt * ai
    mag = jnp.exp(dar)
    abar_re, abar_im = mag * jnp.cos(dai), mag * jnp.sin(dai)
    den = ar * ar + ai * ai
    nr, ni = abar_re - 1.0, abar_im
    f_re = (nr * ar + ni * ai) / den
    f_im = (ni * ar - nr * ai) / den
    abre_ref[...] = abar_re
    abim_ref[...] = abar_im
    br, bi = bre_ref[...], bim_ref[...]
    bbre_ref[...] = f_re[:, None, :] * br - f_im[:, None, :] * bi
    bbim_ref[...] = f_re[:, None, :] * bi + f_im[:, None, :] * br


def _s5_discretise(a_re, a_im, log_dt, b_re_t, b_im_t):
    lg = a_re.shape[0]
    small = jax.ShapeDtypeStruct((lg, SSM_P), F32)
    big = jax.ShapeDtypeStruct((lg, SSM_GC, SSM_P), F32)
    return pl.pallas_call(_s5_disc_kernel, out_shape=[small, small, big, big], name="s5_disc")(
        a_re, a_im, log_dt, b_re_t, b_im_t)


def _s5_kernel(batch, u_ref, bmat_ref, cmat_ref, are_ref, aim_ref, d_ref, h0re_ref, h0im_ref,
               y_ref, xre_ref, xim_ref, x_scr):
    c = pl.program_id(0)
    steps = ROWS // batch
    half = SSM_SLAB_N

    @pl.when(c == 0)
    def _():
        xre_ref[...] = h0re_ref[...]
        xim_ref[...] = h0im_ref[...]

    u = u_ref[...]
    ub = u.astype(BF16)
    for s in range(SSM_SLABS):
        x_scr[:, 2 * half * s:2 * half * (s + 1)] = _dot(ub[:, s * LANES:(s + 1) * LANES], bmat_ref[s])

    for s in range(SSM_SLABS):
        re0 = 2 * half * s
        im0 = re0 + half
        sc = slice(half * s, half * (s + 1))
        ar = jnp.broadcast_to(are_ref[:, sc], (SUBLANES, half))
        ai = jnp.broadcast_to(aim_ref[:, sc], (SUBLANES, half))

        def row_tile(rt, carry, re0=re0, im0=im0, sc=sc, ar=ar, ai=ai):
            r0 = pl.multiple_of(rt * SUBLANES, SUBLANES)

            def step(t, x):
                xr, xi = x
                row = pl.multiple_of(t * batch + r0, SUBLANES)
                nr = ar * xr - ai * xi + x_scr[pl.ds(row, SUBLANES), re0:re0 + half]
                ni = ar * xi + ai * xr + x_scr[pl.ds(row, SUBLANES), im0:im0 + half]
                x_scr[pl.ds(row, SUBLANES), re0:re0 + half] = nr
                x_scr[pl.ds(row, SUBLANES), im0:im0 + half] = ni
                return nr, ni

            init = (xre_ref[pl.ds(r0, SUBLANES), sc], xim_ref[pl.ds(r0, SUBLANES), sc])
            xr, xi = lax.fori_loop(0, steps, step, init, unroll=8)
            xre_ref[pl.ds(r0, SUBLANES), sc] = xr
            xim_ref[pl.ds(r0, SUBLANES), sc] = xi
            return carry

        lax.fori_loop(0, batch // SUBLANES, row_tile, 0)

    ys = [_dot(x_scr[:, 2 * half * s:2 * half * (s + 1)].astype(BF16), cmat_ref[s]) for s in range(SSM_SLABS)]
    y = jnp.concatenate(ys, axis=1) + d_ref[...] * u
    y_ref[...] = jax.nn.gelu(y)


def _s5(z, row_blk0, nblk, batch, bmat, cmat, abre, abim, d, h0re, h0im):
    const2 = lambda c: (0, 0)
    const3 = lambda c: (0, 0, 0)
    st = pl.BlockSpec((batch, SSM_N), const2)
    return pl.pallas_call(
        functools.partial(_s5_kernel, batch),
        grid=(nblk,),
        in_specs=[pl.BlockSpec((ROWS, SSM_W), lambda c: (row_blk0 + c, 4)),
                  pl.BlockSpec((SSM_SLABS, LANES, 2 * SSM_SLAB_N), const3),
                  pl.BlockSpec((SSM_SLABS, 2 * SSM_SLAB_N, LANES), const3),
                  pl.BlockSpec((1, SSM_N), const2),
                  pl.BlockSpec((1, SSM_N), const2),
                  pl.BlockSpec((1, SSM_W), const2),
                  st, st],
        out_specs=[pl.BlockSpec((ROWS, SSM_W), lambda c: (c, 0)), st, st],
        out_shape=[jax.ShapeDtypeStruct((nblk * ROWS, SSM_W), F32),
                   jax.ShapeDtypeStruct((batch, SSM_N), F32),
                   jax.ShapeDtypeStruct((batch, SSM_N), F32)],
        scratch_shapes=[pltpu.VMEM((ROWS, 2 * SSM_N), F32)],
        compiler_params=_params(("arbitrary",), 48),
        name="s5",
    )(z, bmat, cmat, abre, abim, d, h0re, h0im)


def _block_diag_slabs(w, rows_inner):
    gps = SSM_G // SSM_SLABS
    eye = jnp.eye(gps, dtype=w.dtype)
    w4 = w.reshape(SSM_SLABS, gps, w.shape[1], w.shape[2])
    out = w4[:, :, :, None, :] * eye[None, :, None, :, None]
    return out.reshape(SSM_SLABS, gps * w.shape[1], gps * w.shape[2])


def _conv_kernel(batch, bg_ref, cg_ref, hc_ref, buf0_ref, w_ref, b_ref, o_ref, buf_ref, zp_scr):
    c = pl.program_id(0)
    pad = (CONV_K - 1) * batch

    @pl.when(c == 0)
    def _():
        zp_scr[0:pad, :] = buf0_ref[...]

    zc = cg_ref[...] * hc_ref[...]
    zp_scr[pad:pad + ROWS, :] = zc
    y = b_ref[...]
    for j in range(CONV_K):
        y = y + w_ref[j:j + 1, :] * zp_scr[j * batch:j * batch + ROWS, :]
    o_ref[...] = bg_ref[...] * y
    tail = zp_scr[ROWS:ROWS + pad, :]
    buf_ref[...] = tail
    zp_scr[0:pad, :] = tail


def _conv(z, row_blk0, nblk, batch, buf0, w, b):
    pad = (CONV_K - 1) * batch
    const = lambda c: (0, 0)
    return pl.pallas_call(
        functools.partial(_conv_kernel, batch),
        grid=(nblk,),
        in_specs=[pl.BlockSpec((ROWS, CONV_W), lambda c: (row_blk0 + c, 5)),
                  pl.BlockSpec((ROWS, CONV_W), lambda c: (row_blk0 + c, 6)),
                  pl.BlockSpec((ROWS, CONV_W), lambda c: (row_blk0 + c, 7)),
                  pl.BlockSpec((pad, CONV_W), const),
                  pl.BlockSpec((CONV_K, CONV_W), const),
                  pl.BlockSpec((1, CONV_W), const)],
        out_specs=[pl.BlockSpec((ROWS, CONV_W), lambda c: (c, 0)), pl.BlockSpec((pad, CONV_W), const)],
        out_shape=[jax.ShapeDtypeStruct((nblk * ROWS, CONV_W), F32),
                   jax.ShapeDtypeStruct((pad, CONV_W), F32)],
        scratch_shapes=[pltpu.VMEM((ROWS + pad, CONV_W), F32)],
        compiler_params=_params(("arbitrary",), 32),
        name="conv",
    )(z, z, z, buf0, w, b)


def _merge_kernel(x_ref, oa_ref, ys_ref, oc_ref, ga_ref, gb_ref, gc_ref, wr_ref, wa_ref, wb_ref, wc_ref,
                  wm_ref, gf_ref, x1_ref, xn_ref):
    oa = _dot(oa_ref[...].astype(BF16), wr_ref[...])
    ysb = ys_ref[...].astype(BF16)
    ob = _dot(ysb, wa_ref[...]) * jax.nn.sigmoid(_dot(ysb, wb_ref[...]))
    oc = _dot(oc_ref[...].astype(BF16), wc_ref[...])
    merged = (jax.nn.sigmoid(ga_ref[...]) * oa + jax.nn.sigmoid(gb_ref[...]) * ob
              + jax.nn.sigmoid(gc_ref[...]) * oc)
    x1 = x_ref[...] + _dot(merged.astype(BF16), wm_ref[...])
    x1_ref[...] = x1
    xn_ref[...] = _rms(x1, gf_ref[...]).astype(BF16)


def _merge(x, z, oa, ys, oc, wr, wa, wb, wc, wm, gf):
    t = x.shape[0]
    rb = 512
    row = lambda w: pl.BlockSpec((rb, w), lambda i: (i, 0))
    gate = lambda j: pl.BlockSpec((rb, D_MODEL), lambda i: (i, j))
    wsp = lambda k: pl.BlockSpec((k, D_MODEL), lambda i: (0, 0))
    return pl.pallas_call(
        _merge_kernel,
        grid=(t // rb,),
        in_specs=[row(D_MODEL), row(RET_W), row(SSM_W), row(CONV_W), gate(4), gate(5), gate(6),
                  wsp(RET_W), wsp(SSM_W), wsp(SSM_W), wsp(CONV_W), wsp(D_MODEL), wsp(1)],
        out_specs=[row(D_MODEL), row(D_MODEL)],
        out_shape=[jax.ShapeDtypeStruct((t, D_MODEL), F32), jax.ShapeDtypeStruct((t, D_MODEL), BF16)],
        compiler_params=_params(("parallel",), 48),
        name="merge",
    )(x, oa, ys, oc, z, z, z, wr, wa, wb, wc, wm, gf)


def _tree(items, combine):
    while len(items) > 1:
        nxt = [combine(items[i], items[i + 1]) for i in range(0, len(items) - 1, 2)]
        if len(items) % 2:
            nxt.append(items[-1])
        items = nxt
    return items[0]


def _first_max(x, y):
    (vx, ix), (vy, iy) = x, y
    return jnp.maximum(vx, vy), jnp.where(vx >= vy, ix, iy)


def _top16_of_keys(s_scrs, v_scrs, i_scrs):
    def body(r, carry):
        for s_scr, v_scr, i_scr in zip(s_scrs, v_scrs, i_scrs):
            m, idx = _tree([(s_scr[k], float(k)) for k in range(PEER_NKEYS)], _first_max)
            v_scr[r] = m
            i_scr[r] = idx
            for k in range(PEER_NKEYS):
                s_scr[k] = jnp.where(idx == float(k), -jnp.inf, s_scr[k])
        return carry

    lax.fori_loop(0, PEER_TOPK, body, 0)


def _select_kernel(tb, xn_ref, wq_ref, k1_ref, k2_ref, e1_ref, e2_ref, g_ref,
                   s1_scr, s2_scr, v1_scr, i1_scr, v2_scr, i2_scr, cand_scr, sc_scr, se1_scr, se2_scr):
    q = _dot(xn_ref[...], wq_ref[...]).astype(BF16)
    hq = PEER_HEADS * PEER_DQ // 2
    s1 = _dot_nt(k1_ref[...], q[:, :hq])
    s2 = _dot_nt(k2_ref[...], q[:, hq:])
    flats = [float(a * PEER_TOPK + b) for a, b in _CANDS]
    for lt in range(tb // LANES):
        lanes = slice(lt * LANES, (lt + 1) * LANES)
        s1_scr[...] = s1[:, lanes].reshape(PEER_NKEYS, SUBLANES, LANES)
        s2_scr[...] = s2[:, lanes].reshape(PEER_NKEYS, SUBLANES, LANES)
        _top16_of_keys((s1_scr, s2_scr), (v1_scr, v2_scr), (i1_scr, i2_scr))
        for n, (a, b) in enumerate(_CANDS):
            cand_scr[n] = v1_scr[a] + v2_scr[b]

        def body(r, carry):
            m, flat = _tree([(cand_scr[n], f) for n, f in enumerate(flats)], _first_max)
            for n, f in enumerate(flats):
                cand_scr[n] = jnp.where(flat == f, -jnp.inf, cand_scr[n])
            fa = jnp.floor(flat * (1.0 / PEER_TOPK))
            fb = flat - fa * PEER_TOPK
            sc_scr[r] = m
            se1_scr[r] = _tree([jnp.where(fa == float(a), i1_scr[a], 0.0) for a in range(PEER_TOPK)], jnp.add)
            se2_scr[r] = _tree([jnp.where(fb == float(b), i2_scr[b], 0.0) for b in range(PEER_TOPK)], jnp.add)
            return carry

        lax.fori_loop(0, PEER_TOPK, body, 0)
        sc = sc_scr[...]
        ex = jnp.exp(sc - jnp.max(sc, axis=0, keepdims=True))
        gate = ex / jnp.sum(ex, axis=0, keepdims=True)
        rows = slice(lt * LANES, (lt + 1) * LANES)
        g_ref[rows, :] = gate.reshape(PEER_SLOTS, LANES).T
        e1_ref[rows, :] = se1_scr[...].reshape(PEER_SLOTS, LANES).T
        e2_ref[rows, :] = se2_scr[...].reshape(PEER_SLOTS, LANES).T


def _peer_select(xn, wq, k1big, k2big):
    t = xn.shape[0]
    tb = 256
    hq = PEER_HEADS * PEER_DQ // 2
    nk = PEER_NKEYS * PEER_HEADS
    const = lambda i: (0, 0)
    row = lambda dt: jax.ShapeDtypeStruct((t, PEER_SLOTS), dt)
    vec = lambda n: pltpu.VMEM((n, SUBLANES, LANES), F32)
    return pl.pallas_call(
        functools.partial(_select_kernel, tb),
        grid=(t // tb,),
        in_specs=[pl.BlockSpec((tb, D_MODEL), lambda i: (i, 0)),
                  pl.BlockSpec((D_MODEL, 2 * hq), const),
                  pl.BlockSpec((nk, hq), const),
                  pl.BlockSpec((nk, hq), const)],
        out_specs=[pl.BlockSpec((tb, PEER_SLOTS), lambda i: (i, 0))] * 3,
        out_shape=[row(F32), row(F32), row(F32)],
        scratch_shapes=[vec(PEER_NKEYS), vec(PEER_NKEYS), vec(PEER_TOPK), vec(PEER_TOPK), vec(PEER_TOPK), vec(PEER_TOPK),
                        vec(len(_CANDS)), vec(PEER_TOPK), vec(PEER_TOPK), vec(PEER_TOPK)],
        compiler_params=_params(("parallel",), 40),
        name="peer_select",
    )(xn, wq, k1big, k2big)


def _peer_kernel(tb, eb, stride, xn_ref, e1_ref, e2_ref, g_ref, ut_ref, v_ref, x1_ref, out_ref, m_scr):
    e = pl.program_id(1)
    nk1 = eb // PEER_NKEYS

    @pl.when(e == 0)
    def _():
        out_ref[...] = x1_ref[...]

    @pl.when(e == 0)
    def _():
        key = lax.broadcasted_iota(jnp.int32, (PEER_NKEYS, PEER_SLOTS), 0).astype(F32)

        def token(t, carry):
            e1 = e1_ref[pl.ds(t, 1), :]
            e2 = e2_ref[pl.ds(t, 1), :]
            gt = 0.5 * g_ref[pl.ds(t, 1), :]
            a_t = jnp.where(key == e1, gt, 0.0).astype(BF16)
            b_t = jnp.where(key == e2, 1.0, 0.0).astype(BF16)
            m_scr[pl.ds(t, PEER_NKEYS, stride=stride), :] = _dot_nt(a_t, b_t)
            return carry

        lax.fori_loop(0, tb, token, 0, unroll=64)

    s = _dot(xn_ref[...], ut_ref[...])
    t = jnp.tanh(s * (GELU_C + (GELU_C * GELU_A) * (s * s)))
    k1 = e * nk1
    gates = [m_scr[pl.ds(pl.multiple_of((k1 + i) * stride, SUBLANES), tb), :] for i in range(nk1)]
    w = ((s + s * t) * jnp.concatenate(gates, axis=1)).astype(BF16)
    out_ref[...] += _dot(w, v_ref[...])


def _peer_dense(xn, e1, e2, g, ut, v, x1):
    t = xn.shape[0]
    tb, eb = 512, 1024
    stride = tb + SUBLANES
    once = pl.Buffered(1)
    tok = lambda w: pl.BlockSpec((tb, w), lambda i, e: (i, 0), pipeline_mode=once)
    tab = pl.BlockSpec((eb, D_MODEL), lambda i, e: (e, 0))
    return pl.pallas_call(
        functools.partial(_peer_kernel, tb, eb, stride),
        grid=(t // tb, PEER_NEXP // eb),
        in_specs=[tok(D_MODEL), tok(PEER_SLOTS), tok(PEER_SLOTS), tok(PEER_SLOTS),
                  pl.BlockSpec((D_MODEL, eb), lambda i, e: (0, e)), tab, tok(D_MODEL)],
        out_specs=pl.BlockSpec((tb, D_MODEL), lambda i, e: (i, 0)),
        out_shape=jax.ShapeDtypeStruct((t, D_MODEL), F32),
        scratch_shapes=[pltpu.VMEM((PEER_NKEYS * stride, PEER_NKEYS), F32)],
        compiler_params=_params(("parallel", "arbitrary"), 60),
        name="peer_dense",
    )(xn, e1, e2, g, ut, v, x1)


def _norm_kernel(x_ref, g_ref, y_ref):
    y_ref[...] = _rms(x_ref[...], g_ref[...])


def _final_norm(x, g):
    t = x.shape[0]
    return pl.pallas_call(
        _norm_kernel,
        grid=(t // ROWS,),
        in_specs=[pl.BlockSpec((ROWS, D_MODEL), lambda i: (i, 0)), pl.BlockSpec((1, D_MODEL), lambda i: (0, 0))],
        out_specs=pl.BlockSpec((ROWS, D_MODEL), lambda i: (i, 0)),
        out_shape=jax.ShapeDtypeStruct((t, D_MODEL), F32),
        compiler_params=_params(("parallel",), 32),
        name="final_norm",
    )(x, g)


def _time_major(x):
    b, s, d = x.shape
    return x.transpose(1, 0, 2).reshape(s * b, d)


def _batch_major(y, b, s):
    return y.reshape(s, b, y.shape[-1]).transpose(1, 0, 2)


def kernel(x_prompt, x_sample, state_ret, state_ssm_re, state_ssm_im, state_conv, norm_mix, w_in, ret_norm, w_ret_out, ssm_a_re, ssm_a_im, ssm_b_re, ssm_b_im, ssm_c_re, ssm_c_im, ssm_d, ssm_log_dt, w_glu_a, w_glu_b, conv_w, conv_b, w_conv_out, w_mix_out, norm_ffn, peer_wq, peer_k1, peer_k2, peer_u, peer_v, norm_final):
    bp, sp, _ = x_prompt.shape
    bs, ss, _ = x_sample.shape
    tp, ts = bp * sp, bs * ss
    depth = w_in.shape[0]
    groups = ((0, tp // ROWS, bp), (tp // ROWS, ts // ROWS, bs))
    assert tp % ROWS == 0 and ts == ROWS and ROWS % bp == 0 and ROWS // bp == math.gcd(sp, RET_CHUNK)

    x = jnp.concatenate([_time_major(x_prompt), _time_major(x_sample)], axis=0)

    pos_p = jnp.repeat(jnp.arange(sp, dtype=F32), bp)
    pos_s = jnp.repeat(PAST_LEN + jnp.arange(ss, dtype=F32), bs)
    ret_tabs = (_retention_tables(pos_p, bp), _retention_tables(pos_s, bs))

    lg = depth * SSM_G
    abre, abim, bbre, bbim = _s5_discretise(
        ssm_a_re.reshape(lg, SSM_P), ssm_a_im.reshape(lg, SSM_P), ssm_log_dt.reshape(lg, 1),
        ssm_b_re.transpose(0, 1, 3, 2).reshape(lg, SSM_GC, SSM_P),
        ssm_b_im.transpose(0, 1, 3, 2).reshape(lg, SSM_GC, SSM_P))

    ret_p, ret_s, re_p, re_s, im_p, im_s, cv_p, cv_s = [], [], [], [], [], [], [], []
    for l in range(depth):
        sl = slice(l * SSM_G, (l + 1) * SSM_G)
        bmat = jnp.concatenate([_block_diag_slabs(bbre[sl], None), _block_diag_slabs(bbim[sl], None)],
                               axis=2).astype(BF16)
        cmat = jnp.concatenate([_block_diag_slabs(ssm_c_re[l].transpose(0, 2, 1), None),
                                _block_diag_slabs(-ssm_c_im[l].transpose(0, 2, 1), None)],
                               axis=1).astype(BF16)
        are_row = abre[sl].reshape(1, SSM_N)
        aim_row = abim[sl].reshape(1, SSM_N)

        z = _inproj(x, norm_mix[l][None, :], w_in[l].astype(BF16))

        oa_l, ys_l, oc_l = [], [], []
        for gi, (blk0, nblk, batch) in enumerate(groups):
            if gi == 0:
                s0 = jnp.zeros((batch, RET_HEADS, RET_DK, RET_DK), F32)
                h0re = jnp.zeros((batch, SSM_N), F32)
                h0im = jnp.zeros((batch, SSM_N), F32)
                buf0 = jnp.zeros(((CONV_K - 1) * batch, CONV_W), F32)
                bblk = batch
            else:
                s0 = state_ret[l]
                h0re = state_ssm_re[l].reshape(batch, SSM_N)
                h0im = state_ssm_im[l].reshape(batch, SSM_N)
                buf0 = state_conv[l].transpose(1, 0, 2).reshape((CONV_K - 1) * batch, CONV_W)
                bblk = 16
            oa, s_new = _retention(z, blk0, nblk, batch, bblk, ret_tabs[gi], ret_norm[l][None, :], s0)
            ys, xre, xim = _s5(z, blk0, nblk, batch, bmat, cmat, are_row, aim_row, ssm_d[l][None, :], h0re, h0im)
            oc, buf = _conv(z, blk0, nblk, batch, buf0, conv_w[l], conv_b[l][None, :])
            oa_l.append(oa); ys_l.append(ys); oc_l.append(oc)
            cv = buf.reshape(CONV_K - 1, batch, CONV_W).transpose(1, 0, 2)
            sre = xre.reshape(batch, SSM_G, SSM_P)
            sim = xim.reshape(batch, SSM_G, SSM_P)
            if gi == 0:
                ret_p.append(s_new); re_p.append(sre); im_p.append(sim); cv_p.append(cv)
            else:
                ret_s.append(s_new); re_s.append(sre); im_s.append(sim); cv_s.append(cv)

        x1, xn = _merge(x, z, jnp.concatenate(oa_l), jnp.concatenate(ys_l), jnp.concatenate(oc_l),
                        w_ret_out[l].astype(BF16), w_glu_a[l].astype(BF16), w_glu_b[l].astype(BF16),
                        w_conv_out[l].astype(BF16), w_mix_out[l].astype(BF16), norm_ffn[l][None, :])

        hq = PEER_DQ // 2
        wq = peer_wq[l].reshape(D_MODEL, PEER_HEADS, 2, hq).transpose(0, 2, 1, 3).reshape(D_MODEL, -1).astype(BF16)
        eye = jnp.eye(PEER_HEADS, dtype=F32)

        def keys_block_diag(k):
            return (k.transpose(1, 0, 2)[:, :, None, :] * eye[None, :, :, None]).reshape(
                PEER_NKEYS * PEER_HEADS, PEER_HEADS * hq).astype(BF16)

        e1, e2, g = _peer_select(xn, wq, keys_block_diag(peer_k1[l]), keys_block_diag(peer_k2[l]))
        x = _peer_dense(xn, e1, e2, g, peer_u[l].astype(BF16).T, peer_v[l].astype(BF16), x1)

    y = _final_norm(x, norm_final[None, :])
    y_prompt = _batch_major(y[:tp], bp, sp)
    y_sample = _batch_major(y[tp:], bs, ss)
    return (y_prompt, y_sample,
            jnp.stack(ret_p), jnp.stack(ret_s),
            jnp.stack(re_p), jnp.stack(re_s),
            jnp.stack(im_p), jnp.stack(im_s),
            jnp.stack(cv_p), jnp.stack(cv_s))
```

```python
import functools
import math

import jax
import jax.numpy as jnp
from jax import lax
from jax.experimental import pallas as pl
from jax.experimental.pallas import tpu as pltpu

F32 = jnp.float32
BF16 = jnp.bfloat16

D_MODEL = 1024
DEPTH = 2
PAST_LEN = 16384
RET_HEADS = 8
RET_DK = 64
RET_W = 512
RET_CHUNK = 128
ROPE_BASE = 10000.0
SSM_W = 512
SSM_GC = 16
SSM_G = 32
SSM_P = 64
SSM_N = SSM_G * SSM_P
SSM_SLABS = 4
SSM_SLAB_N = SSM_N // SSM_SLABS
CONV_W = 512
CONV_K = 3
PROJ_W = 7168
PEER_HEADS = 8
PEER_DQ = 256
PEER_NKEYS = 128
PEER_TOPK = 16
PEER_NEXP = PEER_NKEYS ** 2
PEER_SLOTS = PEER_HEADS * PEER_TOPK
EPS = 1e-6
GELU_C = math.sqrt(2.0 / math.pi)
GELU_A = 0.044715

ROWS = 1024
LANES = 128
SUBLANES = 8
MXU_DEPTH = 256
MIB = 1024 * 1024

_CANDS = [(a, b) for a in range(PEER_TOPK) for b in range(PEER_TOPK) if (a + 1) * (b + 1) <= PEER_TOPK]


def _params(sem, vmem_mib):
    return pltpu.CompilerParams(dimension_semantics=sem, vmem_limit_bytes=vmem_mib * MIB)


def _rms(x, g):
    return x * lax.rsqrt(jnp.mean(x * x, axis=-1, keepdims=True) + EPS) * g


def _dot(a, b):
    return jnp.dot(a, b, preferred_element_type=F32)


def _dot_nt(a, b):
    return lax.dot_general(a, b, (((1,), (1,)), ((), ())), preferred_element_type=F32)


def _inproj_kernel(x_ref, g_ref, w_ref, z_ref, h_scr):
    @pl.when(pl.program_id(1) == 0)
    def _():
        h_scr[...] = _rms(x_ref[...], g_ref[...]).astype(BF16)

    z_ref[...] = _dot(h_scr[...], w_ref[...])


def _inproj(x, g, w):
    t = x.shape[0]
    nb = 1024
    return pl.pallas_call(
        _inproj_kernel,
        grid=(t // ROWS, PROJ_W // nb),
        in_specs=[pl.BlockSpec((ROWS, D_MODEL), lambda i, j: (i, 0)),
                  pl.BlockSpec((1, D_MODEL), lambda i, j: (0, 0)),
                  pl.BlockSpec((D_MODEL, nb), lambda i, j: (0, j))],
        out_specs=pl.BlockSpec((ROWS, nb), lambda i, j: (i, j)),
        out_shape=jax.ShapeDtypeStruct((t, PROJ_W), F32),
        scratch_shapes=[pltpu.VMEM((ROWS, D_MODEL), BF16)],
        compiler_params=_params(("parallel", "arbitrary"), 40),
        name="inproj",
    )(x, g, w)


def _ret_kernel(batch, bblk, z_ref, cos_ref, sa_ref, sb_ref, qdec_ref, kdec_ref, cdec_ref, gn_ref, s0_ref,
                o_ref, s_ref, qd_scr, kd_scr, v_scr, mask_scr, oacc_scr):
    bb = pl.program_id(0)
    c = pl.program_id(1)
    steps = ROWS // batch
    nslab = RET_W // LANES

    def head_view(ref, h):
        return ref[h // 2, :, (h % 2) * RET_DK:(h % 2 + 1) * RET_DK]

    @pl.when((bb == 0) & (c == 0))
    def _():
        r = lax.broadcasted_iota(jnp.int32, (ROWS, ROWS), 0)
        cc = lax.broadcasted_iota(jnp.int32, (ROWS, ROWS), 1)
        same = (r & (batch - 1)) == (cc & (batch - 1))
        mask_scr[...] = (same & (r >= cc)).astype(F32)

    @pl.when(c == 0)
    def _():
        s_ref[...] = s0_ref[...]

    @pl.when(bb == 0)
    def _():
        cos, sa, sb = cos_ref[...], sa_ref[...], sb_ref[...]

        def rot(x):
            return x * cos + pltpu.roll(x, 32, 1) * sa + pltpu.roll(x, 96, 1) * sb

        for s in range(nslab):
            cols = slice(s * LANES, (s + 1) * LANES)
            qd_scr[s] = rot(z_ref[:, cols]) * qdec_ref[:, cols]
            kd_scr[s] = rot(z_ref[:, RET_W + s * LANES:RET_W + (s + 1) * LANES]) * kdec_ref[:, cols]
            v_scr[s] = z_ref[:, 2 * RET_W + s * LANES:2 * RET_W + (s + 1) * LANES]
        for s in range(nslab):
            outs = []
            for h in (2 * s, 2 * s + 1):
                qh = head_view(qd_scr, h).astype(BF16)
                kh = head_view(kd_scr, h).astype(BF16)
                vh = head_view(v_scr, h).astype(BF16)
                p = (_dot_nt(qh, kh) * mask_scr[...]).astype(BF16)
                outs.append(_dot(p, vh))
            oacc_scr[s] = jnp.concatenate(outs, axis=1)

    def per_seq(bl, carry):
        b = bb * bblk + bl
        rows = pl.ds(b, steps, stride=batch)
        for s in range(nslab):
            qb = qd_scr[s, rows, :]
            kb = kd_scr[s, rows, :]
            vb = v_scr[s, rows, :]
            cross = []
            for hh in range(2):
                h = 2 * s + hh
                hc = slice(hh * RET_DK, (hh + 1) * RET_DK)
                st = s_ref[bl, h]
                cross.append(_dot(qb[:, hc].astype(BF16), st.astype(BF16)))
                upd = lax.dot_general(kb[:, hc].astype(BF16), vb[:, hc].astype(BF16),
                                      (((0,), (0,)), ((), ())), preferred_element_type=F32)
                s_ref[bl, h] = (st + upd) * cdec_ref[:, h * RET_DK:(h + 1) * RET_DK]
            oacc_scr[s, rows, :] = oacc_scr[s, rows, :] + jnp.concatenate(cross, axis=1)
        return carry

    lax.fori_loop(0, bblk, per_seq, 0)

    @pl.when(bb == pl.num_programs(0) - 1)
    def _():
        normed = []
        for h in range(RET_HEADS):
            oh = head_view(oacc_scr, h)
            mu = jnp.mean(oh, axis=-1, keepdims=True)
            dlt = oh - mu
            var = jnp.mean(dlt * dlt, axis=-1, keepdims=True)
            normed.append(dlt * lax.rsqrt(var + EPS))
        o = jnp.concatenate(normed, axis=1) * gn_ref[...]
        o_ref[...] = jax.nn.silu(z_ref[:, 3 * RET_W:4 * RET_W]) * o


def _retention(z, row_blk0, nblk, batch, bblk, tabs, gn, s0):
    cos, sa, sb, qdec, kdec, cdec = tabs
    nbb = batch // bblk
    st_spec = pl.BlockSpec((bblk, RET_HEADS, RET_DK, RET_DK), lambda bb, c: (bb, 0, 0, 0))
    const = lambda bb, c: (0, 0)
    return pl.pallas_call(
        functools.partial(_ret_kernel, batch, bblk),
        grid=(nbb, nblk),
        in_specs=[pl.BlockSpec((ROWS, 4 * RET_W), lambda bb, c: (row_blk0 + c, 0)),
                  pl.BlockSpec((ROWS, LANES), lambda bb, c: (c, 0)),
                  pl.BlockSpec((ROWS, LANES), lambda bb, c: (c, 0)),
                  pl.BlockSpec((ROWS, LANES), lambda bb, c: (c, 0)),
                  pl.BlockSpec((ROWS, RET_W), const),
                  pl.BlockSpec((ROWS, RET_W), const),
                  pl.BlockSpec((1, RET_W), const),
                  pl.BlockSpec((1, RET_W), const),
                  st_spec],
        out_specs=[pl.BlockSpec((ROWS, RET_W), lambda bb, c: (c, 0)), st_spec],
        out_shape=[jax.ShapeDtypeStruct((nblk * ROWS, RET_W), F32),
                   jax.ShapeDtypeStruct((batch, RET_HEADS, RET_DK, RET_DK), F32)],
        scratch_shapes=[pltpu.VMEM((RET_W // LANES, ROWS, LANES), F32)] * 3
        + [pltpu.VMEM((ROWS, ROWS), F32), pltpu.VMEM((RET_W // LANES, ROWS, LANES), F32)],
        compiler_params=_params(("arbitrary", "arbitrary"), 56),
        name="retention",
    )(z, cos, sa, sb, qdec, kdec, cdec, gn, s0)


def _retention_tables(pos, batch):
    half = RET_DK // 2
    freqs = ROPE_BASE ** (-jnp.arange(half, dtype=F32) / half)
    ang = pos[:, None] * freqs[None, :]
    cos, sin = jnp.cos(ang), jnp.sin(ang)
    zero = jnp.zeros_like(sin)
    reps = LANES // RET_DK
    cos_t = jnp.tile(jnp.concatenate([cos, cos], axis=1), (1, reps))
    sa_t = jnp.tile(jnp.concatenate([zero, sin], axis=1), (1, reps))
    sb_t = jnp.tile(jnp.concatenate([-sin, zero], axis=1), (1, reps))
    lg = jnp.log1p(-jnp.exp2(-5.0 - jnp.arange(RET_HEADS, dtype=F32)))
    steps = ROWS // batch
    i1 = (jnp.arange(ROWS) // batch).astype(F32) + 1.0
    qdec = jnp.repeat(jnp.exp(i1[:, None] * lg[None, :]), RET_DK, axis=1)
    kdec = jnp.repeat(jnp.exp(-i1[:, None] * lg[None, :]), RET_DK, axis=1) * (RET_DK ** -0.5)
    cdec = jnp.repeat(jnp.exp(steps * lg), RET_DK)[None, :]
    return cos_t, sa_t, sb_t, qdec, kdec, cdec


def _s5_disc_kernel(are_ref, aim_ref, ldt_ref, bre_ref, bim_ref, abre_ref, abim_ref, bbre_ref, bbim_ref):
    ar, ai = are_ref[...], aim_ref[...]
    dt = jnp.exp(ldt_ref[...])
    dar, dai = dt * ar, dt * ai
    mag = jnp.exp(dar)
    abar_re, abar_im = mag * jnp.cos(dai), mag * jnp.sin(dai)
    den = ar * ar + ai * ai
    nr, ni = abar_re - 1.0, abar_im
    f_re = (nr * ar + ni * ai) / den
    f_im = (ni * ar - nr * ai) / den
    abre_ref[...] = abar_re
    abim_ref[...] = abar_im
    br, bi = bre_ref[...], bim_ref[...]
    bbre_ref[...] = f_re[:, None, :] * br - f_im[:, None, :] * bi
    bbim_ref[...] = f_re[:, None, :] * bi + f_im[:, None, :] * br


def _s5_discretise(a_re, a_im, log_dt, b_re_t, b_im_t):
    lg = a_re.shape[0]
    small = jax.ShapeDtypeStruct((lg, SSM_P), F32)
    big = jax.ShapeDtypeStruct((lg, SSM_GC, SSM_P), F32)
    return pl.pallas_call(_s5_disc_kernel, out_shape=[small, small, big, big], name="s5_disc")(
        a_re, a_im, log_dt, b_re_t, b_im_t)


def _s5_kernel(batch, u_ref, bmat_ref, cmat_ref, are_ref, aim_ref, d_ref, h0re_ref, h0im_ref,
               y_ref, xre_ref, xim_ref, x_scr):
    c = pl.program_id(0)
    steps = ROWS // batch
    half = SSM_SLAB_N

    @pl.when(c == 0)
    def _():
        xre_ref[...] = h0re_ref[...]
        xim_ref[...] = h0im_ref[...]

    u = u_ref[...]
    ub = u.astype(BF16)
    for s in range(SSM_SLABS):
        x_scr[:, 2 * half * s:2 * half * (s + 1)] = _dot(ub[:, s * LANES:(s + 1) * LANES], bmat_ref[s])

    for s in range(SSM_SLABS):
        re0 = 2 * half * s
        im0 = re0 + half
        sc = slice(half * s, half * (s + 1))
        ar = jnp.broadcast_to(are_ref[:, sc], (SUBLANES, half))
        ai = jnp.broadcast_to(aim_ref[:, sc], (SUBLANES, half))

        def row_tile(rt, carry, re0=re0, im0=im0, sc=sc, ar=ar, ai=ai):
            r0 = pl.multiple_of(rt * SUBLANES, SUBLANES)

            def step(t, x):
                xr, xi = x
                row = pl.multiple_of(t * batch + r0, SUBLANES)
                nr = ar * xr - ai * xi + x_scr[pl.ds(row, SUBLANES), re0:re0 + half]
                ni = ar * xi + ai * xr + x_scr[pl.ds(row, SUBLANES), im0:im0 + half]
                x_scr[pl.ds(row, SUBLANES), re0:re0 + half] = nr
                x_scr[pl.ds(row, SUBLANES), im0:im0 + half] = ni
                return nr, ni

            init = (xre_ref[pl.ds(r0, SUBLANES), sc], xim_ref[pl.ds(r0, SUBLANES), sc])
            xr, xi = lax.fori_loop(0, steps, step, init, unroll=8)
            xre_ref[pl.ds(r0, SUBLANES), sc] = xr
            xim_ref[pl.ds(r0, SUBLANES), sc] = xi
            return carry

        lax.fori_loop(0, batch // SUBLANES, row_tile, 0)

    ys = [_dot(x_scr[:, 2 * half * s:2 * half * (s + 1)].astype(BF16), cmat_ref[s]) for s in range(SSM_SLABS)]
    y = jnp.concatenate(ys, axis=1) + d_ref[...] * u
    y_ref[...] = jax.nn.gelu(y)


def _s5(z, row_blk0, nblk, batch, bmat, cmat, abre, abim, d, h0re, h0im):
    const2 = lambda c: (0, 0)
    const3 = lambda c: (0, 0, 0)
    st = pl.BlockSpec((batch, SSM_N), const2)
    return pl.pallas_call(
        functools.partial(_s5_kernel, batch),
        grid=(nblk,),
        in_specs=[pl.BlockSpec((ROWS, SSM_W), lambda c: (row_blk0 + c, 4)),
                  pl.BlockSpec((SSM_SLABS, LANES, 2 * SSM_SLAB_N), const3),
                  pl.BlockSpec((SSM_SLABS, 2 * SSM_SLAB_N, LANES), const3),
                  pl.BlockSpec((1, SSM_N), const2),
                  pl.BlockSpec((1, SSM_N), const2),
                  pl.BlockSpec((1, SSM_W), const2),
                  st, st],
        out_specs=[pl.BlockSpec((ROWS, SSM_W), lambda c: (c, 0)), st, st],
        out_shape=[jax.ShapeDtypeStruct((nblk * ROWS, SSM_W), F32),
                   jax.ShapeDtypeStruct((batch, SSM_N), F32),
                   jax.ShapeDtypeStruct((batch, SSM_N), F32)],
        scratch_shapes=[pltpu.VMEM((ROWS, 2 * SSM_N), F32)],
        compiler_params=_params(("arbitrary",), 48),
        name="s5",
    )(z, bmat, cmat, abre, abim, d, h0re, h0im)


def _block_diag_slabs(w, rows_inner):
    gps = SSM_G // SSM_SLABS
    eye = jnp.eye(gps, dtype=w.dtype)
    w4 = w.reshape(SSM_SLABS, gps, w.shape[1], w.shape[2])
    out = w4[:, :, :, None, :] * eye[None, :, None, :, None]
    return out.reshape(SSM_SLABS, gps * w.shape[1], gps * w.shape[2])


def _conv_kernel(batch, bg_ref, cg_ref, hc_ref, buf0_ref, w_ref, b_ref, o_ref, buf_ref, zp_scr):
    c = pl.program_id(0)
    pad = (CONV_K - 1) * batch

    @pl.when(c == 0)
    def _():
        zp_scr[0:pad, :] = buf0_ref[...]

    zc = cg_ref[...] * hc_ref[...]
    zp_scr[pad:pad + ROWS, :] = zc
    y = b_ref[...]
    for j in range(CONV_K):
        y = y + w_ref[j:j + 1, :] * zp_scr[j * batch:j * batch + ROWS, :]
    o_ref[...] = bg_ref[...] * y
    tail = zp_scr[ROWS:ROWS + pad, :]
    buf_ref[...] = tail
    zp_scr[0:pad, :] = tail


def _conv(z, row_blk0, nblk, batch, buf0, w, b):
    pad = (CONV_K - 1) * batch
    const = lambda c: (0, 0)
    return pl.pallas_call(
        functools.partial(_conv_kernel, batch),
        grid=(nblk,),
        in_specs=[pl.BlockSpec((ROWS, CONV_W), lambda c: (row_blk0 + c, 5)),
                  pl.BlockSpec((ROWS, CONV_W), lambda c: (row_blk0 + c, 6)),
                  pl.BlockSpec((ROWS, CONV_W), lambda c: (row_blk0 + c, 7)),
                  pl.BlockSpec((pad, CONV_W), const),
                  pl.BlockSpec((CONV_K, CONV_W), const),
                  pl.BlockSpec((1, CONV_W), const)],
        out_specs=[pl.BlockSpec((ROWS, CONV_W), lambda c: (c, 0)), pl.BlockSpec((pad, CONV_W), const)],
        out_shape=[jax.ShapeDtypeStruct((nblk * ROWS, CONV_W), F32),
                   jax.ShapeDtypeStruct((pad, CONV_W), F32)],
        scratch_shapes=[pltpu.VMEM((ROWS + pad, CONV_W), F32)],
        compiler_params=_params(("arbitrary",), 32),
        name="conv",
    )(z, z, z, buf0, w, b)


def _merge_kernel(x_ref, oa_ref, ys_ref, oc_ref, ga_ref, gb_ref, gc_ref, wr_ref, wa_ref, wb_ref, wc_ref,
                  wm_ref, gf_ref, x1_ref, xn_ref):
    oa = _dot(oa_ref[...].astype(BF16), wr_ref[...])
    ysb = ys_ref[...].astype(BF16)
    ob = _dot(ysb, wa_ref[...]) * jax.nn.sigmoid(_dot(ysb, wb_ref[...]))
    oc = _dot(oc_ref[...].astype(BF16), wc_ref[...])
    merged = (jax.nn.sigmoid(ga_ref[...]) * oa + jax.nn.sigmoid(gb_ref[...]) * ob
              + jax.nn.sigmoid(gc_ref[...]) * oc)
    x1 = x_ref[...] + _dot(merged.astype(BF16), wm_ref[...])
    x1_ref[...] = x1
    xn_ref[...] = _rms(x1, gf_ref[...]).astype(BF16)


def _merge(x, z, oa, ys, oc, wr, wa, wb, wc, wm, gf):
    t = x.shape[0]
    rb = 512
    row = lambda w: pl.BlockSpec((rb, w), lambda i: (i, 0))
    gate = lambda j: pl.BlockSpec((rb, D_MODEL), lambda i: (i, j))
    wsp = lambda k: pl.BlockSpec((k, D_MODEL), lambda i: (0, 0))
    return pl.pallas_call(
        _merge_kernel,
        grid=(t // rb,),
        in_specs=[row(D_MODEL), row(RET_W), row(SSM_W), row(CONV_W), gate(4), gate(5), gate(6),
                  wsp(RET_W), wsp(SSM_W), wsp(SSM_W), wsp(CONV_W), wsp(D_MODEL), wsp(1)],
        out_specs=[row(D_MODEL), row(D_MODEL)],
        out_shape=[jax.ShapeDtypeStruct((t, D_MODEL), F32), jax.ShapeDtypeStruct((t, D_MODEL), BF16)],
        compiler_params=_params(("parallel",), 48),
        name="merge",
    )(x, oa, ys, oc, z, z, z, wr, wa, wb, wc, wm, gf)


def _tree(items, combine):
    while len(items) > 1:
        nxt = [combine(items[i], items[i + 1]) for i in range(0, len(items) - 1, 2)]
        if len(items) % 2:
            nxt.append(items[-1])
        items = nxt
    return items[0]


def _first_max(x, y):
    (vx, ix), (vy, iy) = x, y
    return jnp.maximum(vx, vy), jnp.where(vx >= vy, ix, iy)


def _top16_of_keys(s_scrs, v_scrs, i_scrs):
    def body(r, carry):
        for s_scr, v_scr, i_scr in zip(s_scrs, v_scrs, i_scrs):
            m, idx = _tree([(s_scr[k], float(k)) for k in range(PEER_NKEYS)], _first_max)
            v_scr[r] = m
            i_scr[r] = idx
            for k in range(PEER_NKEYS):
                s_scr[k] = jnp.where(idx == float(k), -jnp.inf, s_scr[k])
        return carry

    lax.fori_loop(0, PEER_TOPK, body, 0)


def _select_kernel(tb, xn_ref, wq_ref, k1_ref, k2_ref, e1_ref, e2_ref, g_ref,
                   s1_scr, s2_scr, v1_scr, i1_scr, v2_scr, i2_scr, cand_scr, sc_scr, se1_scr, se2_scr):
    q = _dot(xn_ref[...], wq_ref[...]).astype(BF16)
    hq = PEER_HEADS * PEER_DQ // 2
    s1 = _dot_nt(k1_ref[...], q[:, :hq])
    s2 = _dot_nt(k2_ref[...], q[:, hq:])
    flats = [float(a * PEER_TOPK + b) for a, b in _CANDS]
    for lt in range(tb // LANES):
        lanes = slice(lt * LANES, (lt + 1) * LANES)
        s1_scr[...] = s1[:, lanes].reshape(PEER_NKEYS, SUBLANES, LANES)
        s2_scr[...] = s2[:, lanes].reshape(PEER_NKEYS, SUBLANES, LANES)
        _top16_of_keys((s1_scr, s2_scr), (v1_scr, v2_scr), (i1_scr, i2_scr))
        for n, (a, b) in enumerate(_CANDS):
            cand_scr[n] = v1_scr[a] + v2_scr[b]

        def body(r, carry):
            m, flat = _tree([(cand_scr[n], f) for n, f in enumerate(flats)], _first_max)
            for n, f in enumerate(flats):
                cand_scr[n] = jnp.where(flat == f, -jnp.inf, cand_scr[n])
            fa = jnp.floor(flat * (1.0 / PEER_TOPK))
            fb = flat - fa * PEER_TOPK
            sc_scr[r] = m
            se1_scr[r] = _tree([jnp.where(fa == float(a), i1_scr[a], 0.0) for a in range(PEER_TOPK)], jnp.add)
            se2_scr[r] = _tree([jnp.where(fb == float(b), i2_scr[b], 0.0) for b in range(PEER_TOPK)], jnp.add)
            return carry

        lax.fori_loop(0, PEER_TOPK, body, 0)
        sc = sc_scr[...]
        ex = jnp.exp(sc - jnp.max(sc, axis=0, keepdims=True))
        gate = ex / jnp.sum(ex, axis=0, keepdims=True)
        rows = slice(lt * LANES, (lt + 1) * LANES)
        g_ref[rows, :] = gate.reshape(PEER_SLOTS, LANES).T
        e1_ref[rows, :] = se1_scr[...].reshape(PEER_SLOTS, LANES).T
        e2_ref[rows, :] = se2_scr[...].reshape(PEER_SLOTS, LANES).T


def _peer_select(xn, wq, k1big, k2big):
    t = xn.shape[0]
    tb = 256
    hq = PEER_HEADS * PEER_DQ // 2
    nk = PEER_NKEYS * PEER_HEADS
    const = lambda i: (0, 0)
    row = lambda dt: jax.ShapeDtypeStruct((t, PEER_SLOTS), dt)
    vec = lambda n: pltpu.VMEM((n, SUBLANES, LANES), F32)
    return pl.pallas_call(
        functools.partial(_select_kernel, tb),
        grid=(t // tb,),
        in_specs=[pl.BlockSpec((tb, D_MODEL), lambda i: (i, 0)),
                  pl.BlockSpec((D_MODEL, 2 * hq), const),
                  pl.BlockSpec((nk, hq), const),
                  pl.BlockSpec((nk, hq), const)],
        out_specs=[pl.BlockSpec((tb, PEER_SLOTS), lambda i: (i, 0))] * 3,
        out_shape=[row(F32), row(F32), row(F32)],
        scratch_shapes=[vec(PEER_NKEYS), vec(PEER_NKEYS), vec(PEER_TOPK), vec(PEER_TOPK), vec(PEER_TOPK), vec(PEER_TOPK),
                        vec(len(_CANDS)), vec(PEER_TOPK), vec(PEER_TOPK), vec(PEER_TOPK)],
        compiler_params=_params(("parallel",), 40),
        name="peer_select",
    )(xn, wq, k1big, k2big)


def _peer_kernel(tb, eb, stride, xn_ref, e1_ref, e2_ref, g_ref, ut_ref, v_ref, x1_ref, out_ref, m_scr):
    e = pl.program_id(1)
    nk1 = eb // PEER_NKEYS

    @pl.when(e == 0)
    def _():
        out_ref[...] = x1_ref[...]

    @pl.when(e == 0)
    def _():
        key = lax.broadcasted_iota(jnp.int32, (PEER_NKEYS, PEER_SLOTS), 0).astype(F32)

        def token(t, carry):
            e1 = e1_ref[pl.ds(t, 1), :]
            e2 = e2_ref[pl.ds(t, 1), :]
            gt = 0.5 * g_ref[pl.ds(t, 1), :]
            a_t = jnp.where(key == e1, gt, 0.0).astype(BF16)
            b_t = jnp.where(key == e2, 1.0, 0.0).astype(BF16)
            m_scr[pl.ds(t, PEER_NKEYS, stride=stride), :] = _dot_nt(a_t, b_t)
            return carry

        lax.fori_loop(0, tb, token, 0, unroll=64)

    s = _dot(xn_ref[...], ut_ref[...])
    t = jnp.tanh(s * (GELU_C + (GELU_C * GELU_A) * (s * s)))
    k1 = e * nk1
    gates = [m_scr[pl.ds(pl.multiple_of((k1 + i) * stride, SUBLANES), tb), :] for i in range(nk1)]
    w = ((s + s * t) * jnp.concatenate(gates, axis=1)).astype(BF16)
    out_ref[...] += _dot(w, v_ref[...])


def _peer_dense(xn, e1, e2, g, ut, v, x1):
    t = xn.shape[0]
    tb, eb = 512, 1024
    stride = tb + SUBLANES
    once = pl.Buffered(1)
    tok = lambda w: pl.BlockSpec((tb, w), lambda i, e: (i, 0), pipeline_mode=once)
    tab = pl.BlockSpec((eb, D_MODEL), lambda i, e: (e, 0))
    return pl.pallas_call(
        functools.partial(_peer_kernel, tb, eb, stride),
        grid=(t // tb, PEER_NEXP // eb),
        in_specs=[tok(D_MODEL), tok(PEER_SLOTS), tok(PEER_SLOTS), tok(PEER_SLOTS),
                  pl.BlockSpec((D_MODEL, eb), lambda i, e: (0, e)), tab, tok(D_MODEL)],
        out_specs=pl.BlockSpec((tb, D_MODEL), lambda i, e: (i, 0)),
        out_shape=jax.ShapeDtypeStruct((t, D_MODEL), F32),
        scratch_shapes=[pltpu.VMEM((PEER_NKEYS * stride, PEER_NKEYS), F32)],
        compiler_params=_params(("parallel", "arbitrary"), 60),
        name="peer_dense",
    )(xn, e1, e2, g, ut, v, x1)


def _norm_kernel(x_ref, g_ref, y_ref):
    y_ref[...] = _rms(x_ref[...], g_ref[...])


def _final_norm(x, g):
    t = x.shape[0]
    return pl.pallas_call(
        _norm_kernel,
        grid=(t // ROWS,),
        in_specs=[pl.BlockSpec((ROWS, D_MODEL), lambda i: (i, 0)), pl.BlockSpec((1, D_MODEL), lambda i: (0, 0))],
        out_specs=pl.BlockSpec((ROWS, D_MODEL), lambda i: (i, 0)),
        out_shape=jax.ShapeDtypeStruct((t, D_MODEL), F32),
        compiler_params=_params(("parallel",), 32),
        name="final_norm",
    )(x, g)


def _time_major(x):
    b, s, d = x.shape
    return x.transpose(1, 0, 2).reshape(s * b, d)


def _batch_major(y, b, s):
    return y.reshape(s, b, y.shape[-1]).transpose(1, 0, 2)


def kernel(x_prompt, x_sample, state_ret, state_ssm_re, state_ssm_im, state_conv, norm_mix, w_in, ret_norm, w_ret_out, ssm_a_re, ssm_a_im, ssm_b_re, ssm_b_im, ssm_c_re, ssm_c_im, ssm_d, ssm_log_dt, w_glu_a, w_glu_b, conv_w, conv_b, w_conv_out, w_mix_out, norm_ffn, peer_wq, peer_k1, peer_k2, peer_u, peer_v, norm_final):
    bp, sp, _ = x_prompt.shape
    bs, ss, _ = x_sample.shape
    tp, ts = bp * sp, bs * ss
    depth = w_in.shape[0]
    assert tp % ROWS == 0 and ts == ROWS and ROWS % bp == 0 and ROWS // bp == math.gcd(sp, RET_CHUNK)

    xs = [_time_major(x_prompt), _time_major(x_sample)]
    batches = (bp, bs)
    pos = (jnp.repeat(jnp.arange(sp, dtype=F32), bp), jnp.repeat(PAST_LEN + jnp.arange(ss, dtype=F32), bs))
    ret_tabs = [_retention_tables(p, b) for p, b in zip(pos, batches)]

    lg = depth * SSM_G
    abre, abim, bbre, bbim = _s5_discretise(
        ssm_a_re.reshape(lg, SSM_P), ssm_a_im.reshape(lg, SSM_P), ssm_log_dt.reshape(lg, 1),
        ssm_b_re.transpose(0, 1, 3, 2).reshape(lg, SSM_GC, SSM_P),
        ssm_b_im.transpose(0, 1, 3, 2).reshape(lg, SSM_GC, SSM_P))

    hq = PEER_DQ // 2
    eye = jnp.eye(PEER_HEADS, dtype=F32)

    def keys_block_diag(k):
        return (k.transpose(1, 0, 2)[:, :, None, :] * eye[None, :, :, None]).reshape(
            PEER_NKEYS * PEER_HEADS, PEER_HEADS * hq).astype(BF16)

    states = [[[] for _ in range(4)] for _ in range(2)]
    for l in range(depth):
        sl = slice(l * SSM_G, (l + 1) * SSM_G)
        bmat = jnp.concatenate([_block_diag_slabs(bbre[sl], None), _block_diag_slabs(bbim[sl], None)],
                               axis=2).astype(BF16)
        cmat = jnp.concatenate([_block_diag_slabs(ssm_c_re[l].transpose(0, 2, 1), None),
                                _block_diag_slabs(-ssm_c_im[l].transpose(0, 2, 1), None)],
                               axis=1).astype(BF16)
        are_row = abre[sl].reshape(1, SSM_N)
        aim_row = abim[sl].reshape(1, SSM_N)
        w_in_b = w_in[l].astype(BF16)
        proj = [w.astype(BF16) for w in (w_ret_out[l], w_glu_a[l], w_glu_b[l], w_conv_out[l], w_mix_out[l])]
        wq = peer_wq[l].reshape(D_MODEL, PEER_HEADS, 2, hq).transpose(0, 2, 1, 3).reshape(D_MODEL, -1).astype(BF16)
        k1big, k2big = keys_block_diag(peer_k1[l]), keys_block_diag(peer_k2[l])
        ut, vt = peer_u[l].astype(BF16).T, peer_v[l].astype(BF16)

        for gi, batch in enumerate(batches):
            x = xs[gi]
            nblk = x.shape[0] // ROWS
            if gi == 0:
                s0 = jnp.zeros((batch, RET_HEADS, RET_DK, RET_DK), F32)
                h0re = jnp.zeros((batch, SSM_N), F32)
                h0im = jnp.zeros((batch, SSM_N), F32)
                buf0 = jnp.zeros(((CONV_K - 1) * batch, CONV_W), F32)
                bblk = batch
            else:
                s0 = state_ret[l]
                h0re = state_ssm_re[l].reshape(batch, SSM_N)
                h0im = state_ssm_im[l].reshape(batch, SSM_N)
                buf0 = state_conv[l].transpose(1, 0, 2).reshape((CONV_K - 1) * batch, CONV_W)
                bblk = 16
            z = _inproj(x, norm_mix[l][None, :], w_in_b)
            oa, s_new = _retention(z, 0, nblk, batch, bblk, ret_tabs[gi], ret_norm[l][None, :], s0)
            ys, xre, xim = _s5(z, 0, nblk, batch, bmat, cmat, are_row, aim_row, ssm_d[l][None, :], h0re, h0im)
            oc, buf = _conv(z, 0, nblk, batch, buf0, conv_w[l], conv_b[l][None, :])
            st = states[gi]
            st[0].append(s_new)
            st[1].append(xre.reshape(batch, SSM_G, SSM_P))
            st[2].append(xim.reshape(batch, SSM_G, SSM_P))
            st[3].append(buf.reshape(CONV_K - 1, batch, CONV_W).transpose(1, 0, 2))

            x1, xn = _merge(x, z, oa, ys, oc, *proj, norm_ffn[l][None, :])
            e1, e2, g = _peer_select(xn, wq, k1big, k2big)
            xs[gi] = _peer_dense(xn, e1, e2, g, ut, vt, x1)

    y_prompt = _batch_major(_final_norm(xs[0], norm_final[None, :]), bp, sp)
    y_sample = _batch_major(_final_norm(xs[1], norm_final[None, :]), bs, ss)
    (ret_p, re_p, im_p, cv_p), (ret_s, re_s, im_s, cv_s) = states
    return (y_prompt, y_sample,
            jnp.stack(ret_p), jnp.stack(ret_s),
            jnp.stack(re_p), jnp.stack(re_s),
            jnp.stack(im_p), jnp.stack(im_s),
            jnp.stack(cv_p), jnp.stack(cv_s))
```

```python
import functools
import math

import jax
import jax.numpy as jnp
from jax import lax
from jax.experimental import pallas as pl
from jax.experimental.pallas import tpu as pltpu

F32 = jnp.float32
BF16 = jnp.bfloat16

D_MODEL = 1024
DEPTH = 2
PAST_LEN = 16384
RET_HEADS = 8
RET_DK = 64
RET_W = 512
RET_CHUNK = 128
ROPE_BASE = 10000.0
SSM_W = 512
SSM_GC = 16
SSM_G = 32
SSM_P = 64
SSM_N = SSM_G * SSM_P
SSM_SLABS = 4
SSM_SLAB_N = SSM_N // SSM_SLABS
CONV_W = 512
CONV_K = 3
PROJ_W = 7168
PEER_HEADS = 8
PEER_DQ = 256
PEER_NKEYS = 128
PEER_TOPK = 16
PEER_NEXP = PEER_NKEYS ** 2
PEER_SLOTS = PEER_HEADS * PEER_TOPK
EPS = 1e-6
GELU_C = math.sqrt(2.0 / math.pi)
GELU_A = 0.044715

ROWS = 1024
LANES = 128
SUBLANES = 8
MXU_DEPTH = 256
MIB = 1024 * 1024

_CANDS = [(a, b) for a in range(PEER_TOPK) for b in range(PEER_TOPK) if (a + 1) * (b + 1) <= PEER_TOPK]


def _params(sem, vmem_mib):
    return pltpu.CompilerParams(dimension_semantics=sem, vmem_limit_bytes=vmem_mib * MIB)


def _rms(x, g):
    return x * lax.rsqrt(jnp.mean(x * x, axis=-1, keepdims=True) + EPS) * g


def _dot(a, b):
    return jnp.dot(a, b, preferred_element_type=F32)


def _dot_nt(a, b):
    return lax.dot_general(a, b, (((1,), (1,)), ((), ())), preferred_element_type=F32)


def _inproj_kernel(x_ref, g_ref, w_ref, z_ref, h_scr):
    @pl.when(pl.program_id(1) == 0)
    def _():
        h_scr[...] = _rms(x_ref[...], g_ref[...]).astype(BF16)

    z_ref[...] = _dot(h_scr[...], w_ref[...])


def _inproj(x, g, w, layer):
    t = x.shape[0]
    nb = 1024
    return pl.pallas_call(
        _inproj_kernel,
        grid=(t // ROWS, PROJ_W // nb),
        in_specs=[pl.BlockSpec((ROWS, D_MODEL), lambda i, j: (i, 0)),
                  pl.BlockSpec((1, D_MODEL), lambda i, j: (0, 0)),
                  pl.BlockSpec((None, D_MODEL, nb), lambda i, j: (layer, 0, j))],
        out_specs=pl.BlockSpec((ROWS, nb), lambda i, j: (i, j)),
        out_shape=jax.ShapeDtypeStruct((t, PROJ_W), F32),
        scratch_shapes=[pltpu.VMEM((ROWS, D_MODEL), BF16)],
        compiler_params=_params(("parallel", "arbitrary"), 40),
        name="inproj",
    )(x, g, w)


def _ret_kernel(batch, bblk, z_ref, cos_ref, sa_ref, sb_ref, qdec_ref, kdec_ref, cdec_ref, gn_ref, s0_ref,
                o_ref, s_ref, qd_scr, kd_scr, v_scr, mask_scr, oacc_scr):
    bb = pl.program_id(0)
    c = pl.program_id(1)
    steps = ROWS // batch
    nslab = RET_W // LANES

    def head_view(ref, h):
        return ref[h // 2, :, (h % 2) * RET_DK:(h % 2 + 1) * RET_DK]

    @pl.when((bb == 0) & (c == 0))
    def _():
        r = lax.broadcasted_iota(jnp.int32, (ROWS, ROWS), 0)
        cc = lax.broadcasted_iota(jnp.int32, (ROWS, ROWS), 1)
        same = (r & (batch - 1)) == (cc & (batch - 1))
        mask_scr[...] = (same & (r >= cc)).astype(F32)

    @pl.when(c == 0)
    def _():
        s_ref[...] = s0_ref[...]

    @pl.when(bb == 0)
    def _():
        cos, sa, sb = cos_ref[...], sa_ref[...], sb_ref[...]

        def rot(x):
            return x * cos + pltpu.roll(x, 32, 1) * sa + pltpu.roll(x, 96, 1) * sb

        for s in range(nslab):
            cols = slice(s * LANES, (s + 1) * LANES)
            qd_scr[s] = rot(z_ref[:, cols]) * qdec_ref[:, cols]
            kd_scr[s] = rot(z_ref[:, RET_W + s * LANES:RET_W + (s + 1) * LANES]) * kdec_ref[:, cols]
            v_scr[s] = z_ref[:, 2 * RET_W + s * LANES:2 * RET_W + (s + 1) * LANES]
        for s in range(nslab):
            outs = []
            for h in (2 * s, 2 * s + 1):
                qh = head_view(qd_scr, h).astype(BF16)
                kh = head_view(kd_scr, h).astype(BF16)
                vh = head_view(v_scr, h).astype(BF16)
                p = (_dot_nt(qh, kh) * mask_scr[...]).astype(BF16)
                outs.append(_dot(p, vh))
            oacc_scr[s] = jnp.concatenate(outs, axis=1)

    def per_seq(bl, carry):
        b = bb * bblk + bl
        rows = pl.ds(b, steps, stride=batch)
        for s in range(nslab):
            qb = qd_scr[s, rows, :]
            kb = kd_scr[s, rows, :]
            vb = v_scr[s, rows, :]
            cross = []
            for hh in range(2):
                h = 2 * s + hh
                hc = slice(hh * RET_DK, (hh + 1) * RET_DK)
                st = s_ref[bl, h]
                cross.append(_dot(qb[:, hc].astype(BF16), st.astype(BF16)))
                upd = lax.dot_general(kb[:, hc].astype(BF16), vb[:, hc].astype(BF16),
                                      (((0,), (0,)), ((), ())), preferred_element_type=F32)
                s_ref[bl, h] = (st + upd) * cdec_ref[:, h * RET_DK:(h + 1) * RET_DK]
            oacc_scr[s, rows, :] = oacc_scr[s, rows, :] + jnp.concatenate(cross, axis=1)
        return carry

    lax.fori_loop(0, bblk, per_seq, 0)

    @pl.when(bb == pl.num_programs(0) - 1)
    def _():
        normed = []
        for h in range(RET_HEADS):
            oh = head_view(oacc_scr, h)
            mu = jnp.mean(oh, axis=-1, keepdims=True)
            dlt = oh - mu
            var = jnp.mean(dlt * dlt, axis=-1, keepdims=True)
            normed.append(dlt * lax.rsqrt(var + EPS))
        o = jnp.concatenate(normed, axis=1) * gn_ref[...]
        o_ref[...] = jax.nn.silu(z_ref[:, 3 * RET_W:4 * RET_W]) * o


def _retention(z, row_blk0, nblk, batch, bblk, tabs, gn, s0, layer):
    cos, sa, sb, qdec, kdec, cdec = tabs
    nbb = batch // bblk
    st_spec = pl.BlockSpec((bblk, RET_HEADS, RET_DK, RET_DK), lambda bb, c: (bb, 0, 0, 0))
    st_in = pl.BlockSpec((None, bblk, RET_HEADS, RET_DK, RET_DK), lambda bb, c: (layer, bb, 0, 0, 0))
    const = lambda bb, c: (0, 0)
    return pl.pallas_call(
        functools.partial(_ret_kernel, batch, bblk),
        grid=(nbb, nblk),
        in_specs=[pl.BlockSpec((ROWS, 4 * RET_W), lambda bb, c: (row_blk0 + c, 0)),
                  pl.BlockSpec((ROWS, LANES), lambda bb, c: (c, 0)),
                  pl.BlockSpec((ROWS, LANES), lambda bb, c: (c, 0)),
                  pl.BlockSpec((ROWS, LANES), lambda bb, c: (c, 0)),
                  pl.BlockSpec((ROWS, RET_W), const),
                  pl.BlockSpec((ROWS, RET_W), const),
                  pl.BlockSpec((1, RET_W), const),
                  pl.BlockSpec((1, RET_W), const),
                  st_in],
        out_specs=[pl.BlockSpec((ROWS, RET_W), lambda bb, c: (c, 0)), st_spec],
        out_shape=[jax.ShapeDtypeStruct((nblk * ROWS, RET_W), F32),
                   jax.ShapeDtypeStruct((batch, RET_HEADS, RET_DK, RET_DK), F32)],
        scratch_shapes=[pltpu.VMEM((RET_W // LANES, ROWS, LANES), F32)] * 3
        + [pltpu.VMEM((ROWS, ROWS), F32), pltpu.VMEM((RET_W // LANES, ROWS, LANES), F32)],
        compiler_params=_params(("arbitrary", "arbitrary"), 56),
        name="retention",
    )(z, cos, sa, sb, qdec, kdec, cdec, gn, s0)


def _retention_tables(pos, batch):
    half = RET_DK // 2
    freqs = ROPE_BASE ** (-jnp.arange(half, dtype=F32) / half)
    ang = pos[:, None] * freqs[None, :]
    cos, sin = jnp.cos(ang), jnp.sin(ang)
    zero = jnp.zeros_like(sin)
    reps = LANES // RET_DK
    cos_t = jnp.tile(jnp.concatenate([cos, cos], axis=1), (1, reps))
    sa_t = jnp.tile(jnp.concatenate([zero, sin], axis=1), (1, reps))
    sb_t = jnp.tile(jnp.concatenate([-sin, zero], axis=1), (1, reps))
    lg = jnp.log1p(-jnp.exp2(-5.0 - jnp.arange(RET_HEADS, dtype=F32)))
    steps = ROWS // batch
    i1 = (jnp.arange(ROWS) // batch).astype(F32) + 1.0
    qdec = jnp.repeat(jnp.exp(i1[:, None] * lg[None, :]), RET_DK, axis=1)
    kdec = jnp.repeat(jnp.exp(-i1[:, None] * lg[None, :]), RET_DK, axis=1) * (RET_DK ** -0.5)
    cdec = jnp.repeat(jnp.exp(steps * lg), RET_DK)[None, :]
    return cos_t, sa_t, sb_t, qdec, kdec, cdec


def _s5_disc_kernel(are_ref, aim_ref, ldt_ref, bre_ref, bim_ref, abre_ref, abim_ref, bbre_ref, bbim_ref):
    ar, ai = are_ref[...], aim_ref[...]
    dt = jnp.exp(ldt_ref[...])
    dar, dai = dt * ar, dt * ai
    mag = jnp.exp(dar)
    abar_re, abar_im = mag * jnp.cos(dai), mag * jnp.sin(dai)
    den = ar * ar + ai * ai
    nr, ni = abar_re - 1.0, abar_im
    f_re = (nr * ar + ni * ai) / den
    f_im = (ni * ar - nr * ai) / den
    abre_ref[...] = abar_re
    abim_ref[...] = abar_im
    br, bi = bre_ref[...], bim_ref[...]
    bbre_ref[...] = f_re[:, None, :] * br - f_im[:, None, :] * bi
    bbim_ref[...] = f_re[:, None, :] * bi + f_im[:, None, :] * br


def _s5_discretise(a_re, a_im, log_dt, b_re_t, b_im_t):
    lg = a_re.shape[0]
    small = jax.ShapeDtypeStruct((lg, SSM_P), F32)
    big = jax.ShapeDtypeStruct((lg, SSM_GC, SSM_P), F32)
    return pl.pallas_call(_s5_disc_kernel, out_shape=[small, small, big, big], name="s5_disc")(
        a_re, a_im, log_dt, b_re_t, b_im_t)


def _s5_kernel(batch, u_ref, bmat_ref, cmat_ref, are_ref, aim_ref, d_ref, h0re_ref, h0im_ref,
               y_ref, xre_ref, xim_ref, x_scr):
    c = pl.program_id(0)
    steps = ROWS // batch
    half = SSM_SLAB_N

    @pl.when(c == 0)
    def _():
        xre_ref[...] = h0re_ref[...]
        xim_ref[...] = h0im_ref[...]

    u = u_ref[...]
    ub = u.astype(BF16)
    for s in range(SSM_SLABS):
        x_scr[:, 2 * half * s:2 * half * (s + 1)] = _dot(ub[:, s * LANES:(s + 1) * LANES], bmat_ref[s])

    for s in range(SSM_SLABS):
        re0 = 2 * half * s
        im0 = re0 + half
        sc = slice(half * s, half * (s + 1))
        ar = jnp.broadcast_to(are_ref[:, sc], (SUBLANES, half))
        ai = jnp.broadcast_to(aim_ref[:, sc], (SUBLANES, half))

        def row_tile(rt, carry, re0=re0, im0=im0, sc=sc, ar=ar, ai=ai):
            r0 = pl.multiple_of(rt * SUBLANES, SUBLANES)

            def step(t, x):
                xr, xi = x
                row = pl.multiple_of(t * batch + r0, SUBLANES)
                nr = ar * xr - ai * xi + x_scr[pl.ds(row, SUBLANES), re0:re0 + half]
                ni = ar * xi + ai * xr + x_scr[pl.ds(row, SUBLANES), im0:im0 + half]
                x_scr[pl.ds(row, SUBLANES), re0:re0 + half] = nr
                x_scr[pl.ds(row, SUBLANES), im0:im0 + half] = ni
                return nr, ni

            init = (xre_ref[pl.ds(r0, SUBLANES), sc], xim_ref[pl.ds(r0, SUBLANES), sc])
            xr, xi = lax.fori_loop(0, steps, step, init, unroll=8)
            xre_ref[pl.ds(r0, SUBLANES), sc] = xr
            xim_ref[pl.ds(r0, SUBLANES), sc] = xi
            return carry

        lax.fori_loop(0, batch // SUBLANES, row_tile, 0)

    ys = [_dot(x_scr[:, 2 * half * s:2 * half * (s + 1)].astype(BF16), cmat_ref[s]) for s in range(SSM_SLABS)]
    y = jnp.concatenate(ys, axis=1) + d_ref[...] * u
    y_ref[...] = jax.nn.gelu(y)


def _s5(z, row_blk0, nblk, batch, bmat, cmat, abre, abim, d, h0re, h0im):
    const2 = lambda c: (0, 0)
    const3 = lambda c: (0, 0, 0)
    st = pl.BlockSpec((batch, SSM_N), const2)
    return pl.pallas_call(
        functools.partial(_s5_kernel, batch),
        grid=(nblk,),
        in_specs=[pl.BlockSpec((ROWS, SSM_W), lambda c: (row_blk0 + c, 4)),
                  pl.BlockSpec((SSM_SLABS, LANES, 2 * SSM_SLAB_N), const3),
                  pl.BlockSpec((SSM_SLABS, 2 * SSM_SLAB_N, LANES), const3),
                  pl.BlockSpec((1, SSM_N), const2),
                  pl.BlockSpec((1, SSM_N), const2),
                  pl.BlockSpec((1, SSM_W), const2),
                  st, st],
        out_specs=[pl.BlockSpec((ROWS, SSM_W), lambda c: (c, 0)), st, st],
        out_shape=[jax.ShapeDtypeStruct((nblk * ROWS, SSM_W), F32),
                   jax.ShapeDtypeStruct((batch, SSM_N), F32),
                   jax.ShapeDtypeStruct((batch, SSM_N), F32)],
        scratch_shapes=[pltpu.VMEM((ROWS, 2 * SSM_N), F32)],
        compiler_params=_params(("arbitrary",), 48),
        name="s5",
    )(z, bmat, cmat, abre, abim, d, h0re, h0im)


def _block_diag_slabs(w, rows_inner):
    gps = SSM_G // SSM_SLABS
    eye = jnp.eye(gps, dtype=w.dtype)
    w4 = w.reshape(SSM_SLABS, gps, w.shape[1], w.shape[2])
    out = w4[:, :, :, None, :] * eye[None, :, None, :, None]
    return out.reshape(SSM_SLABS, gps * w.shape[1], gps * w.shape[2])


def _conv_kernel(batch, bg_ref, cg_ref, hc_ref, buf0_ref, w_ref, b_ref, o_ref, buf_ref, zp_scr):
    c = pl.program_id(0)
    pad = (CONV_K - 1) * batch

    @pl.when(c == 0)
    def _():
        zp_scr[0:pad, :] = buf0_ref[...]

    zc = cg_ref[...] * hc_ref[...]
    zp_scr[pad:pad + ROWS, :] = zc
    y = b_ref[...]
    for j in range(CONV_K):
        y = y + w_ref[j:j + 1, :] * zp_scr[j * batch:j * batch + ROWS, :]
    o_ref[...] = bg_ref[...] * y
    tail = zp_scr[ROWS:ROWS + pad, :]
    buf_ref[...] = tail
    zp_scr[0:pad, :] = tail


def _conv(z, row_blk0, nblk, batch, buf0, w, b):
    pad = (CONV_K - 1) * batch
    const = lambda c: (0, 0)
    return pl.pallas_call(
        functools.partial(_conv_kernel, batch),
        grid=(nblk,),
        in_specs=[pl.BlockSpec((ROWS, CONV_W), lambda c: (row_blk0 + c, 5)),
                  pl.BlockSpec((ROWS, CONV_W), lambda c: (row_blk0 + c, 6)),
                  pl.BlockSpec((ROWS, CONV_W), lambda c: (row_blk0 + c, 7)),
                  pl.BlockSpec((pad, CONV_W), const),
                  pl.BlockSpec((CONV_K, CONV_W), const),
                  pl.BlockSpec((1, CONV_W), const)],
        out_specs=[pl.BlockSpec((ROWS, CONV_W), lambda c: (c, 0)), pl.BlockSpec((pad, CONV_W), const)],
        out_shape=[jax.ShapeDtypeStruct((nblk * ROWS, CONV_W), F32),
                   jax.ShapeDtypeStruct((pad, CONV_W), F32)],
        scratch_shapes=[pltpu.VMEM((ROWS + pad, CONV_W), F32)],
        compiler_params=_params(("arbitrary",), 32),
        name="conv",
    )(z, z, z, buf0, w, b)


def _merge_kernel(x_ref, oa_ref, ys_ref, oc_ref, ga_ref, gb_ref, gc_ref, wr_ref, wa_ref, wb_ref, wc_ref,
                  wm_ref, gf_ref, x1_ref, xn_ref):
    oa = _dot(oa_ref[...].astype(BF16), wr_ref[...])
    ysb = ys_ref[...].astype(BF16)
    ob = _dot(ysb, wa_ref[...]) * jax.nn.sigmoid(_dot(ysb, wb_ref[...]))
    oc = _dot(oc_ref[...].astype(BF16), wc_ref[...])
    merged = (jax.nn.sigmoid(ga_ref[...]) * oa + jax.nn.sigmoid(gb_ref[...]) * ob
              + jax.nn.sigmoid(gc_ref[...]) * oc)
    x1 = x_ref[...] + _dot(merged.astype(BF16), wm_ref[...])
    x1_ref[...] = x1
    xn_ref[...] = _rms(x1, gf_ref[...]).astype(BF16)


def _merge(x, z, oa, ys, oc, wr, wa, wb, wc, wm, gf):
    t = x.shape[0]
    rb = 512
    row = lambda w: pl.BlockSpec((rb, w), lambda i: (i, 0))
    gate = lambda j: pl.BlockSpec((rb, D_MODEL), lambda i: (i, j))
    wsp = lambda k: pl.BlockSpec((k, D_MODEL), lambda i: (0, 0))
    return pl.pallas_call(
        _merge_kernel,
        grid=(t // rb,),
        in_specs=[row(D_MODEL), row(RET_W), row(SSM_W), row(CONV_W), gate(4), gate(5), gate(6),
                  wsp(RET_W), wsp(SSM_W), wsp(SSM_W), wsp(CONV_W), wsp(D_MODEL), wsp(1)],
        out_specs=[row(D_MODEL), row(D_MODEL)],
        out_shape=[jax.ShapeDtypeStruct((t, D_MODEL), F32), jax.ShapeDtypeStruct((t, D_MODEL), BF16)],
        compiler_params=_params(("parallel",), 48),
        name="merge",
    )(x, oa, ys, oc, z, z, z, wr, wa, wb, wc, wm, gf)


def _tree(items, combine):
    while len(items) > 1:
        nxt = [combine(items[i], items[i + 1]) for i in range(0, len(items) - 1, 2)]
        if len(items) % 2:
            nxt.append(items[-1])
        items = nxt
    return items[0]


def _first_max(x, y):
    (vx, ix), (vy, iy) = x, y
    return jnp.maximum(vx, vy), jnp.where(vx >= vy, ix, iy)


def _top16_of_keys(s_scrs, v_scrs, i_scrs):
    def body(r, carry):
        for s_scr, v_scr, i_scr in zip(s_scrs, v_scrs, i_scrs):
            m, idx = _tree([(s_scr[k], float(k)) for k in range(PEER_NKEYS)], _first_max)
            v_scr[r] = m
            i_scr[r] = idx
            for k in range(PEER_NKEYS):
                s_scr[k] = jnp.where(idx == float(k), -jnp.inf, s_scr[k])
        return carry

    lax.fori_loop(0, PEER_TOPK, body, 0)


def _select_kernel(tb, xn_ref, wq_ref, k1_ref, k2_ref, e1_ref, e2_ref, g_ref,
                   s1_scr, s2_scr, v1_scr, i1_scr, v2_scr, i2_scr, cand_scr, sc_scr, se1_scr, se2_scr):
    q = _dot(xn_ref[...], wq_ref[...]).astype(BF16)
    hq = PEER_HEADS * PEER_DQ // 2
    s1 = _dot_nt(k1_ref[...], q[:, :hq])
    s2 = _dot_nt(k2_ref[...], q[:, hq:])
    flats = [float(a * PEER_TOPK + b) for a, b in _CANDS]
    for lt in range(tb // LANES):
        lanes = slice(lt * LANES, (lt + 1) * LANES)
        s1_scr[...] = s1[:, lanes].reshape(PEER_NKEYS, SUBLANES, LANES)
        s2_scr[...] = s2[:, lanes].reshape(PEER_NKEYS, SUBLANES, LANES)
        _top16_of_keys((s1_scr, s2_scr), (v1_scr, v2_scr), (i1_scr, i2_scr))
        for n, (a, b) in enumerate(_CANDS):
            cand_scr[n] = v1_scr[a] + v2_scr[b]

        def body(r, carry):
            m, flat = _tree([(cand_scr[n], f) for n, f in enumerate(flats)], _first_max)
            for n, f in enumerate(flats):
                cand_scr[n] = jnp.where(flat == f, -jnp.inf, cand_scr[n])
            fa = jnp.floor(flat * (1.0 / PEER_TOPK))
            fb = flat - fa * PEER_TOPK
            sc_scr[r] = m
            se1_scr[r] = _tree([jnp.where(fa == float(a), i1_scr[a], 0.0) for a in range(PEER_TOPK)], jnp.add)
            se2_scr[r] = _tree([jnp.where(fb == float(b), i2_scr[b], 0.0) for b in range(PEER_TOPK)], jnp.add)
            return carry

        lax.fori_loop(0, PEER_TOPK, body, 0)
        sc = sc_scr[...]
        ex = jnp.exp(sc - jnp.max(sc, axis=0, keepdims=True))
        gate = ex / jnp.sum(ex, axis=0, keepdims=True)
        rows = slice(lt * LANES, (lt + 1) * LANES)
        g_ref[rows, :] = gate.reshape(PEER_SLOTS, LANES).T
        e1_ref[rows, :] = se1_scr[...].reshape(PEER_SLOTS, LANES).T
        e2_ref[rows, :] = se2_scr[...].reshape(PEER_SLOTS, LANES).T


def _peer_select(xn, wq, k1big, k2big):
    t = xn.shape[0]
    tb = 256
    hq = PEER_HEADS * PEER_DQ // 2
    nk = PEER_NKEYS * PEER_HEADS
    const = lambda i: (0, 0)
    row = lambda dt: jax.ShapeDtypeStruct((t, PEER_SLOTS), dt)
    vec = lambda n: pltpu.VMEM((n, SUBLANES, LANES), F32)
    return pl.pallas_call(
        functools.partial(_select_kernel, tb),
        grid=(t // tb,),
        in_specs=[pl.BlockSpec((tb, D_MODEL), lambda i: (i, 0)),
                  pl.BlockSpec((D_MODEL, 2 * hq), const),
                  pl.BlockSpec((nk, hq), const),
                  pl.BlockSpec((nk, hq), const)],
        out_specs=[pl.BlockSpec((tb, PEER_SLOTS), lambda i: (i, 0))] * 3,
        out_shape=[row(F32), row(F32), row(F32)],
        scratch_shapes=[vec(PEER_NKEYS), vec(PEER_NKEYS), vec(PEER_TOPK), vec(PEER_TOPK), vec(PEER_TOPK), vec(PEER_TOPK),
                        vec(len(_CANDS)), vec(PEER_TOPK), vec(PEER_TOPK), vec(PEER_TOPK)],
        compiler_params=_params(("parallel",), 40),
        name="peer_select",
    )(xn, wq, k1big, k2big)


def _peer_kernel(tb, eb, stride, xn_ref, e1_ref, e2_ref, g_ref, ut_ref, v_ref, x1_ref, out_ref, m_scr):
    e = pl.program_id(1)
    nk1 = eb // PEER_NKEYS

    @pl.when(e == 0)
    def _():
        out_ref[...] = x1_ref[...]

    @pl.when(e == 0)
    def _():
        key = lax.broadcasted_iota(jnp.int32, (PEER_NKEYS, PEER_SLOTS), 0).astype(F32)

        def token(t, carry):
            e1 = e1_ref[pl.ds(t, 1), :]
            e2 = e2_ref[pl.ds(t, 1), :]
            gt = 0.5 * g_ref[pl.ds(t, 1), :]
            a_t = jnp.where(key == e1, gt, 0.0).astype(BF16)
            b_t = jnp.where(key == e2, 1.0, 0.0).astype(BF16)
            m_scr[pl.ds(t, PEER_NKEYS, stride=stride), :] = _dot_nt(a_t, b_t)
            return carry

        lax.fori_loop(0, tb, token, 0, unroll=64)

    s = _dot(xn_ref[...], ut_ref[...])
    t = jnp.tanh(s * (GELU_C + (GELU_C * GELU_A) * (s * s)))
    k1 = e * nk1
    gates = [m_scr[pl.ds(pl.multiple_of((k1 + i) * stride, SUBLANES), tb), :] for i in range(nk1)]
    w = ((s + s * t) * jnp.concatenate(gates, axis=1)).astype(BF16)
    out_ref[...] += _dot(w, v_ref[...])


def _peer_dense(xn, e1, e2, g, ut, v, x1, layer):
    t = xn.shape[0]
    tb, eb = 512, 1024
    stride = tb + SUBLANES
    once = pl.Buffered(1)
    tok = lambda w: pl.BlockSpec((tb, w), lambda i, e: (i, 0), pipeline_mode=once)
    tab = pl.BlockSpec((None, eb, D_MODEL), lambda i, e: (layer, e, 0))
    return pl.pallas_call(
        functools.partial(_peer_kernel, tb, eb, stride),
        grid=(t // tb, PEER_NEXP // eb),
        in_specs=[tok(D_MODEL), tok(PEER_SLOTS), tok(PEER_SLOTS), tok(PEER_SLOTS),
                  pl.BlockSpec((None, D_MODEL, eb), lambda i, e: (layer, 0, e)), tab, tok(D_MODEL)],
        out_specs=pl.BlockSpec((tb, D_MODEL), lambda i, e: (i, 0)),
        out_shape=jax.ShapeDtypeStruct((t, D_MODEL), F32),
        scratch_shapes=[pltpu.VMEM((PEER_NKEYS * stride, PEER_NKEYS), F32)],
        compiler_params=_params(("parallel", "arbitrary"), 60),
        name="peer_dense",
    )(xn, e1, e2, g, ut, v, x1)


def _norm_kernel(x_ref, g_ref, y_ref):
    y_ref[...] = _rms(x_ref[...], g_ref[...])


def _final_norm(x, g):
    t = x.shape[0]
    return pl.pallas_call(
        _norm_kernel,
        grid=(t // ROWS,),
        in_specs=[pl.BlockSpec((ROWS, D_MODEL), lambda i: (i, 0)), pl.BlockSpec((1, D_MODEL), lambda i: (0, 0))],
        out_specs=pl.BlockSpec((ROWS, D_MODEL), lambda i: (i, 0)),
        out_shape=jax.ShapeDtypeStruct((t, D_MODEL), F32),
        compiler_params=_params(("parallel",), 32),
        name="final_norm",
    )(x, g)


def _time_major(x):
    b, s, d = x.shape
    return x.transpose(1, 0, 2).reshape(s * b, d)


def _batch_major(y, b, s):
    return y.reshape(s, b, y.shape[-1]).transpose(1, 0, 2)


def kernel(x_prompt, x_sample, state_ret, state_ssm_re, state_ssm_im, state_conv, norm_mix, w_in, ret_norm, w_ret_out, ssm_a_re, ssm_a_im, ssm_b_re, ssm_b_im, ssm_c_re, ssm_c_im, ssm_d, ssm_log_dt, w_glu_a, w_glu_b, conv_w, conv_b, w_conv_out, w_mix_out, norm_ffn, peer_wq, peer_k1, peer_k2, peer_u, peer_v, norm_final):
    bp, sp, _ = x_prompt.shape
    bs, ss, _ = x_sample.shape
    tp, ts = bp * sp, bs * ss
    depth = w_in.shape[0]
    assert tp % ROWS == 0 and ts == ROWS and ROWS % bp == 0 and ROWS // bp == math.gcd(sp, RET_CHUNK)

    xs = [_time_major(x_prompt), _time_major(x_sample)]
    batches = (bp, bs)
    pos = (jnp.repeat(jnp.arange(sp, dtype=F32), bp), jnp.repeat(PAST_LEN + jnp.arange(ss, dtype=F32), bs))
    ret_tabs = [_retention_tables(p, b) for p, b in zip(pos, batches)]

    lg = depth * SSM_G
    abre, abim, bbre, bbim = _s5_discretise(
        ssm_a_re.reshape(lg, SSM_P), ssm_a_im.reshape(lg, SSM_P), ssm_log_dt.reshape(lg, 1),
        ssm_b_re.transpose(0, 1, 3, 2).reshape(lg, SSM_GC, SSM_P),
        ssm_b_im.transpose(0, 1, 3, 2).reshape(lg, SSM_GC, SSM_P))

    hq = PEER_DQ // 2
    eye = jnp.eye(PEER_HEADS, dtype=F32)

    def keys_block_diag(k):
        return (k.transpose(1, 0, 2)[:, :, None, :] * eye[None, :, :, None]).reshape(
            PEER_NKEYS * PEER_HEADS, PEER_HEADS * hq).astype(BF16)

    w_in_b = w_in.astype(BF16)
    ut = peer_u.astype(BF16).transpose(0, 2, 1)
    vt = peer_v.astype(BF16)
    zero_ret = jnp.zeros((1, bp, RET_HEADS, RET_DK, RET_DK), F32)

    states = [[[] for _ in range(4)] for _ in range(2)]
    for l in range(depth):
        sl = slice(l * SSM_G, (l + 1) * SSM_G)
        bmat = jnp.concatenate([_block_diag_slabs(bbre[sl], None), _block_diag_slabs(bbim[sl], None)],
                               axis=2).astype(BF16)
        cmat = jnp.concatenate([_block_diag_slabs(ssm_c_re[l].transpose(0, 2, 1), None),
                                _block_diag_slabs(-ssm_c_im[l].transpose(0, 2, 1), None)],
                               axis=1).astype(BF16)
        are_row = abre[sl].reshape(1, SSM_N)
        aim_row = abim[sl].reshape(1, SSM_N)
        proj =[w.astype(BF16) for w in (w_ret_out[l], w_glu_a[l], w_glu_b[l], w_conv_out[l], w_mix_out[l])]
        wq = peer_wq[l].reshape(D_MODEL, PEER_HEADS, 2, hq).transpose(0, 2, 1, 3).reshape(D_MODEL, -1).astype(BF16)
        k1big, k2big = keys_block_diag(peer_k1[l]), keys_block_diag(peer_k2[l])

        for gi, batch in enumerate(batches):
            x = xs[gi]
            nblk = x.shape[0] // ROWS
            if gi == 0:
                s0, s0_layer = zero_ret, 0
                h0re = jnp.zeros((batch, SSM_N), F32)
                h0im = jnp.zeros((batch, SSM_N), F32)
                buf0 = jnp.zeros(((CONV_K - 1) * batch, CONV_W), F32)
                bblk = batch
            else:
                s0, s0_layer = state_ret, l
                h0re = state_ssm_re[l].reshape(batch, SSM_N)
                h0im = state_ssm_im[l].reshape(batch, SSM_N)
                buf0 = state_conv[l].transpose(1, 0, 2).reshape((CONV_K - 1) * batch, CONV_W)
                bblk = 16
            z = _inproj(x, norm_mix[l][None, :], w_in_b, l)
            oa, s_new = _retention(z, 0, nblk, batch, bblk, ret_tabs[gi], ret_norm[l][None, :], s0, s0_layer)
            ys, xre, xim = _s5(z, 0, nblk, batch, bmat, cmat, are_row, aim_row, ssm_d[l][None, :], h0re, h0im)
            oc, buf = _conv(z, 0, nblk, batch, buf0, conv_w[l], conv_b[l][None, :])
            st = states[gi]
            st[0].append(s_new)
            st[1].append(xre.reshape(batch, SSM_G, SSM_P))
            st[2].append(xim.reshape(batch, SSM_G, SSM_P))
            st[3].append(buf.reshape(CONV_K - 1, batch, CONV_W).transpose(1, 0, 2))

            x1, xn = _merge(x, z, oa, ys, oc, *proj, norm_ffn[l][None, :])
            e1, e2, g = _peer_select(xn, wq, k1big, k2big)
            xs[gi] = _peer_dense(xn, e1, e2, g, ut, vt, x1, l)

    y_prompt = _batch_major(_final_norm(xs[0], norm_final[None, :]), bp, sp)
    y_sample = _batch_major(_final_norm(xs[1], norm_final[None, :]), bs, ss)
    (ret_p, re_p, im_p, cv_p), (ret_s, re_s, im_s, cv_s) = states
    return (y_prompt, y_sample,
            jnp.stack(ret_p), jnp.stack(ret_s),
            jnp.stack(re_p), jnp.stack(re_s),
            jnp.stack(im_p), jnp.stack(im_s),
            jnp.stack(cv_p), jnp.stack(cv_s))
```

```python
import functools
import math

import jax
import jax.numpy as jnp
from jax import lax
from jax.experimental import pallas as pl
from jax.experimental.pallas import tpu as pltpu

F32 = jnp.float32
BF16 = jnp.bfloat16

D_MODEL = 1024
DEPTH = 2
PAST_LEN = 16384
RET_HEADS = 8
RET_DK = 64
RET_W = 512
RET_CHUNK = 128
ROPE_BASE = 10000.0
SSM_W = 512
SSM_GC = 16
SSM_G = 32
SSM_P = 64
SSM_N = SSM_G * SSM_P
SSM_SLABS = 4
SSM_SLAB_N = SSM_N // SSM_SLABS
CONV_W = 512
CONV_K = 3
PROJ_W = 7168
PEER_HEADS = 8
PEER_DQ = 256
PEER_NKEYS = 128
PEER_TOPK = 16
PEER_NEXP = PEER_NKEYS ** 2
PEER_SLOTS = PEER_HEADS * PEER_TOPK
TOPK_GROUP = 8
EPS = 1e-6
GELU_C = math.sqrt(2.0 / math.pi)
GELU_A = 0.044715

ROWS = 1024
LANES = 128
SUBLANES = 8
MXU_DEPTH = 256
MIB = 1024 * 1024

_CANDS = [(a, b) for a in range(PEER_TOPK) for b in range(PEER_TOPK) if (a + 1) * (b + 1) <= PEER_TOPK]


def _params(sem, vmem_mib):
    return pltpu.CompilerParams(dimension_semantics=sem, vmem_limit_bytes=vmem_mib * MIB)


def _rms(x, g):
    return x * lax.rsqrt(jnp.mean(x * x, axis=-1, keepdims=True) + EPS) * g


def _dot(a, b):
    return jnp.dot(a, b, preferred_element_type=F32)


def _dot_nt(a, b):
    return lax.dot_general(a, b, (((1,), (1,)), ((), ())), preferred_element_type=F32)


def _inproj_kernel(x_ref, g_ref, w_ref, z_ref, h_scr):
    @pl.when(pl.program_id(1) == 0)
    def _():
        h_scr[...] = _rms(x_ref[...], g_ref[...]).astype(BF16)

    z_ref[...] = _dot(h_scr[...], w_ref[...])


def _inproj(x, g, w, layer):
    t = x.shape[0]
    nb = 1024
    return pl.pallas_call(
        _inproj_kernel,
        grid=(t // ROWS, PROJ_W // nb),
        in_specs=[pl.BlockSpec((ROWS, D_MODEL), lambda i, j: (i, 0)),
                  pl.BlockSpec((1, D_MODEL), lambda i, j: (0, 0)),
                  pl.BlockSpec((None, D_MODEL, nb), lambda i, j: (layer, 0, j))],
        out_specs=pl.BlockSpec((ROWS, nb), lambda i, j: (i, j)),
        out_shape=jax.ShapeDtypeStruct((t, PROJ_W), F32),
        scratch_shapes=[pltpu.VMEM((ROWS, D_MODEL), BF16)],
        compiler_params=_params(("parallel", "arbitrary"), 40),
        name="inproj",
    )(x, g, w)


def _ret_kernel(batch, bblk, z_ref, cos_ref, sa_ref, sb_ref, qdec_ref, kdec_ref, cdec_ref, gn_ref, s0_ref,
                o_ref, s_ref, qd_scr, kd_scr, v_scr, mask_scr, oacc_scr):
    bb = pl.program_id(0)
    c = pl.program_id(1)
    steps = ROWS // batch
    nslab = RET_W // LANES

    def head_view(ref, h):
        return ref[h // 2, :, (h % 2) * RET_DK:(h % 2 + 1) * RET_DK]

    @pl.when((bb == 0) & (c == 0))
    def _():
        r = lax.broadcasted_iota(jnp.int32, (ROWS, ROWS), 0)
        cc = lax.broadcasted_iota(jnp.int32, (ROWS, ROWS), 1)
        same = (r & (batch - 1)) == (cc & (batch - 1))
        mask_scr[...] = (same & (r >= cc)).astype(F32)

    @pl.when(c == 0)
    def _():
        s_ref[...] = s0_ref[...]

    @pl.when(bb == 0)
    def _():
        cos, sa, sb = cos_ref[...], sa_ref[...], sb_ref[...]

        def rot(x):
            return x * cos + pltpu.roll(x, 32, 1) * sa + pltpu.roll(x, 96, 1) * sb

        for s in range(nslab):
            cols = slice(s * LANES, (s + 1) * LANES)
            qd_scr[s] = rot(z_ref[:, cols]) * qdec_ref[:, cols]
            kd_scr[s] = rot(z_ref[:, RET_W + s * LANES:RET_W + (s + 1) * LANES]) * kdec_ref[:, cols]
            v_scr[s] = z_ref[:, 2 * RET_W + s * LANES:2 * RET_W + (s + 1) * LANES]
        for s in range(nslab):
            outs = []
            for h in (2 * s, 2 * s + 1):
                qh = head_view(qd_scr, h).astype(BF16)
                kh = head_view(kd_scr, h).astype(BF16)
                vh = head_view(v_scr, h).astype(BF16)
                p = (_dot_nt(qh, kh) * mask_scr[...]).astype(BF16)
                outs.append(_dot(p, vh))
            oacc_scr[s] = jnp.concatenate(outs, axis=1)

    def per_seq(bl, carry):
        b = bb * bblk + bl
        rows = pl.ds(b, steps, stride=batch)
        for s in range(nslab):
            qb = qd_scr[s, rows, :]
            kb = kd_scr[s, rows, :]
            vb = v_scr[s, rows, :]
            cross = []
            for hh in range(2):
                h = 2 * s + hh
                hc = slice(hh * RET_DK, (hh + 1) * RET_DK)
                st = s_ref[bl, h]
                cross.append(_dot(qb[:, hc].astype(BF16), st.astype(BF16)))
                upd = lax.dot_general(kb[:, hc].astype(BF16), vb[:, hc].astype(BF16),
                                      (((0,), (0,)), ((), ())), preferred_element_type=F32)
                s_ref[bl, h] = (st + upd) * cdec_ref[:, h * RET_DK:(h + 1) * RET_DK]
            oacc_scr[s, rows, :] = oacc_scr[s, rows, :] + jnp.concatenate(cross, axis=1)
        return carry

    lax.fori_loop(0, bblk, per_seq, 0)

    @pl.when(bb == pl.num_programs(0) - 1)
    def _():
        normed = []
        for h in range(RET_HEADS):
            oh = head_view(oacc_scr, h)
            mu = jnp.mean(oh, axis=-1, keepdims=True)
            dlt = oh - mu
            var = jnp.mean(dlt * dlt, axis=-1, keepdims=True)
            normed.append(dlt * lax.rsqrt(var + EPS))
        o = jnp.concatenate(normed, axis=1) * gn_ref[...]
        o_ref[...] = jax.nn.silu(z_ref[:, 3 * RET_W:4 * RET_W]) * o


def _retention(z, row_blk0, nblk, batch, bblk, tabs, gn, s0, layer):
    cos, sa, sb, qdec, kdec, cdec = tabs
    nbb = batch // bblk
    st_spec = pl.BlockSpec((bblk, RET_HEADS, RET_DK, RET_DK), lambda bb, c: (bb, 0, 0, 0))
    st_in = pl.BlockSpec((None, bblk, RET_HEADS, RET_DK, RET_DK), lambda bb, c: (layer, bb, 0, 0, 0))
    const = lambda bb, c: (0, 0)
    return pl.pallas_call(
        functools.partial(_ret_kernel, batch, bblk),
        grid=(nbb, nblk),
        in_specs=[pl.BlockSpec((ROWS, 4 * RET_W), lambda bb, c: (row_blk0 + c, 0)),
                  pl.BlockSpec((ROWS, LANES), lambda bb, c: (c, 0)),
                  pl.BlockSpec((ROWS, LANES), lambda bb, c: (c, 0)),
                  pl.BlockSpec((ROWS, LANES), lambda bb, c: (c, 0)),
                  pl.BlockSpec((ROWS, RET_W), const),
                  pl.BlockSpec((ROWS, RET_W), const),
                  pl.BlockSpec((1, RET_W), const),
                  pl.BlockSpec((1, RET_W), const),
                  st_in],
        out_specs=[pl.BlockSpec((ROWS, RET_W), lambda bb, c: (c, 0)), st_spec],
        out_shape=[jax.ShapeDtypeStruct((nblk * ROWS, RET_W), F32),
                   jax.ShapeDtypeStruct((batch, RET_HEADS, RET_DK, RET_DK), F32)],
        scratch_shapes=[pltpu.VMEM((RET_W // LANES, ROWS, LANES), F32)] * 3
        + [pltpu.VMEM((ROWS, ROWS), F32), pltpu.VMEM((RET_W // LANES, ROWS, LANES), F32)],
        compiler_params=_params(("arbitrary", "arbitrary"), 56),
        name="retention",
    )(z, cos, sa, sb, qdec, kdec, cdec, gn, s0)


def _retention_tables(pos, batch):
    half = RET_DK // 2
    freqs = ROPE_BASE ** (-jnp.arange(half, dtype=F32) / half)
    ang = pos[:, None] * freqs[None, :]
    cos, sin = jnp.cos(ang), jnp.sin(ang)
    zero = jnp.zeros_like(sin)
    reps = LANES // RET_DK
    cos_t = jnp.tile(jnp.concatenate([cos, cos], axis=1), (1, reps))
    sa_t = jnp.tile(jnp.concatenate([zero, sin], axis=1), (1, reps))
    sb_t = jnp.tile(jnp.concatenate([-sin, zero], axis=1), (1, reps))
    lg = jnp.log1p(-jnp.exp2(-5.0 - jnp.arange(RET_HEADS, dtype=F32)))
    steps = ROWS // batch
    i1 = (jnp.arange(ROWS) // batch).astype(F32) + 1.0
    qdec = jnp.repeat(jnp.exp(i1[:, None] * lg[None, :]), RET_DK, axis=1)
    kdec = jnp.repeat(jnp.exp(-i1[:, None] * lg[None, :]), RET_DK, axis=1) * (RET_DK ** -0.5)
    cdec = jnp.repeat(jnp.exp(steps * lg), RET_DK)[None, :]
    return cos_t, sa_t, sb_t, qdec, kdec, cdec


def _s5_disc_kernel(are_ref, aim_ref, ldt_ref, bre_ref, bim_ref, abre_ref, abim_ref, bbre_ref, bbim_ref):
    ar, ai = are_ref[...], aim_ref[...]
    dt = jnp.exp(ldt_ref[...])
    dar, dai = dt * ar, dt * ai
    mag = jnp.exp(dar)
    abar_re, abar_im = mag * jnp.cos(dai), mag * jnp.sin(dai)
    den = ar * ar + ai * ai
    nr, ni = abar_re - 1.0, abar_im
    f_re = (nr * ar + ni * ai) / den
    f_im = (ni * ar - nr * ai) / den
    abre_ref[...] = abar_re
    abim_ref[...] = abar_im
    br, bi = bre_ref[...], bim_ref[...]
    bbre_ref[...] = f_re[:, None, :] * br - f_im[:, None, :] * bi
    bbim_ref[...] = f_re[:, None, :] * bi + f_im[:, None, :] * br


def _s5_discretise(a_re, a_im, log_dt, b_re_t, b_im_t):
    lg = a_re.shape[0]
    small = jax.ShapeDtypeStruct((lg, SSM_P), F32)
    big = jax.ShapeDtypeStruct((lg, SSM_GC, SSM_P), F32)
    return pl.pallas_call(_s5_disc_kernel, out_shape=[small, small, big, big], name="s5_disc")(
        a_re, a_im, log_dt, b_re_t, b_im_t)


def _s5_kernel(batch, u_ref, bmat_ref, cmat_ref, are_ref, aim_ref, d_ref, h0re_ref, h0im_ref,
               y_ref, xre_ref, xim_ref, x_scr):
    c = pl.program_id(0)
    steps = ROWS // batch
    half = SSM_SLAB_N

    @pl.when(c == 0)
    def _():
        xre_ref[...] = h0re_ref[...]
        xim_ref[...] = h0im_ref[...]

    u = u_ref[...]
    ub = u.astype(BF16)
    for s in range(SSM_SLABS):
        x_scr[:, 2 * half * s:2 * half * (s + 1)] = _dot(ub[:, s * LANES:(s + 1) * LANES], bmat_ref[s])

    for s in range(SSM_SLABS):
        re0 = 2 * half * s
        im0 = re0 + half
        sc = slice(half * s, half * (s + 1))
        ar = jnp.broadcast_to(are_ref[:, sc], (SUBLANES, half))
        ai = jnp.broadcast_to(aim_ref[:, sc], (SUBLANES, half))

        def row_tile(rt, carry, re0=re0, im0=im0, sc=sc, ar=ar, ai=ai):
            r0 = pl.multiple_of(rt * SUBLANES, SUBLANES)

            def step(t, x):
                xr, xi = x
                row = pl.multiple_of(t * batch + r0, SUBLANES)
                nr = ar * xr - ai * xi + x_scr[pl.ds(row, SUBLANES), re0:re0 + half]
                ni = ar * xi + ai * xr + x_scr[pl.ds(row, SUBLANES), im0:im0 + half]
                x_scr[pl.ds(row, SUBLANES), re0:re0 + half] = nr
                x_scr[pl.ds(row, SUBLANES), im0:im0 + half] = ni
                return nr, ni

            init = (xre_ref[pl.ds(r0, SUBLANES), sc], xim_ref[pl.ds(r0, SUBLANES), sc])
            xr, xi = lax.fori_loop(0, steps, step, init, unroll=8)
            xre_ref[pl.ds(r0, SUBLANES), sc] = xr
            xim_ref[pl.ds(r0, SUBLANES), sc] = xi
            return carry

        lax.fori_loop(0, batch // SUBLANES, row_tile, 0)

    ys = [_dot(x_scr[:, 2 * half * s:2 * half * (s + 1)].astype(BF16), cmat_ref[s]) for s in range(SSM_SLABS)]
    y = jnp.concatenate(ys, axis=1) + d_ref[...] * u
    y_ref[...] = jax.nn.gelu(y)


def _s5(z, row_blk0, nblk, batch, bmat, cmat, abre, abim, d, h0re, h0im):
    const2 = lambda c: (0, 0)
    const3 = lambda c: (0, 0, 0)
    st = pl.BlockSpec((batch, SSM_N), const2)
    return pl.pallas_call(
        functools.partial(_s5_kernel, batch),
        grid=(nblk,),
        in_specs=[pl.BlockSpec((ROWS, SSM_W), lambda c: (row_blk0 + c, 4)),
                  pl.BlockSpec((SSM_SLABS, LANES, 2 * SSM_SLAB_N), const3),
                  pl.BlockSpec((SSM_SLABS, 2 * SSM_SLAB_N, LANES), const3),
                  pl.BlockSpec((1, SSM_N), const2),
                  pl.BlockSpec((1, SSM_N), const2),
                  pl.BlockSpec((1, SSM_W), const2),
                  st, st],
        out_specs=[pl.BlockSpec((ROWS, SSM_W), lambda c: (c, 0)), st, st],
        out_shape=[jax.ShapeDtypeStruct((nblk * ROWS, SSM_W), F32),
                   jax.ShapeDtypeStruct((batch, SSM_N), F32),
                   jax.ShapeDtypeStruct((batch, SSM_N), F32)],
        scratch_shapes=[pltpu.VMEM((ROWS, 2 * SSM_N), F32)],
        compiler_params=_params(("arbitrary",), 48),
        name="s5",
    )(z, bmat, cmat, abre, abim, d, h0re, h0im)


def _block_diag_slabs(w, rows_inner):
    gps = SSM_G // SSM_SLABS
    eye = jnp.eye(gps, dtype=w.dtype)
    w4 = w.reshape(SSM_SLABS, gps, w.shape[1], w.shape[2])
    out = w4[:, :, :, None, :] * eye[None, :, None, :, None]
    return out.reshape(SSM_SLABS, gps * w.shape[1], gps * w.shape[2])


def _conv_kernel(batch, bg_ref, cg_ref, hc_ref, buf0_ref, w_ref, b_ref, o_ref, buf_ref, zp_scr):
    c = pl.program_id(0)
    pad = (CONV_K - 1) * batch

    @pl.when(c == 0)
    def _():
        zp_scr[0:pad, :] = buf0_ref[...]

    zc = cg_ref[...] * hc_ref[...]
    zp_scr[pad:pad + ROWS, :] = zc
    y = b_ref[...]
    for j in range(CONV_K):
        y = y + w_ref[j:j + 1, :] * zp_scr[j * batch:j * batch + ROWS, :]
    o_ref[...] = bg_ref[...] * y
    tail = zp_scr[ROWS:ROWS + pad, :]
    buf_ref[...] = tail
    zp_scr[0:pad, :] = tail


def _conv(z, row_blk0, nblk, batch, buf0, w, b):
    pad = (CONV_K - 1) * batch
    const = lambda c: (0, 0)
    return pl.pallas_call(
        functools.partial(_conv_kernel, batch),
        grid=(nblk,),
        in_specs=[pl.BlockSpec((ROWS, CONV_W), lambda c: (row_blk0 + c, 5)),
                  pl.BlockSpec((ROWS, CONV_W), lambda c: (row_blk0 + c, 6)),
                  pl.BlockSpec((ROWS, CONV_W), lambda c: (row_blk0 + c, 7)),
                  pl.BlockSpec((pad, CONV_W), const),
                  pl.BlockSpec((CONV_K, CONV_W), const),
                  pl.BlockSpec((1, CONV_W), const)],
        out_specs=[pl.BlockSpec((ROWS, CONV_W), lambda c: (c, 0)), pl.BlockSpec((pad, CONV_W), const)],
        out_shape=[jax.ShapeDtypeStruct((nblk * ROWS, CONV_W), F32),
                   jax.ShapeDtypeStruct((pad, CONV_W), F32)],
        scratch_shapes=[pltpu.VMEM((ROWS + pad, CONV_W), F32)],
        compiler_params=_params(("arbitrary",), 32),
        name="conv",
    )(z, z, z, buf0, w, b)


def _merge_kernel(x_ref, oa_ref, ys_ref, oc_ref, ga_ref, gb_ref, gc_ref, wr_ref, wa_ref, wb_ref, wc_ref,
                  wm_ref, gf_ref, x1_ref, xn_ref):
    oa = _dot(oa_ref[...].astype(BF16), wr_ref[...])
    ysb = ys_ref[...].astype(BF16)
    ob = _dot(ysb, wa_ref[...]) * jax.nn.sigmoid(_dot(ysb, wb_ref[...]))
    oc = _dot(oc_ref[...].astype(BF16), wc_ref[...])
    merged = (jax.nn.sigmoid(ga_ref[...]) * oa + jax.nn.sigmoid(gb_ref[...]) * ob
              + jax.nn.sigmoid(gc_ref[...]) * oc)
    x1 = x_ref[...] + _dot(merged.astype(BF16), wm_ref[...])
    x1_ref[...] = x1
    xn_ref[...] = _rms(x1, gf_ref[...]).astype(BF16)


def _merge(x, z, oa, ys, oc, wr, wa, wb, wc, wm, gf):
    t = x.shape[0]
    rb = 512
    row = lambda w: pl.BlockSpec((rb, w), lambda i: (i, 0))
    gate = lambda j: pl.BlockSpec((rb, D_MODEL), lambda i: (i, j))
    wsp = lambda k: pl.BlockSpec((k, D_MODEL), lambda i: (0, 0))
    return pl.pallas_call(
        _merge_kernel,
        grid=(t // rb,),
        in_specs=[row(D_MODEL), row(RET_W), row(SSM_W), row(CONV_W), gate(4), gate(5), gate(6),
                  wsp(RET_W), wsp(SSM_W), wsp(SSM_W), wsp(CONV_W), wsp(D_MODEL), wsp(1)],
        out_specs=[row(D_MODEL), row(D_MODEL)],
        out_shape=[jax.ShapeDtypeStruct((t, D_MODEL), F32), jax.ShapeDtypeStruct((t, D_MODEL), BF16)],
        compiler_params=_params(("parallel",), 48),
        name="merge",
    )(x, oa, ys, oc, z, z, z, wr, wa, wb, wc, wm, gf)


def _tree(items, combine):
    while len(items) > 1:
        nxt = [combine(items[i], items[i + 1]) for i in range(0, len(items) - 1, 2)]
        if len(items) % 2:
            nxt.append(items[-1])
        items = nxt
    return items[0]


def _first_max(x, y):
    (vx, ix), (vy, iy) = x, y
    return jnp.maximum(vx, vy), jnp.where(vx >= vy, ix, iy)


def _top16_of_keys(s_scrs, gv_scrs, gi_scrs, v_scrs, i_scrs):
    grp = TOPK_GROUP
    ngrp = PEER_NKEYS // grp
    nbits = ngrp.bit_length() - 1

    for s_scr, gv, gi in zip(s_scrs, gv_scrs, gi_scrs):
        for g in range(ngrp):
            gv[g], gi[g] = _tree([(s_scr[g * grp + p], float(g * grp + p)) for p in range(grp)], _first_max)

    def body(r, carry):
        for s_scr, gv, gi, v_scr, i_scr in zip(s_scrs, gv_scrs, gi_scrs, v_scrs, i_scrs):
            m, idx = _tree([(gv[g], gi[g]) for g in range(ngrp)], _first_max)
            v_scr[r] = m
            i_scr[r] = idx
            gid = jnp.floor(idx * (1.0 / grp))
            rel = idx - gid * grp
            bits, rest = [], gid
            for _ in range(nbits):
                half = jnp.floor(rest * 0.5)
                bits.append(rest - 2.0 * half == 1.0)
                rest = half
            cands = []
            for p in range(grp):
                level = [s_scr[g * grp + p] for g in range(ngrp)]
                for bit in bits:
                    level = [jnp.where(bit, level[j + 1], level[j]) for j in range(0, len(level), 2)]
                val = level[0]
                left = (val < m) | ((val == m) & (rel < float(p)))
                cands.append((jnp.where(left, val, -jnp.inf), float(p)))
            nv, npos = _tree(cands, _first_max)
            ni = gid * grp + npos
            for g in range(ngrp):
                hit = gid == float(g)
                gv[g] = jnp.where(hit, nv, gv[g])
                gi[g] = jnp.where(hit, ni, gi[g])
        return carry

    lax.fori_loop(0, PEER_TOPK, body, 0)


def _select_kernel(tb, xn_ref, wq_ref, k1_ref, k2_ref, e1_ref, e2_ref, g_ref,
                   s1_scr, s2_scr, gv1_scr, gi1_scr, gv2_scr, gi2_scr, v1_scr, i1_scr, v2_scr, i2_scr,
                   cand_scr, sc_scr, se1_scr, se2_scr):
    q = _dot(xn_ref[...], wq_ref[...]).astype(BF16)
    hq = PEER_HEADS * PEER_DQ // 2
    s1 = _dot_nt(k1_ref[...], q[:, :hq])
    s2 = _dot_nt(k2_ref[...], q[:, hq:])
    flats = [float(a * PEER_TOPK + b) for a, b in _CANDS]
    for lt in range(tb // LANES):
        lanes = slice(lt * LANES, (lt + 1) * LANES)
        s1_scr[...] = s1[:, lanes].reshape(PEER_NKEYS, SUBLANES, LANES)
        s2_scr[...] = s2[:, lanes].reshape(PEER_NKEYS, SUBLANES, LANES)
        _top16_of_keys((s1_scr, s2_scr), (gv1_scr, gv2_scr), (gi1_scr, gi2_scr), (v1_scr, v2_scr), (i1_scr, i2_scr))
        for n, (a, b) in enumerate(_CANDS):
            cand_scr[n] = v1_scr[a] + v2_scr[b]

        def body(r, carry):
            m, flat = _tree([(cand_scr[n], f) for n, f in enumerate(flats)], _first_max)
            for n, f in enumerate(flats):
                cand_scr[n] = jnp.where(flat == f, -jnp.inf, cand_scr[n])
            fa = jnp.floor(flat * (1.0 / PEER_TOPK))
            fb = flat - fa * PEER_TOPK
            sc_scr[r] = m
            se1_scr[r] = _tree([jnp.where(fa == float(a), i1_scr[a], 0.0) for a in range(PEER_TOPK)], jnp.add)
            se2_scr[r] = _tree([jnp.where(fb == float(b), i2_scr[b], 0.0) for b in range(PEER_TOPK)], jnp.add)
            return carry

        lax.fori_loop(0, PEER_TOPK, body, 0)
        sc = sc_scr[...]
        ex = jnp.exp(sc - jnp.max(sc, axis=0, keepdims=True))
        gate = ex / jnp.sum(ex, axis=0, keepdims=True)
        rows = slice(lt * LANES, (lt + 1) * LANES)
        g_ref[rows, :] = gate.reshape(PEER_SLOTS, LANES).T
        e1_ref[rows, :] = se1_scr[...].reshape(PEER_SLOTS, LANES).T
        e2_ref[rows, :] = se2_scr[...].reshape(PEER_SLOTS, LANES).T


def _peer_select(xn, wq, k1big, k2big):
    t = xn.shape[0]
    tb = 256
    hq = PEER_HEADS * PEER_DQ // 2
    nk = PEER_NKEYS * PEER_HEADS
    const = lambda i: (0, 0)
    row = lambda dt: jax.ShapeDtypeStruct((t, PEER_SLOTS), dt)
    vec = lambda n: pltpu.VMEM((n, SUBLANES, LANES), F32)
    return pl.pallas_call(
        functools.partial(_select_kernel, tb),
        grid=(t // tb,),
        in_specs=[pl.BlockSpec((tb, D_MODEL), lambda i: (i, 0)),
                  pl.BlockSpec((D_MODEL, 2 * hq), const),
                  pl.BlockSpec((nk, hq), const),
                  pl.BlockSpec((nk, hq), const)],
        out_specs=[pl.BlockSpec((tb, PEER_SLOTS), lambda i: (i, 0))] * 3,
        out_shape=[row(F32), row(F32), row(F32)],
        scratch_shapes=[vec(PEER_NKEYS), vec(PEER_NKEYS)] + [vec(PEER_NKEYS // TOPK_GROUP)] * 4
        + [vec(PEER_TOPK), vec(PEER_TOPK), vec(PEER_TOPK), vec(PEER_TOPK),
                        vec(len(_CANDS)), vec(PEER_TOPK), vec(PEER_TOPK), vec(PEER_TOPK)],
        compiler_params=_params(("parallel",), 40),
        name="peer_select",
    )(xn, wq, k1big, k2big)


def _peer_kernel(tb, eb, stride, final_norm, xn_ref, e1_ref, e2_ref, g_ref, ut_ref, v_ref, x1_ref, gain_ref,
                 out_ref, m_scr):
    e = pl.program_id(1)
    nk1 = eb // PEER_NKEYS

    @pl.when(e == 0)
    def _():
        out_ref[...] = x1_ref[...]

    @pl.when(e == 0)
    def _():
        key = lax.broadcasted_iota(jnp.int32, (PEER_NKEYS, PEER_SLOTS), 0).astype(F32)

        def token(t, carry):
            e1 = e1_ref[pl.ds(t, 1), :]
            e2 = e2_ref[pl.ds(t, 1), :]
            gt = 0.5 * g_ref[pl.ds(t, 1), :]
            a_t = jnp.where(key == e1, gt, 0.0).astype(BF16)
            b_t = jnp.where(key == e2, 1.0, 0.0).astype(BF16)
            m_scr[pl.ds(t, PEER_NKEYS, stride=stride), :] = _dot_nt(a_t, b_t)
            return carry

        lax.fori_loop(0, tb, token, 0, unroll=64)

    s = _dot(xn_ref[...], ut_ref[...])
    t = jnp.tanh(s * (GELU_C + (GELU_C * GELU_A) * (s * s)))
    k1 = e * nk1
    gates = [m_scr[pl.ds(pl.multiple_of((k1 + i) * stride, SUBLANES), tb), :] for i in range(nk1)]
    w = ((s + s * t) * jnp.concatenate(gates, axis=1)).astype(BF16)
    out_ref[...] += _dot(w, v_ref[...])

    if final_norm:
        @pl.when(e == pl.num_programs(1) - 1)
        def _():
            out_ref[...] = _rms(out_ref[...], gain_ref[...])


def _peer_dense(xn, e1, e2, g, ut, v, x1, layer, gain, final_norm):
    t = xn.shape[0]
    tb, eb = 512, 1024
    stride = tb + SUBLANES
    once = pl.Buffered(1)
    tok = lambda w: pl.BlockSpec((tb, w), lambda i, e: (i, 0), pipeline_mode=once)
    tab = pl.BlockSpec((None, eb, D_MODEL), lambda i, e: (layer, e, 0))
    return pl.pallas_call(
        functools.partial(_peer_kernel, tb, eb, stride, final_norm),
        grid=(t // tb, PEER_NEXP // eb),
        in_specs=[tok(D_MODEL), tok(PEER_SLOTS), tok(PEER_SLOTS), tok(PEER_SLOTS),
                  pl.BlockSpec((None, D_MODEL, eb), lambda i, e: (layer, 0, e)), tab, tok(D_MODEL),
                  pl.BlockSpec((1, D_MODEL), lambda i, e: (0, 0))],
        out_specs=pl.BlockSpec((tb, D_MODEL), lambda i, e: (i, 0)),
        out_shape=jax.ShapeDtypeStruct((t, D_MODEL), F32),
        scratch_shapes=[pltpu.VMEM((PEER_NKEYS * stride, PEER_NKEYS), F32)],
        compiler_params=_params(("parallel", "arbitrary"), 60),
        name="peer_dense",
    )(xn, e1, e2, g, ut, v, x1, gain)


def _time_major(x):
    b, s, d = x.shape
    return x.transpose(1, 0, 2).reshape(s * b, d)


def _batch_major(y, b, s):
    return y.reshape(s, b, y.shape[-1]).transpose(1, 0, 2)


def kernel(x_prompt, x_sample, state_ret, state_ssm_re, state_ssm_im, state_conv, norm_mix, w_in, ret_norm, w_ret_out, ssm_a_re, ssm_a_im, ssm_b_re, ssm_b_im, ssm_c_re, ssm_c_im, ssm_d, ssm_log_dt, w_glu_a, w_glu_b, conv_w, conv_b, w_conv_out, w_mix_out, norm_ffn, peer_wq, peer_k1, peer_k2, peer_u, peer_v, norm_final):
    bp, sp, _ = x_prompt.shape
    bs, ss, _ = x_sample.shape
    tp, ts = bp * sp, bs * ss
    depth = w_in.shape[0]
    assert tp % ROWS == 0 and ts == ROWS and ROWS % bp == 0 and ROWS // bp == math.gcd(sp, RET_CHUNK)

    xs = [_time_major(x_prompt), _time_major(x_sample)]
    batches = (bp, bs)
    pos = (jnp.repeat(jnp.arange(sp, dtype=F32), bp), jnp.repeat(PAST_LEN + jnp.arange(ss, dtype=F32), bs))
    ret_tabs = [_retention_tables(p, b) for p, b in zip(pos, batches)]

    lg = depth * SSM_G
    abre, abim, bbre, bbim = _s5_discretise(
        ssm_a_re.reshape(lg, SSM_P), ssm_a_im.reshape(lg, SSM_P), ssm_log_dt.reshape(lg, 1),
        ssm_b_re.transpose(0, 1, 3, 2).reshape(lg, SSM_GC, SSM_P),
        ssm_b_im.transpose(0, 1, 3, 2).reshape(lg, SSM_GC, SSM_P))

    hq = PEER_DQ // 2
    eye = jnp.eye(PEER_HEADS, dtype=F32)

    def keys_block_diag(k):
        return (k.transpose(1, 0, 2)[:, :, None, :] * eye[None, :, :, None]).reshape(
            PEER_NKEYS * PEER_HEADS, PEER_HEADS * hq).astype(BF16)

    w_in_b = w_in.astype(BF16)
    ut = peer_u.astype(BF16).transpose(0, 2, 1)
    vt = peer_v.astype(BF16)
    zero_ret = jnp.zeros((1, bp, RET_HEADS, RET_DK, RET_DK), F32)

    states = [[[] for _ in range(4)] for _ in range(2)]
    for l in range(depth):
        sl = slice(l * SSM_G, (l + 1) * SSM_G)
        bmat = jnp.concatenate([_block_diag_slabs(bbre[sl], None), _block_diag_slabs(bbim[sl], None)],
                               axis=2).astype(BF16)
        cmat = jnp.concatenate([_block_diag_slabs(ssm_c_re[l].transpose(0, 2, 1), None),
                                _block_diag_slabs(-ssm_c_im[l].transpose(0, 2, 1), None)],
                               axis=1).astype(BF16)
        are_row = abre[sl].reshape(1, SSM_N)
        aim_row = abim[sl].reshape(1, SSM_N)
        proj =[w.astype(BF16) for w in (w_ret_out[l], w_glu_a[l], w_glu_b[l], w_conv_out[l], w_mix_out[l])]
        wq = peer_wq[l].reshape(D_MODEL, PEER_HEADS, 2, hq).transpose(0, 2, 1, 3).reshape(D_MODEL, -1).astype(BF16)
        k1big, k2big = keys_block_diag(peer_k1[l]), keys_block_diag(peer_k2[l])

        for gi, batch in enumerate(batches):
            x = xs[gi]
            nblk = x.shape[0] // ROWS
            if gi == 0:
                s0, s0_layer = zero_ret, 0
                h0re = jnp.zeros((batch, SSM_N), F32)
                h0im = jnp.zeros((batch, SSM_N), F32)
                buf0 = jnp.zeros(((CONV_K - 1) * batch, CONV_W), F32)
                bblk = batch
            else:
                s0, s0_layer = state_ret, l
                h0re = state_ssm_re[l].reshape(batch, SSM_N)
                h0im = state_ssm_im[l].reshape(batch, SSM_N)
                buf0 = state_conv[l].transpose(1, 0, 2).reshape((CONV_K - 1) * batch, CONV_W)
                bblk = 16
            z = _inproj(x, norm_mix[l][None, :], w_in_b, l)
            oa, s_new = _retention(z, 0, nblk, batch, bblk, ret_tabs[gi], ret_norm[l][None, :], s0, s0_layer)
            ys, xre, xim = _s5(z, 0, nblk, batch, bmat, cmat, are_row, aim_row, ssm_d[l][None, :], h0re, h0im)
            oc, buf = _conv(z, 0, nblk, batch, buf0, conv_w[l], conv_b[l][None, :])
            st = states[gi]
            st[0].append(s_new)
            st[1].append(xre.reshape(batch, SSM_G, SSM_P))
            st[2].append(xim.reshape(batch, SSM_G, SSM_P))
            st[3].append(buf.reshape(CONV_K - 1, batch, CONV_W).transpose(1, 0, 2))

            x1, xn = _merge(x, z, oa, ys, oc, *proj, norm_ffn[l][None, :])
            e1, e2, g = _peer_select(xn, wq, k1big, k2big)
            xs[gi] = _peer_dense(xn, e1, e2, g, ut, vt, x1, l, norm_final[None, :], l == depth - 1)

    y_prompt = _batch_major(xs[0], bp, sp)
    y_sample = _batch_major(xs[1], bs, ss)
    (ret_p, re_p, im_p, cv_p), (ret_s, re_s, im_s, cv_s) = states
    return (y_prompt, y_sample,
            jnp.stack(ret_p), jnp.stack(ret_s),
            jnp.stack(re_p), jnp.stack(re_s),
            jnp.stack(im_p), jnp.stack(im_s),
            jnp.stack(cv_p), jnp.stack(cv_s))
```

```python
import functools
import math

import jax
import jax.numpy as jnp
from jax import lax
from jax.experimental import pallas as pl
from jax.experimental.pallas import tpu as pltpu

F32 = jnp.float32
BF16 = jnp.bfloat16

D_MODEL = 1024
DEPTH = 2
PAST_LEN = 16384
RET_HEADS = 8
RET_DK = 64
RET_W = 512
RET_CHUNK = 128
ROPE_BASE = 10000.0
SSM_W = 512
SSM_GC = 16
SSM_G = 32
SSM_P = 64
SSM_N = SSM_G * SSM_P
SSM_SLABS = 4
SSM_SLAB_N = SSM_N // SSM_SLABS
CONV_W = 512
CONV_K = 3
PROJ_W = 7168
PEER_HEADS = 8
PEER_DQ = 256
PEER_NKEYS = 128
PEER_TOPK = 16
PEER_NEXP = PEER_NKEYS ** 2
PEER_SLOTS = PEER_HEADS * PEER_TOPK
TOPK_GROUP = 8
EPS = 1e-6
GELU_C = math.sqrt(2.0 / math.pi)
GELU_A = 0.044715

ROWS = 1024
LANES = 128
SUBLANES = 8
MXU_DEPTH = 256
MIB = 1024 * 1024

_CANDS = [(a, b) for a in range(PEER_TOPK) for b in range(PEER_TOPK) if (a + 1) * (b + 1) <= PEER_TOPK]


def _params(sem, vmem_mib):
    return pltpu.CompilerParams(dimension_semantics=sem, vmem_limit_bytes=vmem_mib * MIB)


def _rms(x, g):
    return x * lax.rsqrt(jnp.mean(x * x, axis=-1, keepdims=True) + EPS) * g


def _dot(a, b):
    return jnp.dot(a, b, preferred_element_type=F32)


def _dot_nt(a, b):
    return lax.dot_general(a, b, (((1,), (1,)), ((), ())), preferred_element_type=F32)


def _inproj_kernel(x_ref, g_ref, w_ref, z_ref, h_scr):
    @pl.when(pl.program_id(1) == 0)
    def _():
        h_scr[...] = _rms(x_ref[...], g_ref[...]).astype(BF16)

    z_ref[...] = _dot(h_scr[...], w_ref[...])


def _inproj(x, g, w, layer):
    t = x.shape[0]
    nb = 1024
    return pl.pallas_call(
        _inproj_kernel,
        grid=(t // ROWS, PROJ_W // nb),
        in_specs=[pl.BlockSpec((ROWS, D_MODEL), lambda i, j: (i, 0)),
                  pl.BlockSpec((1, D_MODEL), lambda i, j: (0, 0)),
                  pl.BlockSpec((None, D_MODEL, nb), lambda i, j: (layer, 0, j))],
        out_specs=pl.BlockSpec((ROWS, nb), lambda i, j: (i, j)),
        out_shape=jax.ShapeDtypeStruct((t, PROJ_W), F32),
        scratch_shapes=[pltpu.VMEM((ROWS, D_MODEL), BF16)],
        compiler_params=_params(("parallel", "arbitrary"), 40),
        name="inproj",
    )(x, g, w)


def _ret_kernel(batch, bblk, z_ref, cos_ref, sa_ref, sb_ref, qdec_ref, kdec_ref, cdec_ref, gn_ref, s0_ref,
                o_ref, s_ref, qd_scr, kd_scr, v_scr, mask_scr, oacc_scr):
    bb = pl.program_id(0)
    c = pl.program_id(1)
    steps = ROWS // batch
    nslab = RET_W // LANES

    def head_view(ref, h):
        return ref[h // 2, :, (h % 2) * RET_DK:(h % 2 + 1) * RET_DK]

    seq_local = steps >= LANES
    msize = steps if seq_local else ROWS

    @pl.when((bb == 0) & (c == 0))
    def _():
        r = lax.broadcasted_iota(jnp.int32, (msize, msize), 0)
        cc = lax.broadcasted_iota(jnp.int32, (msize, msize), 1)
        if seq_local:
            mask_scr[...] = (r >= cc).astype(F32)
        else:
            same = (r & (batch - 1)) == (cc & (batch - 1))
            mask_scr[...] = (same & (r >= cc)).astype(F32)

    @pl.when(c == 0)
    def _():
        s_ref[...] = s0_ref[...]

    @pl.when(bb == 0)
    def _():
        cos, sa, sb = cos_ref[...], sa_ref[...], sb_ref[...]

        def rot(x):
            return x * cos + pltpu.roll(x, 32, 1) * sa + pltpu.roll(x, 96, 1) * sb

        for s in range(nslab):
            cols = slice(s * LANES, (s + 1) * LANES)
            qd_scr[s] = rot(z_ref[:, cols]) * qdec_ref[:, cols]
            kd_scr[s] = rot(z_ref[:, RET_W + s * LANES:RET_W + (s + 1) * LANES]) * kdec_ref[:, cols]
            v_scr[s] = z_ref[:, 2 * RET_W + s * LANES:2 * RET_W + (s + 1) * LANES]
        if not seq_local:
            for s in range(nslab):
                outs = []
                for h in (2 * s, 2 * s + 1):
                    qh = head_view(qd_scr, h).astype(BF16)
                    kh = head_view(kd_scr, h).astype(BF16)
                    vh = head_view(v_scr, h).astype(BF16)
                    p = (_dot_nt(qh, kh) * mask_scr[...]).astype(BF16)
                    outs.append(_dot(p, vh))
                oacc_scr[s] = jnp.concatenate(outs, axis=1)

    def per_seq(bl, carry):
        b = bb * bblk + bl
        rows = pl.ds(b, steps, stride=batch)
        for s in range(nslab):
            qb = qd_scr[s, rows, :]
            kb = kd_scr[s, rows, :]
            vb = v_scr[s, rows, :]
            outs = []
            for hh in range(2):
                h = 2 * s + hh
                hc = slice(hh * RET_DK, (hh + 1) * RET_DK)
                q16, k16, v16 = qb[:, hc].astype(BF16), kb[:, hc].astype(BF16), vb[:, hc].astype(BF16)
                st = s_ref[bl, h]
                o = _dot(q16, st.astype(BF16))
                if seq_local:
                    p = (_dot_nt(q16, k16) * mask_scr[...]).astype(BF16)
                    o = _dot(p, v16) + o
                outs.append(o)
                upd = lax.dot_general(k16, v16, (((0,), (0,)), ((), ())), preferred_element_type=F32)
                s_ref[bl, h] = (st + upd) * cdec_ref[:, h * RET_DK:(h + 1) * RET_DK]
            o2 = jnp.concatenate(outs, axis=1)
            oacc_scr[s, rows, :] = o2 if seq_local else oacc_scr[s, rows, :] + o2
        return carry

    lax.fori_loop(0, bblk, per_seq, 0, unroll=4)

    @pl.when(bb == pl.num_programs(0) - 1)
    def _():
        r = lax.broadcasted_iota(jnp.int32, (LANES, LANES), 0) // RET_DK
        cc = lax.broadcasted_iota(jnp.int32, (LANES, LANES), 1) // RET_DK
        avg = jnp.where(r == cc, 1.0 / RET_DK, 0.0).astype(BF16)

        def seg_mean(x):
            hi = x.astype(BF16)
            lo = (x - hi.astype(F32)).astype(BF16)
            return _dot(hi, avg) + _dot(lo, avg)

        normed = []
        for s in range(nslab):
            o2 = oacc_scr[s]
            dlt = o2 - seg_mean(o2)
            normed.append(dlt * lax.rsqrt(seg_mean(dlt * dlt) + EPS))
        o = jnp.concatenate(normed, axis=1) * gn_ref[...]
        o_ref[...] = jax.nn.silu(z_ref[:, 3 * RET_W:4 * RET_W]) * o


def _retention(z, row_blk0, nblk, batch, bblk, tabs, gn, s0, layer):
    cos, sa, sb, qdec, kdec, cdec = tabs
    nbb = batch // bblk
    steps = ROWS // batch
    msize = steps if steps >= LANES else ROWS
    st_spec = pl.BlockSpec((bblk, RET_HEADS, RET_DK, RET_DK), lambda bb, c: (bb, 0, 0, 0))
    st_in = pl.BlockSpec((None, bblk, RET_HEADS, RET_DK, RET_DK), lambda bb, c: (layer, bb, 0, 0, 0))
    const = lambda bb, c: (0, 0)
    return pl.pallas_call(
        functools.partial(_ret_kernel, batch, bblk),
        grid=(nbb, nblk),
        in_specs=[pl.BlockSpec((ROWS, 4 * RET_W), lambda bb, c: (row_blk0 + c, 0)),
                  pl.BlockSpec((ROWS, LANES), lambda bb, c: (c, 0)),
                  pl.BlockSpec((ROWS, LANES), lambda bb, c: (c, 0)),
                  pl.BlockSpec((ROWS, LANES), lambda bb, c: (c, 0)),
                  pl.BlockSpec((ROWS, RET_W), const),
                  pl.BlockSpec((ROWS, RET_W), const),
                  pl.BlockSpec((1, RET_W), const),
                  pl.BlockSpec((1, RET_W), const),
                  st_in],
        out_specs=[pl.BlockSpec((ROWS, RET_W), lambda bb, c: (c, 0)), st_spec],
        out_shape=[jax.ShapeDtypeStruct((nblk * ROWS, RET_W), F32),
                   jax.ShapeDtypeStruct((batch, RET_HEADS, RET_DK, RET_DK), F32)],
        scratch_shapes=[pltpu.VMEM((RET_W // LANES, ROWS, LANES), F32)] * 3
        + [pltpu.VMEM((msize, msize), F32), pltpu.VMEM((RET_W // LANES, ROWS, LANES), F32)],
        compiler_params=_params(("arbitrary", "arbitrary"), 56),
        name="retention",
    )(z, cos, sa, sb, qdec, kdec, cdec, gn, s0)


def _retention_tables(pos, batch):
    half = RET_DK // 2
    freqs = ROPE_BASE ** (-jnp.arange(half, dtype=F32) / half)
    ang = pos[:, None] * freqs[None, :]
    cos, sin = jnp.cos(ang), jnp.sin(ang)
    zero = jnp.zeros_like(sin)
    reps = LANES // RET_DK
    cos_t = jnp.tile(jnp.concatenate([cos, cos], axis=1), (1, reps))
    sa_t = jnp.tile(jnp.concatenate([zero, sin], axis=1), (1, reps))
    sb_t = jnp.tile(jnp.concatenate([-sin, zero], axis=1), (1, reps))
    lg = jnp.log1p(-jnp.exp2(-5.0 - jnp.arange(RET_HEADS, dtype=F32)))
    steps = ROWS // batch
    i1 = (jnp.arange(ROWS) // batch).astype(F32) + 1.0
    qdec = jnp.repeat(jnp.exp(i1[:, None] * lg[None, :]), RET_DK, axis=1)
    kdec = jnp.repeat(jnp.exp(-i1[:, None] * lg[None, :]), RET_DK, axis=1) * (RET_DK ** -0.5)
    cdec = jnp.repeat(jnp.exp(steps * lg), RET_DK)[None, :]
    return cos_t, sa_t, sb_t, qdec, kdec, cdec


def _s5_disc_kernel(are_ref, aim_ref, ldt_ref, bre_ref, bim_ref, abre_ref, abim_ref, bbre_ref, bbim_ref):
    ar, ai = are_ref[...], aim_ref[...]
    dt = jnp.exp(ldt_ref[...])
    dar, dai = dt * ar, dt * ai
    mag = jnp.exp(dar)
    abar_re, abar_im = mag * jnp.cos(dai), mag * jnp.sin(dai)
    den = ar * ar + ai * ai
    nr, ni = abar_re - 1.0, abar_im
    f_re = (nr * ar + ni * ai) / den
    f_im = (ni * ar - nr * ai) / den
    abre_ref[...] = abar_re
    abim_ref[...] = abar_im
    br, bi = bre_ref[...], bim_ref[...]
    bbre_ref[...] = f_re[:, None, :] * br - f_im[:, None, :] * bi
    bbim_ref[...] = f_re[:, None, :] * bi + f_im[:, None, :] * br


def _s5_discretise(a_re, a_im, log_dt, b_re_t, b_im_t):
    lg = a_re.shape[0]
    small = jax.ShapeDtypeStruct((lg, SSM_P), F32)
    big = jax.ShapeDtypeStruct((lg, SSM_GC, SSM_P), F32)
    return pl.pallas_call(_s5_disc_kernel, out_shape=[small, small, big, big], name="s5_disc")(
        a_re, a_im, log_dt, b_re_t, b_im_t)


def _s5_kernel(batch, u_ref, bmat_ref, cmat_ref, are_ref, aim_ref, d_ref, h0re_ref, h0im_ref,
               y_ref, xre_ref, xim_ref, x_scr):
    c = pl.program_id(0)
    steps = ROWS // batch
    half = SSM_SLAB_N

    @pl.when(c == 0)
    def _():
        xre_ref[...] = h0re_ref[...]
        xim_ref[...] = h0im_ref[...]

    u = u_ref[...]
    ub = u.astype(BF16)
    for s in range(SSM_SLABS):
        x_scr[:, 2 * half * s:2 * half * (s + 1)] = _dot(ub[:, s * LANES:(s + 1) * LANES], bmat_ref[s])

    for s in range(SSM_SLABS):
        re0 = 2 * half * s
        im0 = re0 + half
        sc = slice(half * s, half * (s + 1))
        ar = jnp.broadcast_to(are_ref[:, sc], (SUBLANES, half))
        ai = jnp.broadcast_to(aim_ref[:, sc], (SUBLANES, half))

        def row_tile(rt, carry, re0=re0, im0=im0, sc=sc, ar=ar, ai=ai):
            r0 = pl.multiple_of(rt * SUBLANES, SUBLANES)

            def step(t, x):
                xr, xi = x
                row = pl.multiple_of(t * batch + r0, SUBLANES)
                nr = ar * xr - ai * xi + x_scr[pl.ds(row, SUBLANES), re0:re0 + half]
                ni = ar * xi + ai * xr + x_scr[pl.ds(row, SUBLANES), im0:im0 + half]
                x_scr[pl.ds(row, SUBLANES), re0:re0 + half] = nr
                x_scr[pl.ds(row, SUBLANES), im0:im0 + half] = ni
                return nr, ni

            init = (xre_ref[pl.ds(r0, SUBLANES), sc], xim_ref[pl.ds(r0, SUBLANES), sc])
            xr, xi = lax.fori_loop(0, steps, step, init, unroll=8)
            xre_ref[pl.ds(r0, SUBLANES), sc] = xr
            xim_ref[pl.ds(r0, SUBLANES), sc] = xi
            return carry

        lax.fori_loop(0, batch // SUBLANES, row_tile, 0)

    ys = [_dot(x_scr[:, 2 * half * s:2 * half * (s + 1)].astype(BF16), cmat_ref[s]) for s in range(SSM_SLABS)]
    y = jnp.concatenate(ys, axis=1) + d_ref[...] * u
    y_ref[...] = jax.nn.gelu(y)


def _s5(z, row_blk0, nblk, batch, bmat, cmat, abre, abim, d, h0re, h0im):
    const2 = lambda c: (0, 0)
    const3 = lambda c: (0, 0, 0)
    st = pl.BlockSpec((batch, SSM_N), const2)
    return pl.pallas_call(
        functools.partial(_s5_kernel, batch),
        grid=(nblk,),
        in_specs=[pl.BlockSpec((ROWS, SSM_W), lambda c: (row_blk0 + c, 4)),
                  pl.BlockSpec((SSM_SLABS, LANES, 2 * SSM_SLAB_N), const3),
                  pl.BlockSpec((SSM_SLABS, 2 * SSM_SLAB_N, LANES), const3),
                  pl.BlockSpec((1, SSM_N), const2),
                  pl.BlockSpec((1, SSM_N), const2),
                  pl.BlockSpec((1, SSM_W), const2),
                  st, st],
        out_specs=[pl.BlockSpec((ROWS, SSM_W), lambda c: (c, 0)), st, st],
        out_shape=[jax.ShapeDtypeStruct((nblk * ROWS, SSM_W), F32),
                   jax.ShapeDtypeStruct((batch, SSM_N), F32),
                   jax.ShapeDtypeStruct((batch, SSM_N), F32)],
        scratch_shapes=[pltpu.VMEM((ROWS, 2 * SSM_N), F32)],
        compiler_params=_params(("arbitrary",), 48),
        name="s5",
    )(z, bmat, cmat, abre, abim, d, h0re, h0im)


def _block_diag_slabs(w, rows_inner):
    gps = SSM_G // SSM_SLABS
    eye = jnp.eye(gps, dtype=w.dtype)
    w4 = w.reshape(SSM_SLABS, gps, w.shape[1], w.shape[2])
    out = w4[:, :, :, None, :] * eye[None, :, None, :, None]
    return out.reshape(SSM_SLABS, gps * w.shape[1], gps * w.shape[2])


def _conv_kernel(batch, bg_ref, cg_ref, hc_ref, buf0_ref, w_ref, b_ref, o_ref, buf_ref, zp_scr):
    c = pl.program_id(0)
    pad = (CONV_K - 1) * batch

    @pl.when(c == 0)
    def _():
        zp_scr[0:pad, :] = buf0_ref[...]

    zc = cg_ref[...] * hc_ref[...]
    zp_scr[pad:pad + ROWS, :] = zc
    y = b_ref[...]
    for j in range(CONV_K):
        y = y + w_ref[j:j + 1, :] * zp_scr[j * batch:j * batch + ROWS, :]
    o_ref[...] = bg_ref[...] * y
    tail = zp_scr[ROWS:ROWS + pad, :]
    buf_ref[...] = tail
    zp_scr[0:pad, :] = tail


def _conv(z, row_blk0, nblk, batch, buf0, w, b):
    pad = (CONV_K - 1) * batch
    const = lambda c: (0, 0)
    return pl.pallas_call(
        functools.partial(_conv_kernel, batch),
        grid=(nblk,),
        in_specs=[pl.BlockSpec((ROWS, CONV_W), lambda c: (row_blk0 + c, 5)),
                  pl.BlockSpec((ROWS, CONV_W), lambda c: (row_blk0 + c, 6)),
                  pl.BlockSpec((ROWS, CONV_W), lambda c: (row_blk0 + c, 7)),
                  pl.BlockSpec((pad, CONV_W), const),
                  pl.BlockSpec((CONV_K, CONV_W), const),
                  pl.BlockSpec((1, CONV_W), const)],
        out_specs=[pl.BlockSpec((ROWS, CONV_W), lambda c: (c, 0)), pl.BlockSpec((pad, CONV_W), const)],
        out_shape=[jax.ShapeDtypeStruct((nblk * ROWS, CONV_W), F32),
                   jax.ShapeDtypeStruct((pad, CONV_W), F32)],
        scratch_shapes=[pltpu.VMEM((ROWS + pad, CONV_W), F32)],
        compiler_params=_params(("arbitrary",), 32),
        name="conv",
    )(z, z, z, buf0, w, b)


def _merge_kernel(x_ref, oa_ref, ys_ref, oc_ref, ga_ref, gb_ref, gc_ref, wr_ref, wa_ref, wb_ref, wc_ref,
                  wm_ref, gf_ref, x1_ref, xn_ref):
    oa = _dot(oa_ref[...].astype(BF16), wr_ref[...])
    ysb = ys_ref[...].astype(BF16)
    ob = _dot(ysb, wa_ref[...]) * jax.nn.sigmoid(_dot(ysb, wb_ref[...]))
    oc = _dot(oc_ref[...].astype(BF16), wc_ref[...])
    merged = (jax.nn.sigmoid(ga_ref[...]) * oa + jax.nn.sigmoid(gb_ref[...]) * ob
              + jax.nn.sigmoid(gc_ref[...]) * oc)
    x1 = x_ref[...] + _dot(merged.astype(BF16), wm_ref[...])
    x1_ref[...] = x1
    xn_ref[...] = _rms(x1, gf_ref[...]).astype(BF16)


def _merge(x, z, oa, ys, oc, wr, wa, wb, wc, wm, gf):
    t = x.shape[0]
    rb = 512
    row = lambda w: pl.BlockSpec((rb, w), lambda i: (i, 0))
    gate = lambda j: pl.BlockSpec((rb, D_MODEL), lambda i: (i, j))
    wsp = lambda k: pl.BlockSpec((k, D_MODEL), lambda i: (0, 0))
    return pl.pallas_call(
        _merge_kernel,
        grid=(t // rb,),
        in_specs=[row(D_MODEL), row(RET_W), row(SSM_W), row(CONV_W), gate(4), gate(5), gate(6),
                  wsp(RET_W), wsp(SSM_W), wsp(SSM_W), wsp(CONV_W), wsp(D_MODEL), wsp(1)],
        out_specs=[row(D_MODEL), row(D_MODEL)],
        out_shape=[jax.ShapeDtypeStruct((t, D_MODEL), F32), jax.ShapeDtypeStruct((t, D_MODEL), BF16)],
        compiler_params=_params(("parallel",), 48),
        name="merge",
    )(x, oa, ys, oc, z, z, z, wr, wa, wb, wc, wm, gf)


def _tree(items, combine):
    while len(items) > 1:
        nxt = [combine(items[i], items[i + 1]) for i in range(0, len(items) - 1, 2)]
        if len(items) % 2:
            nxt.append(items[-1])
        items = nxt
    return items[0]


def _first_max(x, y):
    (vx, ix), (vy, iy) = x, y
    return jnp.maximum(vx, vy), jnp.where(vx >= vy, ix, iy)


def _top16_of_keys(s_scrs, gv_scrs, gi_scrs, v_scrs, i_scrs):
    grp = TOPK_GROUP
    ngrp = PEER_NKEYS // grp
    nbits = ngrp.bit_length() - 1

    for s_scr, gv, gi in zip(s_scrs, gv_scrs, gi_scrs):
        for g in range(ngrp):
            gv[g], gi[g] = _tree([(s_scr[g * grp + p], float(g * grp + p)) for p in range(grp)], _first_max)

    def body(r, carry):
        for s_scr, gv, gi, v_scr, i_scr in zip(s_scrs, gv_scrs, gi_scrs, v_scrs, i_scrs):
            m, idx = _tree([(gv[g], gi[g]) for g in range(ngrp)], _first_max)
            v_scr[r] = m
            i_scr[r] = idx
            gid = jnp.floor(idx * (1.0 / grp))
            rel = idx - gid * grp
            bits, rest = [], gid
            for _ in range(nbits):
                half = jnp.floor(rest * 0.5)
                bits.append(rest - 2.0 * half == 1.0)
                rest = half
            cands = []
            for p in range(grp):
                level = [s_scr[g * grp + p] for g in range(ngrp)]
                for bit in bits:
                    level = [jnp.where(bit, level[j + 1], level[j]) for j in range(0, len(level), 2)]
                val = level[0]
                left = (val < m) | ((val == m) & (rel < float(p)))
                cands.append((jnp.where(left, val, -jnp.inf), float(p)))
            nv, npos = _tree(cands, _first_max)
            ni = gid * grp + npos
            for g in range(ngrp):
                hit = gid == float(g)
                gv[g] = jnp.where(hit, nv, gv[g])
                gi[g] = jnp.where(hit, ni, gi[g])
        return carry

    lax.fori_loop(0, PEER_TOPK, body, 0)


def _select_kernel(tb, xn_ref, wq_ref, k1_ref, k2_ref, e1_ref, e2_ref, g_ref,
                   s1_scr, s2_scr, gv1_scr, gi1_scr, gv2_scr, gi2_scr, v1_scr, i1_scr, v2_scr, i2_scr,
                   cand_scr, sc_scr, se1_scr, se2_scr):
    q = _dot(xn_ref[...], wq_ref[...]).astype(BF16)
    hq = PEER_HEADS * PEER_DQ // 2
    s1 = _dot_nt(k1_ref[...], q[:, :hq])
    s2 = _dot_nt(k2_ref[...], q[:, hq:])
    flats = [float(a * PEER_TOPK + b) for a, b in _CANDS]
    for lt in range(tb // LANES):
        lanes = slice(lt * LANES, (lt + 1) * LANES)
        s1_scr[...] = s1[:, lanes].reshape(PEER_NKEYS, SUBLANES, LANES)
        s2_scr[...] = s2[:, lanes].reshape(PEER_NKEYS, SUBLANES, LANES)
        _top16_of_keys((s1_scr, s2_scr), (gv1_scr, gv2_scr), (gi1_scr, gi2_scr), (v1_scr, v2_scr), (i1_scr, i2_scr))
        for n, (a, b) in enumerate(_CANDS):
            cand_scr[n] = v1_scr[a] + v2_scr[b]

        def body(r, carry):
            m, flat = _tree([(cand_scr[n], f) for n, f in enumerate(flats)], _first_max)
            for n, f in enumerate(flats):
                cand_scr[n] = jnp.where(flat == f, -jnp.inf, cand_scr[n])
            fa = jnp.floor(flat * (1.0 / PEER_TOPK))
            fb = flat - fa * PEER_TOPK
            sc_scr[r] = m
            se1_scr[r] = _tree([jnp.where(fa == float(a), i1_scr[a], 0.0) for a in range(PEER_TOPK)], jnp.add)
            se2_scr[r] = _tree([jnp.where(fb == float(b), i2_scr[b], 0.0) for b in range(PEER_TOPK)], jnp.add)
            return carry

        lax.fori_loop(0, PEER_TOPK, body, 0)
        sc = sc_scr[...]
        ex = jnp.exp(sc - jnp.max(sc, axis=0, keepdims=True))
        gate = ex / jnp.sum(ex, axis=0, keepdims=True)
        rows = slice(lt * LANES, (lt + 1) * LANES)
        g_ref[rows, :] = gate.reshape(PEER_SLOTS, LANES).T
        e1_ref[rows, :] = se1_scr[...].reshape(PEER_SLOTS, LANES).T
        e2_ref[rows, :] = se2_scr[...].reshape(PEER_SLOTS, LANES).T


def _peer_select(xn, wq, k1big, k2big):
    t = xn.shape[0]
    tb = 256
    hq = PEER_HEADS * PEER_DQ // 2
    nk = PEER_NKEYS * PEER_HEADS
    const = lambda i: (0, 0)
    row = lambda dt: jax.ShapeDtypeStruct((t, PEER_SLOTS), dt)
    vec = lambda n: pltpu.VMEM((n, SUBLANES, LANES), F32)
    return pl.pallas_call(
        functools.partial(_select_kernel, tb),
        grid=(t // tb,),
        in_specs=[pl.BlockSpec((tb, D_MODEL), lambda i: (i, 0)),
                  pl.BlockSpec((D_MODEL, 2 * hq), const),
                  pl.BlockSpec((nk, hq), const),
                  pl.BlockSpec((nk, hq), const)],
        out_specs=[pl.BlockSpec((tb, PEER_SLOTS), lambda i: (i, 0))] * 3,
        out_shape=[row(F32), row(F32), row(F32)],
        scratch_shapes=[vec(PEER_NKEYS), vec(PEER_NKEYS)] + [vec(PEER_NKEYS // TOPK_GROUP)] * 4
        + [vec(PEER_TOPK), vec(PEER_TOPK), vec(PEER_TOPK), vec(PEER_TOPK),
                        vec(len(_CANDS)), vec(PEER_TOPK), vec(PEER_TOPK), vec(PEER_TOPK)],
        compiler_params=_params(("parallel",), 40),
        name="peer_select",
    )(xn, wq, k1big, k2big)


def _peer_kernel(tb, eb, stride, final_norm, xn_ref, e1_ref, e2_ref, g_ref, ut_ref, v_ref, x1_ref, gain_ref,
                 out_ref, m_scr):
    e = pl.program_id(1)
    nk1 = eb // PEER_NKEYS

    @pl.when(e == 0)
    def _():
        out_ref[...] = x1_ref[...]

    @pl.when(e == 0)
    def _():
        key = lax.broadcasted_iota(jnp.int32, (PEER_NKEYS, PEER_SLOTS), 0).astype(F32)

        def token(t, carry):
            e1 = e1_ref[pl.ds(t, 1), :]
            e2 = e2_ref[pl.ds(t, 1), :]
            gt = 0.5 * g_ref[pl.ds(t, 1), :]
            a_t = jnp.where(key == e1, gt, 0.0).astype(BF16)
            b_t = jnp.where(key == e2, 1.0, 0.0).astype(BF16)
            m_scr[pl.ds(t, PEER_NKEYS, stride=stride), :] = _dot_nt(a_t, b_t)
            return carry

        lax.fori_loop(0, tb, token, 0, unroll=64)

    s = _dot(xn_ref[...], ut_ref[...])
    t = jnp.tanh(s * (GELU_C + (GELU_C * GELU_A) * (s * s)))
    k1 = e * nk1
    gates = [m_scr[pl.ds(pl.multiple_of((k1 + i) * stride, SUBLANES), tb), :] for i in range(nk1)]
    w = ((s + s * t) * jnp.concatenate(gates, axis=1)).astype(BF16)
    out_ref[...] += _dot(w, v_ref[...])

    if final_norm:
        @pl.when(e == pl.num_programs(1) - 1)
        def _():
            out_ref[...] = _rms(out_ref[...], gain_ref[...])


def _peer_dense(xn, e1, e2, g, ut, v, x1, layer, gain, final_norm):
    t = xn.shape[0]
    tb, eb = 512, 1024
    stride = tb + SUBLANES
    once = pl.Buffered(1)
    tok = lambda w: pl.BlockSpec((tb, w), lambda i, e: (i, 0), pipeline_mode=once)
    tab = pl.BlockSpec((None, eb, D_MODEL), lambda i, e: (layer, e, 0))
    return pl.pallas_call(
        functools.partial(_peer_kernel, tb, eb, stride, final_norm),
        grid=(t // tb, PEER_NEXP // eb),
        in_specs=[tok(D_MODEL), tok(PEER_SLOTS), tok(PEER_SLOTS), tok(PEER_SLOTS),
                  pl.BlockSpec((None, D_MODEL, eb), lambda i, e: (layer, 0, e)), tab, tok(D_MODEL),
                  pl.BlockSpec((1, D_MODEL), lambda i, e: (0, 0))],
        out_specs=pl.BlockSpec((tb, D_MODEL), lambda i, e: (i, 0)),
        out_shape=jax.ShapeDtypeStruct((t, D_MODEL), F32),
        scratch_shapes=[pltpu.VMEM((PEER_NKEYS * stride, PEER_NKEYS), F32)],
        compiler_params=_params(("parallel", "arbitrary"), 60),
        name="peer_dense",
    )(xn, e1, e2, g, ut, v, x1, gain)


def _time_major(x):
    b, s, d = x.shape
    return x.transpose(1, 0, 2).reshape(s * b, d)


def _batch_major(y, b, s):
    return y.reshape(s, b, y.shape[-1]).transpose(1, 0, 2)


def kernel(x_prompt, x_sample, state_ret, state_ssm_re, state_ssm_im, state_conv, norm_mix, w_in, ret_norm, w_ret_out, ssm_a_re, ssm_a_im, ssm_b_re, ssm_b_im, ssm_c_re, ssm_c_im, ssm_d, ssm_log_dt, w_glu_a, w_glu_b, conv_w, conv_b, w_conv_out, w_mix_out, norm_ffn, peer_wq, peer_k1, peer_k2, peer_u, peer_v, norm_final):
    bp, sp, _ = x_prompt.shape
    bs, ss, _ = x_sample.shape
    tp, ts = bp * sp, bs * ss
    depth = w_in.shape[0]
    assert tp % ROWS == 0 and ts == ROWS and ROWS % bp == 0 and ROWS // bp == math.gcd(sp, RET_CHUNK)

    xs = [_time_major(x_prompt), _time_major(x_sample)]
    batches = (bp, bs)
    pos = (jnp.repeat(jnp.arange(sp, dtype=F32), bp), jnp.repeat(PAST_LEN + jnp.arange(ss, dtype=F32), bs))
    ret_tabs = [_retention_tables(p, b) for p, b in zip(pos, batches)]

    lg = depth * SSM_G
    abre, abim, bbre, bbim = _s5_discretise(
        ssm_a_re.reshape(lg, SSM_P), ssm_a_im.reshape(lg, SSM_P), ssm_log_dt.reshape(lg, 1),
        ssm_b_re.transpose(0, 1, 3, 2).reshape(lg, SSM_GC, SSM_P),
        ssm_b_im.transpose(0, 1, 3, 2).reshape(lg, SSM_GC, SSM_P))

    hq = PEER_DQ // 2
    eye = jnp.eye(PEER_HEADS, dtype=F32)

    def keys_block_diag(k):
        return (k.transpose(1, 0, 2)[:, :, None, :] * eye[None, :, :, None]).reshape(
            PEER_NKEYS * PEER_HEADS, PEER_HEADS * hq).astype(BF16)

    w_in_b = w_in.astype(BF16)
    ut = peer_u.astype(BF16).transpose(0, 2, 1)
    vt = peer_v.astype(BF16)
    zero_ret = jnp.zeros((1, bp, RET_HEADS, RET_DK, RET_DK), F32)

    states = [[[] for _ in range(4)] for _ in range(2)]
    for l in range(depth):
        sl = slice(l * SSM_G, (l + 1) * SSM_G)
        bmat = jnp.concatenate([_block_diag_slabs(bbre[sl], None), _block_diag_slabs(bbim[sl], None)],
                               axis=2).astype(BF16)
        cmat = jnp.concatenate([_block_diag_slabs(ssm_c_re[l].transpose(0, 2, 1), None),
                                _block_diag_slabs(-ssm_c_im[l].transpose(0, 2, 1), None)],
                               axis=1).astype(BF16)
        are_row = abre[sl].reshape(1, SSM_N)
        aim_row = abim[sl].reshape(1, SSM_N)
        proj =[w.astype(BF16) for w in (w_ret_out[l], w_glu_a[l], w_glu_b[l], w_conv_out[l], w_mix_out[l])]
        wq = peer_wq[l].reshape(D_MODEL, PEER_HEADS, 2, hq).transpose(0, 2, 1, 3).reshape(D_MODEL, -1).astype(BF16)
        k1big, k2big = keys_block_diag(peer_k1[l]), keys_block_diag(peer_k2[l])

        for gi, batch in enumerate(batches):
            x = xs[gi]
            nblk = x.shape[0] // ROWS
            if gi == 0:
                s0, s0_layer = zero_ret, 0
                h0re = jnp.zeros((batch, SSM_N), F32)
                h0im = jnp.zeros((batch, SSM_N), F32)
                buf0 = jnp.zeros(((CONV_K - 1) * batch, CONV_W), F32)
                bblk = batch
            else:
                s0, s0_layer = state_ret, l
                h0re = state_ssm_re[l].reshape(batch, SSM_N)
                h0im = state_ssm_im[l].reshape(batch, SSM_N)
                buf0 = state_conv[l].transpose(1, 0, 2).reshape((CONV_K - 1) * batch, CONV_W)
                bblk = 16
            z = _inproj(x, norm_mix[l][None, :], w_in_b, l)
            oa, s_new = _retention(z, 0, nblk, batch, bblk, ret_tabs[gi], ret_norm[l][None, :], s0, s0_layer)
            ys, xre, xim = _s5(z, 0, nblk, batch, bmat, cmat, are_row, aim_row, ssm_d[l][None, :], h0re, h0im)
            oc, buf = _conv(z, 0, nblk, batch, buf0, conv_w[l], conv_b[l][None, :])
            st = states[gi]
            st[0].append(s_new)
            st[1].append(xre.reshape(batch, SSM_G, SSM_P))
            st[2].append(xim.reshape(batch, SSM_G, SSM_P))
            st[3].append(buf.reshape(CONV_K - 1, batch, CONV_W).transpose(1, 0, 2))

            x1, xn = _merge(x, z, oa, ys, oc, *proj, norm_ffn[l][None, :])
            e1, e2, g = _peer_select(xn, wq, k1big, k2big)
            xs[gi] = _peer_dense(xn, e1, e2, g, ut, vt, x1, l, norm_final[None, :], l == depth - 1)

    y_prompt = _batch_major(xs[0], bp, sp)
    y_sample = _batch_major(xs[1], bs, ss)
    (ret_p, re_p, im_p, cv_p), (ret_s, re_s, im_s, cv_s) = states
    return (y_prompt, y_sample,
            jnp.stack(ret_p), jnp.stack(ret_s),
            jnp.stack(re_p), jnp.stack(re_s),
            jnp.stack(im_p), jnp.stack(im_s),
            jnp.stack(cv_p), jnp.stack(cv_s))
```

```python
import functools
import math

import jax
import jax.numpy as jnp
from jax import lax
from jax.experimental import pallas as pl
from jax.experimental.pallas import tpu as pltpu

F32 = jnp.float32
BF16 = jnp.bfloat16

D_MODEL = 1024
DEPTH = 2
PAST_LEN = 16384
RET_HEADS = 8
RET_DK = 64
RET_W = 512
RET_CHUNK = 128
ROPE_BASE = 10000.0
SSM_W = 512
SSM_GC = 16
SSM_G = 32
SSM_P = 64
SSM_N = SSM_G * SSM_P
SSM_SLABS = 4
SSM_SLAB_N = SSM_N // SSM_SLABS
CONV_W = 512
CONV_K = 3
PROJ_W = 7168
PEER_HEADS = 8
PEER_DQ = 256
PEER_NKEYS = 128
PEER_TOPK = 16
PEER_NEXP = PEER_NKEYS ** 2
PEER_SLOTS = PEER_HEADS * PEER_TOPK
TOPK_GROUP = 8
EPS = 1e-6
GELU_C = math.sqrt(2.0 / math.pi)
GELU_A = 0.044715

ROWS = 1024
LANES = 128
SUBLANES = 8
MXU_DEPTH = 256
MIB = 1024 * 1024


def _params(sem, vmem_mib):
    return pltpu.CompilerParams(dimension_semantics=sem, vmem_limit_bytes=vmem_mib * MIB)


def _rms(x, g):
    return x * lax.rsqrt(jnp.mean(x * x, axis=-1, keepdims=True) + EPS) * g


def _dot(a, b):
    return jnp.dot(a, b, preferred_element_type=F32)


def _dot_nt(a, b):
    return lax.dot_general(a, b, (((1,), (1,)), ((), ())), preferred_element_type=F32)


def _inproj_kernel(x_ref, g_ref, w_ref, z_ref, h_scr):
    @pl.when(pl.program_id(1) == 0)
    def _():
        h_scr[...] = _rms(x_ref[...], g_ref[...]).astype(BF16)

    z_ref[...] = _dot(h_scr[...], w_ref[...])


def _inproj(x, g, w, layer):
    t = x.shape[0]
    nb = 1024
    return pl.pallas_call(
        _inproj_kernel,
        grid=(t // ROWS, PROJ_W // nb),
        in_specs=[pl.BlockSpec((ROWS, D_MODEL), lambda i, j: (i, 0)),
                  pl.BlockSpec((1, D_MODEL), lambda i, j: (0, 0)),
                  pl.BlockSpec((None, D_MODEL, nb), lambda i, j: (layer, 0, j))],
        out_specs=pl.BlockSpec((ROWS, nb), lambda i, j: (i, j)),
        out_shape=jax.ShapeDtypeStruct((t, PROJ_W), F32),
        scratch_shapes=[pltpu.VMEM((ROWS, D_MODEL), BF16)],
        compiler_params=_params(("parallel", "arbitrary"), 40),
        name="inproj",
    )(x, g, w)


def _ret_kernel(batch, bblk, z_ref, cos_ref, sa_ref, sb_ref, qdec_ref, kdec_ref, cdec_ref, gn_ref, s0_ref,
                o_ref, s_ref, qd_scr, kd_scr, v_scr, mask_scr, oacc_scr):
    bb = pl.program_id(0)
    c = pl.program_id(1)
    steps = ROWS // batch
    nslab = RET_W // LANES

    def head_view(ref, h):
        return ref[h // 2, :, (h % 2) * RET_DK:(h % 2 + 1) * RET_DK]

    seq_local = steps >= LANES
    msize = steps if seq_local else ROWS

    @pl.when((bb == 0) & (c == 0))
    def _():
        r = lax.broadcasted_iota(jnp.int32, (msize, msize), 0)
        cc = lax.broadcasted_iota(jnp.int32, (msize, msize), 1)
        if seq_local:
            mask_scr[...] = (r >= cc).astype(F32)
        else:
            same = (r & (batch - 1)) == (cc & (batch - 1))
            mask_scr[...] = (same & (r >= cc)).astype(F32)

    @pl.when(c == 0)
    def _():
        s_ref[...] = s0_ref[...]

    @pl.when(bb == 0)
    def _():
        cos, sa, sb = cos_ref[...], sa_ref[...], sb_ref[...]

        def rot(x):
            return x * cos + pltpu.roll(x, 32, 1) * sa + pltpu.roll(x, 96, 1) * sb

        for s in range(nslab):
            cols = slice(s * LANES, (s + 1) * LANES)
            qd_scr[s] = rot(z_ref[:, cols]) * qdec_ref[:, cols]
            kd_scr[s] = rot(z_ref[:, RET_W + s * LANES:RET_W + (s + 1) * LANES]) * kdec_ref[:, cols]
            v_scr[s] = z_ref[:, 2 * RET_W + s * LANES:2 * RET_W + (s + 1) * LANES]
        if not seq_local:
            for s in range(nslab):
                outs = []
                for h in (2 * s, 2 * s + 1):
                    qh = head_view(qd_scr, h).astype(BF16)
                    kh = head_view(kd_scr, h).astype(BF16)
                    vh = head_view(v_scr, h).astype(BF16)
                    p = (_dot_nt(qh, kh) * mask_scr[...]).astype(BF16)
                    outs.append(_dot(p, vh))
                oacc_scr[s] = jnp.concatenate(outs, axis=1)

    def per_seq(bl, carry):
        b = bb * bblk + bl
        rows = pl.ds(b, steps, stride=batch)
        for s in range(nslab):
            qb = qd_scr[s, rows, :]
            kb = kd_scr[s, rows, :]
            vb = v_scr[s, rows, :]
            outs = []
            for hh in range(2):
                h = 2 * s + hh
                hc = slice(hh * RET_DK, (hh + 1) * RET_DK)
                q16, k16, v16 = qb[:, hc].astype(BF16), kb[:, hc].astype(BF16), vb[:, hc].astype(BF16)
                st = s_ref[bl, h]
                o = _dot(q16, st.astype(BF16))
                if seq_local:
                    p = (_dot_nt(q16, k16) * mask_scr[...]).astype(BF16)
                    o = _dot(p, v16) + o
                outs.append(o)
                upd = lax.dot_general(k16, v16, (((0,), (0,)), ((), ())), preferred_element_type=F32)
                s_ref[bl, h] = (st + upd) * cdec_ref[:, h * RET_DK:(h + 1) * RET_DK]
            o2 = jnp.concatenate(outs, axis=1)
            oacc_scr[s, rows, :] = o2 if seq_local else oacc_scr[s, rows, :] + o2
        return carry

    lax.fori_loop(0, bblk, per_seq, 0, unroll=4)

    @pl.when(bb == pl.num_programs(0) - 1)
    def _():
        r = lax.broadcasted_iota(jnp.int32, (LANES, LANES), 0) // RET_DK
        cc = lax.broadcasted_iota(jnp.int32, (LANES, LANES), 1) // RET_DK
        avg = jnp.where(r == cc, 1.0 / RET_DK, 0.0).astype(BF16)

        def seg_mean(x):
            hi = x.astype(BF16)
            lo = (x - hi.astype(F32)).astype(BF16)
            return _dot(hi, avg) + _dot(lo, avg)

        normed = []
        for s in range(nslab):
            o2 = oacc_scr[s]
            dlt = o2 - seg_mean(o2)
            normed.append(dlt * lax.rsqrt(seg_mean(dlt * dlt) + EPS))
        o = jnp.concatenate(normed, axis=1) * gn_ref[...]
        o_ref[...] = jax.nn.silu(z_ref[:, 3 * RET_W:4 * RET_W]) * o


def _retention(z, row_blk0, nblk, batch, bblk, tabs, gn, s0, layer):
    cos, sa, sb, qdec, kdec, cdec = tabs
    nbb = batch // bblk
    steps = ROWS // batch
    msize = steps if steps >= LANES else ROWS
    st_spec = pl.BlockSpec((bblk, RET_HEADS, RET_DK, RET_DK), lambda bb, c: (bb, 0, 0, 0))
    st_in = pl.BlockSpec((None, bblk, RET_HEADS, RET_DK, RET_DK), lambda bb, c: (layer, bb, 0, 0, 0))
    const = lambda bb, c: (0, 0)
    return pl.pallas_call(
        functools.partial(_ret_kernel, batch, bblk),
        grid=(nbb, nblk),
        in_specs=[pl.BlockSpec((ROWS, 4 * RET_W), lambda bb, c: (row_blk0 + c, 0)),
                  pl.BlockSpec((ROWS, LANES), lambda bb, c: (c, 0)),
                  pl.BlockSpec((ROWS, LANES), lambda bb, c: (c, 0)),
                  pl.BlockSpec((ROWS, LANES), lambda bb, c: (c, 0)),
                  pl.BlockSpec((ROWS, RET_W), const),
                  pl.BlockSpec((ROWS, RET_W), const),
                  pl.BlockSpec((1, RET_W), const),
                  pl.BlockSpec((1, RET_W), const),
                  st_in],
        out_specs=[pl.BlockSpec((ROWS, RET_W), lambda bb, c: (c, 0)), st_spec],
        out_shape=[jax.ShapeDtypeStruct((nblk * ROWS, RET_W), F32),
                   jax.ShapeDtypeStruct((batch, RET_HEADS, RET_DK, RET_DK), F32)],
        scratch_shapes=[pltpu.VMEM((RET_W // LANES, ROWS, LANES), F32)] * 3
        + [pltpu.VMEM((msize, msize), F32), pltpu.VMEM((RET_W // LANES, ROWS, LANES), F32)],
        compiler_params=_params(("arbitrary", "arbitrary"), 56),
        name="retention",
    )(z, cos, sa, sb, qdec, kdec, cdec, gn, s0)


def _retention_tables(pos, batch):
    half = RET_DK // 2
    freqs = ROPE_BASE ** (-jnp.arange(half, dtype=F32) / half)
    ang = pos[:, None] * freqs[None, :]
    cos, sin = jnp.cos(ang), jnp.sin(ang)
    zero = jnp.zeros_like(sin)
    reps = LANES // RET_DK
    cos_t = jnp.tile(jnp.concatenate([cos, cos], axis=1), (1, reps))
    sa_t = jnp.tile(jnp.concatenate([zero, sin], axis=1), (1, reps))
    sb_t = jnp.tile(jnp.concatenate([-sin, zero], axis=1), (1, reps))
    lg = jnp.log1p(-jnp.exp2(-5.0 - jnp.arange(RET_HEADS, dtype=F32)))
    steps = ROWS // batch
    i1 = (jnp.arange(ROWS) // batch).astype(F32) + 1.0
    qdec = jnp.repeat(jnp.exp(i1[:, None] * lg[None, :]), RET_DK, axis=1)
    kdec = jnp.repeat(jnp.exp(-i1[:, None] * lg[None, :]), RET_DK, axis=1) * (RET_DK ** -0.5)
    cdec = jnp.repeat(jnp.exp(steps * lg), RET_DK)[None, :]
    return cos_t, sa_t, sb_t, qdec, kdec, cdec


def _s5_disc_kernel(are_ref, aim_ref, ldt_ref, bre_ref, bim_ref, abre_ref, abim_ref, bbre_ref, bbim_ref):
    ar, ai = are_ref[...], aim_ref[...]
    dt = jnp.exp(ldt_ref[...])
    dar, dai = dt * ar, dt * ai
    mag = jnp.exp(dar)
    abar_re, abar_im = mag * jnp.cos(dai), mag * jnp.sin(dai)
    den = ar * ar + ai * ai
    nr, ni = abar_re - 1.0, abar_im
    f_re = (nr * ar + ni * ai) / den
    f_im = (ni * ar - nr * ai) / den
    abre_ref[...] = abar_re
    abim_ref[...] = abar_im
    br, bi = bre_ref[...], bim_ref[...]
    bbre_ref[...] = f_re[:, None, :] * br - f_im[:, None, :] * bi
    bbim_ref[...] = f_re[:, None, :] * bi + f_im[:, None, :] * br


def _s5_discretise(a_re, a_im, log_dt, b_re_t, b_im_t):
    lg = a_re.shape[0]
    small = jax.ShapeDtypeStruct((lg, SSM_P), F32)
    big = jax.ShapeDtypeStruct((lg, SSM_GC, SSM_P), F32)
    return pl.pallas_call(_s5_disc_kernel, out_shape=[small, small, big, big], name="s5_disc")(
        a_re, a_im, log_dt, b_re_t, b_im_t)


def _s5_kernel(batch, u_ref, bmat_ref, cmat_ref, are_ref, aim_ref, d_ref, h0re_ref, h0im_ref,
               y_ref, xre_ref, xim_ref, x_scr):
    c = pl.program_id(0)
    steps = ROWS // batch
    half = SSM_SLAB_N

    @pl.when(c == 0)
    def _():
        xre_ref[...] = h0re_ref[...]
        xim_ref[...] = h0im_ref[...]

    u = u_ref[...]
    ub = u.astype(BF16)
    for s in range(SSM_SLABS):
        x_scr[:, 2 * half * s:2 * half * (s + 1)] = _dot(ub[:, s * LANES:(s + 1) * LANES], bmat_ref[s])

    for s in range(SSM_SLABS):
        re0 = 2 * half * s
        im0 = re0 + half
        sc = slice(half * s, half * (s + 1))
        ar = jnp.broadcast_to(are_ref[:, sc], (SUBLANES, half))
        ai = jnp.broadcast_to(aim_ref[:, sc], (SUBLANES, half))

        def row_tile(rt, carry, re0=re0, im0=im0, sc=sc, ar=ar, ai=ai):
            r0 = pl.multiple_of(rt * SUBLANES, SUBLANES)

            def step(t, x):
                xr, xi = x
                row = pl.multiple_of(t * batch + r0, SUBLANES)
                nr = ar * xr - ai * xi + x_scr[pl.ds(row, SUBLANES), re0:re0 + half]
                ni = ar * xi + ai * xr + x_scr[pl.ds(row, SUBLANES), im0:im0 + half]
                x_scr[pl.ds(row, SUBLANES), re0:re0 + half] = nr
                x_scr[pl.ds(row, SUBLANES), im0:im0 + half] = ni
                return nr, ni

            init = (xre_ref[pl.ds(r0, SUBLANES), sc], xim_ref[pl.ds(r0, SUBLANES), sc])
            xr, xi = lax.fori_loop(0, steps, step, init, unroll=8)
            xre_ref[pl.ds(r0, SUBLANES), sc] = xr
            xim_ref[pl.ds(r0, SUBLANES), sc] = xi
            return carry

        lax.fori_loop(0, batch // SUBLANES, row_tile, 0)

    ys = [_dot(x_scr[:, 2 * half * s:2 * half * (s + 1)].astype(BF16), cmat_ref[s]) for s in range(SSM_SLABS)]
    y = jnp.concatenate(ys, axis=1) + d_ref[...] * u
    y_ref[...] = jax.nn.gelu(y)


def _s5(z, row_blk0, nblk, batch, bmat, cmat, abre, abim, d, h0re, h0im):
    const2 = lambda c: (0, 0)
    const3 = lambda c: (0, 0, 0)
    st = pl.BlockSpec((batch, SSM_N), const2)
    return pl.pallas_call(
        functools.partial(_s5_kernel, batch),
        grid=(nblk,),
        in_specs=[pl.BlockSpec((ROWS, SSM_W), lambda c: (row_blk0 + c, 4)),
                  pl.BlockSpec((SSM_SLABS, LANES, 2 * SSM_SLAB_N), const3),
                  pl.BlockSpec((SSM_SLABS, 2 * SSM_SLAB_N, LANES), const3),
                  pl.BlockSpec((1, SSM_N), const2),
                  pl.BlockSpec((1, SSM_N), const2),
                  pl.BlockSpec((1, SSM_W), const2),
                  st, st],
        out_specs=[pl.BlockSpec((ROWS, SSM_W), lambda c: (c, 0)), st, st],
        out_shape=[jax.ShapeDtypeStruct((nblk * ROWS, SSM_W), F32),
                   jax.ShapeDtypeStruct((batch, SSM_N), F32),
                   jax.ShapeDtypeStruct((batch, SSM_N), F32)],
        scratch_shapes=[pltpu.VMEM((ROWS, 2 * SSM_N), F32)],
        compiler_params=_params(("arbitrary",), 48),
        name="s5",
    )(z, bmat, cmat, abre, abim, d, h0re, h0im)


def _block_diag_slabs(w, nslab):
    gps = SSM_G // nslab
    eye = jnp.eye(gps, dtype=w.dtype)
    w4 = w.reshape(nslab, gps, w.shape[1], w.shape[2])
    out = w4[:, :, :, None, :] * eye[None, :, None, :, None]
    return out.reshape(nslab, gps * w.shape[1], gps * w.shape[2])


def _conv_kernel(batch, bg_ref, cg_ref, hc_ref, buf0_ref, w_ref, b_ref, o_ref, buf_ref, zp_scr):
    c = pl.program_id(0)
    pad = (CONV_K - 1) * batch

    @pl.when(c == 0)
    def _():
        zp_scr[0:pad, :] = buf0_ref[...]

    zc = cg_ref[...] * hc_ref[...]
    zp_scr[pad:pad + ROWS, :] = zc
    y = b_ref[...]
    for j in range(CONV_K):
        y = y + w_ref[j:j + 1, :] * zp_scr[j * batch:j * batch + ROWS, :]
    o_ref[...] = bg_ref[...] * y
    tail = zp_scr[ROWS:ROWS + pad, :]
    buf_ref[...] = tail
    zp_scr[0:pad, :] = tail


def _conv(z, row_blk0, nblk, batch, buf0, w, b):
    pad = (CONV_K - 1) * batch
    const = lambda c: (0, 0)
    return pl.pallas_call(
        functools.partial(_conv_kernel, batch),
        grid=(nblk,),
        in_specs=[pl.BlockSpec((ROWS, CONV_W), lambda c: (row_blk0 + c, 5)),
                  pl.BlockSpec((ROWS, CONV_W), lambda c: (row_blk0 + c, 6)),
                  pl.BlockSpec((ROWS, CONV_W), lambda c: (row_blk0 + c, 7)),
                  pl.BlockSpec((pad, CONV_W), const),
                  pl.BlockSpec((CONV_K, CONV_W), const),
                  pl.BlockSpec((1, CONV_W), const)],
        out_specs=[pl.BlockSpec((ROWS, CONV_W), lambda c: (c, 0)), pl.BlockSpec((pad, CONV_W), const)],
        out_shape=[jax.ShapeDtypeStruct((nblk * ROWS, CONV_W), F32),
                   jax.ShapeDtypeStruct((pad, CONV_W), F32)],
        scratch_shapes=[pltpu.VMEM((ROWS + pad, CONV_W), F32)],
        compiler_params=_params(("arbitrary",), 32),
        name="conv",
    )(z, z, z, buf0, w, b)


def _merge_kernel(x_ref, oa_ref, ys_ref, oc_ref, ga_ref, gb_ref, gc_ref, wr_ref, wa_ref, wb_ref, wc_ref,
                  wm_ref, gf_ref, x1_ref, xn_ref):
    oa = _dot(oa_ref[...].astype(BF16), wr_ref[...])
    ysb = ys_ref[...].astype(BF16)
    ob = _dot(ysb, wa_ref[...]) * jax.nn.sigmoid(_dot(ysb, wb_ref[...]))
    oc = _dot(oc_ref[...].astype(BF16), wc_ref[...])
    merged = (jax.nn.sigmoid(ga_ref[...]) * oa + jax.nn.sigmoid(gb_ref[...]) * ob
              + jax.nn.sigmoid(gc_ref[...]) * oc)
    x1 = x_ref[...] + _dot(merged.astype(BF16), wm_ref[...])
    x1_ref[...] = x1
    xn_ref[...] = _rms(x1, gf_ref[...]).astype(BF16)


def _merge(x, z, oa, ys, oc, wr, wa, wb, wc, wm, gf):
    t = x.shape[0]
    rb = 512
    row = lambda w: pl.BlockSpec((rb, w), lambda i: (i, 0))
    gate = lambda j: pl.BlockSpec((rb, D_MODEL), lambda i: (i, j))
    wsp = lambda k: pl.BlockSpec((k, D_MODEL), lambda i: (0, 0))
    return pl.pallas_call(
        _merge_kernel,
        grid=(t // rb,),
        in_specs=[row(D_MODEL), row(RET_W), row(SSM_W), row(CONV_W), gate(4), gate(5), gate(6),
                  wsp(RET_W), wsp(SSM_W), wsp(SSM_W), wsp(CONV_W), wsp(D_MODEL), wsp(1)],
        out_specs=[row(D_MODEL), row(D_MODEL)],
        out_shape=[jax.ShapeDtypeStruct((t, D_MODEL), F32), jax.ShapeDtypeStruct((t, D_MODEL), BF16)],
        compiler_params=_params(("parallel",), 48),
        name="merge",
    )(x, oa, ys, oc, z, z, z, wr, wa, wb, wc, wm, gf)


def _tree(items, combine):
    while len(items) > 1:
        nxt = [combine(items[i], items[i + 1]) for i in range(0, len(items) - 1, 2)]
        if len(items) % 2:
            nxt.append(items[-1])
        items = nxt
    return items[0]


def _first_max(x, y):
    (vx, ix), (vy, iy) = x, y
    return jnp.maximum(vx, vy), jnp.where(vx >= vy, ix, iy)


def _bits(x, n):
    out, rest = [], x
    for _ in range(n):
        half = jnp.floor(rest * 0.5)
        out.append(rest - 2.0 * half == 1.0)
        rest = half
    return out


def _mux(vals, bits):
    level = list(vals)
    for bit in bits:
        level = [jnp.where(bit, level[j + 1], level[j]) for j in range(0, len(level), 2)]
    return level[0]


def _top16_of_keys(s_scrs, gv_scrs, gi_scrs, v_scrs, i_scrs):
    grp = TOPK_GROUP
    ngrp = PEER_NKEYS // grp
    nbits = ngrp.bit_length() - 1

    for s_scr, gv, gi in zip(s_scrs, gv_scrs, gi_scrs):
        for g in range(ngrp):
            gv[g], gi[g] = _tree([(s_scr[g * grp + p], float(g * grp + p)) for p in range(grp)], _first_max)

    def body(r, carry):
        for s_scr, gv, gi, v_scr, i_scr in zip(s_scrs, gv_scrs, gi_scrs, v_scrs, i_scrs):
            m, idx = _tree([(gv[g], gi[g]) for g in range(ngrp)], _first_max)
            v_scr[r] = m
            i_scr[r] = idx
            gid = jnp.floor(idx * (1.0 / grp))
            rel = idx - gid * grp
            bits = _bits(gid, nbits)
            cands = []
            for p in range(grp):
                val = _mux([s_scr[g * grp + p] for g in range(ngrp)], bits)
                left = (val < m) | ((val == m) & (rel < float(p)))
                cands.append((jnp.where(left, val, -jnp.inf), float(p)))
            nv, npos = _tree(cands, _first_max)
            ni = gid * grp + npos
            for g in range(ngrp):
                hit = gid == float(g)
                gv[g] = jnp.where(hit, nv, gv[g])
                gi[g] = jnp.where(hit, ni, gi[g])
        return carry

    lax.fori_loop(0, PEER_TOPK, body, 0)


def _select_kernel(tb, xn_ref, wq_ref, k1_ref, k2_ref, e1_ref, e2_ref, g_ref,
                   s1_scr, s2_scr, gv1_scr, gi1_scr, gv2_scr, gi2_scr, v1_scr, i1_scr, v2_scr, i2_scr,
                   hv_scr, hb_scr, sc_scr, se1_scr, se2_scr):
    q = _dot(xn_ref[...], wq_ref[...]).astype(BF16)
    hq = PEER_HEADS * PEER_DQ // 2
    s1 = _dot_nt(k1_ref[...], q[:, :hq])
    s2 = _dot_nt(k2_ref[...], q[:, hq:])
    kbits = PEER_TOPK.bit_length() - 1
    for lt in range(tb // LANES):
        lanes = slice(lt * LANES, (lt + 1) * LANES)
        s1_scr[...] = s1[:, lanes].reshape(PEER_NKEYS, SUBLANES, LANES)
        s2_scr[...] = s2[:, lanes].reshape(PEER_NKEYS, SUBLANES, LANES)
        _top16_of_keys((s1_scr, s2_scr), (gv1_scr, gv2_scr), (gi1_scr, gi2_scr), (v1_scr, v2_scr), (i1_scr, i2_scr))

        for a in range(PEER_TOPK):
            hv_scr[a] = v1_scr[a] + v2_scr[0]
            hb_scr[a] = jnp.zeros((SUBLANES, LANES), F32)

        def body(r, carry):
            m, a_sel = _tree([(hv_scr[a], float(a)) for a in range(PEER_TOPK)], _first_max)
            abits = _bits(a_sel, kbits)
            b_sel = _mux([hb_scr[a] for a in range(PEER_TOPK)], abits)
            bbits = _bits(b_sel, kbits)
            sc_scr[r] = m
            se1_scr[r] = _mux([i1_scr[a] for a in range(PEER_TOPK)], abits)
            se2_scr[r] = _mux([i2_scr[b] for b in range(PEER_TOPK)], bbits)
            nb = b_sel + 1.0
            v2_next = _mux([v2_scr[(b + 1) % PEER_TOPK] for b in range(PEER_TOPK)], bbits)
            v1_sel = _mux([v1_scr[a] for a in range(PEER_TOPK)], abits)
            live = (a_sel + 1.0) * (nb + 1.0) <= float(PEER_TOPK)
            nv = jnp.where(live, v1_sel + v2_next, -jnp.inf)
            for a in range(PEER_TOPK):
                hit = a_sel == float(a)
                hv_scr[a] = jnp.where(hit, nv, hv_scr[a])
                hb_scr[a] = jnp.where(hit, nb, hb_scr[a])
            return carry

        lax.fori_loop(0, PEER_TOPK, body, 0)
        sc = sc_scr[...]
        ex = jnp.exp(sc - jnp.max(sc, axis=0, keepdims=True))
        gate = ex / jnp.sum(ex, axis=0, keepdims=True)
        rows = slice(lt * LANES, (lt + 1) * LANES)
        g_ref[rows, :] = gate.reshape(PEER_SLOTS, LANES).T
        e1_ref[rows, :] = se1_scr[...].reshape(PEER_SLOTS, LANES).T
        e2_ref[rows, :] = se2_scr[...].reshape(PEER_SLOTS, LANES).T


def _peer_select(xn, wq, k1big, k2big):
    t = xn.shape[0]
    tb = 256
    hq = PEER_HEADS * PEER_DQ // 2
    nk = PEER_NKEYS * PEER_HEADS
    const = lambda i: (0, 0)
    row = lambda dt: jax.ShapeDtypeStruct((t, PEER_SLOTS), dt)
    vec = lambda n: pltpu.VMEM((n, SUBLANES, LANES), F32)
    return pl.pallas_call(
        functools.partial(_select_kernel, tb),
        grid=(t // tb,),
        in_specs=[pl.BlockSpec((tb, D_MODEL), lambda i: (i, 0)),
                  pl.BlockSpec((D_MODEL, 2 * hq), const),
                  pl.BlockSpec((nk, hq), const),
                  pl.BlockSpec((nk, hq), const)],
        out_specs=[pl.BlockSpec((tb, PEER_SLOTS), lambda i: (i, 0))] * 3,
        out_shape=[row(F32), row(F32), row(F32)],
        scratch_shapes=[vec(PEER_NKEYS), vec(PEER_NKEYS)] + [vec(PEER_NKEYS // TOPK_GROUP)] * 4
        + [vec(PEER_TOPK), vec(PEER_TOPK), vec(PEER_TOPK), vec(PEER_TOPK),
                        vec(PEER_TOPK), vec(PEER_TOPK), vec(PEER_TOPK), vec(PEER_TOPK), vec(PEER_TOPK)],
        compiler_params=_params(("parallel",), 40),
        name="peer_select",
    )(xn, wq, k1big, k2big)


def _peer_kernel(tb, eb, stride, final_norm, xn_ref, e1_ref, e2_ref, g_ref, ut_ref, v_ref, x1_ref, gain_ref,
                 out_ref, m_scr):
    e = pl.program_id(1)
    nk1 = eb // PEER_NKEYS

    @pl.when(e == 0)
    def _():
        out_ref[...] = x1_ref[...]

    @pl.when(e == 0)
    def _():
        key = lax.broadcasted_iota(jnp.int32, (PEER_NKEYS, PEER_SLOTS), 0).astype(F32)

        def token(t, carry):
            e1 = e1_ref[pl.ds(t, 1), :]
            e2 = e2_ref[pl.ds(t, 1), :]
            gt = 0.5 * g_ref[pl.ds(t, 1), :]
            a_t = jnp.where(key == e1, gt, 0.0).astype(BF16)
            b_t = jnp.where(key == e2, 1.0, 0.0).astype(BF16)
            m_scr[pl.ds(t, PEER_NKEYS, stride=stride), :] = _dot_nt(a_t, b_t)
            return carry

        lax.fori_loop(0, tb, token, 0, unroll=64)

    s = _dot(xn_ref[...], ut_ref[...])
    t = jnp.tanh(s * (GELU_C + (GELU_C * GELU_A) * (s * s)))
    k1 = e * nk1
    gates = [m_scr[pl.ds(pl.multiple_of((k1 + i) * stride, SUBLANES), tb), :] for i in range(nk1)]
    w = ((s + s * t) * jnp.concatenate(gates, axis=1)).astype(BF16)
    out_ref[...] += _dot(w, v_ref[...])

    if final_norm:
        @pl.when(e == pl.num_programs(1) - 1)
        def _():
            out_ref[...] = _rms(out_ref[...], gain_ref[...])


def _peer_dense(xn, e1, e2, g, ut, v, x1, layer, gain, final_norm):
    t = xn.shape[0]
    tb, eb = 512, 1024
    stride = tb + SUBLANES
    once = pl.Buffered(1)
    tok = lambda w: pl.BlockSpec((tb, w), lambda i, e: (i, 0), pipeline_mode=once)
    tab = pl.BlockSpec((None, eb, D_MODEL), lambda i, e: (layer, e, 0))
    return pl.pallas_call(
        functools.partial(_peer_kernel, tb, eb, stride, final_norm),
        grid=(t // tb, PEER_NEXP // eb),
        in_specs=[tok(D_MODEL), tok(PEER_SLOTS), tok(PEER_SLOTS), tok(PEER_SLOTS),
                  pl.BlockSpec((None, D_MODEL, eb), lambda i, e: (layer, 0, e)), tab, tok(D_MODEL),
                  pl.BlockSpec((1, D_MODEL), lambda i, e: (0, 0))],
        out_specs=pl.BlockSpec((tb, D_MODEL), lambda i, e: (i, 0)),
        out_shape=jax.ShapeDtypeStruct((t, D_MODEL), F32),
        scratch_shapes=[pltpu.VMEM((PEER_NKEYS * stride, PEER_NKEYS), F32)],
        compiler_params=_params(("parallel", "arbitrary"), 60),
        name="peer_dense",
    )(xn, e1, e2, g, ut, v, x1, gain)


def _time_major(x):
    b, s, d = x.shape
    return x.transpose(1, 0, 2).reshape(s * b, d)


def _batch_major(y, b, s):
    return y.reshape(s, b, y.shape[-1]).transpose(1, 0, 2)


def kernel(x_prompt, x_sample, state_ret, state_ssm_re, state_ssm_im, state_conv, norm_mix, w_in, ret_norm, w_ret_out, ssm_a_re, ssm_a_im, ssm_b_re, ssm_b_im, ssm_c_re, ssm_c_im, ssm_d, ssm_log_dt, w_glu_a, w_glu_b, conv_w, conv_b, w_conv_out, w_mix_out, norm_ffn, peer_wq, peer_k1, peer_k2, peer_u, peer_v, norm_final):
    bp, sp, _ = x_prompt.shape
    bs, ss, _ = x_sample.shape
    tp, ts = bp * sp, bs * ss
    depth = w_in.shape[0]
    assert tp % ROWS == 0 and ts == ROWS and ROWS % bp == 0 and ROWS // bp == math.gcd(sp, RET_CHUNK)

    xs = [_time_major(x_prompt), _time_major(x_sample)]
    batches = (bp, bs)
    pos = (jnp.repeat(jnp.arange(sp, dtype=F32), bp), jnp.repeat(PAST_LEN + jnp.arange(ss, dtype=F32), bs))
    ret_tabs = [_retention_tables(p, b) for p, b in zip(pos, batches)]

    lg = depth * SSM_G
    abre, abim, bbre, bbim = _s5_discretise(
        ssm_a_re.reshape(lg, SSM_P), ssm_a_im.reshape(lg, SSM_P), ssm_log_dt.reshape(lg, 1),
        ssm_b_re.transpose(0, 1, 3, 2).reshape(lg, SSM_GC, SSM_P),
        ssm_b_im.transpose(0, 1, 3, 2).reshape(lg, SSM_GC, SSM_P))

    hq = PEER_DQ // 2
    eye = jnp.eye(PEER_HEADS, dtype=F32)

    def keys_block_diag(k):
        return (k.transpose(1, 0, 2)[:, :, None, :] * eye[None, :, :, None]).reshape(
            PEER_NKEYS * PEER_HEADS, PEER_HEADS * hq).astype(BF16)

    w_in_b = w_in.astype(BF16)
    ut = peer_u.astype(BF16).transpose(0, 2, 1)
    vt = peer_v.astype(BF16)
    zero_ret = jnp.zeros((1, bp, RET_HEADS, RET_DK, RET_DK), F32)

    states = [[[] for _ in range(4)] for _ in range(2)]
    for l in range(depth):
        sl = slice(l * SSM_G, (l + 1) * SSM_G)
        bmat = jnp.concatenate([_block_diag_slabs(bbre[sl], SSM_SLABS), _block_diag_slabs(bbim[sl], SSM_SLABS)],
                               axis=2).astype(BF16)
        cmat = jnp.concatenate([_block_diag_slabs(ssm_c_re[l].transpose(0, 2, 1), SSM_SLABS),
                                _block_diag_slabs(-ssm_c_im[l].transpose(0, 2, 1), SSM_SLABS)],
                               axis=1).astype(BF16)
        are_row = abre[sl].reshape(1, SSM_N)
        aim_row = abim[sl].reshape(1, SSM_N)
        proj =[w.astype(BF16) for w in (w_ret_out[l], w_glu_a[l], w_glu_b[l], w_conv_out[l], w_mix_out[l])]
        wq = peer_wq[l].reshape(D_MODEL, PEER_HEADS, 2, hq).transpose(0, 2, 1, 3).reshape(D_MODEL, -1).astype(BF16)
        k1big, k2big = keys_block_diag(peer_k1[l]), keys_block_diag(peer_k2[l])

        for gi, batch in enumerate(batches):
            x = xs[gi]
            nblk = x.shape[0] // ROWS
            if gi == 0:
                s0, s0_layer = zero_ret, 0
                h0re = jnp.zeros((batch, SSM_N), F32)
                h0im = jnp.zeros((batch, SSM_N), F32)
                buf0 = jnp.zeros(((CONV_K - 1) * batch, CONV_W), F32)
                bblk = batch
            else:
                s0, s0_layer = state_ret, l
                h0re = state_ssm_re[l].reshape(batch, SSM_N)
                h0im = state_ssm_im[l].reshape(batch, SSM_N)
                buf0 = state_conv[l].transpose(1, 0, 2).reshape((CONV_K - 1) * batch, CONV_W)
                bblk = 16
            z = _inproj(x, norm_mix[l][None, :], w_in_b, l)
            oa, s_new = _retention(z, 0, nblk, batch, bblk, ret_tabs[gi], ret_norm[l][None, :], s0, s0_layer)
            ys, xre, xim = _s5(z, 0, nblk, batch, bmat, cmat, are_row, aim_row, ssm_d[l][None, :], h0re, h0im)
            oc, buf = _conv(z, 0, nblk, batch, buf0, conv_w[l], conv_b[l][None, :])
            st = states[gi]
            st[0].append(s_new)
            st[1].append(xre.reshape(batch, SSM_G, SSM_P))
            st[2].append(xim.reshape(batch, SSM_G, SSM_P))
            st[3].append(buf.reshape(CONV_K - 1, batch, CONV_W).transpose(1, 0, 2))

            x1, xn = _merge(x, z, oa, ys, oc, *proj, norm_ffn[l][None, :])
            e1, e2, g = _peer_select(xn, wq, k1big, k2big)
            xs[gi] = _peer_dense(xn, e1, e2, g, ut, vt, x1, l, norm_final[None, :], l == depth - 1)

    y_prompt = _batch_major(xs[0], bp, sp)
    y_sample = _batch_major(xs[1], bs, ss)
    (ret_p, re_p, im_p, cv_p), (ret_s, re_s, im_s, cv_s) = states
    return (y_prompt, y_sample,
            jnp.stack(ret_p), jnp.stack(ret_s),
            jnp.stack(re_p), jnp.stack(re_s),
            jnp.stack(im_p), jnp.stack(im_s),
            jnp.stack(cv_p), jnp.stack(cv_s))
```

```python
import functools
import math

import jax
import jax.numpy as jnp
from jax import lax
from jax.experimental import pallas as pl
from jax.experimental.pallas import tpu as pltpu

F32 = jnp.float32
BF16 = jnp.bfloat16

D_MODEL = 1024
DEPTH = 2
PAST_LEN = 16384
RET_HEADS = 8
RET_DK = 64
RET_W = 512
RET_CHUNK = 128
ROPE_BASE = 10000.0
SSM_W = 512
SSM_GC = 16
SSM_G = 32
SSM_P = 64
SSM_N = SSM_G * SSM_P
SSM_SLABS = 4
SSM_SLAB_N = SSM_N // SSM_SLABS
CONV_W = 512
CONV_K = 3
PROJ_W = 7168
PEER_HEADS = 8
PEER_DQ = 256
PEER_NKEYS = 128
PEER_TOPK = 16
PEER_NEXP = PEER_NKEYS ** 2
PEER_SLOTS = PEER_HEADS * PEER_TOPK
TOPK_GROUP = 8
EPS = 1e-6
GELU_C = math.sqrt(2.0 / math.pi)
GELU_A = 0.044715

ROWS = 1024
LANES = 128
SUBLANES = 8
MXU_DEPTH = 256
MIB = 1024 * 1024


def _params(sem, vmem_mib):
    return pltpu.CompilerParams(dimension_semantics=sem, vmem_limit_bytes=vmem_mib * MIB)


def _rms(x, g):
    return x * lax.rsqrt(jnp.mean(x * x, axis=-1, keepdims=True) + EPS) * g


def _dot(a, b):
    return jnp.dot(a, b, preferred_element_type=F32)


def _dot_nt(a, b):
    return lax.dot_general(a, b, (((1,), (1,)), ((), ())), preferred_element_type=F32)


def _inproj_kernel(x_ref, g_ref, w_ref, z_ref, h_scr):
    @pl.when(pl.program_id(1) == 0)
    def _():
        h_scr[...] = _rms(x_ref[...], g_ref[...]).astype(BF16)

    z_ref[...] = _dot(h_scr[...], w_ref[...])


def _inproj_bm_kernel(x_ref, g_ref, w_ref, z_ref, xtm_ref, h_scr):
    @pl.when(pl.program_id(1) == 0)
    def _():
        x = pltpu.einshape("btd->(tb)d", x_ref[...])
        xtm_ref[...] = x
        h_scr[...] = _rms(x, g_ref[...]).astype(BF16)

    z_ref[...] = _dot(h_scr[...], w_ref[...])


def _inproj(x, g, w, layer):
    batch_major = x.ndim == 3
    nb = 1024
    if batch_major:
        batch, seq, _ = x.shape
        t = batch * seq
        x_spec = pl.BlockSpec((batch, ROWS // batch, D_MODEL), lambda i, j: (0, i, 0))
    else:
        t = x.shape[0]
        x_spec = pl.BlockSpec((ROWS, D_MODEL), lambda i, j: (i, 0))
    z_spec = pl.BlockSpec((ROWS, nb), lambda i, j: (i, j))
    z_shape = jax.ShapeDtypeStruct((t, PROJ_W), F32)
    return pl.pallas_call(
        _inproj_bm_kernel if batch_major else _inproj_kernel,
        grid=(t // ROWS, PROJ_W // nb),
        in_specs=[x_spec,
                  pl.BlockSpec((1, D_MODEL), lambda i, j: (0, 0)),
                  pl.BlockSpec((None, D_MODEL, nb), lambda i, j: (layer, 0, j))],
        out_specs=[z_spec, pl.BlockSpec((ROWS, D_MODEL), lambda i, j: (i, 0))] if batch_major else z_spec,
        out_shape=[z_shape, jax.ShapeDtypeStruct((t, D_MODEL), F32)] if batch_major else z_shape,
        scratch_shapes=[pltpu.VMEM((ROWS, D_MODEL), BF16)],
        compiler_params=_params(("parallel", "arbitrary"), 48),
        name="inproj",
    )(x, g, w)


def _ret_kernel(batch, bblk, z_ref, cos_ref, sa_ref, sb_ref, qdec_ref, kdec_ref, cdec_ref, gn_ref, s0_ref,
                o_ref, s_ref, qd_scr, kd_scr, v_scr, mask_scr, oacc_scr):
    bb = pl.program_id(0)
    c = pl.program_id(1)
    steps = ROWS // batch
    nslab = RET_W // LANES

    def head_view(ref, h):
        return ref[h // 2, :, (h % 2) * RET_DK:(h % 2 + 1) * RET_DK]

    seq_local = steps >= LANES
    msize = steps if seq_local else ROWS

    @pl.when((bb == 0) & (c == 0))
    def _():
        r = lax.broadcasted_iota(jnp.int32, (msize, msize), 0)
        cc = lax.broadcasted_iota(jnp.int32, (msize, msize), 1)
        if seq_local:
            mask_scr[...] = (r >= cc).astype(F32)
        else:
            same = (r & (batch - 1)) == (cc & (batch - 1))
            mask_scr[...] = (same & (r >= cc)).astype(F32)

    @pl.when(c == 0)
    def _():
        s_ref[...] = s0_ref[...]

    @pl.when(bb == 0)
    def _():
        cos, sa, sb = cos_ref[...], sa_ref[...], sb_ref[...]

        def rot(x):
            return x * cos + pltpu.roll(x, 32, 1) * sa + pltpu.roll(x, 96, 1) * sb

        for s in range(nslab):
            cols = slice(s * LANES, (s + 1) * LANES)
            qd_scr[s] = rot(z_ref[:, cols]) * qdec_ref[:, cols]
            kd_scr[s] = rot(z_ref[:, RET_W + s * LANES:RET_W + (s + 1) * LANES]) * kdec_ref[:, cols]
            v_scr[s] = z_ref[:, 2 * RET_W + s * LANES:2 * RET_W + (s + 1) * LANES]
        if not seq_local:
            for s in range(nslab):
                outs = []
                for h in (2 * s, 2 * s + 1):
                    qh = head_view(qd_scr, h).astype(BF16)
                    kh = head_view(kd_scr, h).astype(BF16)
                    vh = head_view(v_scr, h).astype(BF16)
                    p = (_dot_nt(qh, kh) * mask_scr[...]).astype(BF16)
                    outs.append(_dot(p, vh))
                oacc_scr[s] = jnp.concatenate(outs, axis=1)

    def per_seq(bl, carry):
        b = bb * bblk + bl
        rows = pl.ds(b, steps, stride=batch)
        for s in range(nslab):
            qb = qd_scr[s, rows, :]
            kb = kd_scr[s, rows, :]
            vb = v_scr[s, rows, :]
            outs = []
            for hh in range(2):
                h = 2 * s + hh
                hc = slice(hh * RET_DK, (hh + 1) * RET_DK)
                q16, k16, v16 = qb[:, hc].astype(BF16), kb[:, hc].astype(BF16), vb[:, hc].astype(BF16)
                st = s_ref[bl, h]
                o = _dot(q16, st.astype(BF16))
                if seq_local:
                    p = (_dot_nt(q16, k16) * mask_scr[...]).astype(BF16)
                    o = _dot(p, v16) + o
                outs.append(o)
                upd = lax.dot_general(k16, v16, (((0,), (0,)), ((), ())), preferred_element_type=F32)
                s_ref[bl, h] = (st + upd) * cdec_ref[:, h * RET_DK:(h + 1) * RET_DK]
            o2 = jnp.concatenate(outs, axis=1)
            oacc_scr[s, rows, :] = o2 if seq_local else oacc_scr[s, rows, :] + o2
        return carry

    lax.fori_loop(0, bblk, per_seq, 0, unroll=4)

    @pl.when(bb == pl.num_programs(0) - 1)
    def _():
        r = lax.broadcasted_iota(jnp.int32, (LANES, LANES), 0) // RET_DK
        cc = lax.broadcasted_iota(jnp.int32, (LANES, LANES), 1) // RET_DK
        avg = jnp.where(r == cc, 1.0 / RET_DK, 0.0).astype(BF16)

        def seg_mean(x):
            hi = x.astype(BF16)
            lo = (x - hi.astype(F32)).astype(BF16)
            return _dot(hi, avg) + _dot(lo, avg)

        normed = []
        for s in range(nslab):
            o2 = oacc_scr[s]
            dlt = o2 - seg_mean(o2)
            normed.append(dlt * lax.rsqrt(seg_mean(dlt * dlt) + EPS))
        o = jnp.concatenate(normed, axis=1) * gn_ref[...]
        o_ref[...] = jax.nn.silu(z_ref[:, 3 * RET_W:4 * RET_W]) * o


def _retention(z, row_blk0, nblk, batch, bblk, tabs, gn, s0, layer):
    cos, sa, sb, qdec, kdec, cdec = tabs
    nbb = batch // bblk
    steps = ROWS // batch
    msize = steps if steps >= LANES else ROWS
    st_spec = pl.BlockSpec((bblk, RET_HEADS, RET_DK, RET_DK), lambda bb, c: (bb, 0, 0, 0))
    st_in = pl.BlockSpec((None, bblk, RET_HEADS, RET_DK, RET_DK), lambda bb, c: (layer, bb, 0, 0, 0))
    const = lambda bb, c: (0, 0)
    return pl.pallas_call(
        functools.partial(_ret_kernel, batch, bblk),
        grid=(nbb, nblk),
        in_specs=[pl.BlockSpec((ROWS, 4 * RET_W), lambda bb, c: (row_blk0 + c, 0)),
                  pl.BlockSpec((ROWS, LANES), lambda bb, c: (c, 0)),
                  pl.BlockSpec((ROWS, LANES), lambda bb, c: (c, 0)),
                  pl.BlockSpec((ROWS, LANES), lambda bb, c: (c, 0)),
                  pl.BlockSpec((ROWS, RET_W), const),
                  pl.BlockSpec((ROWS, RET_W), const),
                  pl.BlockSpec((1, RET_W), const),
                  pl.BlockSpec((1, RET_W), const),
                  st_in],
        out_specs=[pl.BlockSpec((ROWS, RET_W), lambda bb, c: (c, 0)), st_spec],
        out_shape=[jax.ShapeDtypeStruct((nblk * ROWS, RET_W), F32),
                   jax.ShapeDtypeStruct((batch, RET_HEADS, RET_DK, RET_DK), F32)],
        scratch_shapes=[pltpu.VMEM((RET_W // LANES, ROWS, LANES), F32)] * 3
        + [pltpu.VMEM((msize, msize), F32), pltpu.VMEM((RET_W // LANES, ROWS, LANES), F32)],
        compiler_params=_params(("arbitrary", "arbitrary"), 56),
        name="retention",
    )(z, cos, sa, sb, qdec, kdec, cdec, gn, s0)


def _retention_tables(pos, batch):
    half = RET_DK // 2
    freqs = ROPE_BASE ** (-jnp.arange(half, dtype=F32) / half)
    ang = pos[:, None] * freqs[None, :]
    cos, sin = jnp.cos(ang), jnp.sin(ang)
    zero = jnp.zeros_like(sin)
    reps = LANES // RET_DK
    cos_t = jnp.tile(jnp.concatenate([cos, cos], axis=1), (1, reps))
    sa_t = jnp.tile(jnp.concatenate([zero, sin], axis=1), (1, reps))
    sb_t = jnp.tile(jnp.concatenate([-sin, zero], axis=1), (1, reps))
    lg = jnp.log1p(-jnp.exp2(-5.0 - jnp.arange(RET_HEADS, dtype=F32)))
    steps = ROWS // batch
    i1 = (jnp.arange(ROWS) // batch).astype(F32) + 1.0
    qdec = jnp.repeat(jnp.exp(i1[:, None] * lg[None, :]), RET_DK, axis=1)
    kdec = jnp.repeat(jnp.exp(-i1[:, None] * lg[None, :]), RET_DK, axis=1) * (RET_DK ** -0.5)
    cdec = jnp.repeat(jnp.exp(steps * lg), RET_DK)[None, :]
    return cos_t, sa_t, sb_t, qdec, kdec, cdec


def _s5_disc_kernel(are_ref, aim_ref, ldt_ref, bre_ref, bim_ref, abre_ref, abim_ref, bbre_ref, bbim_ref):
    ar, ai = are_ref[...], aim_ref[...]
    dt = jnp.exp(ldt_ref[...])
    dar, dai = dt * ar, dt * ai
    mag = jnp.exp(dar)
    abar_re, abar_im = mag * jnp.cos(dai), mag * jnp.sin(dai)
    den = ar * ar + ai * ai
    nr, ni = abar_re - 1.0, abar_im
    f_re = (nr * ar + ni * ai) / den
    f_im = (ni * ar - nr * ai) / den
    abre_ref[...] = abar_re
    abim_ref[...] = abar_im
    br, bi = bre_ref[...], bim_ref[...]
    bbre_ref[...] = f_re[:, None, :] * br - f_im[:, None, :] * bi
    bbim_ref[...] = f_re[:, None, :] * bi + f_im[:, None, :] * br


def _s5_discretise(a_re, a_im, log_dt, b_re_t, b_im_t):
    lg = a_re.shape[0]
    small = jax.ShapeDtypeStruct((lg, SSM_P), F32)
    big = jax.ShapeDtypeStruct((lg, SSM_GC, SSM_P), F32)
    return pl.pallas_call(_s5_disc_kernel, out_shape=[small, small, big, big], name="s5_disc")(
        a_re, a_im, log_dt, b_re_t, b_im_t)


def _s5_kernel(batch, u_ref, bmat_ref, cmat_ref, are_ref, aim_ref, d_ref, h0re_ref, h0im_ref,
               y_ref, xre_ref, xim_ref, x_scr):
    c = pl.program_id(0)
    steps = ROWS // batch
    half = SSM_SLAB_N

    @pl.when(c == 0)
    def _():
        xre_ref[...] = h0re_ref[...]
        xim_ref[...] = h0im_ref[...]

    u = u_ref[...]
    ub = u.astype(BF16)
    for s in range(SSM_SLABS):
        x_scr[:, 2 * half * s:2 * half * (s + 1)] = _dot(ub[:, s * LANES:(s + 1) * LANES], bmat_ref[s])

    for s in range(SSM_SLABS):
        re0 = 2 * half * s
        im0 = re0 + half
        sc = slice(half * s, half * (s + 1))
        ar = jnp.broadcast_to(are_ref[:, sc], (SUBLANES, half))
        ai = jnp.broadcast_to(aim_ref[:, sc], (SUBLANES, half))

        def row_tile(rt, carry, re0=re0, im0=im0, sc=sc, ar=ar, ai=ai):
            r0 = pl.multiple_of(rt * SUBLANES, SUBLANES)

            def step(t, x):
                xr, xi = x
                row = pl.multiple_of(t * batch + r0, SUBLANES)
                nr = ar * xr - ai * xi + x_scr[pl.ds(row, SUBLANES), re0:re0 + half]
                ni = ar * xi + ai * xr + x_scr[pl.ds(row, SUBLANES), im0:im0 + half]
                x_scr[pl.ds(row, SUBLANES), re0:re0 + half] = nr
                x_scr[pl.ds(row, SUBLANES), im0:im0 + half] = ni
                return nr, ni

            init = (xre_ref[pl.ds(r0, SUBLANES), sc], xim_ref[pl.ds(r0, SUBLANES), sc])
            xr, xi = lax.fori_loop(0, steps, step, init, unroll=8)
            xre_ref[pl.ds(r0, SUBLANES), sc] = xr
            xim_ref[pl.ds(r0, SUBLANES), sc] = xi
            return carry

        lax.fori_loop(0, batch // SUBLANES, row_tile, 0)

    ys = [_dot(x_scr[:, 2 * half * s:2 * half * (s + 1)].astype(BF16), cmat_ref[s]) for s in range(SSM_SLABS)]
    y = jnp.concatenate(ys, axis=1) + d_ref[...] * u
    y_ref[...] = jax.nn.gelu(y)


def _s5(z, row_blk0, nblk, batch, bmat, cmat, abre, abim, d, h0re, h0im):
    const2 = lambda c: (0, 0)
    const3 = lambda c: (0, 0, 0)
    st = pl.BlockSpec((batch, SSM_N), const2)
    return pl.pallas_call(
        functools.partial(_s5_kernel, batch),
        grid=(nblk,),
        in_specs=[pl.BlockSpec((ROWS, SSM_W), lambda c: (row_blk0 + c, 4)),
                  pl.BlockSpec((SSM_SLABS, LANES, 2 * SSM_SLAB_N), const3),
                  pl.BlockSpec((SSM_SLABS, 2 * SSM_SLAB_N, LANES), const3),
                  pl.BlockSpec((1, SSM_N), const2),
                  pl.BlockSpec((1, SSM_N), const2),
                  pl.BlockSpec((1, SSM_W), const2),
                  st, st],
        out_specs=[pl.BlockSpec((ROWS, SSM_W), lambda c: (c, 0)), st, st],
        out_shape=[jax.ShapeDtypeStruct((nblk * ROWS, SSM_W), F32),
                   jax.ShapeDtypeStruct((batch, SSM_N), F32),
                   jax.ShapeDtypeStruct((batch, SSM_N), F32)],
        scratch_shapes=[pltpu.VMEM((ROWS, 2 * SSM_N), F32)],
        compiler_params=_params(("arbitrary",), 48),
        name="s5",
    )(z, bmat, cmat, abre, abim, d, h0re, h0im)


def _block_diag_slabs(w, nslab):
    gps = SSM_G // nslab
    eye = jnp.eye(gps, dtype=w.dtype)
    w4 = w.reshape(nslab, gps, w.shape[1], w.shape[2])
    out = w4[:, :, :, None, :] * eye[None, :, None, :, None]
    return out.reshape(nslab, gps * w.shape[1], gps * w.shape[2])


def _conv_kernel(batch, bg_ref, cg_ref, hc_ref, buf0_ref, w_ref, b_ref, o_ref, buf_ref, zp_scr):
    c = pl.program_id(0)
    pad = (CONV_K - 1) * batch

    @pl.when(c == 0)
    def _():
        zp_scr[0:pad, :] = buf0_ref[...]

    zc = cg_ref[...] * hc_ref[...]
    zp_scr[pad:pad + ROWS, :] = zc
    y = b_ref[...]
    for j in range(CONV_K):
        y = y + w_ref[j:j + 1, :] * zp_scr[j * batch:j * batch + ROWS, :]
    o_ref[...] = bg_ref[...] * y
    tail = zp_scr[ROWS:ROWS + pad, :]
    buf_ref[...] = tail
    zp_scr[0:pad, :] = tail


def _conv(z, row_blk0, nblk, batch, buf0, w, b):
    pad = (CONV_K - 1) * batch
    const = lambda c: (0, 0)
    return pl.pallas_call(
        functools.partial(_conv_kernel, batch),
        grid=(nblk,),
        in_specs=[pl.BlockSpec((ROWS, CONV_W), lambda c: (row_blk0 + c, 5)),
                  pl.BlockSpec((ROWS, CONV_W), lambda c: (row_blk0 + c, 6)),
                  pl.BlockSpec((ROWS, CONV_W), lambda c: (row_blk0 + c, 7)),
                  pl.BlockSpec((pad, CONV_W), const),
                  pl.BlockSpec((CONV_K, CONV_W), const),
                  pl.BlockSpec((1, CONV_W), const)],
        out_specs=[pl.BlockSpec((ROWS, CONV_W), lambda c: (c, 0)), pl.BlockSpec((pad, CONV_W), const)],
        out_shape=[jax.ShapeDtypeStruct((nblk * ROWS, CONV_W), F32),
                   jax.ShapeDtypeStruct((pad, CONV_W), F32)],
        scratch_shapes=[pltpu.VMEM((ROWS + pad, CONV_W), F32)],
        compiler_params=_params(("arbitrary",), 32),
        name="conv",
    )(z, z, z, buf0, w, b)


def _merge_kernel(x_ref, oa_ref, ys_ref, oc_ref, ga_ref, gb_ref, gc_ref, wr_ref, wa_ref, wb_ref, wc_ref,
                  wm_ref, gf_ref, x1_ref, xn_ref):
    oa = _dot(oa_ref[...].astype(BF16), wr_ref[...])
    ysb = ys_ref[...].astype(BF16)
    ob = _dot(ysb, wa_ref[...]) * jax.nn.sigmoid(_dot(ysb, wb_ref[...]))
    oc = _dot(oc_ref[...].astype(BF16), wc_ref[...])
    merged = (jax.nn.sigmoid(ga_ref[...]) * oa + jax.nn.sigmoid(gb_ref[...]) * ob
              + jax.nn.sigmoid(gc_ref[...]) * oc)
    x1 = x_ref[...] + _dot(merged.astype(BF16), wm_ref[...])
    x1_ref[...] = x1
    xn_ref[...] = _rms(x1, gf_ref[...]).astype(BF16)


def _merge(x, z, oa, ys, oc, wr, wa, wb, wc, wm, gf):
    t = x.shape[0]
    rb = 512
    row = lambda w: pl.BlockSpec((rb, w), lambda i: (i, 0))
    gate = lambda j: pl.BlockSpec((rb, D_MODEL), lambda i: (i, j))
    wsp = lambda k: pl.BlockSpec((k, D_MODEL), lambda i: (0, 0))
    return pl.pallas_call(
        _merge_kernel,
        grid=(t // rb,),
        in_specs=[row(D_MODEL), row(RET_W), row(SSM_W), row(CONV_W), gate(4), gate(5), gate(6),
                  wsp(RET_W), wsp(SSM_W), wsp(SSM_W), wsp(CONV_W), wsp(D_MODEL), wsp(1)],
        out_specs=[row(D_MODEL), row(D_MODEL)],
        out_shape=[jax.ShapeDtypeStruct((t, D_MODEL), F32), jax.ShapeDtypeStruct((t, D_MODEL), BF16)],
        compiler_params=_params(("parallel",), 48),
        name="merge",
    )(x, oa, ys, oc, z, z, z, wr, wa, wb, wc, wm, gf)


def _tree(items, combine):
    while len(items) > 1:
        nxt = [combine(items[i], items[i + 1]) for i in range(0, len(items) - 1, 2)]
        if len(items) % 2:
            nxt.append(items[-1])
        items = nxt
    return items[0]


def _first_max(x, y):
    (vx, ix), (vy, iy) = x, y
    return jnp.maximum(vx, vy), jnp.where(vx >= vy, ix, iy)


def _bits(x, n):
    out, rest = [], x
    for _ in range(n):
        half = jnp.floor(rest * 0.5)
        out.append(rest - 2.0 * half == 1.0)
        rest = half
    return out


def _mux(vals, bits):
    level = list(vals)
    for bit in bits:
        level = [jnp.where(bit, level[j + 1], level[j]) for j in range(0, len(level), 2)]
    return level[0]


def _top16_of_keys(s_scrs, gv_scrs, gi_scrs, v_scrs, i_scrs):
    grp = TOPK_GROUP
    ngrp = PEER_NKEYS // grp
    nbits = ngrp.bit_length() - 1

    for s_scr, gv, gi in zip(s_scrs, gv_scrs, gi_scrs):
        for g in range(ngrp):
            gv[g], gi[g] = _tree([(s_scr[g * grp + p], float(g * grp + p)) for p in range(grp)], _first_max)

    def body(r, carry):
        for s_scr, gv, gi, v_scr, i_scr in zip(s_scrs, gv_scrs, gi_scrs, v_scrs, i_scrs):
            m, idx = _tree([(gv[g], gi[g]) for g in range(ngrp)], _first_max)
            v_scr[r] = m
            i_scr[r] = idx
            gid = jnp.floor(idx * (1.0 / grp))
            rel = idx - gid * grp
            bits = _bits(gid, nbits)
            cands = []
            for p in range(grp):
                val = _mux([s_scr[g * grp + p] for g in range(ngrp)], bits)
                left = (val < m) | ((val == m) & (rel < float(p)))
                cands.append((jnp.where(left, val, -jnp.inf), float(p)))
            nv, npos = _tree(cands, _first_max)
            ni = gid * grp + npos
            for g in range(ngrp):
                hit = gid == float(g)
                gv[g] = jnp.where(hit, nv, gv[g])
                gi[g] = jnp.where(hit, ni, gi[g])
        return carry

    lax.fori_loop(0, PEER_TOPK, body, 0)


def _select_kernel(tb, xn_ref, wq_ref, k1_ref, k2_ref, e1_ref, e2_ref, g_ref,
                   s1_scr, s2_scr, gv1_scr, gi1_scr, gv2_scr, gi2_scr, v1_scr, i1_scr, v2_scr, i2_scr,
                   hv_scr, hb_scr, sc_scr, se1_scr, se2_scr):
    q = _dot(xn_ref[...], wq_ref[...]).astype(BF16)
    hq = PEER_HEADS * PEER_DQ // 2
    s1 = _dot_nt(k1_ref[...], q[:, :hq])
    s2 = _dot_nt(k2_ref[...], q[:, hq:])
    kbits = PEER_TOPK.bit_length() - 1
    for lt in range(tb // LANES):
        lanes = slice(lt * LANES, (lt + 1) * LANES)
        s1_scr[...] = s1[:, lanes].reshape(PEER_NKEYS, SUBLANES, LANES)
        s2_scr[...] = s2[:, lanes].reshape(PEER_NKEYS, SUBLANES, LANES)
        _top16_of_keys((s1_scr, s2_scr), (gv1_scr, gv2_scr), (gi1_scr, gi2_scr), (v1_scr, v2_scr), (i1_scr, i2_scr))

        for a in range(PEER_TOPK):
            hv_scr[a] = v1_scr[a] + v2_scr[0]
            hb_scr[a] = jnp.zeros((SUBLANES, LANES), F32)

        def body(r, carry):
            m, a_sel = _tree([(hv_scr[a], float(a)) for a in range(PEER_TOPK)], _first_max)
            abits = _bits(a_sel, kbits)
            b_sel = _mux([hb_scr[a] for a in range(PEER_TOPK)], abits)
            bbits = _bits(b_sel, kbits)
            sc_scr[r] = m
            se1_scr[r] = _mux([i1_scr[a] for a in range(PEER_TOPK)], abits)
            se2_scr[r] = _mux([i2_scr[b] for b in range(PEER_TOPK)], bbits)
            nb = b_sel + 1.0
            v2_next = _mux([v2_scr[(b + 1) % PEER_TOPK] for b in range(PEER_TOPK)], bbits)
            v1_sel = _mux([v1_scr[a] for a in range(PEER_TOPK)], abits)
            live = (a_sel + 1.0) * (nb + 1.0) <= float(PEER_TOPK)
            nv = jnp.where(live, v1_sel + v2_next, -jnp.inf)
            for a in range(PEER_TOPK):
                hit = a_sel == float(a)
                hv_scr[a] = jnp.where(hit, nv, hv_scr[a])
                hb_scr[a] = jnp.where(hit, nb, hb_scr[a])
            return carry

        lax.fori_loop(0, PEER_TOPK, body, 0)
        sc = sc_scr[...]
        ex = jnp.exp(sc - jnp.max(sc, axis=0, keepdims=True))
        gate = ex / jnp.sum(ex, axis=0, keepdims=True)
        rows = slice(lt * LANES, (lt + 1) * LANES)
        g_ref[rows, :] = gate.reshape(PEER_SLOTS, LANES).T
        e1_ref[rows, :] = se1_scr[...].reshape(PEER_SLOTS, LANES).T
        e2_ref[rows, :] = se2_scr[...].reshape(PEER_SLOTS, LANES).T


def _peer_select(xn, wq, k1big, k2big):
    t = xn.shape[0]
    tb = 256
    hq = PEER_HEADS * PEER_DQ // 2
    nk = PEER_NKEYS * PEER_HEADS
    const = lambda i: (0, 0)
    row = lambda dt: jax.ShapeDtypeStruct((t, PEER_SLOTS), dt)
    vec = lambda n: pltpu.VMEM((n, SUBLANES, LANES), F32)
    return pl.pallas_call(
        functools.partial(_select_kernel, tb),
        grid=(t // tb,),
        in_specs=[pl.BlockSpec((tb, D_MODEL), lambda i: (i, 0)),
                  pl.BlockSpec((D_MODEL, 2 * hq), const),
                  pl.BlockSpec((nk, hq), const),
                  pl.BlockSpec((nk, hq), const)],
        out_specs=[pl.BlockSpec((tb, PEER_SLOTS), lambda i: (i, 0))] * 3,
        out_shape=[row(F32), row(F32), row(F32)],
        scratch_shapes=[vec(PEER_NKEYS), vec(PEER_NKEYS)] + [vec(PEER_NKEYS // TOPK_GROUP)] * 4
        + [vec(PEER_TOPK), vec(PEER_TOPK), vec(PEER_TOPK), vec(PEER_TOPK),
                        vec(PEER_TOPK), vec(PEER_TOPK), vec(PEER_TOPK), vec(PEER_TOPK), vec(PEER_TOPK)],
        compiler_params=_params(("parallel",), 40),
        name="peer_select",
    )(xn, wq, k1big, k2big)


def _peer_kernel(tb, eb, stride, final_norm, out_batch, xn_ref, e1_ref, e2_ref, g_ref, ut_ref, v_ref, x1_ref,
                 gain_ref, out_ref, m_scr, *acc_scr):
    e = pl.program_id(1)
    nk1 = eb // PEER_NKEYS
    acc_ref = acc_scr[0] if out_batch else out_ref

    @pl.when(e == 0)
    def _():
        acc_ref[...] = x1_ref[...]

    @pl.when(e == 0)
    def _():
        key = lax.broadcasted_iota(jnp.int32, (PEER_NKEYS, PEER_SLOTS), 0).astype(F32)

        def token(t, carry):
            e1 = e1_ref[pl.ds(t, 1), :]
            e2 = e2_ref[pl.ds(t, 1), :]
            gt = 0.5 * g_ref[pl.ds(t, 1), :]
            a_t = jnp.where(key == e1, gt, 0.0).astype(BF16)
            b_t = jnp.where(key == e2, 1.0, 0.0).astype(BF16)
            m_scr[pl.ds(t, PEER_NKEYS, stride=stride), :] = _dot_nt(a_t, b_t)
            return carry

        lax.fori_loop(0, tb, token, 0, unroll=64)

    s = _dot(xn_ref[...], ut_ref[...])
    t = jnp.tanh(s * (GELU_C + (GELU_C * GELU_A) * (s * s)))
    k1 = e * nk1
    gates = [m_scr[pl.ds(pl.multiple_of((k1 + i) * stride, SUBLANES), tb), :] for i in range(nk1)]
    w = ((s + s * t) * jnp.concatenate(gates, axis=1)).astype(BF16)
    acc_ref[...] += _dot(w, v_ref[...])

    if final_norm or out_batch:
        @pl.when(e == pl.num_programs(1) - 1)
        def _():
            y = acc_ref[...]
            if final_norm:
                y = _rms(y, gain_ref[...])
            out_ref[...] = pltpu.einshape("(tb)d->btd", y, b=out_batch) if out_batch else y


def _peer_dense(xn, e1, e2, g, ut, v, x1, layer, gain, final_norm, out_batch=0):
    t = xn.shape[0]
    tb, eb = 512, 1024
    stride = tb + SUBLANES
    once = pl.Buffered(1)
    tok = lambda w: pl.BlockSpec((tb, w), lambda i, e: (i, 0), pipeline_mode=once)
    tab = pl.BlockSpec((None, eb, D_MODEL), lambda i, e: (layer, e, 0))
    if out_batch:
        out_spec = pl.BlockSpec((out_batch, tb // out_batch, D_MODEL), lambda i, e: (0, i, 0))
        out_shape = jax.ShapeDtypeStruct((out_batch, t // out_batch, D_MODEL), F32)
        acc = [pltpu.VMEM((tb, D_MODEL), F32)]
    else:
        out_spec = pl.BlockSpec((tb, D_MODEL), lambda i, e: (i, 0))
        out_shape = jax.ShapeDtypeStruct((t, D_MODEL), F32)
        acc = []
    return pl.pallas_call(
        functools.partial(_peer_kernel, tb, eb, stride, final_norm, out_batch),
        grid=(t // tb, PEER_NEXP // eb),
        in_specs=[tok(D_MODEL), tok(PEER_SLOTS), tok(PEER_SLOTS), tok(PEER_SLOTS),
                  pl.BlockSpec((None, D_MODEL, eb), lambda i, e: (layer, 0, e)), tab, tok(D_MODEL),
                  pl.BlockSpec((1, D_MODEL), lambda i, e: (0, 0))],
        out_specs=out_spec,
        out_shape=out_shape,
        scratch_shapes=[pltpu.VMEM((PEER_NKEYS * stride, PEER_NKEYS), F32)] + acc,
        compiler_params=_params(("parallel", "arbitrary"), 60),
        name="peer_dense",
    )(xn, e1, e2, g, ut, v, x1, gain)


def _time_major(x):
    b, s, d = x.shape
    return x.transpose(1, 0, 2).reshape(s * b, d)


def _batch_major(y, b, s):
    return y.reshape(s, b, y.shape[-1]).transpose(1, 0, 2)


def kernel(x_prompt, x_sample, state_ret, state_ssm_re, state_ssm_im, state_conv, norm_mix, w_in, ret_norm, w_ret_out, ssm_a_re, ssm_a_im, ssm_b_re, ssm_b_im, ssm_c_re, ssm_c_im, ssm_d, ssm_log_dt, w_glu_a, w_glu_b, conv_w, conv_b, w_conv_out, w_mix_out, norm_ffn, peer_wq, peer_k1, peer_k2, peer_u, peer_v, norm_final):
    bp, sp, _ = x_prompt.shape
    bs, ss, _ = x_sample.shape
    tp, ts = bp * sp, bs * ss
    depth = w_in.shape[0]
    assert tp % ROWS == 0 and ts == ROWS and ROWS % bp == 0 and ROWS // bp == math.gcd(sp, RET_CHUNK)

    xs = [x_prompt, _time_major(x_sample)]
    batches = (bp, bs)
    pos = (jnp.repeat(jnp.arange(sp, dtype=F32), bp), jnp.repeat(PAST_LEN + jnp.arange(ss, dtype=F32), bs))
    ret_tabs = [_retention_tables(p, b) for p, b in zip(pos, batches)]

    lg = depth * SSM_G
    abre, abim, bbre, bbim = _s5_discretise(
        ssm_a_re.reshape(lg, SSM_P), ssm_a_im.reshape(lg, SSM_P), ssm_log_dt.reshape(lg, 1),
        ssm_b_re.transpose(0, 1, 3, 2).reshape(lg, SSM_GC, SSM_P),
        ssm_b_im.transpose(0, 1, 3, 2).reshape(lg, SSM_GC, SSM_P))

    hq = PEER_DQ // 2
    eye = jnp.eye(PEER_HEADS, dtype=F32)

    def keys_block_diag(k):
        return (k.transpose(1, 0, 2)[:, :, None, :] * eye[None, :, :, None]).reshape(
            PEER_NKEYS * PEER_HEADS, PEER_HEADS * hq).astype(BF16)

    w_in_b = w_in.astype(BF16)
    ut = peer_u.astype(BF16).transpose(0, 2, 1)
    vt = peer_v.astype(BF16)
    zero_ret = jnp.zeros((1, bp, RET_HEADS, RET_DK, RET_DK), F32)

    states = [[[] for _ in range(4)] for _ in range(2)]
    for l in range(depth):
        sl = slice(l * SSM_G, (l + 1) * SSM_G)
        bmat = jnp.concatenate([_block_diag_slabs(bbre[sl], SSM_SLABS), _block_diag_slabs(bbim[sl], SSM_SLABS)],
                               axis=2).astype(BF16)
        cmat = jnp.concatenate([_block_diag_slabs(ssm_c_re[l].transpose(0, 2, 1), SSM_SLABS),
                                _block_diag_slabs(-ssm_c_im[l].transpose(0, 2, 1), SSM_SLABS)],
                               axis=1).astype(BF16)
        are_row = abre[sl].reshape(1, SSM_N)
        aim_row = abim[sl].reshape(1, SSM_N)
        proj =[w.astype(BF16) for w in (w_ret_out[l], w_glu_a[l], w_glu_b[l], w_conv_out[l], w_mix_out[l])]
        wq = peer_wq[l].reshape(D_MODEL, PEER_HEADS, 2, hq).transpose(0, 2, 1, 3).reshape(D_MODEL, -1).astype(BF16)
        k1big, k2big = keys_block_diag(peer_k1[l]), keys_block_diag(peer_k2[l])

        for gi, batch in enumerate(batches):
            x = xs[gi]
            nblk = x.size // (ROWS * D_MODEL)
            if gi == 0:
                s0, s0_layer = zero_ret, 0
                h0re = jnp.zeros((batch, SSM_N), F32)
                h0im = jnp.zeros((batch, SSM_N), F32)
                buf0 = jnp.zeros(((CONV_K - 1) * batch, CONV_W), F32)
                bblk = batch
            else:
                s0, s0_layer = state_ret, l
                h0re = state_ssm_re[l].reshape(batch, SSM_N)
                h0im = state_ssm_im[l].reshape(batch, SSM_N)
                buf0 = state_conv[l].transpose(1, 0, 2).reshape((CONV_K - 1) * batch, CONV_W)
                bblk = 16
            z = _inproj(x, norm_mix[l][None, :], w_in_b, l)
            if x.ndim == 3:
                z, x = z
            oa, s_new = _retention(z, 0, nblk, batch, bblk, ret_tabs[gi], ret_norm[l][None, :], s0, s0_layer)
            ys, xre, xim = _s5(z, 0, nblk, batch, bmat, cmat, are_row, aim_row, ssm_d[l][None, :], h0re, h0im)
            oc, buf = _conv(z, 0, nblk, batch, buf0, conv_w[l], conv_b[l][None, :])
            st = states[gi]
            st[0].append(s_new)
            st[1].append(xre.reshape(batch, SSM_G, SSM_P))
            st[2].append(xim.reshape(batch, SSM_G, SSM_P))
            st[3].append(buf.reshape(CONV_K - 1, batch, CONV_W).transpose(1, 0, 2))

            x1, xn = _merge(x, z, oa, ys, oc, *proj, norm_ffn[l][None, :])
            e1, e2, g = _peer_select(xn, wq, k1big, k2big)
            last = l == depth - 1
            xs[gi] = _peer_dense(xn, e1, e2, g, ut, vt, x1, l, norm_final[None, :], last,
                                 out_batch=bp if last and gi == 0 else 0)

    y_prompt = xs[0]
    y_sample = _batch_major(xs[1], bs, ss)
    (ret_p, re_p, im_p, cv_p), (ret_s, re_s, im_s, cv_s) = states
    return (y_prompt, y_sample,
            jnp.stack(ret_p), jnp.stack(ret_s),
            jnp.stack(re_p), jnp.stack(re_s),
            jnp.stack(im_p), jnp.stack(im_s),
            jnp.stack(cv_p), jnp.stack(cv_s))
```

```python
import functools
import math

import jax
import jax.numpy as jnp
from jax import lax
from jax.experimental import pallas as pl
from jax.experimental.pallas import tpu as pltpu

F32 = jnp.float32
BF16 = jnp.bfloat16

D_MODEL = 1024
DEPTH = 2
PAST_LEN = 16384
RET_HEADS = 8
RET_DK = 64
RET_W = 512
RET_CHUNK = 128
ROPE_BASE = 10000.0
SSM_W = 512
SSM_GC = 16
SSM_G = 32
SSM_P = 64
SSM_N = SSM_G * SSM_P
SSM_SLABS = 4
SSM_SLAB_N = SSM_N // SSM_SLABS
CONV_W = 512
CONV_K = 3
PROJ_W = 7168
PEER_HEADS = 8
PEER_DQ = 256
PEER_NKEYS = 128
PEER_TOPK = 16
PEER_NEXP = PEER_NKEYS ** 2
PEER_SLOTS = PEER_HEADS * PEER_TOPK
TOPK_GROUP = 8
EPS = 1e-6
GELU_C = math.sqrt(2.0 / math.pi)
GELU_A = 0.044715

ROWS = 1024
LANES = 128
SUBLANES = 8
MXU_DEPTH = 256
MIB = 1024 * 1024


def _params(sem, vmem_mib):
    return pltpu.CompilerParams(dimension_semantics=sem, vmem_limit_bytes=vmem_mib * MIB)


def _rms(x, g):
    return x * lax.rsqrt(jnp.mean(x * x, axis=-1, keepdims=True) + EPS) * g


def _dot(a, b):
    return jnp.dot(a, b, preferred_element_type=F32)


def _dot_nt(a, b):
    return lax.dot_general(a, b, (((1,), (1,)), ((), ())), preferred_element_type=F32)


def _inproj_kernel(x_ref, g_ref, w_ref, z_ref, h_scr):
    @pl.when(pl.program_id(1) == 0)
    def _():
        h_scr[...] = _rms(x_ref[...], g_ref[...]).astype(BF16)

    z_ref[...] = _dot(h_scr[...], w_ref[...])


def _inproj_bm_kernel(x_ref, g_ref, w_ref, z_ref, xtm_ref, h_scr):
    @pl.when(pl.program_id(1) == 0)
    def _():
        x = pltpu.einshape("btd->(tb)d", x_ref[...])
        xtm_ref[...] = x
        h_scr[...] = _rms(x, g_ref[...]).astype(BF16)

    z_ref[...] = _dot(h_scr[...], w_ref[...])


def _inproj(x, g, w, layer):
    batch_major = x.ndim == 3
    nb = 1024
    if batch_major:
        batch, seq, _ = x.shape
        t = batch * seq
        x_spec = pl.BlockSpec((batch, ROWS // batch, D_MODEL), lambda i, j: (0, i, 0))
    else:
        t = x.shape[0]
        x_spec = pl.BlockSpec((ROWS, D_MODEL), lambda i, j: (i, 0))
    z_spec = pl.BlockSpec((ROWS, nb), lambda i, j: (i, j))
    z_shape = jax.ShapeDtypeStruct((t, PROJ_W), F32)
    return pl.pallas_call(
        _inproj_bm_kernel if batch_major else _inproj_kernel,
        grid=(t // ROWS, PROJ_W // nb),
        in_specs=[x_spec,
                  pl.BlockSpec((1, D_MODEL), lambda i, j: (0, 0)),
                  pl.BlockSpec((None, D_MODEL, nb), lambda i, j: (layer, 0, j))],
        out_specs=[z_spec, pl.BlockSpec((ROWS, D_MODEL), lambda i, j: (i, 0))] if batch_major else z_spec,
        out_shape=[z_shape, jax.ShapeDtypeStruct((t, D_MODEL), F32)] if batch_major else z_shape,
        scratch_shapes=[pltpu.VMEM((ROWS, D_MODEL), BF16)],
        compiler_params=_params(("parallel", "arbitrary"), 48),
        name="inproj",
    )(x, g, w)


def _ret_kernel(batch, bblk, aliased, *refs):
    refs = [r for i, r in enumerate(refs) if not (aliased and i == 9)]
    (z_ref, cos_ref, sa_ref, sb_ref, qdec_ref, kdec_ref, cdec_ref, gn_ref, s0_ref,
     o_ref, s_ref, qd_scr, kd_scr, v_scr, mask_scr, oacc_scr) = refs
    bb = pl.program_id(0)
    c = pl.program_id(1)
    steps = ROWS // batch
    nslab = RET_W // LANES

    def head_view(ref, h):
        return ref[h // 2, :, (h % 2) * RET_DK:(h % 2 + 1) * RET_DK]

    seq_local = steps >= LANES
    msize = steps if seq_local else ROWS

    @pl.when((bb == 0) & (c == 0))
    def _():
        r = lax.broadcasted_iota(jnp.int32, (msize, msize), 0)
        cc = lax.broadcasted_iota(jnp.int32, (msize, msize), 1)
        if seq_local:
            mask_scr[...] = (r >= cc).astype(F32)
        else:
            same = (r & (batch - 1)) == (cc & (batch - 1))
            mask_scr[...] = (same & (r >= cc)).astype(F32)

    @pl.when(c == 0)
    def _():
        s_ref[...] = s0_ref[...]

    @pl.when(bb == 0)
    def _():
        cos, sa, sb = cos_ref[...], sa_ref[...], sb_ref[...]

        def rot(x):
            return x * cos + pltpu.roll(x, 32, 1) * sa + pltpu.roll(x, 96, 1) * sb

        for s in range(nslab):
            cols = slice(s * LANES, (s + 1) * LANES)
            qd_scr[s] = rot(z_ref[:, cols]) * qdec_ref[:, cols]
            kd_scr[s] = rot(z_ref[:, RET_W + s * LANES:RET_W + (s + 1) * LANES]) * kdec_ref[:, cols]
            v_scr[s] = z_ref[:, 2 * RET_W + s * LANES:2 * RET_W + (s + 1) * LANES]
        if not seq_local:
            for s in range(nslab):
                outs = []
                for h in (2 * s, 2 * s + 1):
                    qh = head_view(qd_scr, h).astype(BF16)
                    kh = head_view(kd_scr, h).astype(BF16)
                    vh = head_view(v_scr, h).astype(BF16)
                    p = (_dot_nt(qh, kh) * mask_scr[...]).astype(BF16)
                    outs.append(_dot(p, vh))
                oacc_scr[s] = jnp.concatenate(outs, axis=1)

    def per_seq(bl, carry):
        b = bb * bblk + bl
        rows = pl.ds(b, steps, stride=batch)
        for s in range(nslab):
            qb = qd_scr[s, rows, :]
            kb = kd_scr[s, rows, :]
            vb = v_scr[s, rows, :]
            outs = []
            for hh in range(2):
                h = 2 * s + hh
                hc = slice(hh * RET_DK, (hh + 1) * RET_DK)
                q16, k16, v16 = qb[:, hc].astype(BF16), kb[:, hc].astype(BF16), vb[:, hc].astype(BF16)
                st = s_ref[bl, h]
                o = _dot(q16, st.astype(BF16))
                if seq_local:
                    p = (_dot_nt(q16, k16) * mask_scr[...]).astype(BF16)
                    o = _dot(p, v16) + o
                outs.append(o)
                upd = lax.dot_general(k16, v16, (((0,), (0,)), ((), ())), preferred_element_type=F32)
                s_ref[bl, h] = (st + upd) * cdec_ref[:, h * RET_DK:(h + 1) * RET_DK]
            o2 = jnp.concatenate(outs, axis=1)
            oacc_scr[s, rows, :] = o2 if seq_local else oacc_scr[s, rows, :] + o2
        return carry

    lax.fori_loop(0, bblk, per_seq, 0, unroll=4)

    @pl.when(bb == pl.num_programs(0) - 1)
    def _():
        r = lax.broadcasted_iota(jnp.int32, (LANES, LANES), 0) // RET_DK
        cc = lax.broadcasted_iota(jnp.int32, (LANES, LANES), 1) // RET_DK
        avg = jnp.where(r == cc, 1.0 / RET_DK, 0.0).astype(BF16)

        def seg_mean(x):
            hi = x.astype(BF16)
            lo = (x - hi.astype(F32)).astype(BF16)
            return _dot(hi, avg) + _dot(lo, avg)

        normed = []
        for s in range(nslab):
            o2 = oacc_scr[s]
            dlt = o2 - seg_mean(o2)
            normed.append(dlt * lax.rsqrt(seg_mean(dlt * dlt) + EPS))
        o = jnp.concatenate(normed, axis=1) * gn_ref[...]
        o_ref[...] = jax.nn.silu(z_ref[:, 3 * RET_W:4 * RET_W]) * o


def _retention(z, row_blk0, nblk, batch, bblk, tabs, gn, s0, layer, stacked, out_layer, depth):
    cos, sa, sb, qdec, kdec, cdec = tabs
    nbb = batch // bblk
    steps = ROWS // batch
    msize = steps if steps >= LANES else ROWS
    st_spec = pl.BlockSpec((None, bblk, RET_HEADS, RET_DK, RET_DK), lambda bb, c: (out_layer, bb, 0, 0, 0))
    st_in = pl.BlockSpec((None, bblk, RET_HEADS, RET_DK, RET_DK), lambda bb, c: (layer, bb, 0, 0, 0))
    aliased = stacked is not None
    const = lambda bb, c: (0, 0)
    return pl.pallas_call(
        functools.partial(_ret_kernel, batch, bblk, aliased),
        grid=(nbb, nblk),
        in_specs=[pl.BlockSpec((ROWS, 4 * RET_W), lambda bb, c: (row_blk0 + c, 0)),
                  pl.BlockSpec((ROWS, LANES), lambda bb, c: (c, 0)),
                  pl.BlockSpec((ROWS, LANES), lambda bb, c: (c, 0)),
                  pl.BlockSpec((ROWS, LANES), lambda bb, c: (c, 0)),
                  pl.BlockSpec((ROWS, RET_W), const),
                  pl.BlockSpec((ROWS, RET_W), const),
                  pl.BlockSpec((1, RET_W), const),
                  pl.BlockSpec((1, RET_W), const),
                  st_in] + ([pl.BlockSpec(memory_space=pl.ANY)] if aliased else []),
        out_specs=[pl.BlockSpec((ROWS, RET_W), lambda bb, c: (c, 0)), st_spec],
        out_shape=[jax.ShapeDtypeStruct((nblk * ROWS, RET_W), F32),
                   jax.ShapeDtypeStruct((depth, batch, RET_HEADS, RET_DK, RET_DK), F32)],
        input_output_aliases={9: 1} if aliased else {},
        scratch_shapes=[pltpu.VMEM((RET_W // LANES, ROWS, LANES), F32)] * 3
        + [pltpu.VMEM((msize, msize), F32), pltpu.VMEM((RET_W // LANES, ROWS, LANES), F32)],
        compiler_params=_params(("arbitrary", "arbitrary"), 56),
        name="retention",
    )(z, cos, sa, sb, qdec, kdec, cdec, gn, s0, *([stacked] if aliased else []))


def _retention_tables(pos, batch):
    half = RET_DK // 2
    lane = jnp.arange(LANES)
    upper = (lane % RET_DK) >= half
    freqs = (ROPE_BASE ** (-jnp.arange(half, dtype=F32) / half))[lane % half]
    ang = pos[:, None] * freqs[None, :]
    cos_t, sin = jnp.cos(ang), jnp.sin(ang)
    sa_t = jnp.where(upper[None, :], sin, 0.0)
    sb_t = jnp.where(upper[None, :], 0.0, -sin)
    lg = jnp.repeat(jnp.log1p(-jnp.exp2(-5.0 - jnp.arange(RET_HEADS, dtype=F32))), RET_DK)
    steps = ROWS // batch
    i1 = (jnp.arange(ROWS) // batch).astype(F32) + 1.0
    qdec = jnp.exp(i1[:, None] * lg[None, :])
    kdec = jnp.exp(-i1[:, None] * lg[None, :]) * (RET_DK ** -0.5)
    cdec = jnp.exp(steps * lg)[None, :]
    return cos_t, sa_t, sb_t, qdec, kdec, cdec


def _s5_disc_kernel(are_ref, aim_ref, ldt_ref, bre_ref, bim_ref, abre_ref, abim_ref, bbre_ref, bbim_ref):
    ar, ai = are_ref[...], aim_ref[...]
    dt = jnp.exp(ldt_ref[...])
    dar, dai = dt * ar, dt * ai
    mag = jnp.exp(dar)
    abar_re, abar_im = mag * jnp.cos(dai), mag * jnp.sin(dai)
    den = ar * ar + ai * ai
    nr, ni = abar_re - 1.0, abar_im
    f_re = (nr * ar + ni * ai) / den
    f_im = (ni * ar - nr * ai) / den
    abre_ref[...] = abar_re
    abim_ref[...] = abar_im
    br, bi = bre_ref[...], bim_ref[...]
    bbre_ref[...] = f_re[:, None, :] * br - f_im[:, None, :] * bi
    bbim_ref[...] = f_re[:, None, :] * bi + f_im[:, None, :] * br


def _s5_discretise(a_re, a_im, log_dt, b_re_t, b_im_t):
    lg = a_re.shape[0]
    small = jax.ShapeDtypeStruct((lg, SSM_P), F32)
    big = jax.ShapeDtypeStruct((lg, SSM_GC, SSM_P), F32)
    return pl.pallas_call(_s5_disc_kernel, out_shape=[small, small, big, big], name="s5_disc")(
        a_re, a_im, log_dt, b_re_t, b_im_t)


def _s5_kernel(batch, u_ref, bmat_ref, cmat_ref, are_ref, aim_ref, d_ref, h0re_ref, h0im_ref,
               y_ref, xre_ref, xim_ref, x_scr):
    c = pl.program_id(0)
    steps = ROWS // batch
    half = SSM_SLAB_N

    @pl.when(c == 0)
    def _():
        xre_ref[...] = h0re_ref[...]
        xim_ref[...] = h0im_ref[...]

    u = u_ref[...]
    ub = u.astype(BF16)
    for s in range(SSM_SLABS):
        x_scr[:, 2 * half * s:2 * half * (s + 1)] = _dot(ub[:, s * LANES:(s + 1) * LANES], bmat_ref[s])

    for s in range(SSM_SLABS):
        re0 = 2 * half * s
        im0 = re0 + half
        sc = slice(half * s, half * (s + 1))
        ar = jnp.broadcast_to(are_ref[:, sc], (SUBLANES, half))
        ai = jnp.broadcast_to(aim_ref[:, sc], (SUBLANES, half))

        def row_tile(rt, carry, re0=re0, im0=im0, sc=sc, ar=ar, ai=ai):
            r0 = pl.multiple_of(rt * SUBLANES, SUBLANES)

            def step(t, x):
                xr, xi = x
                row = pl.multiple_of(t * batch + r0, SUBLANES)
                nr = ar * xr - ai * xi + x_scr[pl.ds(row, SUBLANES), re0:re0 + half]
                ni = ar * xi + ai * xr + x_scr[pl.ds(row, SUBLANES), im0:im0 + half]
                x_scr[pl.ds(row, SUBLANES), re0:re0 + half] = nr
                x_scr[pl.ds(row, SUBLANES), im0:im0 + half] = ni
                return nr, ni

            init = (xre_ref[pl.ds(r0, SUBLANES), sc], xim_ref[pl.ds(r0, SUBLANES), sc])
            xr, xi = lax.fori_loop(0, steps, step, init, unroll=8)
            xre_ref[pl.ds(r0, SUBLANES), sc] = xr
            xim_ref[pl.ds(r0, SUBLANES), sc] = xi
            return carry

        lax.fori_loop(0, batch // SUBLANES, row_tile, 0)

    ys = [_dot(x_scr[:, 2 * half * s:2 * half * (s + 1)].astype(BF16), cmat_ref[s]) for s in range(SSM_SLABS)]
    y = jnp.concatenate(ys, axis=1) + d_ref[...] * u
    y_ref[...] = jax.nn.gelu(y)


def _s5(z, row_blk0, nblk, batch, bmat, cmat, abre, abim, d, h0re, h0im):
    const2 = lambda c: (0, 0)
    const3 = lambda c: (0, 0, 0)
    st = pl.BlockSpec((batch, SSM_N), const2)
    return pl.pallas_call(
        functools.partial(_s5_kernel, batch),
        grid=(nblk,),
        in_specs=[pl.BlockSpec((ROWS, SSM_W), lambda c: (row_blk0 + c, 4)),
                  pl.BlockSpec((SSM_SLABS, LANES, 2 * SSM_SLAB_N), const3),
                  pl.BlockSpec((SSM_SLABS, 2 * SSM_SLAB_N, LANES), const3),
                  pl.BlockSpec((1, SSM_N), const2),
                  pl.BlockSpec((1, SSM_N), const2),
                  pl.BlockSpec((1, SSM_W), const2),
                  st, st],
        out_specs=[pl.BlockSpec((ROWS, SSM_W), lambda c: (c, 0)), st, st],
        out_shape=[jax.ShapeDtypeStruct((nblk * ROWS, SSM_W), F32),
                   jax.ShapeDtypeStruct((batch, SSM_N), F32),
                   jax.ShapeDtypeStruct((batch, SSM_N), F32)],
        scratch_shapes=[pltpu.VMEM((ROWS, 2 * SSM_N), F32)],
        compiler_params=_params(("arbitrary",), 48),
        name="s5",
    )(z, bmat, cmat, abre, abim, d, h0re, h0im)


def _block_diag_slabs(w, nslab):
    gps = SSM_G // nslab
    eye = jnp.eye(gps, dtype=w.dtype)
    w4 = w.reshape(nslab, gps, w.shape[1], w.shape[2])
    out = w4[:, :, :, None, :] * eye[None, :, None, :, None]
    return out.reshape(nslab, gps * w.shape[1], gps * w.shape[2])


def _conv_kernel(batch, bg_ref, cg_ref, hc_ref, buf0_ref, w_ref, b_ref, o_ref, buf_ref, zp_scr):
    c = pl.program_id(0)
    pad = (CONV_K - 1) * batch

    @pl.when(c == 0)
    def _():
        zp_scr[0:pad, :] = buf0_ref[...]

    zc = cg_ref[...] * hc_ref[...]
    zp_scr[pad:pad + ROWS, :] = zc
    y = b_ref[...]
    for j in range(CONV_K):
        y = y + w_ref[j:j + 1, :] * zp_scr[j * batch:j * batch + ROWS, :]
    o_ref[...] = bg_ref[...] * y
    tail = zp_scr[ROWS:ROWS + pad, :]
    buf_ref[...] = tail
    zp_scr[0:pad, :] = tail


def _conv(z, row_blk0, nblk, batch, buf0, w, b):
    pad = (CONV_K - 1) * batch
    const = lambda c: (0, 0)
    return pl.pallas_call(
        functools.partial(_conv_kernel, batch),
        grid=(nblk,),
        in_specs=[pl.BlockSpec((ROWS, CONV_W), lambda c: (row_blk0 + c, 5)),
                  pl.BlockSpec((ROWS, CONV_W), lambda c: (row_blk0 + c, 6)),
                  pl.BlockSpec((ROWS, CONV_W), lambda c: (row_blk0 + c, 7)),
                  pl.BlockSpec((pad, CONV_W), const),
                  pl.BlockSpec((CONV_K, CONV_W), const),
                  pl.BlockSpec((1, CONV_W), const)],
        out_specs=[pl.BlockSpec((ROWS, CONV_W), lambda c: (c, 0)), pl.BlockSpec((pad, CONV_W), const)],
        out_shape=[jax.ShapeDtypeStruct((nblk * ROWS, CONV_W), F32),
                   jax.ShapeDtypeStruct((pad, CONV_W), F32)],
        scratch_shapes=[pltpu.VMEM((ROWS + pad, CONV_W), F32)],
        compiler_params=_params(("arbitrary",), 32),
        name="conv",
    )(z, z, z, buf0, w, b)


def _merge_kernel(x_ref, oa_ref, ys_ref, oc_ref, ga_ref, gb_ref, gc_ref, wr_ref, wa_ref, wb_ref, wc_ref,
                  wm_ref, gf_ref, x1_ref, xn_ref):
    oa = _dot(oa_ref[...].astype(BF16), wr_ref[...])
    ysb = ys_ref[...].astype(BF16)
    ob = _dot(ysb, wa_ref[...]) * jax.nn.sigmoid(_dot(ysb, wb_ref[...]))
    oc = _dot(oc_ref[...].astype(BF16), wc_ref[...])
    merged = (jax.nn.sigmoid(ga_ref[...]) * oa + jax.nn.sigmoid(gb_ref[...]) * ob
              + jax.nn.sigmoid(gc_ref[...]) * oc)
    x1 = x_ref[...] + _dot(merged.astype(BF16), wm_ref[...])
    x1_ref[...] = x1
    xn_ref[...] = _rms(x1, gf_ref[...]).astype(BF16)


def _merge(x, z, oa, ys, oc, wr, wa, wb, wc, wm, gf):
    t = x.shape[0]
    rb = 512
    row = lambda w: pl.BlockSpec((rb, w), lambda i: (i, 0))
    gate = lambda j: pl.BlockSpec((rb, D_MODEL), lambda i: (i, j))
    wsp = lambda k: pl.BlockSpec((k, D_MODEL), lambda i: (0, 0))
    return pl.pallas_call(
        _merge_kernel,
        grid=(t // rb,),
        in_specs=[row(D_MODEL), row(RET_W), row(SSM_W), row(CONV_W), gate(4), gate(5), gate(6),
                  wsp(RET_W), wsp(SSM_W), wsp(SSM_W), wsp(CONV_W), wsp(D_MODEL), wsp(1)],
        out_specs=[row(D_MODEL), row(D_MODEL)],
        out_shape=[jax.ShapeDtypeStruct((t, D_MODEL), F32), jax.ShapeDtypeStruct((t, D_MODEL), BF16)],
        compiler_params=_params(("parallel",), 48),
        name="merge",
    )(x, oa, ys, oc, z, z, z, wr, wa, wb, wc, wm, gf)


def _tree(items, combine):
    while len(items) > 1:
        nxt = [combine(items[i], items[i + 1]) for i in range(0, len(items) - 1, 2)]
        if len(items) % 2:
            nxt.append(items[-1])
        items = nxt
    return items[0]


def _first_max(x, y):
    (vx, ix), (vy, iy) = x, y
    return jnp.maximum(vx, vy), jnp.where(vx >= vy, ix, iy)


def _bits(x, n):
    out, rest = [], x
    for _ in range(n):
        half = jnp.floor(rest * 0.5)
        out.append(rest - 2.0 * half == 1.0)
        rest = half
    return out


def _mux(vals, bits):
    level = list(vals)
    for bit in bits:
        level = [jnp.where(bit, level[j + 1], level[j]) for j in range(0, len(level), 2)]
    return level[0]


def _top16_of_keys(s_scrs, gv_scrs, gi_scrs, v_scrs, i_scrs):
    grp = TOPK_GROUP
    ngrp = PEER_NKEYS // grp
    nbits = ngrp.bit_length() - 1

    for s_scr, gv, gi in zip(s_scrs, gv_scrs, gi_scrs):
        for g in range(ngrp):
            gv[g], gi[g] = _tree([(s_scr[g * grp + p], float(g * grp + p)) for p in range(grp)], _first_max)

    def body(r, carry):
        for s_scr, gv, gi, v_scr, i_scr in zip(s_scrs, gv_scrs, gi_scrs, v_scrs, i_scrs):
            m, idx = _tree([(gv[g], gi[g]) for g in range(ngrp)], _first_max)
            v_scr[r] = m
            i_scr[r] = idx
            gid = jnp.floor(idx * (1.0 / grp))
            rel = idx - gid * grp
            bits = _bits(gid, nbits)
            cands = []
            for p in range(grp):
                val = _mux([s_scr[g * grp + p] for g in range(ngrp)], bits)
                left = (val < m) | ((val == m) & (rel < float(p)))
                cands.append((jnp.where(left, val, -jnp.inf), float(p)))
            nv, npos = _tree(cands, _first_max)
            ni = gid * grp + npos
            for g in range(ngrp):
                hit = gid == float(g)
                gv[g] = jnp.where(hit, nv, gv[g])
                gi[g] = jnp.where(hit, ni, gi[g])
        return carry

    lax.fori_loop(0, PEER_TOPK, body, 0)


def _select_kernel(tb, xn_ref, wq_ref, k1_ref, k2_ref, e1_ref, e2_ref, g_ref,
                   s1_scr, s2_scr, gv1_scr, gi1_scr, gv2_scr, gi2_scr, v1_scr, i1_scr, v2_scr, i2_scr,
                   hv_scr, hb_scr, sc_scr, se1_scr, se2_scr):
    q = _dot(xn_ref[...], wq_ref[...]).astype(BF16)
    hq = PEER_HEADS * PEER_DQ // 2
    s1 = _dot_nt(k1_ref[...], q[:, :hq])
    s2 = _dot_nt(k2_ref[...], q[:, hq:])
    kbits = PEER_TOPK.bit_length() - 1
    for lt in range(tb // LANES):
        lanes = slice(lt * LANES, (lt + 1) * LANES)
        s1_scr[...] = s1[:, lanes].reshape(PEER_NKEYS, SUBLANES, LANES)
        s2_scr[...] = s2[:, lanes].reshape(PEER_NKEYS, SUBLANES, LANES)
        _top16_of_keys((s1_scr, s2_scr), (gv1_scr, gv2_scr), (gi1_scr, gi2_scr), (v1_scr, v2_scr), (i1_scr, i2_scr))

        for a in range(PEER_TOPK):
            hv_scr[a] = v1_scr[a] + v2_scr[0]
            hb_scr[a] = jnp.zeros((SUBLANES, LANES), F32)

        def body(r, carry):
            m, a_sel = _tree([(hv_scr[a], float(a)) for a in range(PEER_TOPK)], _first_max)
            abits = _bits(a_sel, kbits)
            b_sel = _mux([hb_scr[a] for a in range(PEER_TOPK)], abits)
            bbits = _bits(b_sel, kbits)
            sc_scr[r] = m
            se1_scr[r] = _mux([i1_scr[a] for a in range(PEER_TOPK)], abits)
            se2_scr[r] = _mux([i2_scr[b] for b in range(PEER_TOPK)], bbits)
            nb = b_sel + 1.0
            v2_next = _mux([v2_scr[(b + 1) % PEER_TOPK] for b in range(PEER_TOPK)], bbits)
            v1_sel = _mux([v1_scr[a] for a in range(PEER_TOPK)], abits)
            live = (a_sel + 1.0) * (nb + 1.0) <= float(PEER_TOPK)
            nv = jnp.where(live, v1_sel + v2_next, -jnp.inf)
            for a in range(PEER_TOPK):
                hit = a_sel == float(a)
                hv_scr[a] = jnp.where(hit, nv, hv_scr[a])
                hb_scr[a] = jnp.where(hit, nb, hb_scr[a])
            return carry

        lax.fori_loop(0, PEER_TOPK, body, 0)
        sc = sc_scr[...]
        ex = jnp.exp(sc - jnp.max(sc, axis=0, keepdims=True))
        gate = ex / jnp.sum(ex, axis=0, keepdims=True)
        rows = slice(lt * LANES, (lt + 1) * LANES)
        g_ref[rows, :] = gate.reshape(PEER_SLOTS, LANES).T
        e1_ref[rows, :] = se1_scr[...].reshape(PEER_SLOTS, LANES).T
        e2_ref[rows, :] = se2_scr[...].reshape(PEER_SLOTS, LANES).T


def _peer_select(xn, wq, k1big, k2big):
    t = xn.shape[0]
    tb = 256
    hq = PEER_HEADS * PEER_DQ // 2
    nk = PEER_NKEYS * PEER_HEADS
    const = lambda i: (0, 0)
    row = lambda dt: jax.ShapeDtypeStruct((t, PEER_SLOTS), dt)
    vec = lambda n: pltpu.VMEM((n, SUBLANES, LANES), F32)
    return pl.pallas_call(
        functools.partial(_select_kernel, tb),
        grid=(t // tb,),
        in_specs=[pl.BlockSpec((tb, D_MODEL), lambda i: (i, 0)),
                  pl.BlockSpec((D_MODEL, 2 * hq), const),
                  pl.BlockSpec((nk, hq), const),
                  pl.BlockSpec((nk, hq), const)],
        out_specs=[pl.BlockSpec((tb, PEER_SLOTS), lambda i: (i, 0))] * 3,
        out_shape=[row(F32), row(F32), row(F32)],
        scratch_shapes=[vec(PEER_NKEYS), vec(PEER_NKEYS)] + [vec(PEER_NKEYS // TOPK_GROUP)] * 4
        + [vec(PEER_TOPK), vec(PEER_TOPK), vec(PEER_TOPK), vec(PEER_TOPK),
                        vec(PEER_TOPK), vec(PEER_TOPK), vec(PEER_TOPK), vec(PEER_TOPK), vec(PEER_TOPK)],
        compiler_params=_params(("parallel",), 40),
        name="peer_select",
    )(xn, wq, k1big, k2big)


def _peer_kernel(tb, eb, stride, final_norm, out_batch, xn_ref, e1_ref, e2_ref, g_ref, ut_ref, v_ref, x1_ref,
                 gain_ref, out_ref, m_scr, *acc_scr):
    e = pl.program_id(1)
    nk1 = eb // PEER_NKEYS
    acc_ref = acc_scr[0] if out_batch else out_ref

    @pl.when(e == 0)
    def _():
        acc_ref[...] = x1_ref[...]

    @pl.when(e == 0)
    def _():
        key = lax.broadcasted_iota(jnp.int32, (PEER_NKEYS, PEER_SLOTS), 0).astype(F32)

        def token(t, carry):
            e1 = e1_ref[pl.ds(t, 1), :]
            e2 = e2_ref[pl.ds(t, 1), :]
            gt = 0.5 * g_ref[pl.ds(t, 1), :]
            a_t = jnp.where(key == e1, gt, 0.0).astype(BF16)
            b_t = jnp.where(key == e2, 1.0, 0.0).astype(BF16)
            m_scr[pl.ds(t, PEER_NKEYS, stride=stride), :] = _dot_nt(a_t, b_t)
            return carry

        lax.fori_loop(0, tb, token, 0, unroll=64)

    s = _dot(xn_ref[...], ut_ref[...])
    t = jnp.tanh(s * (GELU_C + (GELU_C * GELU_A) * (s * s)))
    k1 = e * nk1
    gates = [m_scr[pl.ds(pl.multiple_of((k1 + i) * stride, SUBLANES), tb), :] for i in range(nk1)]
    w = ((s + s * t) * jnp.concatenate(gates, axis=1)).astype(BF16)
    acc_ref[...] += _dot(w, v_ref[...])

    if final_norm or out_batch:
        @pl.when(e == pl.num_programs(1) - 1)
        def _():
            y = acc_ref[...]
            if final_norm:
                y = _rms(y, gain_ref[...])
            out_ref[...] = pltpu.einshape("(tb)d->btd", y, b=out_batch) if out_batch else y


def _peer_dense(xn, e1, e2, g, ut, v, x1, layer, gain, final_norm, out_batch=0):
    t = xn.shape[0]
    tb, eb = 512, 1024
    stride = tb + SUBLANES
    once = pl.Buffered(1)
    tok = lambda w: pl.BlockSpec((tb, w), lambda i, e: (i, 0), pipeline_mode=once)
    tab = pl.BlockSpec((None, eb, D_MODEL), lambda i, e: (layer, e, 0))
    if out_batch:
        out_spec = pl.BlockSpec((out_batch, tb // out_batch, D_MODEL), lambda i, e: (0, i, 0))
        out_shape = jax.ShapeDtypeStruct((out_batch, t // out_batch, D_MODEL), F32)
        acc = [pltpu.VMEM((tb, D_MODEL), F32)]
    else:
        out_spec = pl.BlockSpec((tb, D_MODEL), lambda i, e: (i, 0))
        out_shape = jax.ShapeDtypeStruct((t, D_MODEL), F32)
        acc = []
    return pl.pallas_call(
        functools.partial(_peer_kernel, tb, eb, stride, final_norm, out_batch),
        grid=(t // tb, PEER_NEXP // eb),
        in_specs=[tok(D_MODEL), tok(PEER_SLOTS), tok(PEER_SLOTS), tok(PEER_SLOTS),
                  pl.BlockSpec((None, D_MODEL, eb), lambda i, e: (layer, 0, e)), tab, tok(D_MODEL),
                  pl.BlockSpec((1, D_MODEL), lambda i, e: (0, 0))],
        out_specs=out_spec,
        out_shape=out_shape,
        scratch_shapes=[pltpu.VMEM((PEER_NKEYS * stride, PEER_NKEYS), F32)] + acc,
        compiler_params=_params(("parallel", "arbitrary"), 60),
        name="peer_dense",
    )(xn, e1, e2, g, ut, v, x1, gain)


def _time_major(x):
    b, s, d = x.shape
    return x.transpose(1, 0, 2).reshape(s * b, d)


def _batch_major(y, b, s):
    return y.reshape(s, b, y.shape[-1]).transpose(1, 0, 2)


def kernel(x_prompt, x_sample, state_ret, state_ssm_re, state_ssm_im, state_conv, norm_mix, w_in, ret_norm, w_ret_out, ssm_a_re, ssm_a_im, ssm_b_re, ssm_b_im, ssm_c_re, ssm_c_im, ssm_d, ssm_log_dt, w_glu_a, w_glu_b, conv_w, conv_b, w_conv_out, w_mix_out, norm_ffn, peer_wq, peer_k1, peer_k2, peer_u, peer_v, norm_final):
    bp, sp, _ = x_prompt.shape
    bs, ss, _ = x_sample.shape
    tp, ts = bp * sp, bs * ss
    depth = w_in.shape[0]
    assert tp % ROWS == 0 and ts == ROWS and ROWS % bp == 0 and ROWS // bp == math.gcd(sp, RET_CHUNK)

    xs = [x_prompt, _time_major(x_sample)]
    batches = (bp, bs)
    pos = (jnp.repeat(jnp.arange(sp, dtype=F32), bp), jnp.repeat(PAST_LEN + jnp.arange(ss, dtype=F32), bs))
    ret_tabs = [_retention_tables(p, b) for p, b in zip(pos, batches)]

    lg = depth * SSM_G
    abre, abim, bbre, bbim = _s5_discretise(
        ssm_a_re.reshape(lg, SSM_P), ssm_a_im.reshape(lg, SSM_P), ssm_log_dt.reshape(lg, 1),
        ssm_b_re.transpose(0, 1, 3, 2).reshape(lg, SSM_GC, SSM_P),
        ssm_b_im.transpose(0, 1, 3, 2).reshape(lg, SSM_GC, SSM_P))

    hq = PEER_DQ // 2
    eye = jnp.eye(PEER_HEADS, dtype=F32)

    def keys_block_diag(k):
        return (k.transpose(1, 0, 2)[:, :, None, :] * eye[None, :, :, None]).reshape(
            PEER_NKEYS * PEER_HEADS, PEER_HEADS * hq).astype(BF16)

    w_in_b = w_in.astype(BF16)
    ut = peer_u.astype(BF16).transpose(0, 2, 1)
    vt = peer_v.astype(BF16)
    zero_ret = jnp.zeros((1, bp, RET_HEADS, RET_DK, RET_DK), F32)

    states = [[[] for _ in range(4)] for _ in range(2)]
    ret_stacked = [None, None]
    for l in range(depth):
        sl = slice(l * SSM_G, (l + 1) * SSM_G)
        bmat = jnp.concatenate([_block_diag_slabs(bbre[sl], SSM_SLABS), _block_diag_slabs(bbim[sl], SSM_SLABS)],
                               axis=2).astype(BF16)
        cmat = jnp.concatenate([_block_diag_slabs(ssm_c_re[l].transpose(0, 2, 1), SSM_SLABS),
                                _block_diag_slabs(-ssm_c_im[l].transpose(0, 2, 1), SSM_SLABS)],
                               axis=1).astype(BF16)
        are_row = abre[sl].reshape(1, SSM_N)
        aim_row = abim[sl].reshape(1, SSM_N)
        proj =[w.astype(BF16) for w in (w_ret_out[l], w_glu_a[l], w_glu_b[l], w_conv_out[l], w_mix_out[l])]
        wq = peer_wq[l].reshape(D_MODEL, PEER_HEADS, 2, hq).transpose(0, 2, 1, 3).reshape(D_MODEL, -1).astype(BF16)
        k1big, k2big = keys_block_diag(peer_k1[l]), keys_block_diag(peer_k2[l])

        for gi, batch in enumerate(batches):
            x = xs[gi]
            nblk = x.size // (ROWS * D_MODEL)
            if gi == 0:
                s0, s0_layer = zero_ret, 0
                h0re = jnp.zeros((batch, SSM_N), F32)
                h0im = jnp.zeros((batch, SSM_N), F32)
                buf0 = jnp.zeros(((CONV_K - 1) * batch, CONV_W), F32)
                bblk = batch
            else:
                s0, s0_layer = state_ret, l
                h0re = state_ssm_re[l].reshape(batch, SSM_N)
                h0im = state_ssm_im[l].reshape(batch, SSM_N)
                buf0 = state_conv[l].transpose(1, 0, 2).reshape((CONV_K - 1) * batch, CONV_W)
                bblk = 16
            z = _inproj(x, norm_mix[l][None, :], w_in_b, l)
            if x.ndim == 3:
                z, x = z
            oa, ret_stacked[gi] = _retention(z, 0, nblk, batch, bblk, ret_tabs[gi], ret_norm[l][None, :], s0, s0_layer,
                                             ret_stacked[gi], l, depth)
            ys, xre, xim = _s5(z, 0, nblk, batch, bmat, cmat, are_row, aim_row, ssm_d[l][None, :], h0re, h0im)
            oc, buf = _conv(z, 0, nblk, batch, buf0, conv_w[l], conv_b[l][None, :])
            st = states[gi]
            st[1].append(xre.reshape(batch, SSM_G, SSM_P))
            st[2].append(xim.reshape(batch, SSM_G, SSM_P))
            st[3].append(buf.reshape(CONV_K - 1, batch, CONV_W).transpose(1, 0, 2))

            x1, xn = _merge(x, z, oa, ys, oc, *proj, norm_ffn[l][None, :])
            e1, e2, g = _peer_select(xn, wq, k1big, k2big)
            last = l == depth - 1
            xs[gi] = _peer_dense(xn, e1, e2, g, ut, vt, x1, l, norm_final[None, :], last,
                                 out_batch=bp if last and gi == 0 else 0)

    y_prompt = xs[0]
    y_sample = _batch_major(xs[1], bs, ss)
    (_, re_p, im_p, cv_p), (_, re_s, im_s, cv_s) = states
    return (y_prompt, y_sample,
            ret_stacked[0], ret_stacked[1],
            jnp.stack(re_p), jnp.stack(re_s),
            jnp.stack(im_p), jnp.stack(im_s),
            jnp.stack(cv_p), jnp.stack(cv_s))
```

```python
import functools
import math

import jax
import jax.numpy as jnp
from jax import lax
from jax.experimental import pallas as pl
from jax.experimental.pallas import tpu as pltpu

F32 = jnp.float32
BF16 = jnp.bfloat16

D_MODEL = 1024
DEPTH = 2
PAST_LEN = 16384
RET_HEADS = 8
RET_DK = 64
RET_W = 512
RET_CHUNK = 128
ROPE_BASE = 10000.0
SSM_W = 512
SSM_GC = 16
SSM_G = 32
SSM_P = 64
SSM_N = SSM_G * SSM_P
SSM_SLABS = 4
SSM_SLAB_N = SSM_N // SSM_SLABS
CONV_W = 512
CONV_K = 3
PROJ_W = 7168
PEER_HEADS = 8
PEER_DQ = 256
PEER_NKEYS = 128
PEER_TOPK = 16
PEER_NEXP = PEER_NKEYS ** 2
PEER_SLOTS = PEER_HEADS * PEER_TOPK
TOPK_GROUP = 8
EPS = 1e-6
GELU_C = math.sqrt(2.0 / math.pi)
GELU_A = 0.044715

ROWS = 1024
LANES = 128
SUBLANES = 8
MXU_DEPTH = 256
MIB = 1024 * 1024


def _params(sem, vmem_mib):
    return pltpu.CompilerParams(dimension_semantics=sem, vmem_limit_bytes=vmem_mib * MIB)


def _rms(x, g):
    return x * lax.rsqrt(jnp.mean(x * x, axis=-1, keepdims=True) + EPS) * g


def _dot(a, b):
    return jnp.dot(a, b, preferred_element_type=F32)


def _dot_nt(a, b):
    return lax.dot_general(a, b, (((1,), (1,)), ((), ())), preferred_element_type=F32)


def _inproj_kernel(x_ref, g_ref, w_ref, z_ref, h_scr):
    @pl.when(pl.program_id(1) == 0)
    def _():
        h_scr[...] = _rms(x_ref[...], g_ref[...]).astype(BF16)

    z_ref[...] = _dot(h_scr[...], w_ref[...])


def _inproj_bm_kernel(x_ref, g_ref, w_ref, z_ref, xtm_ref, h_scr):
    @pl.when(pl.program_id(1) == 0)
    def _():
        x = pltpu.einshape("btd->(tb)d", x_ref[...])
        xtm_ref[...] = x
        h_scr[...] = _rms(x, g_ref[...]).astype(BF16)

    z_ref[...] = _dot(h_scr[...], w_ref[...])


def _inproj(x, g, w, layer):
    batch_major = x.ndim == 3
    nb = 1024
    if batch_major:
        batch, seq, _ = x.shape
        t = batch * seq
        x_spec = pl.BlockSpec((batch, ROWS // batch, D_MODEL), lambda i, j: (0, i, 0))
    else:
        t = x.shape[0]
        x_spec = pl.BlockSpec((ROWS, D_MODEL), lambda i, j: (i, 0))
    z_spec = pl.BlockSpec((ROWS, nb), lambda i, j: (i, j))
    z_shape = jax.ShapeDtypeStruct((t, PROJ_W), F32)
    return pl.pallas_call(
        _inproj_bm_kernel if batch_major else _inproj_kernel,
        grid=(t // ROWS, PROJ_W // nb),
        in_specs=[x_spec,
                  pl.BlockSpec((1, D_MODEL), lambda i, j: (0, 0)),
                  pl.BlockSpec((None, D_MODEL, nb), lambda i, j: (layer, 0, j))],
        out_specs=[z_spec, pl.BlockSpec((ROWS, D_MODEL), lambda i, j: (i, 0))] if batch_major else z_spec,
        out_shape=[z_shape, jax.ShapeDtypeStruct((t, D_MODEL), F32)] if batch_major else z_shape,
        scratch_shapes=[pltpu.VMEM((ROWS, D_MODEL), BF16)],
        compiler_params=_params(("parallel", "arbitrary"), 48),
        name="inproj",
    )(x, g, w)


def _ret_kernel(batch, bblk, aliased, *refs):
    refs = [r for i, r in enumerate(refs) if not (aliased and i == 9)]
    (z_ref, cos_ref, sa_ref, sb_ref, qdec_ref, kdec_ref, cdec_ref, gn_ref, s0_ref,
     o_ref, s_ref, qd_scr, kd_scr, v_scr, mask_scr, oacc_scr) = refs
    bb = pl.program_id(0)
    c = pl.program_id(1)
    steps = ROWS // batch
    nslab = RET_W // LANES

    def head_view(ref, h):
        return ref[h // 2, :, (h % 2) * RET_DK:(h % 2 + 1) * RET_DK]

    seq_local = steps >= LANES
    msize = steps if seq_local else ROWS

    @pl.when((bb == 0) & (c == 0))
    def _():
        r = lax.broadcasted_iota(jnp.int32, (msize, msize), 0)
        cc = lax.broadcasted_iota(jnp.int32, (msize, msize), 1)
        if seq_local:
            mask_scr[...] = (r >= cc).astype(F32)
        else:
            same = (r & (batch - 1)) == (cc & (batch - 1))
            mask_scr[...] = (same & (r >= cc)).astype(F32)

    @pl.when(c == 0)
    def _():
        s_ref[...] = s0_ref[...]

    @pl.when(bb == 0)
    def _():
        def per_row(t_ref):
            return jnp.broadcast_to(t_ref[...][:, None, :], (steps, batch, LANES)).reshape(ROWS, LANES)

        cos, sa, sb = per_row(cos_ref), per_row(sa_ref), per_row(sb_ref)

        def rot(x):
            return x * cos + pltpu.roll(x, 32, 1) * sa + pltpu.roll(x, 96, 1) * sb

        for s in range(nslab):
            cols = slice(s * LANES, (s + 1) * LANES)
            qd_scr[s] = rot(z_ref[:, cols]) * qdec_ref[:, cols]
            kd_scr[s] = rot(z_ref[:, RET_W + s * LANES:RET_W + (s + 1) * LANES]) * kdec_ref[:, cols]
            v_scr[s] = z_ref[:, 2 * RET_W + s * LANES:2 * RET_W + (s + 1) * LANES]
        if not seq_local:
            for s in range(nslab):
                outs = []
                for h in (2 * s, 2 * s + 1):
                    qh = head_view(qd_scr, h).astype(BF16)
                    kh = head_view(kd_scr, h).astype(BF16)
                    vh = head_view(v_scr, h).astype(BF16)
                    p = (_dot_nt(qh, kh) * mask_scr[...]).astype(BF16)
                    outs.append(_dot(p, vh))
                oacc_scr[s] = jnp.concatenate(outs, axis=1)

    def per_seq(bl, carry):
        b = bb * bblk + bl
        rows = pl.ds(b, steps, stride=batch)
        for s in range(nslab):
            qb = qd_scr[s, rows, :]
            kb = kd_scr[s, rows, :]
            vb = v_scr[s, rows, :]
            outs = []
            for hh in range(2):
                h = 2 * s + hh
                hc = slice(hh * RET_DK, (hh + 1) * RET_DK)
                q16, k16, v16 = qb[:, hc].astype(BF16), kb[:, hc].astype(BF16), vb[:, hc].astype(BF16)
                st = s_ref[bl, h]
                o = _dot(q16, st.astype(BF16))
                if seq_local:
                    p = (_dot_nt(q16, k16) * mask_scr[...]).astype(BF16)
                    o = _dot(p, v16) + o
                outs.append(o)
                upd = lax.dot_general(k16, v16, (((0,), (0,)), ((), ())), preferred_element_type=F32)
                s_ref[bl, h] = (st + upd) * cdec_ref[:, h * RET_DK:(h + 1) * RET_DK]
            o2 = jnp.concatenate(outs, axis=1)
            oacc_scr[s, rows, :] = o2 if seq_local else oacc_scr[s, rows, :] + o2
        return carry

    lax.fori_loop(0, bblk, per_seq, 0, unroll=4)

    @pl.when(bb == pl.num_programs(0) - 1)
    def _():
        r = lax.broadcasted_iota(jnp.int32, (LANES, LANES), 0) // RET_DK
        cc = lax.broadcasted_iota(jnp.int32, (LANES, LANES), 1) // RET_DK
        avg = jnp.where(r == cc, 1.0 / RET_DK, 0.0).astype(BF16)

        def seg_mean(x):
            hi = x.astype(BF16)
            lo = (x - hi.astype(F32)).astype(BF16)
            return _dot(hi, avg) + _dot(lo, avg)

        normed = []
        for s in range(nslab):
            o2 = oacc_scr[s]
            dlt = o2 - seg_mean(o2)
            normed.append(dlt * lax.rsqrt(seg_mean(dlt * dlt) + EPS))
        o = jnp.concatenate(normed, axis=1) * gn_ref[...]
        o_ref[...] = jax.nn.silu(z_ref[:, 3 * RET_W:4 * RET_W]) * o


def _retention(z, row_blk0, nblk, batch, bblk, tabs, gn, s0, layer, stacked, out_layer, depth):
    cos, sa, sb, qdec, kdec, cdec = tabs
    nbb = batch // bblk
    steps = ROWS // batch
    msize = steps if steps >= LANES else ROWS
    st_spec = pl.BlockSpec((None, bblk, RET_HEADS, RET_DK, RET_DK), lambda bb, c: (out_layer, bb, 0, 0, 0))
    st_in = pl.BlockSpec((None, bblk, RET_HEADS, RET_DK, RET_DK), lambda bb, c: (layer, bb, 0, 0, 0))
    aliased = stacked is not None
    const = lambda bb, c: (0, 0)
    return pl.pallas_call(
        functools.partial(_ret_kernel, batch, bblk, aliased),
        grid=(nbb, nblk),
        in_specs=[pl.BlockSpec((ROWS, 4 * RET_W), lambda bb, c: (row_blk0 + c, 0)),
                  pl.BlockSpec((steps, LANES), lambda bb, c: (c, 0)),
                  pl.BlockSpec((steps, LANES), lambda bb, c: (c, 0)),
                  pl.BlockSpec((steps, LANES), lambda bb, c: (c, 0)),
                  pl.BlockSpec((ROWS, RET_W), const),
                  pl.BlockSpec((ROWS, RET_W), const),
                  pl.BlockSpec((1, RET_W), const),
                  pl.BlockSpec((1, RET_W), const),
                  st_in] + ([pl.BlockSpec(memory_space=pl.ANY)] if aliased else []),
        out_specs=[pl.BlockSpec((ROWS, RET_W), lambda bb, c: (c, 0)), st_spec],
        out_shape=[jax.ShapeDtypeStruct((nblk * ROWS, RET_W), F32),
                   jax.ShapeDtypeStruct((depth, batch, RET_HEADS, RET_DK, RET_DK), F32)],
        input_output_aliases={9: 1} if aliased else {},
        scratch_shapes=[pltpu.VMEM((RET_W // LANES, ROWS, LANES), F32)] * 3
        + [pltpu.VMEM((msize, msize), F32), pltpu.VMEM((RET_W // LANES, ROWS, LANES), F32)],
        compiler_params=_params(("arbitrary", "arbitrary"), 56),
        name="retention",
    )(z, cos, sa, sb, qdec, kdec, cdec, gn, s0, *([stacked] if aliased else []))


def _retention_tables(pos, batch):
    half = RET_DK // 2
    lane = jnp.arange(LANES)
    upper = (lane % RET_DK) >= half
    freqs = (ROPE_BASE ** (-jnp.arange(half, dtype=F32) / half))[lane % half]
    ang = pos[:, None] * freqs[None, :]
    cos_t, sin = jnp.cos(ang), jnp.sin(ang)
    sa_t = jnp.where(upper[None, :], sin, 0.0)
    sb_t = jnp.where(upper[None, :], 0.0, -sin)
    lg = jnp.repeat(jnp.log1p(-jnp.exp2(-5.0 - jnp.arange(RET_HEADS, dtype=F32))), RET_DK)
    steps = ROWS // batch
    i1 = (jnp.arange(ROWS) // batch).astype(F32) + 1.0
    qdec = jnp.exp(i1[:, None] * lg[None, :])
    kdec = jnp.exp(-i1[:, None] * lg[None, :]) * (RET_DK ** -0.5)
    cdec = jnp.exp(steps * lg)[None, :]
    return cos_t, sa_t, sb_t, qdec, kdec, cdec


def _s5_disc_kernel(are_ref, aim_ref, ldt_ref, bre_ref, bim_ref, abre_ref, abim_ref, bbre_ref, bbim_ref):
    ar, ai = are_ref[...], aim_ref[...]
    dt = jnp.exp(ldt_ref[...])
    dar, dai = dt * ar, dt * ai
    mag = jnp.exp(dar)
    abar_re, abar_im = mag * jnp.cos(dai), mag * jnp.sin(dai)
    den = ar * ar + ai * ai
    nr, ni = abar_re - 1.0, abar_im
    f_re = (nr * ar + ni * ai) / den
    f_im = (ni * ar - nr * ai) / den
    abre_ref[...] = abar_re
    abim_ref[...] = abar_im
    br, bi = bre_ref[...], bim_ref[...]
    bbre_ref[...] = f_re[:, None, :] * br - f_im[:, None, :] * bi
    bbim_ref[...] = f_re[:, None, :] * bi + f_im[:, None, :] * br


def _s5_discretise(a_re, a_im, log_dt, b_re_t, b_im_t):
    lg = a_re.shape[0]
    small = jax.ShapeDtypeStruct((lg, SSM_P), F32)
    big = jax.ShapeDtypeStruct((lg, SSM_GC, SSM_P), F32)
    return pl.pallas_call(_s5_disc_kernel, out_shape=[small, small, big, big], name="s5_disc")(
        a_re, a_im, log_dt, b_re_t, b_im_t)


def _s5_kernel(batch, u_ref, bmat_ref, cmat_ref, are_ref, aim_ref, d_ref, h0re_ref, h0im_ref,
               y_ref, xre_ref, xim_ref, x_scr):
    c = pl.program_id(0)
    steps = ROWS // batch
    half = SSM_SLAB_N

    @pl.when(c == 0)
    def _():
        xre_ref[...] = h0re_ref[...]
        xim_ref[...] = h0im_ref[...]

    u = u_ref[...]
    ub = u.astype(BF16)
    for s in range(SSM_SLABS):
        x_scr[:, 2 * half * s:2 * half * (s + 1)] = _dot(ub[:, s * LANES:(s + 1) * LANES], bmat_ref[s])

    for s in range(SSM_SLABS):
        re0 = 2 * half * s
        im0 = re0 + half
        sc = slice(half * s, half * (s + 1))
        ar = jnp.broadcast_to(are_ref[:, sc], (SUBLANES, half))
        ai = jnp.broadcast_to(aim_ref[:, sc], (SUBLANES, half))

        def row_tile(rt, carry, re0=re0, im0=im0, sc=sc, ar=ar, ai=ai):
            r0 = pl.multiple_of(rt * SUBLANES, SUBLANES)

            def step(t, x):
                xr, xi = x
                row = pl.multiple_of(t * batch + r0, SUBLANES)
                nr = ar * xr - ai * xi + x_scr[pl.ds(row, SUBLANES), re0:re0 + half]
                ni = ar * xi + ai * xr + x_scr[pl.ds(row, SUBLANES), im0:im0 + half]
                x_scr[pl.ds(row, SUBLANES), re0:re0 + half] = nr
                x_scr[pl.ds(row, SUBLANES), im0:im0 + half] = ni
                return nr, ni

            init = (xre_ref[pl.ds(r0, SUBLANES), sc], xim_ref[pl.ds(r0, SUBLANES), sc])
            xr, xi = lax.fori_loop(0, steps, step, init, unroll=8)
            xre_ref[pl.ds(r0, SUBLANES), sc] = xr
            xim_ref[pl.ds(r0, SUBLANES), sc] = xi
            return carry

        lax.fori_loop(0, batch // SUBLANES, row_tile, 0)

    ys = [_dot(x_scr[:, 2 * half * s:2 * half * (s + 1)].astype(BF16), cmat_ref[s]) for s in range(SSM_SLABS)]
    y = jnp.concatenate(ys, axis=1) + d_ref[...] * u
    y_ref[...] = jax.nn.gelu(y)


def _s5(z, row_blk0, nblk, batch, bmat, cmat, abre, abim, d, h0re, h0im):
    const2 = lambda c: (0, 0)
    const3 = lambda c: (0, 0, 0)
    st = pl.BlockSpec((batch, SSM_N), const2)
    return pl.pallas_call(
        functools.partial(_s5_kernel, batch),
        grid=(nblk,),
        in_specs=[pl.BlockSpec((ROWS, SSM_W), lambda c: (row_blk0 + c, 4)),
                  pl.BlockSpec((SSM_SLABS, LANES, 2 * SSM_SLAB_N), const3),
                  pl.BlockSpec((SSM_SLABS, 2 * SSM_SLAB_N, LANES), const3),
                  pl.BlockSpec((1, SSM_N), const2),
                  pl.BlockSpec((1, SSM_N), const2),
                  pl.BlockSpec((1, SSM_W), const2),
                  st, st],
        out_specs=[pl.BlockSpec((ROWS, SSM_W), lambda c: (c, 0)), st, st],
        out_shape=[jax.ShapeDtypeStruct((nblk * ROWS, SSM_W), F32),
                   jax.ShapeDtypeStruct((batch, SSM_N), F32),
                   jax.ShapeDtypeStruct((batch, SSM_N), F32)],
        scratch_shapes=[pltpu.VMEM((ROWS, 2 * SSM_N), F32)],
        compiler_params=_params(("arbitrary",), 48),
        name="s5",
    )(z, bmat, cmat, abre, abim, d, h0re, h0im)


def _block_diag_slabs(w, nslab):
    gps = SSM_G // nslab
    eye = jnp.eye(gps, dtype=w.dtype)
    w4 = w.reshape(nslab, gps, w.shape[1], w.shape[2])
    out = w4[:, :, :, None, :] * eye[None, :, None, :, None]
    return out.reshape(nslab, gps * w.shape[1], gps * w.shape[2])


def _merge_kernel(batch, x_ref, oa_ref, ys_ref, bg_ref, cg_ref, hc_ref, buf0_ref, cw_ref, cb_ref,
                  ga_ref, gb_ref, gc_ref, wr_ref, wa_ref, wb_ref, wc_ref, wm_ref, gf_ref,
                  x1_ref, xn_ref, buf_ref, zp_scr):
    rb = x_ref.shape[0]
    pad = (CONV_K - 1) * batch

    @pl.when(pl.program_id(0) == 0)
    def _():
        zp_scr[0:pad, :] = buf0_ref[...]

    zc = cg_ref[...] * hc_ref[...]
    zp_scr[pad:pad + rb, :] = zc
    y = cb_ref[...]
    for j in range(CONV_K):
        y = y + cw_ref[j:j + 1, :] * zp_scr[j * batch:j * batch + rb, :]
    oc_pre = bg_ref[...] * y
    tail = zp_scr[rb:rb + pad, :]
    buf_ref[...] = tail
    zp_scr[0:pad, :] = tail

    oa = _dot(oa_ref[...].astype(BF16), wr_ref[...])
    ysb = ys_ref[...].astype(BF16)
    ob = _dot(ysb, wa_ref[...]) * jax.nn.sigmoid(_dot(ysb, wb_ref[...]))
    oc = _dot(oc_pre.astype(BF16), wc_ref[...])
    merged = (jax.nn.sigmoid(ga_ref[...]) * oa + jax.nn.sigmoid(gb_ref[...]) * ob
              + jax.nn.sigmoid(gc_ref[...]) * oc)
    x1 = x_ref[...] + _dot(merged.astype(BF16), wm_ref[...])
    x1_ref[...] = x1
    xn_ref[...] = _rms(x1, gf_ref[...]).astype(BF16)


def _merge(x, z, oa, ys, batch, buf0, conv_w, conv_b, wr, wa, wb, wc, wm, gf):
    t = x.shape[0]
    rb = 512
    pad = (CONV_K - 1) * batch
    assert pad <= rb
    row = lambda w: pl.BlockSpec((rb, w), lambda i: (i, 0))
    zcol = lambda w, j: pl.BlockSpec((rb, w), lambda i: (i, j))
    const = lambda r, w: pl.BlockSpec((r, w), lambda i: (0, 0))
    return pl.pallas_call(
        functools.partial(_merge_kernel, batch),
        grid=(t // rb,),
        in_specs=[row(D_MODEL), row(RET_W), row(SSM_W),
                  zcol(CONV_W, 5), zcol(CONV_W, 6), zcol(CONV_W, 7),
                  const(pad, CONV_W), const(CONV_K, CONV_W), const(1, CONV_W),
                  zcol(D_MODEL, 4), zcol(D_MODEL, 5), zcol(D_MODEL, 6),
                  const(RET_W, D_MODEL), const(SSM_W, D_MODEL), const(SSM_W, D_MODEL), const(CONV_W, D_MODEL),
                  const(D_MODEL, D_MODEL), const(1, D_MODEL)],
        out_specs=[row(D_MODEL), row(D_MODEL), const(pad, CONV_W)],
        out_shape=[jax.ShapeDtypeStruct((t, D_MODEL), F32), jax.ShapeDtypeStruct((t, D_MODEL), BF16),
                   jax.ShapeDtypeStruct((pad, CONV_W), F32)],
        scratch_shapes=[pltpu.VMEM((rb + pad, CONV_W), F32)],
        compiler_params=_params(("arbitrary",), 48),
        name="merge",
    )(x, oa, ys, z, z, z, buf0, conv_w, conv_b, z, z, z, wr, wa, wb, wc, wm, gf)


def _tree(items, combine):
    while len(items) > 1:
        nxt = [combine(items[i], items[i + 1]) for i in range(0, len(items) - 1, 2)]
        if len(items) % 2:
            nxt.append(items[-1])
        items = nxt
    return items[0]


def _first_max(x, y):
    (vx, ix), (vy, iy) = x, y
    return jnp.maximum(vx, vy), jnp.where(vx >= vy, ix, iy)


def _bits(x, n):
    out, rest = [], x
    for _ in range(n):
        half = jnp.floor(rest * 0.5)
        out.append(rest - 2.0 * half == 1.0)
        rest = half
    return out


def _mux(vals, bits):
    level = list(vals)
    for bit in bits:
        level = [jnp.where(bit, level[j + 1], level[j]) for j in range(0, len(level), 2)]
    return level[0]


def _top16_of_keys(s_scrs, gv_scrs, gi_scrs, v_scrs, i_scrs):
    grp = TOPK_GROUP
    ngrp = PEER_NKEYS // grp
    nbits = ngrp.bit_length() - 1

    for s_scr, gv, gi in zip(s_scrs, gv_scrs, gi_scrs):
        for g in range(ngrp):
            gv[g], gi[g] = _tree([(s_scr[g * grp + p], float(g * grp + p)) for p in range(grp)], _first_max)

    def body(r, carry):
        for s_scr, gv, gi, v_scr, i_scr in zip(s_scrs, gv_scrs, gi_scrs, v_scrs, i_scrs):
            m, idx = _tree([(gv[g], gi[g]) for g in range(ngrp)], _first_max)
            v_scr[r] = m
            i_scr[r] = idx
            gid = jnp.floor(idx * (1.0 / grp))
            rel = idx - gid * grp
            bits = _bits(gid, nbits)
            cands = []
            for p in range(grp):
                val = _mux([s_scr[g * grp + p] for g in range(ngrp)], bits)
                left = (val < m) | ((val == m) & (rel < float(p)))
                cands.append((jnp.where(left, val, -jnp.inf), float(p)))
            nv, npos = _tree(cands, _first_max)
            ni = gid * grp + npos
            for g in range(ngrp):
                hit = gid == float(g)
                gv[g] = jnp.where(hit, nv, gv[g])
                gi[g] = jnp.where(hit, ni, gi[g])
        return carry

    lax.fori_loop(0, PEER_TOPK, body, 0)


def _select_kernel(tb, xn_ref, wq_ref, k1_ref, k2_ref, e1_ref, e2_ref, g_ref,
                   s1_scr, s2_scr, gv1_scr, gi1_scr, gv2_scr, gi2_scr, v1_scr, i1_scr, v2_scr, i2_scr,
                   hv_scr, hb_scr, sc_scr, se1_scr, se2_scr):
    q = _dot(xn_ref[...], wq_ref[...]).astype(BF16)
    hq = PEER_HEADS * PEER_DQ // 2
    s1 = _dot_nt(k1_ref[...], q[:, :hq])
    s2 = _dot_nt(k2_ref[...], q[:, hq:])
    kbits = PEER_TOPK.bit_length() - 1
    for lt in range(tb // LANES):
        lanes = slice(lt * LANES, (lt + 1) * LANES)
        s1_scr[...] = s1[:, lanes].reshape(PEER_NKEYS, SUBLANES, LANES)
        s2_scr[...] = s2[:, lanes].reshape(PEER_NKEYS, SUBLANES, LANES)
        _top16_of_keys((s1_scr, s2_scr), (gv1_scr, gv2_scr), (gi1_scr, gi2_scr), (v1_scr, v2_scr), (i1_scr, i2_scr))

        for a in range(PEER_TOPK):
            hv_scr[a] = v1_scr[a] + v2_scr[0]
            hb_scr[a] = jnp.zeros((SUBLANES, LANES), F32)

        def body(r, carry):
            m, a_sel = _tree([(hv_scr[a], float(a)) for a in range(PEER_TOPK)], _first_max)
            abits = _bits(a_sel, kbits)
            b_sel = _mux([hb_scr[a] for a in range(PEER_TOPK)], abits)
            bbits = _bits(b_sel, kbits)
            sc_scr[r] = m
            se1_scr[r] = _mux([i1_scr[a] for a in range(PEER_TOPK)], abits)
            se2_scr[r] = _mux([i2_scr[b] for b in range(PEER_TOPK)], bbits)
            nb = b_sel + 1.0
            v2_next = _mux([v2_scr[(b + 1) % PEER_TOPK] for b in range(PEER_TOPK)], bbits)
            v1_sel = _mux([v1_scr[a] for a in range(PEER_TOPK)], abits)
            live = (a_sel + 1.0) * (nb + 1.0) <= float(PEER_TOPK)
            nv = jnp.where(live, v1_sel + v2_next, -jnp.inf)
            for a in range(PEER_TOPK):
                hit = a_sel == float(a)
                hv_scr[a] = jnp.where(hit, nv, hv_scr[a])
                hb_scr[a] = jnp.where(hit, nb, hb_scr[a])
            return carry

        lax.fori_loop(0, PEER_TOPK, body, 0)
        sc = sc_scr[...]
        ex = jnp.exp(sc - jnp.max(sc, axis=0, keepdims=True))
        gate = ex / jnp.sum(ex, axis=0, keepdims=True)
        rows = slice(lt * LANES, (lt + 1) * LANES)
        g_ref[rows, :] = gate.reshape(PEER_SLOTS, LANES).T
        e1_ref[rows, :] = se1_scr[...].reshape(PEER_SLOTS, LANES).T
        e2_ref[rows, :] = se2_scr[...].reshape(PEER_SLOTS, LANES).T


def _peer_select(xn, wq, k1big, k2big):
    t = xn.shape[0]
    tb = 256
    hq = PEER_HEADS * PEER_DQ // 2
    nk = PEER_NKEYS * PEER_HEADS
    const = lambda i: (0, 0)
    row = lambda dt: jax.ShapeDtypeStruct((t, PEER_SLOTS), dt)
    vec = lambda n: pltpu.VMEM((n, SUBLANES, LANES), F32)
    return pl.pallas_call(
        functools.partial(_select_kernel, tb),
        grid=(t // tb,),
        in_specs=[pl.BlockSpec((tb, D_MODEL), lambda i: (i, 0)),
                  pl.BlockSpec((D_MODEL, 2 * hq), const),
                  pl.BlockSpec((nk, hq), const),
                  pl.BlockSpec((nk, hq), const)],
        out_specs=[pl.BlockSpec((tb, PEER_SLOTS), lambda i: (i, 0))] * 3,
        out_shape=[row(F32), row(F32), row(F32)],
        scratch_shapes=[vec(PEER_NKEYS), vec(PEER_NKEYS)] + [vec(PEER_NKEYS // TOPK_GROUP)] * 4
        + [vec(PEER_TOPK), vec(PEER_TOPK), vec(PEER_TOPK), vec(PEER_TOPK),
                        vec(PEER_TOPK), vec(PEER_TOPK), vec(PEER_TOPK), vec(PEER_TOPK), vec(PEER_TOPK)],
        compiler_params=_params(("parallel",), 40),
        name="peer_select",
    )(xn, wq, k1big, k2big)


def _peer_kernel(tb, eb, stride, final_norm, out_batch, xn_ref, e1_ref, e2_ref, g_ref, ut_ref, v_ref, x1_ref,
                 gain_ref, out_ref, m_scr, *acc_scr):
    e = pl.program_id(1)
    nk1 = eb // PEER_NKEYS
    acc_ref = acc_scr[0] if out_batch else out_ref

    @pl.when(e == 0)
    def _():
        acc_ref[...] = x1_ref[...]

    @pl.when(e == 0)
    def _():
        key = lax.broadcasted_iota(jnp.int32, (PEER_NKEYS, PEER_SLOTS), 0).astype(F32)

        def token(t, carry):
            e1 = e1_ref[pl.ds(t, 1), :]
            e2 = e2_ref[pl.ds(t, 1), :]
            gt = 0.5 * g_ref[pl.ds(t, 1), :]
            a_t = jnp.where(key == e1, gt, 0.0).astype(BF16)
            b_t = jnp.where(key == e2, 1.0, 0.0).astype(BF16)
            m_scr[pl.ds(t, PEER_NKEYS, stride=stride), :] = _dot_nt(a_t, b_t)
            return carry

        lax.fori_loop(0, tb, token, 0, unroll=64)

    s = _dot(xn_ref[...], ut_ref[...])
    t = jnp.tanh(s * (GELU_C + (GELU_C * GELU_A) * (s * s)))
    k1 = e * nk1
    gates = [m_scr[pl.ds(pl.multiple_of((k1 + i) * stride, SUBLANES), tb), :] for i in range(nk1)]
    w = ((s + s * t) * jnp.concatenate(gates, axis=1)).astype(BF16)
    acc_ref[...] += _dot(w, v_ref[...])

    if final_norm or out_batch:
        @pl.when(e == pl.num_programs(1) - 1)
        def _():
            y = acc_ref[...]
            if final_norm:
                y = _rms(y, gain_ref[...])
            out_ref[...] = pltpu.einshape("(tb)d->btd", y, b=out_batch) if out_batch else y


def _peer_dense(xn, e1, e2, g, ut, v, x1, layer, gain, final_norm, out_batch=0):
    t = xn.shape[0]
    tb, eb = 512, 1024
    stride = tb + SUBLANES
    once = pl.Buffered(1)
    tok = lambda w: pl.BlockSpec((tb, w), lambda i, e: (i, 0), pipeline_mode=once)
    tab = pl.BlockSpec((None, eb, D_MODEL), lambda i, e: (layer, e, 0))
    if out_batch:
        out_spec = pl.BlockSpec((out_batch, tb // out_batch, D_MODEL), lambda i, e: (0, i, 0))
        out_shape = jax.ShapeDtypeStruct((out_batch, t // out_batch, D_MODEL), F32)
        acc = [pltpu.VMEM((tb, D_MODEL), F32)]
    else:
        out_spec = pl.BlockSpec((tb, D_MODEL), lambda i, e: (i, 0))
        out_shape = jax.ShapeDtypeStruct((t, D_MODEL), F32)
        acc = []
    return pl.pallas_call(
        functools.partial(_peer_kernel, tb, eb, stride, final_norm, out_batch),
        grid=(t // tb, PEER_NEXP // eb),
        in_specs=[tok(D_MODEL), tok(PEER_SLOTS), tok(PEER_SLOTS), tok(PEER_SLOTS),
                  pl.BlockSpec((None, D_MODEL, eb), lambda i, e: (layer, 0, e)), tab, tok(D_MODEL),
                  pl.BlockSpec((1, D_MODEL), lambda i, e: (0, 0))],
        out_specs=out_spec,
        out_shape=out_shape,
        scratch_shapes=[pltpu.VMEM((PEER_NKEYS * stride, PEER_NKEYS), F32)] + acc,
        compiler_params=_params(("parallel", "arbitrary"), 60),
        name="peer_dense",
    )(xn, e1, e2, g, ut, v, x1, gain)


def _time_major(x):
    b, s, d = x.shape
    return x.transpose(1, 0, 2).reshape(s * b, d)


def _batch_major(y, b, s):
    return y.reshape(s, b, y.shape[-1]).transpose(1, 0, 2)


def kernel(x_prompt, x_sample, state_ret, state_ssm_re, state_ssm_im, state_conv, norm_mix, w_in, ret_norm, w_ret_out, ssm_a_re, ssm_a_im, ssm_b_re, ssm_b_im, ssm_c_re, ssm_c_im, ssm_d, ssm_log_dt, w_glu_a, w_glu_b, conv_w, conv_b, w_conv_out, w_mix_out, norm_ffn, peer_wq, peer_k1, peer_k2, peer_u, peer_v, norm_final):
    bp, sp, _ = x_prompt.shape
    bs, ss, _ = x_sample.shape
    tp, ts = bp * sp, bs * ss
    depth = w_in.shape[0]
    assert tp % ROWS == 0 and ts == ROWS and ROWS % bp == 0 and ROWS // bp == math.gcd(sp, RET_CHUNK)

    xs = [x_prompt, _time_major(x_sample)]
    batches = (bp, bs)
    pos = (jnp.arange(sp, dtype=F32), PAST_LEN + jnp.arange(ss, dtype=F32))
    ret_tabs = [_retention_tables(p, b) for p, b in zip(pos, batches)]

    lg = depth * SSM_G
    abre, abim, bbre, bbim = _s5_discretise(
        ssm_a_re.reshape(lg, SSM_P), ssm_a_im.reshape(lg, SSM_P), ssm_log_dt.reshape(lg, 1),
        ssm_b_re.transpose(0, 1, 3, 2).reshape(lg, SSM_GC, SSM_P),
        ssm_b_im.transpose(0, 1, 3, 2).reshape(lg, SSM_GC, SSM_P))

    hq = PEER_DQ // 2
    eye = jnp.eye(PEER_HEADS, dtype=F32)

    def keys_block_diag(k):
        return (k.transpose(1, 0, 2)[:, :, None, :] * eye[None, :, :, None]).reshape(
            PEER_NKEYS * PEER_HEADS, PEER_HEADS * hq).astype(BF16)

    w_in_b = w_in.astype(BF16)
    ut = peer_u.astype(BF16).transpose(0, 2, 1)
    vt = peer_v.astype(BF16)
    zero_ret = jnp.zeros((1, bp, RET_HEADS, RET_DK, RET_DK), F32)

    states = [[[] for _ in range(4)] for _ in range(2)]
    ret_stacked = [None, None]
    for l in range(depth):
        sl = slice(l * SSM_G, (l + 1) * SSM_G)
        bmat = jnp.concatenate([_block_diag_slabs(bbre[sl], SSM_SLABS), _block_diag_slabs(bbim[sl], SSM_SLABS)],
                               axis=2).astype(BF16)
        cmat = jnp.concatenate([_block_diag_slabs(ssm_c_re[l].transpose(0, 2, 1), SSM_SLABS),
                                _block_diag_slabs(-ssm_c_im[l].transpose(0, 2, 1), SSM_SLABS)],
                               axis=1).astype(BF16)
        are_row = abre[sl].reshape(1, SSM_N)
        aim_row = abim[sl].reshape(1, SSM_N)
        proj =[w.astype(BF16) for w in (w_ret_out[l], w_glu_a[l], w_glu_b[l], w_conv_out[l], w_mix_out[l])]
        wq = peer_wq[l].reshape(D_MODEL, PEER_HEADS, 2, hq).transpose(0, 2, 1, 3).reshape(D_MODEL, -1).astype(BF16)
        k1big, k2big = keys_block_diag(peer_k1[l]), keys_block_diag(peer_k2[l])

        for gi, batch in enumerate(batches):
            x = xs[gi]
            nblk = x.size // (ROWS * D_MODEL)
            if gi == 0:
                s0, s0_layer = zero_ret, 0
                h0re = jnp.zeros((batch, SSM_N), F32)
                h0im = jnp.zeros((batch, SSM_N), F32)
                buf0 = jnp.zeros(((CONV_K - 1) * batch, CONV_W), F32)
                bblk = batch
            else:
                s0, s0_layer = state_ret, l
                h0re = state_ssm_re[l].reshape(batch, SSM_N)
                h0im = state_ssm_im[l].reshape(batch, SSM_N)
                buf0 = state_conv[l].transpose(1, 0, 2).reshape((CONV_K - 1) * batch, CONV_W)
                bblk = 16
            z = _inproj(x, norm_mix[l][None, :], w_in_b, l)
            if x.ndim == 3:
                z, x = z
            oa, ret_stacked[gi] = _retention(z, 0, nblk, batch, bblk, ret_tabs[gi], ret_norm[l][None, :], s0, s0_layer,
                                             ret_stacked[gi], l, depth)
            ys, xre, xim = _s5(z, 0, nblk, batch, bmat, cmat, are_row, aim_row, ssm_d[l][None, :], h0re, h0im)
            st = states[gi]
            st[1].append(xre.reshape(batch, SSM_G, SSM_P))
            st[2].append(xim.reshape(batch, SSM_G, SSM_P))

            x1, xn, buf = _merge(x, z, oa, ys, batch, buf0, conv_w[l], conv_b[l][None, :], *proj, norm_ffn[l][None, :])
            st[3].append(buf.reshape(CONV_K - 1, batch, CONV_W).transpose(1, 0, 2))
            e1, e2, g = _peer_select(xn, wq, k1big, k2big)
            last = l == depth - 1
            xs[gi] = _peer_dense(xn, e1, e2, g, ut, vt, x1, l, norm_final[None, :], last,
                                 out_batch=bp if last and gi == 0 else 0)

    y_prompt = xs[0]
    y_sample = _batch_major(xs[1], bs, ss)
    (_, re_p, im_p, cv_p), (_, re_s, im_s, cv_s) = states
    return (y_prompt, y_sample,
            ret_stacked[0], ret_stacked[1],
            jnp.stack(re_p), jnp.stack(re_s),
            jnp.stack(im_p), jnp.stack(im_s),
            jnp.stack(cv_p), jnp.stack(cv_s))
```

```python
import functools
import math

import jax
import jax.numpy as jnp
from jax import lax
from jax.experimental import pallas as pl
from jax.experimental.pallas import tpu as pltpu

F32 = jnp.float32
BF16 = jnp.bfloat16

D_MODEL = 1024
PAST_LEN = 16384
RET_HEADS = 8
RET_DK = 64
RET_W = 512
RET_CHUNK = 128
ROPE_BASE = 10000.0
SSM_W = 512
SSM_GC = 16
SSM_G = 32
SSM_P = 64
SSM_N = SSM_G * SSM_P
SSM_SLABS = 4
SSM_SLAB_N = SSM_N // SSM_SLABS
CONV_W = 512
CONV_K = 3
PROJ_W = 7168
PEER_HEADS = 8
PEER_DQ = 256
PEER_NKEYS = 128
PEER_TOPK = 16
PEER_NEXP = PEER_NKEYS ** 2
PEER_SLOTS = PEER_HEADS * PEER_TOPK
TOPK_GROUP = 8
EPS = 1e-6
GELU_C = math.sqrt(2.0 / math.pi)
GELU_A = 0.044715

ROWS = 1024
LANES = 128
SUBLANES = 8
MIB = 1024 * 1024


def _params(sem, vmem_mib):
    return pltpu.CompilerParams(dimension_semantics=sem, vmem_limit_bytes=vmem_mib * MIB)


def _rms(x, g):
    return x * lax.rsqrt(jnp.mean(x * x, axis=-1, keepdims=True) + EPS) * g


def _dot(a, b):
    return jnp.dot(a, b, preferred_element_type=F32)


def _dot_nt(a, b):
    return lax.dot_general(a, b, (((1,), (1,)), ((), ())), preferred_element_type=F32)


def _inproj_kernel(x_ref, g_ref, w_ref, z_ref, h_scr):
    @pl.when(pl.program_id(1) == 0)
    def _():
        h_scr[...] = _rms(x_ref[...], g_ref[...]).astype(BF16)

    z_ref[...] = _dot(h_scr[...], w_ref[...])


def _inproj_bm_kernel(x_ref, g_ref, w_ref, z_ref, xtm_ref, h_scr):
    @pl.when(pl.program_id(1) == 0)
    def _():
        x = pltpu.einshape("btd->(tb)d", x_ref[...])
        xtm_ref[...] = x
        h_scr[...] = _rms(x, g_ref[...]).astype(BF16)

    z_ref[...] = _dot(h_scr[...], w_ref[...])


def _inproj(x, g, w, layer):
    batch_major = x.ndim == 3
    nb = 1024
    if batch_major:
        batch, seq, _ = x.shape
        t = batch * seq
        x_spec = pl.BlockSpec((batch, ROWS // batch, D_MODEL), lambda i, j: (0, i, 0))
    else:
        t = x.shape[0]
        x_spec = pl.BlockSpec((ROWS, D_MODEL), lambda i, j: (i, 0))
    z_spec = pl.BlockSpec((ROWS, nb), lambda i, j: (i, j))
    z_shape = jax.ShapeDtypeStruct((t, PROJ_W), F32)
    return pl.pallas_call(
        _inproj_bm_kernel if batch_major else _inproj_kernel,
        grid=(t // ROWS, PROJ_W // nb),
        in_specs=[x_spec,
                  pl.BlockSpec((1, D_MODEL), lambda i, j: (0, 0)),
                  pl.BlockSpec((None, D_MODEL, nb), lambda i, j: (layer, 0, j))],
        out_specs=[z_spec, pl.BlockSpec((ROWS, D_MODEL), lambda i, j: (i, 0))] if batch_major else z_spec,
        out_shape=[z_shape, jax.ShapeDtypeStruct((t, D_MODEL), F32)] if batch_major else z_shape,
        scratch_shapes=[pltpu.VMEM((ROWS, D_MODEL), BF16)],
        compiler_params=_params(("parallel", "arbitrary"), 48),
        name="inproj",
    )(x, g, w)


def _ret_kernel(batch, bblk, aliased, *refs):
    refs = [r for i, r in enumerate(refs) if not (aliased and i == 9)]
    (z_ref, cos_ref, sa_ref, sb_ref, qdec_ref, kdec_ref, cdec_ref, gn_ref, s0_ref,
     o_ref, s_ref, qd_scr, kd_scr, v_scr, mask_scr, oacc_scr) = refs
    bb = pl.program_id(0)
    c = pl.program_id(1)
    steps = ROWS // batch
    nslab = RET_W // LANES

    def head_view(ref, h):
        return ref[h // 2, :, (h % 2) * RET_DK:(h % 2 + 1) * RET_DK]

    seq_local = steps >= LANES
    msize = steps if seq_local else ROWS

    @pl.when((bb == 0) & (c == 0))
    def _():
        r = lax.broadcasted_iota(jnp.int32, (msize, msize), 0)
        cc = lax.broadcasted_iota(jnp.int32, (msize, msize), 1)
        if seq_local:
            mask_scr[...] = (r >= cc).astype(F32)
        else:
            same = (r & (batch - 1)) == (cc & (batch - 1))
            mask_scr[...] = (same & (r >= cc)).astype(F32)

    @pl.when(c == 0)
    def _():
        s_ref[...] = s0_ref[...]

    @pl.when(bb == 0)
    def _():
        def per_row(t_ref):
            return jnp.broadcast_to(t_ref[...][:, None, :], (steps, batch, LANES)).reshape(ROWS, LANES)

        cos, sa, sb = per_row(cos_ref), per_row(sa_ref), per_row(sb_ref)

        def rot(x):
            return x * cos + pltpu.roll(x, 32, 1) * sa + pltpu.roll(x, 96, 1) * sb

        for s in range(nslab):
            cols = slice(s * LANES, (s + 1) * LANES)
            qd_scr[s] = rot(z_ref[:, cols]) * qdec_ref[:, cols]
            kd_scr[s] = rot(z_ref[:, RET_W + s * LANES:RET_W + (s + 1) * LANES]) * kdec_ref[:, cols]
            v_scr[s] = z_ref[:, 2 * RET_W + s * LANES:2 * RET_W + (s + 1) * LANES]
        if not seq_local:
            for s in range(nslab):
                outs = []
                for h in (2 * s, 2 * s + 1):
                    qh = head_view(qd_scr, h).astype(BF16)
                    kh = head_view(kd_scr, h).astype(BF16)
                    vh = head_view(v_scr, h).astype(BF16)
                    p = (_dot_nt(qh, kh) * mask_scr[...]).astype(BF16)
                    outs.append(_dot(p, vh))
                oacc_scr[s] = jnp.concatenate(outs, axis=1)

    def per_seq(bl, carry):
        b = bb * bblk + bl
        rows = pl.ds(b, steps, stride=batch)
        for s in range(nslab):
            qb = qd_scr[s, rows, :]
            kb = kd_scr[s, rows, :]
            vb = v_scr[s, rows, :]
            outs = []
            for hh in range(2):
                h = 2 * s + hh
                hc = slice(hh * RET_DK, (hh + 1) * RET_DK)
                q16, k16, v16 = qb[:, hc].astype(BF16), kb[:, hc].astype(BF16), vb[:, hc].astype(BF16)
                st = s_ref[bl, h]
                o = _dot(q16, st.astype(BF16))
                if seq_local:
                    p = (_dot_nt(q16, k16) * mask_scr[...]).astype(BF16)
                    o = _dot(p, v16) + o
                outs.append(o)
                upd = lax.dot_general(k16, v16, (((0,), (0,)), ((), ())), preferred_element_type=F32)
                s_ref[bl, h] = (st + upd) * cdec_ref[:, h * RET_DK:(h + 1) * RET_DK]
            o2 = jnp.concatenate(outs, axis=1)
            oacc_scr[s, rows, :] = o2 if seq_local else oacc_scr[s, rows, :] + o2
        return carry

    lax.fori_loop(0, bblk, per_seq, 0, unroll=4)

    @pl.when(bb == pl.num_programs(0) - 1)
    def _():
        r = lax.broadcasted_iota(jnp.int32, (LANES, LANES), 0) // RET_DK
        cc = lax.broadcasted_iota(jnp.int32, (LANES, LANES), 1) // RET_DK
        avg = jnp.where(r == cc, 1.0 / RET_DK, 0.0).astype(BF16)

        def seg_mean(x):
            hi = x.astype(BF16)
            lo = (x - hi.astype(F32)).astype(BF16)
            return _dot(hi, avg) + _dot(lo, avg)

        normed = []
        for s in range(nslab):
            o2 = oacc_scr[s]
            dlt = o2 - seg_mean(o2)
            normed.append(dlt * lax.rsqrt(seg_mean(dlt * dlt) + EPS))
        o = jnp.concatenate(normed, axis=1) * gn_ref[...]
        o_ref[...] = jax.nn.silu(z_ref[:, 3 * RET_W:4 * RET_W]) * o


def _retention(z, nblk, batch, bblk, tabs, gn, s0, layer, stacked, out_layer, depth):
    cos, sa, sb, qdec, kdec, cdec = tabs
    nbb = batch // bblk
    steps = ROWS // batch
    msize = steps if steps >= LANES else ROWS
    st_spec = pl.BlockSpec((None, bblk, RET_HEADS, RET_DK, RET_DK), lambda bb, c: (out_layer, bb, 0, 0, 0))
    st_in = pl.BlockSpec((None, bblk, RET_HEADS, RET_DK, RET_DK), lambda bb, c: (layer, bb, 0, 0, 0))
    aliased = stacked is not None
    const = lambda bb, c: (0, 0)
    return pl.pallas_call(
        functools.partial(_ret_kernel, batch, bblk, aliased),
        grid=(nbb, nblk),
        in_specs=[pl.BlockSpec((ROWS, 4 * RET_W), lambda bb, c: (c, 0)),
                  pl.BlockSpec((steps, LANES), lambda bb, c: (c, 0)),
                  pl.BlockSpec((steps, LANES), lambda bb, c: (c, 0)),
                  pl.BlockSpec((steps, LANES), lambda bb, c: (c, 0)),
                  pl.BlockSpec((ROWS, RET_W), const),
                  pl.BlockSpec((ROWS, RET_W), const),
                  pl.BlockSpec((1, RET_W), const),
                  pl.BlockSpec((1, RET_W), const),
                  st_in] + ([pl.BlockSpec(memory_space=pl.ANY)] if aliased else []),
        out_specs=[pl.BlockSpec((ROWS, RET_W), lambda bb, c: (c, 0)), st_spec],
        out_shape=[jax.ShapeDtypeStruct((nblk * ROWS, RET_W), F32),
                   jax.ShapeDtypeStruct((depth, batch, RET_HEADS, RET_DK, RET_DK), F32)],
        input_output_aliases={9: 1} if aliased else {},
        scratch_shapes=[pltpu.VMEM((RET_W // LANES, ROWS, LANES), F32)] * 3
        + [pltpu.VMEM((msize, msize), F32), pltpu.VMEM((RET_W // LANES, ROWS, LANES), F32)],
        compiler_params=_params(("arbitrary", "arbitrary"), 56),
        name="retention",
    )(z, cos, sa, sb, qdec, kdec, cdec, gn, s0, *([stacked] if aliased else []))


def _retention_tables(pos, batch):
    half = RET_DK // 2
    lane = jnp.arange(LANES)
    upper = (lane % RET_DK) >= half
    freqs = (ROPE_BASE ** (-jnp.arange(half, dtype=F32) / half))[lane % half]
    ang = pos[:, None] * freqs[None, :]
    cos_t, sin = jnp.cos(ang), jnp.sin(ang)
    sa_t = jnp.where(upper[None, :], sin, 0.0)
    sb_t = jnp.where(upper[None, :], 0.0, -sin)
    lg = jnp.repeat(jnp.log1p(-jnp.exp2(-5.0 - jnp.arange(RET_HEADS, dtype=F32))), RET_DK)
    steps = ROWS // batch
    i1 = (jnp.arange(ROWS) // batch).astype(F32) + 1.0
    qdec = jnp.exp(i1[:, None] * lg[None, :])
    kdec = jnp.exp(-i1[:, None] * lg[None, :]) * (RET_DK ** -0.5)
    cdec = jnp.exp(steps * lg)[None, :]
    return cos_t, sa_t, sb_t, qdec, kdec, cdec


def _s5_disc_kernel(are_ref, aim_ref, ldt_ref, bre_ref, bim_ref, abre_ref, abim_ref, bbre_ref, bbim_ref):
    ar, ai = are_ref[...], aim_ref[...]
    dt = jnp.exp(ldt_ref[...])
    dar, dai = dt * ar, dt * ai
    mag = jnp.exp(dar)
    abar_re, abar_im = mag * jnp.cos(dai), mag * jnp.sin(dai)
    den = ar * ar + ai * ai
    nr, ni = abar_re - 1.0, abar_im
    f_re = (nr * ar + ni * ai) / den
    f_im = (ni * ar - nr * ai) / den
    abre_ref[...] = abar_re
    abim_ref[...] = abar_im
    br, bi = bre_ref[...], bim_ref[...]
    bbre_ref[...] = f_re[:, None, :] * br - f_im[:, None, :] * bi
    bbim_ref[...] = f_re[:, None, :] * bi + f_im[:, None, :] * br


def _s5_discretise(a_re, a_im, log_dt, b_re_t, b_im_t):
    lg = a_re.shape[0]
    small = jax.ShapeDtypeStruct((lg, SSM_P), F32)
    big = jax.ShapeDtypeStruct((lg, SSM_GC, SSM_P), F32)
    return pl.pallas_call(_s5_disc_kernel, out_shape=[small, small, big, big], name="s5_disc")(
        a_re, a_im, log_dt, b_re_t, b_im_t)


def _s5_kernel(batch, u_ref, bmat_ref, cmat_ref, are_ref, aim_ref, d_ref, h0re_ref, h0im_ref,
               y_ref, xre_ref, xim_ref, x_scr):
    c = pl.program_id(0)
    steps = ROWS // batch
    half = SSM_SLAB_N

    @pl.when(c == 0)
    def _():
        xre_ref[...] = h0re_ref[...]
        xim_ref[...] = h0im_ref[...]

    u = u_ref[...]
    ub = u.astype(BF16)
    for s in range(SSM_SLABS):
        x_scr[:, 2 * half * s:2 * half * (s + 1)] = _dot(ub[:, s * LANES:(s + 1) * LANES], bmat_ref[s])

    for s in range(SSM_SLABS):
        re0 = 2 * half * s
        im0 = re0 + half
        sc = slice(half * s, half * (s + 1))
        ar = jnp.broadcast_to(are_ref[:, sc], (SUBLANES, half))
        ai = jnp.broadcast_to(aim_ref[:, sc], (SUBLANES, half))

        def row_tile(rt, carry, re0=re0, im0=im0, sc=sc, ar=ar, ai=ai):
            r0 = pl.multiple_of(rt * SUBLANES, SUBLANES)

            def step(t, x):
                xr, xi = x
                row = pl.multiple_of(t * batch + r0, SUBLANES)
                nr = ar * xr - ai * xi + x_scr[pl.ds(row, SUBLANES), re0:re0 + half]
                ni = ar * xi + ai * xr + x_scr[pl.ds(row, SUBLANES), im0:im0 + half]
                x_scr[pl.ds(row, SUBLANES), re0:re0 + half] = nr
                x_scr[pl.ds(row, SUBLANES), im0:im0 + half] = ni
                return nr, ni

            init = (xre_ref[pl.ds(r0, SUBLANES), sc], xim_ref[pl.ds(r0, SUBLANES), sc])
            xr, xi = lax.fori_loop(0, steps, step, init, unroll=8)
            xre_ref[pl.ds(r0, SUBLANES), sc] = xr
            xim_ref[pl.ds(r0, SUBLANES), sc] = xi
            return carry

        lax.fori_loop(0, batch // SUBLANES, row_tile, 0)

    ys = [_dot(x_scr[:, 2 * half * s:2 * half * (s + 1)].astype(BF16), cmat_ref[s]) for s in range(SSM_SLABS)]
    y = jnp.concatenate(ys, axis=1) + d_ref[...] * u
    y_ref[...] = jax.nn.gelu(y)


def _s5(z, nblk, batch, bmat, cmat, abre, abim, d, h0re, h0im):
    const2 = lambda c: (0, 0)
    const3 = lambda c: (0, 0, 0)
    st = pl.BlockSpec((batch, SSM_N), const2)
    return pl.pallas_call(
        functools.partial(_s5_kernel, batch),
        grid=(nblk,),
        in_specs=[pl.BlockSpec((ROWS, SSM_W), lambda c: (c, 4)),
                  pl.BlockSpec((SSM_SLABS, LANES, 2 * SSM_SLAB_N), const3),
                  pl.BlockSpec((SSM_SLABS, 2 * SSM_SLAB_N, LANES), const3),
                  pl.BlockSpec((1, SSM_N), const2),
                  pl.BlockSpec((1, SSM_N), const2),
                  pl.BlockSpec((1, SSM_W), const2),
                  st, st],
        out_specs=[pl.BlockSpec((ROWS, SSM_W), lambda c: (c, 0)), st, st],
        out_shape=[jax.ShapeDtypeStruct((nblk * ROWS, SSM_W), F32),
                   jax.ShapeDtypeStruct((batch, SSM_N), F32),
                   jax.ShapeDtypeStruct((batch, SSM_N), F32)],
        scratch_shapes=[pltpu.VMEM((ROWS, 2 * SSM_N), F32)],
        compiler_params=_params(("arbitrary",), 48),
        name="s5",
    )(z, bmat, cmat, abre, abim, d, h0re, h0im)


def _block_diag_slabs(w, nslab):
    gps = SSM_G // nslab
    eye = jnp.eye(gps, dtype=w.dtype)
    w4 = w.reshape(nslab, gps, w.shape[1], w.shape[2])
    out = w4[:, :, :, None, :] * eye[None, :, None, :, None]
    return out.reshape(nslab, gps * w.shape[1], gps * w.shape[2])


def _merge_kernel(batch, x_ref, oa_ref, ys_ref, bg_ref, cg_ref, hc_ref, buf0_ref, cw_ref, cb_ref,
                  ga_ref, gb_ref, gc_ref, wr_ref, wa_ref, wb_ref, wc_ref, wm_ref, gf_ref,
                  x1_ref, xn_ref, buf_ref, zp_scr):
    rb = x_ref.shape[0]
    pad = (CONV_K - 1) * batch

    @pl.when(pl.program_id(0) == 0)
    def _():
        zp_scr[0:pad, :] = buf0_ref[...]

    zc = cg_ref[...] * hc_ref[...]
    zp_scr[pad:pad + rb, :] = zc
    y = cb_ref[...]
    for j in range(CONV_K):
        y = y + cw_ref[j:j + 1, :] * zp_scr[j * batch:j * batch + rb, :]
    oc_pre = bg_ref[...] * y
    tail = zp_scr[rb:rb + pad, :]
    buf_ref[...] = tail
    zp_scr[0:pad, :] = tail

    oa = _dot(oa_ref[...].astype(BF16), wr_ref[...])
    ysb = ys_ref[...].astype(BF16)
    ob = _dot(ysb, wa_ref[...]) * jax.nn.sigmoid(_dot(ysb, wb_ref[...]))
    oc = _dot(oc_pre.astype(BF16), wc_ref[...])
    merged = (jax.nn.sigmoid(ga_ref[...]) * oa + jax.nn.sigmoid(gb_ref[...]) * ob
              + jax.nn.sigmoid(gc_ref[...]) * oc)
    x1 = x_ref[...] + _dot(merged.astype(BF16), wm_ref[...])
    x1_ref[...] = x1
    xn_ref[...] = _rms(x1, gf_ref[...]).astype(BF16)


def _merge(x, z, oa, ys, batch, buf0, conv_w, conv_b, wr, wa, wb, wc, wm, gf):
    t = x.shape[0]
    rb = 512
    pad = (CONV_K - 1) * batch
    assert pad <= rb
    row = lambda w: pl.BlockSpec((rb, w), lambda i: (i, 0))
    zcol = lambda w, j: pl.BlockSpec((rb, w), lambda i: (i, j))
    const = lambda r, w: pl.BlockSpec((r, w), lambda i: (0, 0))
    return pl.pallas_call(
        functools.partial(_merge_kernel, batch),
        grid=(t // rb,),
        in_specs=[row(D_MODEL), row(RET_W), row(SSM_W),
                  zcol(CONV_W, 5), zcol(CONV_W, 6), zcol(CONV_W, 7),
                  const(pad, CONV_W), const(CONV_K, CONV_W), const(1, CONV_W),
                  zcol(D_MODEL, 4), zcol(D_MODEL, 5), zcol(D_MODEL, 6),
                  const(RET_W, D_MODEL), const(SSM_W, D_MODEL), const(SSM_W, D_MODEL), const(CONV_W, D_MODEL),
                  const(D_MODEL, D_MODEL), const(1, D_MODEL)],
        out_specs=[row(D_MODEL), row(D_MODEL), const(pad, CONV_W)],
        out_shape=[jax.ShapeDtypeStruct((t, D_MODEL), F32), jax.ShapeDtypeStruct((t, D_MODEL), BF16),
                   jax.ShapeDtypeStruct((pad, CONV_W), F32)],
        scratch_shapes=[pltpu.VMEM((rb + pad, CONV_W), F32)],
        compiler_params=_params(("arbitrary",), 48),
        name="merge",
    )(x, oa, ys, z, z, z, buf0, conv_w, conv_b, z, z, z, wr, wa, wb, wc, wm, gf)


def _tree(items, combine):
    while len(items) > 1:
        nxt = [combine(items[i], items[i + 1]) for i in range(0, len(items) - 1, 2)]
        if len(items) % 2:
            nxt.append(items[-1])
        items = nxt
    return items[0]


def _first_max(x, y):
    (vx, ix), (vy, iy) = x, y
    return jnp.maximum(vx, vy), jnp.where(vx >= vy, ix, iy)


def _bits(x, n):
    out, rest = [], x
    for _ in range(n):
        half = jnp.floor(rest * 0.5)
        out.append(rest - 2.0 * half == 1.0)
        rest = half
    return out


def _mux(vals, bits):
    level = list(vals)
    for bit in bits:
        level = [jnp.where(bit, level[j + 1], level[j]) for j in range(0, len(level), 2)]
    return level[0]


def _top16_of_keys(s_scrs, gv_scrs, gi_scrs, v_scrs, i_scrs):
    grp = TOPK_GROUP
    ngrp = PEER_NKEYS // grp
    nbits = ngrp.bit_length() - 1

    for s_scr, gv, gi in zip(s_scrs, gv_scrs, gi_scrs):
        for g in range(ngrp):
            gv[g], gi[g] = _tree([(s_scr[g * grp + p], float(g * grp + p)) for p in range(grp)], _first_max)

    def body(r, carry):
        for s_scr, gv, gi, v_scr, i_scr in zip(s_scrs, gv_scrs, gi_scrs, v_scrs, i_scrs):
            m, idx = _tree([(gv[g], gi[g]) for g in range(ngrp)], _first_max)
            v_scr[r] = m
            i_scr[r] = idx
            gid = jnp.floor(idx * (1.0 / grp))
            rel = idx - gid * grp
            bits = _bits(gid, nbits)
            cands = []
            for p in range(grp):
                val = _mux([s_scr[g * grp + p] for g in range(ngrp)], bits)
                left = (val < m) | ((val == m) & (rel < float(p)))
                cands.append((jnp.where(left, val, -jnp.inf), float(p)))
            nv, npos = _tree(cands, _first_max)
            ni = gid * grp + npos
            for g in range(ngrp):
                hit = gid == float(g)
                gv[g] = jnp.where(hit, nv, gv[g])
                gi[g] = jnp.where(hit, ni, gi[g])
        return carry

    lax.fori_loop(0, PEER_TOPK, body, 0)


def _select_kernel(tb, xn_ref, wq_ref, k1_ref, k2_ref, e1_ref, e2_ref, g_ref,
                   s1_scr, s2_scr, gv1_scr, gi1_scr, gv2_scr, gi2_scr, v1_scr, i1_scr, v2_scr, i2_scr,
                   hv_scr, hb_scr, sc_scr, se1_scr, se2_scr):
    q = _dot(xn_ref[...], wq_ref[...]).astype(BF16)
    hq = PEER_HEADS * PEER_DQ // 2
    s1 = _dot_nt(k1_ref[...], q[:, :hq])
    s2 = _dot_nt(k2_ref[...], q[:, hq:])
    kbits = PEER_TOPK.bit_length() - 1
    for lt in range(tb // LANES):
        lanes = slice(lt * LANES, (lt + 1) * LANES)
        s1_scr[...] = s1[:, lanes].reshape(PEER_NKEYS, SUBLANES, LANES)
        s2_scr[...] = s2[:, lanes].reshape(PEER_NKEYS, SUBLANES, LANES)
        _top16_of_keys((s1_scr, s2_scr), (gv1_scr, gv2_scr), (gi1_scr, gi2_scr), (v1_scr, v2_scr), (i1_scr, i2_scr))

        for a in range(PEER_TOPK):
            hv_scr[a] = v1_scr[a] + v2_scr[0]
            hb_scr[a] = jnp.zeros((SUBLANES, LANES), F32)

        def body(r, carry):
            m, a_sel = _tree([(hv_scr[a], float(a)) for a in range(PEER_TOPK)], _first_max)
            abits = _bits(a_sel, kbits)
            b_sel = _mux([hb_scr[a] for a in range(PEER_TOPK)], abits)
            bbits = _bits(b_sel, kbits)
            sc_scr[r] = m
            se1_scr[r] = _mux([i1_scr[a] for a in range(PEER_TOPK)], abits)
            se2_scr[r] = _mux([i2_scr[b] for b in range(PEER_TOPK)], bbits)
            nb = b_sel + 1.0
            v2_next = _mux([v2_scr[(b + 1) % PEER_TOPK] for b in range(PEER_TOPK)], bbits)
            v1_sel = _mux([v1_scr[a] for a in range(PEER_TOPK)], abits)
            live = (a_sel + 1.0) * (nb + 1.0) <= float(PEER_TOPK)
            nv = jnp.where(live, v1_sel + v2_next, -jnp.inf)
            for a in range(PEER_TOPK):
                hit = a_sel == float(a)
                hv_scr[a] = jnp.where(hit, nv, hv_scr[a])
                hb_scr[a] = jnp.where(hit, nb, hb_scr[a])
            return carry

        lax.fori_loop(0, PEER_TOPK, body, 0)
        sc = sc_scr[...]
        ex = jnp.exp(sc - jnp.max(sc, axis=0, keepdims=True))
        gate = ex / jnp.sum(ex, axis=0, keepdims=True)
        rows = slice(lt * LANES, (lt + 1) * LANES)
        g_ref[rows, :] = gate.reshape(PEER_SLOTS, LANES).T
        e1_ref[rows, :] = se1_scr[...].reshape(PEER_SLOTS, LANES).T
        e2_ref[rows, :] = se2_scr[...].reshape(PEER_SLOTS, LANES).T


def _peer_select(xn, wq, k1big, k2big):
    t = xn.shape[0]
    tb = 256
    hq = PEER_HEADS * PEER_DQ // 2
    nk = PEER_NKEYS * PEER_HEADS
    const = lambda i: (0, 0)
    row = lambda dt: jax.ShapeDtypeStruct((t, PEER_SLOTS), dt)
    vec = lambda n: pltpu.VMEM((n, SUBLANES, LANES), F32)
    return pl.pallas_call(
        functools.partial(_select_kernel, tb),
        grid=(t // tb,),
        in_specs=[pl.BlockSpec((tb, D_MODEL), lambda i: (i, 0)),
                  pl.BlockSpec((D_MODEL, 2 * hq), const),
                  pl.BlockSpec((nk, hq), const),
                  pl.BlockSpec((nk, hq), const)],
        out_specs=[pl.BlockSpec((tb, PEER_SLOTS), lambda i: (i, 0))] * 3,
        out_shape=[row(F32), row(F32), row(F32)],
        scratch_shapes=[vec(PEER_NKEYS), vec(PEER_NKEYS)] + [vec(PEER_NKEYS // TOPK_GROUP)] * 4
        + [vec(PEER_TOPK), vec(PEER_TOPK), vec(PEER_TOPK), vec(PEER_TOPK),
                        vec(PEER_TOPK), vec(PEER_TOPK), vec(PEER_TOPK), vec(PEER_TOPK), vec(PEER_TOPK)],
        compiler_params=_params(("parallel",), 40),
        name="peer_select",
    )(xn, wq, k1big, k2big)


def _peer_kernel(tb, eb, stride, final_norm, out_batch, xn_ref, e1_ref, e2_ref, g_ref, ut_ref, v_ref, x1_ref,
                 gain_ref, out_ref, m_scr, *acc_scr):
    e = pl.program_id(1)
    nk1 = eb // PEER_NKEYS
    acc_ref = acc_scr[0] if out_batch else out_ref

    @pl.when(e == 0)
    def _():
        acc_ref[...] = x1_ref[...]

    @pl.when(e == 0)
    def _():
        key = lax.broadcasted_iota(jnp.int32, (PEER_NKEYS, PEER_SLOTS), 0).astype(F32)

        def token(t, carry):
            e1 = e1_ref[pl.ds(t, 1), :]
            e2 = e2_ref[pl.ds(t, 1), :]
            gt = 0.5 * g_ref[pl.ds(t, 1), :]
            a_t = jnp.where(key == e1, gt, 0.0).astype(BF16)
            b_t = jnp.where(key == e2, 1.0, 0.0).astype(BF16)
            m_scr[pl.ds(t, PEER_NKEYS, stride=stride), :] = _dot_nt(a_t, b_t)
            return carry

        lax.fori_loop(0, tb, token, 0, unroll=128)

    s = _dot(xn_ref[...], ut_ref[...])
    t = jnp.tanh(s * (GELU_C + (GELU_C * GELU_A) * (s * s)))
    k1 = e * nk1
    gates = [m_scr[pl.ds(pl.multiple_of((k1 + i) * stride, SUBLANES), tb), :] for i in range(nk1)]
    w = ((s + s * t) * jnp.concatenate(gates, axis=1)).astype(BF16)
    acc_ref[...] += _dot(w, v_ref[...])

    if final_norm or out_batch:
        @pl.when(e == pl.num_programs(1) - 1)
        def _():
            y = acc_ref[...]
            if final_norm:
                y = _rms(y, gain_ref[...])
            out_ref[...] = pltpu.einshape("(tb)d->btd", y, b=out_batch) if out_batch else y


def _peer_dense(xn, e1, e2, g, ut, v, x1, layer, gain, final_norm, out_batch=0):
    t = xn.shape[0]
    tb, eb = 512, 1024
    stride = tb + SUBLANES
    once = pl.Buffered(1)
    tok = lambda w: pl.BlockSpec((tb, w), lambda i, e: (i, 0), pipeline_mode=once)
    tab = pl.BlockSpec((None, eb, D_MODEL), lambda i, e: (layer, e, 0))
    if out_batch:
        out_spec = pl.BlockSpec((out_batch, tb // out_batch, D_MODEL), lambda i, e: (0, i, 0))
        out_shape = jax.ShapeDtypeStruct((out_batch, t // out_batch, D_MODEL), F32)
        acc = [pltpu.VMEM((tb, D_MODEL), F32)]
    else:
        out_spec = pl.BlockSpec((tb, D_MODEL), lambda i, e: (i, 0))
        out_shape = jax.ShapeDtypeStruct((t, D_MODEL), F32)
        acc = []
    return pl.pallas_call(
        functools.partial(_peer_kernel, tb, eb, stride, final_norm, out_batch),
        grid=(t // tb, PEER_NEXP // eb),
        in_specs=[tok(D_MODEL), tok(PEER_SLOTS), tok(PEER_SLOTS), tok(PEER_SLOTS),
                  pl.BlockSpec((None, D_MODEL, eb), lambda i, e: (layer, 0, e)), tab, tok(D_MODEL),
                  pl.BlockSpec((1, D_MODEL), lambda i, e: (0, 0))],
        out_specs=out_spec,
        out_shape=out_shape,
        scratch_shapes=[pltpu.VMEM((PEER_NKEYS * stride, PEER_NKEYS), F32)] + acc,
        compiler_params=_params(("parallel", "arbitrary"), 60),
        name="peer_dense",
    )(xn, e1, e2, g, ut, v, x1, gain)


def _time_major(x):
    b, s, d = x.shape
    return x.transpose(1, 0, 2).reshape(s * b, d)


def _batch_major(y, b, s):
    return y.reshape(s, b, y.shape[-1]).transpose(1, 0, 2)


def kernel(x_prompt, x_sample, state_ret, state_ssm_re, state_ssm_im, state_conv, norm_mix, w_in, ret_norm, w_ret_out, ssm_a_re, ssm_a_im, ssm_b_re, ssm_b_im, ssm_c_re, ssm_c_im, ssm_d, ssm_log_dt, w_glu_a, w_glu_b, conv_w, conv_b, w_conv_out, w_mix_out, norm_ffn, peer_wq, peer_k1, peer_k2, peer_u, peer_v, norm_final):
    bp, sp, _ = x_prompt.shape
    bs, ss, _ = x_sample.shape
    tp, ts = bp * sp, bs * ss
    depth = w_in.shape[0]
    assert tp % ROWS == 0 and ts == ROWS and ROWS % bp == 0 and ROWS // bp == math.gcd(sp, RET_CHUNK)

    xs = [x_prompt, _time_major(x_sample)]
    batches = (bp, bs)
    pos = (jnp.arange(sp, dtype=F32), PAST_LEN + jnp.arange(ss, dtype=F32))
    ret_tabs = [_retention_tables(p, b) for p, b in zip(pos, batches)]

    lg = depth * SSM_G
    abre, abim, bbre, bbim = _s5_discretise(
        ssm_a_re.reshape(lg, SSM_P), ssm_a_im.reshape(lg, SSM_P), ssm_log_dt.reshape(lg, 1),
        ssm_b_re.transpose(0, 1, 3, 2).reshape(lg, SSM_GC, SSM_P),
        ssm_b_im.transpose(0, 1, 3, 2).reshape(lg, SSM_GC, SSM_P))

    hq = PEER_DQ // 2
    eye = jnp.eye(PEER_HEADS, dtype=F32)

    def keys_block_diag(k):
        return (k.transpose(1, 0, 2)[:, :, None, :] * eye[None, :, :, None]).reshape(
            PEER_NKEYS * PEER_HEADS, PEER_HEADS * hq).astype(BF16)

    w_in_b = w_in.astype(BF16)
    ut = peer_u.astype(BF16).transpose(0, 2, 1)
    vt = peer_v.astype(BF16)
    zero_ret = jnp.zeros((1, bp, RET_HEADS, RET_DK, RET_DK), F32)

    states = [[[] for _ in range(3)] for _ in range(2)]
    ret_stacked = [None, None]
    for l in range(depth):
        sl = slice(l * SSM_G, (l + 1) * SSM_G)
        bmat = jnp.concatenate([_block_diag_slabs(bbre[sl], SSM_SLABS), _block_diag_slabs(bbim[sl], SSM_SLABS)],
                               axis=2).astype(BF16)
        cmat = jnp.concatenate([_block_diag_slabs(ssm_c_re[l].transpose(0, 2, 1), SSM_SLABS),
                                _block_diag_slabs(-ssm_c_im[l].transpose(0, 2, 1), SSM_SLABS)],
                               axis=1).astype(BF16)
        are_row = abre[sl].reshape(1, SSM_N)
        aim_row = abim[sl].reshape(1, SSM_N)
        proj = [w.astype(BF16) for w in (w_ret_out[l], w_glu_a[l], w_glu_b[l], w_conv_out[l], w_mix_out[l])]
        wq = peer_wq[l].reshape(D_MODEL, PEER_HEADS, 2, hq).transpose(0, 2, 1, 3).reshape(D_MODEL, -1).astype(BF16)
        k1big, k2big = keys_block_diag(peer_k1[l]), keys_block_diag(peer_k2[l])

        for gi, batch in enumerate(batches):
            x = xs[gi]
            nblk = x.size // (ROWS * D_MODEL)
            if gi == 0:
                s0, s0_layer = zero_ret, 0
                h0re = jnp.zeros((batch, SSM_N), F32)
                h0im = jnp.zeros((batch, SSM_N), F32)
                buf0 = jnp.zeros(((CONV_K - 1) * batch, CONV_W), F32)
                bblk = batch
            else:
                s0, s0_layer = state_ret, l
                h0re = state_ssm_re[l].reshape(batch, SSM_N)
                h0im = state_ssm_im[l].reshape(batch, SSM_N)
                buf0 = state_conv[l].transpose(1, 0, 2).reshape((CONV_K - 1) * batch, CONV_W)
                bblk = 16
            z = _inproj(x, norm_mix[l][None, :], w_in_b, l)
            if x.ndim == 3:
                z, x = z
            oa, ret_stacked[gi] = _retention(z, nblk, batch, bblk, ret_tabs[gi], ret_norm[l][None, :], s0, s0_layer,
                                          ret_stacked[gi], l, depth)
            ys, xre, xim = _s5(z, nblk, batch, bmat, cmat, are_row, aim_row, ssm_d[l][None, :], h0re, h0im)
            st = states[gi]
            st[0].append(xre.reshape(batch, SSM_G, SSM_P))
            st[1].append(xim.reshape(batch, SSM_G, SSM_P))

            x1, xn, buf = _merge(x, z, oa, ys, batch, buf0, conv_w[l], conv_b[l][None, :], *proj, norm_ffn[l][None, :])
            st[2].append(buf.reshape(CONV_K - 1, batch, CONV_W).transpose(1, 0, 2))
            e1, e2, g = _peer_select(xn, wq, k1big, k2big)
            last = l == depth - 1
            xs[gi] = _peer_dense(xn, e1, e2, g, ut, vt, x1, l, norm_final[None, :], last,
                                 out_batch=bp if last and gi == 0 else 0)

    y_prompt = xs[0]
    y_sample = _batch_major(xs[1], bs, ss)
    (re_p, im_p, cv_p), (re_s, im_s, cv_s) = states
    return (y_prompt, y_sample,
            ret_stacked[0], ret_stacked[1],
            jnp.stack(re_p), jnp.stack(re_s),
            jnp.stack(im_p), jnp.stack(im_s),
            jnp.stack(cv_p), jnp.stack(cv_s))
```

```python
import functools
import math

import jax
import jax.numpy as jnp
from jax import lax
from jax.experimental import pallas as pl
from jax.experimental.pallas import tpu as pltpu

F32 = jnp.float32
BF16 = jnp.bfloat16

D_MODEL = 1024
PAST_LEN = 16384
RET_HEADS = 8
RET_DK = 64
RET_W = 512
RET_CHUNK = 128
ROPE_BASE = 10000.0
SSM_W = 512
SSM_GC = 16
SSM_G = 32
SSM_P = 64
SSM_N = SSM_G * SSM_P
SSM_SLABS = 4
SSM_SLAB_N = SSM_N // SSM_SLABS
CONV_W = 512
CONV_K = 3
PROJ_W = 7168
PEER_HEADS = 8
PEER_DQ = 256
PEER_NKEYS = 128
PEER_TOPK = 16
PEER_NEXP = PEER_NKEYS ** 2
PEER_SLOTS = PEER_HEADS * PEER_TOPK
TOPK_GROUP = 8
EPS = 1e-6
GELU_C = math.sqrt(2.0 / math.pi)
GELU_A = 0.044715

ROWS = 1024
LANES = 128
SUBLANES = 8
MIB = 1024 * 1024


def _params(sem, vmem_mib):
    return pltpu.CompilerParams(dimension_semantics=sem, vmem_limit_bytes=vmem_mib * MIB)


def _rms(x, g):
    return x * lax.rsqrt(jnp.mean(x * x, axis=-1, keepdims=True) + EPS) * g


def _dot(a, b):
    return jnp.dot(a, b, preferred_element_type=F32)


def _dot_nt(a, b):
    return lax.dot_general(a, b, (((1,), (1,)), ((), ())), preferred_element_type=F32)


def _inproj_kernel(x_ref, g_ref, w_ref, z_ref, h_scr):
    @pl.when(pl.program_id(1) == 0)
    def _():
        h_scr[...] = _rms(x_ref[...], g_ref[...]).astype(BF16)

    z_ref[...] = _dot(h_scr[...], w_ref[...])


def _inproj_bm_kernel(x_ref, g_ref, w_ref, z_ref, xtm_ref, h_scr):
    @pl.when(pl.program_id(1) == 0)
    def _():
        x = pltpu.einshape("btd->(tb)d", x_ref[...])
        xtm_ref[...] = x
        h_scr[...] = _rms(x, g_ref[...]).astype(BF16)

    z_ref[...] = _dot(h_scr[...], w_ref[...])


def _inproj(x, g, w, layer):
    batch_major = x.ndim == 3
    nb = 1024
    if batch_major:
        batch, seq, _ = x.shape
        t = batch * seq
        x_spec = pl.BlockSpec((batch, ROWS // batch, D_MODEL), lambda i, j: (0, i, 0))
    else:
        t = x.shape[0]
        x_spec = pl.BlockSpec((ROWS, D_MODEL), lambda i, j: (i, 0))
    z_spec = pl.BlockSpec((ROWS, nb), lambda i, j: (i, j))
    z_shape = jax.ShapeDtypeStruct((t, PROJ_W), F32)
    return pl.pallas_call(
        _inproj_bm_kernel if batch_major else _inproj_kernel,
        grid=(t // ROWS, PROJ_W // nb),
        in_specs=[x_spec,
                  pl.BlockSpec((1, D_MODEL), lambda i, j: (0, 0)),
                  pl.BlockSpec((None, D_MODEL, nb), lambda i, j: (layer, 0, j))],
        out_specs=[z_spec, pl.BlockSpec((ROWS, D_MODEL), lambda i, j: (i, 0))] if batch_major else z_spec,
        out_shape=[z_shape, jax.ShapeDtypeStruct((t, D_MODEL), F32)] if batch_major else z_shape,
        scratch_shapes=[pltpu.VMEM((ROWS, D_MODEL), BF16)],
        compiler_params=_params(("parallel", "arbitrary"), 48),
        name="inproj",
    )(x, g, w)


def _ret_kernel(batch, bblk, aliased, out_layer, *refs):
    refs = [r for i, r in enumerate(refs) if not (aliased and i == 9)]
    (z_ref, cos_ref, sa_ref, sb_ref, qdec_ref, kdec_ref, cdec_ref, gn_ref, s0_ref,
     o_ref, st_ref, qd_scr, kd_scr, v_scr, mask_scr, oacc_scr) = refs
    s_ref = st_ref if aliased else st_ref.at[out_layer]
    bb = pl.program_id(0)
    c = pl.program_id(1)
    steps = ROWS // batch
    nslab = RET_W // LANES

    def head_view(ref, h):
        return ref[h // 2, :, (h % 2) * RET_DK:(h % 2 + 1) * RET_DK]

    seq_local = steps >= LANES
    msize = steps if seq_local else ROWS

    @pl.when((bb == 0) & (c == 0))
    def _():
        r = lax.broadcasted_iota(jnp.int32, (msize, msize), 0)
        cc = lax.broadcasted_iota(jnp.int32, (msize, msize), 1)
        if seq_local:
            mask_scr[...] = (r >= cc).astype(F32)
        else:
            same = (r & (batch - 1)) == (cc & (batch - 1))
            mask_scr[...] = (same & (r >= cc)).astype(F32)

    @pl.when(c == 0)
    def _():
        s_ref[...] = s0_ref[...]
        if not aliased:
            for k in range(st_ref.shape[0]):
                if k != out_layer:
                    st_ref[k] = jnp.zeros(st_ref.shape[1:], F32)

    @pl.when(bb == 0)
    def _():
        def per_row(t_ref):
            return jnp.broadcast_to(t_ref[...][:, None, :], (steps, batch, LANES)).reshape(ROWS, LANES)

        cos, sa, sb = per_row(cos_ref), per_row(sa_ref), per_row(sb_ref)

        def rot(x):
            return x * cos + pltpu.roll(x, 32, 1) * sa + pltpu.roll(x, 96, 1) * sb

        for s in range(nslab):
            cols = slice(s * LANES, (s + 1) * LANES)
            qd_scr[s] = rot(z_ref[:, cols]) * qdec_ref[:, cols]
            kd_scr[s] = rot(z_ref[:, RET_W + s * LANES:RET_W + (s + 1) * LANES]) * kdec_ref[:, cols]
            v_scr[s] = z_ref[:, 2 * RET_W + s * LANES:2 * RET_W + (s + 1) * LANES]
        if not seq_local:
            for s in range(nslab):
                outs = []
                for h in (2 * s, 2 * s + 1):
                    qh = head_view(qd_scr, h).astype(BF16)
                    kh = head_view(kd_scr, h).astype(BF16)
                    vh = head_view(v_scr, h).astype(BF16)
                    p = (_dot_nt(qh, kh) * mask_scr[...]).astype(BF16)
                    outs.append(_dot(p, vh))
                oacc_scr[s] = jnp.concatenate(outs, axis=1)

    def per_seq(bl, carry):
        b = bb * bblk + bl
        rows = pl.ds(b, steps, stride=batch)
        for s in range(nslab):
            qb = qd_scr[s, rows, :]
            kb = kd_scr[s, rows, :]
            vb = v_scr[s, rows, :]
            outs = []
            for hh in range(2):
                h = 2 * s + hh
                hc = slice(hh * RET_DK, (hh + 1) * RET_DK)
                q16, k16, v16 = qb[:, hc].astype(BF16), kb[:, hc].astype(BF16), vb[:, hc].astype(BF16)
                st = s_ref[bl, h]
                o = _dot(q16, st.astype(BF16))
                if seq_local:
                    p = (_dot_nt(q16, k16) * mask_scr[...]).astype(BF16)
                    o = _dot(p, v16) + o
                outs.append(o)
                upd = lax.dot_general(k16, v16, (((0,), (0,)), ((), ())), preferred_element_type=F32)
                s_ref[bl, h] = (st + upd) * cdec_ref[:, h * RET_DK:(h + 1) * RET_DK]
            o2 = jnp.concatenate(outs, axis=1)
            oacc_scr[s, rows, :] = o2 if seq_local else oacc_scr[s, rows, :] + o2
        return carry

    lax.fori_loop(0, bblk, per_seq, 0, unroll=4)

    @pl.when(bb == pl.num_programs(0) - 1)
    def _():
        r = lax.broadcasted_iota(jnp.int32, (LANES, LANES), 0) // RET_DK
        cc = lax.broadcasted_iota(jnp.int32, (LANES, LANES), 1) // RET_DK
        avg = jnp.where(r == cc, 1.0 / RET_DK, 0.0).astype(BF16)

        def seg_mean(x):
            hi = x.astype(BF16)
            lo = (x - hi.astype(F32)).astype(BF16)
            return _dot(hi, avg) + _dot(lo, avg)

        normed = []
        for s in range(nslab):
            o2 = oacc_scr[s]
            dlt = o2 - seg_mean(o2)
            normed.append(dlt * lax.rsqrt(seg_mean(dlt * dlt) + EPS))
        o = jnp.concatenate(normed, axis=1) * gn_ref[...]
        o_ref[...] = jax.nn.silu(z_ref[:, 3 * RET_W:4 * RET_W]) * o


def _retention(z, nblk, batch, bblk, tabs, gn, s0, layer, stacked, out_layer, depth):
    cos, sa, sb, qdec, kdec, cdec = tabs
    nbb = batch // bblk
    steps = ROWS // batch
    msize = steps if steps >= LANES else ROWS
    aliased = stacked is not None
    if aliased:
        st_spec = pl.BlockSpec((None, bblk, RET_HEADS, RET_DK, RET_DK), lambda bb, c: (out_layer, bb, 0, 0, 0))
    else:
        st_spec = pl.BlockSpec((depth, bblk, RET_HEADS, RET_DK, RET_DK), lambda bb, c: (0, bb, 0, 0, 0))
    st_in = pl.BlockSpec((None, bblk, RET_HEADS, RET_DK, RET_DK), lambda bb, c: (layer, bb, 0, 0, 0))
    const = lambda bb, c: (0, 0)
    return pl.pallas_call(
        functools.partial(_ret_kernel, batch, bblk, aliased, out_layer),
        grid=(nbb, nblk),
        in_specs=[pl.BlockSpec((ROWS, 4 * RET_W), lambda bb, c: (c, 0)),
                  pl.BlockSpec((steps, LANES), lambda bb, c: (c, 0)),
                  pl.BlockSpec((steps, LANES), lambda bb, c: (c, 0)),
                  pl.BlockSpec((steps, LANES), lambda bb, c: (c, 0)),
                  pl.BlockSpec((ROWS, RET_W), const),
                  pl.BlockSpec((ROWS, RET_W), const),
                  pl.BlockSpec((1, RET_W), const),
                  pl.BlockSpec((1, RET_W), const),
                  st_in] + ([pl.BlockSpec(memory_space=pl.ANY)] if aliased else []),
        out_specs=[pl.BlockSpec((ROWS, RET_W), lambda bb, c: (c, 0)), st_spec],
        out_shape=[jax.ShapeDtypeStruct((nblk * ROWS, RET_W), F32),
                   jax.ShapeDtypeStruct((depth, batch, RET_HEADS, RET_DK, RET_DK), F32)],
        input_output_aliases={9: 1} if aliased else {},
        scratch_shapes=[pltpu.VMEM((RET_W // LANES, ROWS, LANES), F32)] * 3
        + [pltpu.VMEM((msize, msize), F32), pltpu.VMEM((RET_W // LANES, ROWS, LANES), F32)],
        compiler_params=_params(("arbitrary", "arbitrary"), 56),
        name="retention",
    )(z, cos, sa, sb, qdec, kdec, cdec, gn, s0, *([stacked] if aliased else []))


def _retention_tables(pos, batch):
    half = RET_DK // 2
    lane = jnp.arange(LANES)
    upper = (lane % RET_DK) >= half
    freqs = (ROPE_BASE ** (-jnp.arange(half, dtype=F32) / half))[lane % half]
    ang = pos[:, None] * freqs[None, :]
    cos_t, sin = jnp.cos(ang), jnp.sin(ang)
    sa_t = jnp.where(upper[None, :], sin, 0.0)
    sb_t = jnp.where(upper[None, :], 0.0, -sin)
    lg = jnp.repeat(jnp.log1p(-jnp.exp2(-5.0 - jnp.arange(RET_HEADS, dtype=F32))), RET_DK)
    steps = ROWS // batch
    i1 = (jnp.arange(ROWS) // batch).astype(F32) + 1.0
    qdec = jnp.exp(i1[:, None] * lg[None, :])
    kdec = jnp.exp(-i1[:, None] * lg[None, :]) * (RET_DK ** -0.5)
    cdec = jnp.exp(steps * lg)[None, :]
    return cos_t, sa_t, sb_t, qdec, kdec, cdec


def _s5_disc_kernel(are_ref, aim_ref, ldt_ref, bre_ref, bim_ref, abre_ref, abim_ref, bbre_ref, bbim_ref):
    ar, ai = are_ref[...], aim_ref[...]
    dt = jnp.exp(ldt_ref[...])
    dar, dai = dt * ar, dt * ai
    mag = jnp.exp(dar)
    abar_re, abar_im = mag * jnp.cos(dai), mag * jnp.sin(dai)
    den = ar * ar + ai * ai
    nr, ni = abar_re - 1.0, abar_im
    f_re = (nr * ar + ni * ai) / den
    f_im = (ni * ar - nr * ai) / den
    abre_ref[...] = abar_re
    abim_ref[...] = abar_im
    br, bi = bre_ref[...], bim_ref[...]
    bbre_ref[...] = f_re[:, None, :] * br - f_im[:, None, :] * bi
    bbim_ref[...] = f_re[:, None, :] * bi + f_im[:, None, :] * br


def _s5_discretise(a_re, a_im, log_dt, b_re_t, b_im_t):
    lg = a_re.shape[0]
    small = jax.ShapeDtypeStruct((lg, SSM_P), F32)
    big = jax.ShapeDtypeStruct((lg, SSM_GC, SSM_P), F32)
    return pl.pallas_call(_s5_disc_kernel, out_shape=[small, small, big, big], name="s5_disc")(
        a_re, a_im, log_dt, b_re_t, b_im_t)


def _s5_kernel(batch, u_ref, bmat_ref, cmat_ref, are_ref, aim_ref, d_ref, h0re_ref, h0im_ref,
               y_ref, xre_ref, xim_ref, x_scr):
    c = pl.program_id(0)
    steps = ROWS // batch
    half = SSM_SLAB_N

    @pl.when(c == 0)
    def _():
        xre_ref[...] = h0re_ref[...]
        xim_ref[...] = h0im_ref[...]

    u = u_ref[...]
    ub = u.astype(BF16)
    for s in range(SSM_SLABS):
        x_scr[:, 2 * half * s:2 * half * (s + 1)] = _dot(ub[:, s * LANES:(s + 1) * LANES], bmat_ref[s])

    for s in range(SSM_SLABS):
        re0 = 2 * half * s
        im0 = re0 + half
        sc = slice(half * s, half * (s + 1))
        ar = jnp.broadcast_to(are_ref[:, sc], (SUBLANES, half))
        ai = jnp.broadcast_to(aim_ref[:, sc], (SUBLANES, half))

        def row_tile(rt, carry, re0=re0, im0=im0, sc=sc, ar=ar, ai=ai):
            r0 = pl.multiple_of(rt * SUBLANES, SUBLANES)

            def step(t, x):
                xr, xi = x
                row = pl.multiple_of(t * batch + r0, SUBLANES)
                nr = ar * xr - ai * xi + x_scr[pl.ds(row, SUBLANES), re0:re0 + half]
                ni = ar * xi + ai * xr + x_scr[pl.ds(row, SUBLANES), im0:im0 + half]
                x_scr[pl.ds(row, SUBLANES), re0:re0 + half] = nr
                x_scr[pl.ds(row, SUBLANES), im0:im0 + half] = ni
                return nr, ni

            init = (xre_ref[pl.ds(r0, SUBLANES), sc], xim_ref[pl.ds(r0, SUBLANES), sc])
            xr, xi = lax.fori_loop(0, steps, step, init, unroll=8)
            xre_ref[pl.ds(r0, SUBLANES), sc] = xr
            xim_ref[pl.ds(r0, SUBLANES), sc] = xi
            return carry

        lax.fori_loop(0, batch // SUBLANES, row_tile, 0)

    ys = [_dot(x_scr[:, 2 * half * s:2 * half * (s + 1)].astype(BF16), cmat_ref[s]) for s in range(SSM_SLABS)]
    y = jnp.concatenate(ys, axis=1) + d_ref[...] * u
    y_ref[...] = jax.nn.gelu(y)


def _s5(z, nblk, batch, bmat, cmat, abre, abim, d, h0re, h0im):
    const2 = lambda c: (0, 0)
    const3 = lambda c: (0, 0, 0)
    st = pl.BlockSpec((batch, SSM_N), const2)
    return pl.pallas_call(
        functools.partial(_s5_kernel, batch),
        grid=(nblk,),
        in_specs=[pl.BlockSpec((ROWS, SSM_W), lambda c: (c, 4)),
                  pl.BlockSpec((SSM_SLABS, LANES, 2 * SSM_SLAB_N), const3),
                  pl.BlockSpec((SSM_SLABS, 2 * SSM_SLAB_N, LANES), const3),
                  pl.BlockSpec((1, SSM_N), const2),
                  pl.BlockSpec((1, SSM_N), const2),
                  pl.BlockSpec((1, SSM_W), const2),
                  st, st],
        out_specs=[pl.BlockSpec((ROWS, SSM_W), lambda c: (c, 0)), st, st],
        out_shape=[jax.ShapeDtypeStruct((nblk * ROWS, SSM_W), F32),
                   jax.ShapeDtypeStruct((batch, SSM_N), F32),
                   jax.ShapeDtypeStruct((batch, SSM_N), F32)],
        scratch_shapes=[pltpu.VMEM((ROWS, 2 * SSM_N), F32)],
        compiler_params=_params(("arbitrary",), 48),
        name="s5",
    )(z, bmat, cmat, abre, abim, d, h0re, h0im)


def _block_diag_slabs(w, nslab):
    gps = SSM_G // nslab
    eye = jnp.eye(gps, dtype=w.dtype)
    w4 = w.reshape(nslab, gps, w.shape[1], w.shape[2])
    out = w4[:, :, :, None, :] * eye[None, :, None, :, None]
    return out.reshape(nslab, gps * w.shape[1], gps * w.shape[2])


def _merge_kernel(batch, x_ref, oa_ref, ys_ref, bg_ref, cg_ref, hc_ref, buf0_ref, cw_ref, cb_ref,
                  ga_ref, gb_ref, gc_ref, wr_ref, wa_ref, wb_ref, wc_ref, wm_ref, gf_ref,
                  x1_ref, xn_ref, buf_ref, zp_scr):
    rb = x_ref.shape[0]
    pad = (CONV_K - 1) * batch

    @pl.when(pl.program_id(0) == 0)
    def _():
        zp_scr[0:pad, :] = buf0_ref[...]

    zc = cg_ref[...] * hc_ref[...]
    zp_scr[pad:pad + rb, :] = zc
    y = cb_ref[...]
    for j in range(CONV_K):
        y = y + cw_ref[j:j + 1, :] * zp_scr[j * batch:j * batch + rb, :]
    oc_pre = bg_ref[...] * y
    tail = zp_scr[rb:rb + pad, :]
    buf_ref[...] = tail
    zp_scr[0:pad, :] = tail

    oa = _dot(oa_ref[...].astype(BF16), wr_ref[...])
    ysb = ys_ref[...].astype(BF16)
    ob = _dot(ysb, wa_ref[...]) * jax.nn.sigmoid(_dot(ysb, wb_ref[...]))
    oc = _dot(oc_pre.astype(BF16), wc_ref[...])
    merged = (jax.nn.sigmoid(ga_ref[...]) * oa + jax.nn.sigmoid(gb_ref[...]) * ob
              + jax.nn.sigmoid(gc_ref[...]) * oc)
    x1 = x_ref[...] + _dot(merged.astype(BF16), wm_ref[...])
    x1_ref[...] = x1
    xn_ref[...] = _rms(x1, gf_ref[...]).astype(BF16)


def _merge(x, z, oa, ys, batch, buf0, conv_w, conv_b, wr, wa, wb, wc, wm, gf):
    t = x.shape[0]
    rb = 512
    pad = (CONV_K - 1) * batch
    assert pad <= rb
    row = lambda w: pl.BlockSpec((rb, w), lambda i: (i, 0))
    zcol = lambda w, j: pl.BlockSpec((rb, w), lambda i: (i, j))
    const = lambda r, w: pl.BlockSpec((r, w), lambda i: (0, 0))
    return pl.pallas_call(
        functools.partial(_merge_kernel, batch),
        grid=(t // rb,),
        in_specs=[row(D_MODEL), row(RET_W), row(SSM_W),
                  zcol(CONV_W, 5), zcol(CONV_W, 6), zcol(CONV_W, 7),
                  const(pad, CONV_W), const(CONV_K, CONV_W), const(1, CONV_W),
                  zcol(D_MODEL, 4), zcol(D_MODEL, 5), zcol(D_MODEL, 6),
                  const(RET_W, D_MODEL), const(SSM_W, D_MODEL), const(SSM_W, D_MODEL), const(CONV_W, D_MODEL),
                  const(D_MODEL, D_MODEL), const(1, D_MODEL)],
        out_specs=[row(D_MODEL), row(D_MODEL), const(pad, CONV_W)],
        out_shape=[jax.ShapeDtypeStruct((t, D_MODEL), F32), jax.ShapeDtypeStruct((t, D_MODEL), BF16),
                   jax.ShapeDtypeStruct((pad, CONV_W), F32)],
        scratch_shapes=[pltpu.VMEM((rb + pad, CONV_W), F32)],
        compiler_params=_params(("arbitrary",), 48),
        name="merge",
    )(x, oa, ys, z, z, z, buf0, conv_w, conv_b, z, z, z, wr, wa, wb, wc, wm, gf)


def _tree(items, combine):
    while len(items) > 1:
        nxt = [combine(items[i], items[i + 1]) for i in range(0, len(items) - 1, 2)]
        if len(items) % 2:
            nxt.append(items[-1])
        items = nxt
    return items[0]


def _first_max(x, y):
    (vx, ix), (vy, iy) = x, y
    return jnp.maximum(vx, vy), jnp.where(vx >= vy, ix, iy)


def _bits(x, n):
    out, rest = [], x
    for _ in range(n):
        half = jnp.floor(rest * 0.5)
        out.append(rest - 2.0 * half == 1.0)
        rest = half
    return out


def _mux(vals, bits):
    level = list(vals)
    for bit in bits:
        level = [jnp.where(bit, level[j + 1], level[j]) for j in range(0, len(level), 2)]
    return level[0]


def _top16_of_keys(s_scrs, gv_scrs, gi_scrs, v_scrs, i_scrs):
    grp = TOPK_GROUP
    ngrp = PEER_NKEYS // grp
    nbits = ngrp.bit_length() - 1

    for s_scr, gv, gi in zip(s_scrs, gv_scrs, gi_scrs):
        for g in range(ngrp):
            gv[g], gi[g] = _tree([(s_scr[g * grp + p], float(g * grp + p)) for p in range(grp)], _first_max)

    def body(r, carry):
        for s_scr, gv, gi, v_scr, i_scr in zip(s_scrs, gv_scrs, gi_scrs, v_scrs, i_scrs):
            m, idx = _tree([(gv[g], gi[g]) for g in range(ngrp)], _first_max)
            v_scr[r] = m
            i_scr[r] = idx
            gid = jnp.floor(idx * (1.0 / grp))
            rel = idx - gid * grp
            bits = _bits(gid, nbits)
            cands = []
            for p in range(grp):
                val = _mux([s_scr[g * grp + p] for g in range(ngrp)], bits)
                left = (val < m) | ((val == m) & (rel < float(p)))
                cands.append((jnp.where(left, val, -jnp.inf), float(p)))
            nv, npos = _tree(cands, _first_max)
            ni = gid * grp + npos
            for g in range(ngrp):
                hit = gid == float(g)
                gv[g] = jnp.where(hit, nv, gv[g])
                gi[g] = jnp.where(hit, ni, gi[g])
        return carry

    lax.fori_loop(0, PEER_TOPK, body, 0)


def _select_kernel(tb, xn_ref, wq_ref, k1_ref, k2_ref, e1_ref, e2_ref, g_ref,
                   s1_scr, s2_scr, gv1_scr, gi1_scr, gv2_scr, gi2_scr, v1_scr, i1_scr, v2_scr, i2_scr,
                   hv_scr, hb_scr, sc_scr, se1_scr, se2_scr):
    q = _dot(xn_ref[...], wq_ref[...]).astype(BF16)
    hq = PEER_HEADS * PEER_DQ // 2
    s1 = _dot_nt(k1_ref[...], q[:, :hq])
    s2 = _dot_nt(k2_ref[...], q[:, hq:])
    kbits = PEER_TOPK.bit_length() - 1
    for lt in range(tb // LANES):
        lanes = slice(lt * LANES, (lt + 1) * LANES)
        s1_scr[...] = s1[:, lanes].reshape(PEER_NKEYS, SUBLANES, LANES)
        s2_scr[...] = s2[:, lanes].reshape(PEER_NKEYS, SUBLANES, LANES)
        _top16_of_keys((s1_scr, s2_scr), (gv1_scr, gv2_scr), (gi1_scr, gi2_scr), (v1_scr, v2_scr), (i1_scr, i2_scr))

        for a in range(PEER_TOPK):
            hv_scr[a] = v1_scr[a] + v2_scr[0]
            hb_scr[a] = jnp.zeros((SUBLANES, LANES), F32)

        def body(r, carry):
            m, a_sel = _tree([(hv_scr[a], float(a)) for a in range(PEER_TOPK)], _first_max)
            abits = _bits(a_sel, kbits)
            b_sel = _mux([hb_scr[a] for a in range(PEER_TOPK)], abits)
            bbits = _bits(b_sel, kbits)
            sc_scr[r] = m
            se1_scr[r] = _mux([i1_scr[a] for a in range(PEER_TOPK)], abits)
            se2_scr[r] = _mux([i2_scr[b] for b in range(PEER_TOPK)], bbits)
            nb = b_sel + 1.0
            v2_next = _mux([v2_scr[(b + 1) % PEER_TOPK] for b in range(PEER_TOPK)], bbits)
            v1_sel = _mux([v1_scr[a] for a in range(PEER_TOPK)], abits)
            live = (a_sel + 1.0) * (nb + 1.0) <= float(PEER_TOPK)
            nv = jnp.where(live, v1_sel + v2_next, -jnp.inf)
            for a in range(PEER_TOPK):
                hit = a_sel == float(a)
                hv_scr[a] = jnp.where(hit, nv, hv_scr[a])
                hb_scr[a] = jnp.where(hit, nb, hb_scr[a])
            return carry

        lax.fori_loop(0, PEER_TOPK, body, 0)
        sc = sc_scr[...]
        ex = jnp.exp(sc - jnp.max(sc, axis=0, keepdims=True))
        gate = ex / jnp.sum(ex, axis=0, keepdims=True)
        rows = slice(lt * LANES, (lt + 1) * LANES)
        g_ref[rows, :] = gate.reshape(PEER_SLOTS, LANES).T
        e1_ref[rows, :] = se1_scr[...].reshape(PEER_SLOTS, LANES).T
        e2_ref[rows, :] = se2_scr[...].reshape(PEER_SLOTS, LANES).T


def _peer_select(xn, wq, k1big, k2big):
    t = xn.shape[0]
    tb = 256
    hq = PEER_HEADS * PEER_DQ // 2
    nk = PEER_NKEYS * PEER_HEADS
    const = lambda i: (0, 0)
    row = lambda dt: jax.ShapeDtypeStruct((t, PEER_SLOTS), dt)
    vec = lambda n: pltpu.VMEM((n, SUBLANES, LANES), F32)
    return pl.pallas_call(
        functools.partial(_select_kernel, tb),
        grid=(t // tb,),
        in_specs=[pl.BlockSpec((tb, D_MODEL), lambda i: (i, 0)),
                  pl.BlockSpec((D_MODEL, 2 * hq), const),
                  pl.BlockSpec((nk, hq), const),
                  pl.BlockSpec((nk, hq), const)],
        out_specs=[pl.BlockSpec((tb, PEER_SLOTS), lambda i: (i, 0))] * 3,
        out_shape=[row(F32), row(F32), row(F32)],
        scratch_shapes=[vec(PEER_NKEYS), vec(PEER_NKEYS)] + [vec(PEER_NKEYS // TOPK_GROUP)] * 4
        + [vec(PEER_TOPK), vec(PEER_TOPK), vec(PEER_TOPK), vec(PEER_TOPK),
                        vec(PEER_TOPK), vec(PEER_TOPK), vec(PEER_TOPK), vec(PEER_TOPK), vec(PEER_TOPK)],
        compiler_params=_params(("parallel",), 40),
        name="peer_select",
    )(xn, wq, k1big, k2big)


def _peer_kernel(tb, eb, stride, final_norm, out_batch, xn_ref, e1_ref, e2_ref, g_ref, ut_ref, v_ref, x1_ref,
                 gain_ref, out_ref, m_scr, *acc_scr):
    e = pl.program_id(1)
    nk1 = eb // PEER_NKEYS
    acc_ref = acc_scr[0] if out_batch else out_ref

    @pl.when(e == 0)
    def _():
        acc_ref[...] = x1_ref[...]

    @pl.when(e == 0)
    def _():
        key = lax.broadcasted_iota(jnp.int32, (PEER_NKEYS, PEER_SLOTS), 0).astype(F32)

        def token(t, carry):
            e1 = e1_ref[pl.ds(t, 1), :]
            e2 = e2_ref[pl.ds(t, 1), :]
            gt = 0.5 * g_ref[pl.ds(t, 1), :]
            a_t = jnp.where(key == e1, gt, 0.0).astype(BF16)
            b_t = jnp.where(key == e2, 1.0, 0.0).astype(BF16)
            m_scr[pl.ds(t, PEER_NKEYS, stride=stride), :] = _dot_nt(a_t, b_t)
            return carry

        lax.fori_loop(0, tb, token, 0, unroll=128)

    s = _dot(xn_ref[...], ut_ref[...])
    t = jnp.tanh(s * (GELU_C + (GELU_C * GELU_A) * (s * s)))
    k1 = e * nk1
    gates = [m_scr[pl.ds(pl.multiple_of((k1 + i) * stride, SUBLANES), tb), :] for i in range(nk1)]
    w = ((s + s * t) * jnp.concatenate(gates, axis=1)).astype(BF16)
    acc_ref[...] += _dot(w, v_ref[...])

    if final_norm or out_batch:
        @pl.when(e == pl.num_programs(1) - 1)
        def _():
            y = acc_ref[...]
            if final_norm:
                y = _rms(y, gain_ref[...])
            out_ref[...] = pltpu.einshape("(tb)d->btd", y, b=out_batch) if out_batch else y


def _peer_dense(xn, e1, e2, g, ut, v, x1, layer, gain, final_norm, out_batch=0):
    t = xn.shape[0]
    tb, eb = 512, 1024
    stride = tb + SUBLANES
    once = pl.Buffered(1)
    tok = lambda w: pl.BlockSpec((tb, w), lambda i, e: (i, 0), pipeline_mode=once)
    tab = pl.BlockSpec((None, eb, D_MODEL), lambda i, e: (layer, e, 0))
    if out_batch:
        out_spec = pl.BlockSpec((out_batch, tb // out_batch, D_MODEL), lambda i, e: (0, i, 0))
        out_shape = jax.ShapeDtypeStruct((out_batch, t // out_batch, D_MODEL), F32)
        acc = [pltpu.VMEM((tb, D_MODEL), F32)]
    else:
        out_spec = pl.BlockSpec((tb, D_MODEL), lambda i, e: (i, 0))
        out_shape = jax.ShapeDtypeStruct((t, D_MODEL), F32)
        acc = []
    return pl.pallas_call(
        functools.partial(_peer_kernel, tb, eb, stride, final_norm, out_batch),
        grid=(t // tb, PEER_NEXP // eb),
        in_specs=[tok(D_MODEL), tok(PEER_SLOTS), tok(PEER_SLOTS), tok(PEER_SLOTS),
                  pl.BlockSpec((None, D_MODEL, eb), lambda i, e: (layer, 0, e)), tab, tok(D_MODEL),
                  pl.BlockSpec((1, D_MODEL), lambda i, e: (0, 0))],
        out_specs=out_spec,
        out_shape=out_shape,
        scratch_shapes=[pltpu.VMEM((PEER_NKEYS * stride, PEER_NKEYS), F32)] + acc,
        compiler_params=_params(("parallel", "arbitrary"), 60),
        name="peer_dense",
    )(xn, e1, e2, g, ut, v, x1, gain)


def _time_major(x):
    b, s, d = x.shape
    return x.transpose(1, 0, 2).reshape(s * b, d)


def _batch_major(y, b, s):
    return y.reshape(s, b, y.shape[-1]).transpose(1, 0, 2)


def kernel(x_prompt, x_sample, state_ret, state_ssm_re, state_ssm_im, state_conv, norm_mix, w_in, ret_norm, w_ret_out, ssm_a_re, ssm_a_im, ssm_b_re, ssm_b_im, ssm_c_re, ssm_c_im, ssm_d, ssm_log_dt, w_glu_a, w_glu_b, conv_w, conv_b, w_conv_out, w_mix_out, norm_ffn, peer_wq, peer_k1, peer_k2, peer_u, peer_v, norm_final):
    bp, sp, _ = x_prompt.shape
    bs, ss, _ = x_sample.shape
    tp, ts = bp * sp, bs * ss
    depth = w_in.shape[0]
    assert tp % ROWS == 0 and ts == ROWS and ROWS % bp == 0 and ROWS // bp == math.gcd(sp, RET_CHUNK)

    xs = [x_prompt, _time_major(x_sample)]
    batches = (bp, bs)
    pos = (jnp.arange(sp, dtype=F32), PAST_LEN + jnp.arange(ss, dtype=F32))
    ret_tabs = [_retention_tables(p, b) for p, b in zip(pos, batches)]

    lg = depth * SSM_G
    abre, abim, bbre, bbim = _s5_discretise(
        ssm_a_re.reshape(lg, SSM_P), ssm_a_im.reshape(lg, SSM_P), ssm_log_dt.reshape(lg, 1),
        ssm_b_re.transpose(0, 1, 3, 2).reshape(lg, SSM_GC, SSM_P),
        ssm_b_im.transpose(0, 1, 3, 2).reshape(lg, SSM_GC, SSM_P))

    hq = PEER_DQ // 2
    eye = jnp.eye(PEER_HEADS, dtype=F32)

    def keys_block_diag(k):
        return (k.transpose(1, 0, 2)[:, :, None, :] * eye[None, :, :, None]).reshape(
            PEER_NKEYS * PEER_HEADS, PEER_HEADS * hq).astype(BF16)

    w_in_b = w_in.astype(BF16)
    ut = peer_u.astype(BF16).transpose(0, 2, 1)
    vt = peer_v.astype(BF16)
    zero_ret = jnp.zeros((1, bp, RET_HEADS, RET_DK, RET_DK), F32)

    states = [[[] for _ in range(3)] for _ in range(2)]
    ret_stacked = [None, None]
    for l in range(depth):
        sl = slice(l * SSM_G, (l + 1) * SSM_G)
        bmat = jnp.concatenate([_block_diag_slabs(bbre[sl], SSM_SLABS), _block_diag_slabs(bbim[sl], SSM_SLABS)],
                               axis=2).astype(BF16)
        cmat = jnp.concatenate([_block_diag_slabs(ssm_c_re[l].transpose(0, 2, 1), SSM_SLABS),
                                _block_diag_slabs(-ssm_c_im[l].transpose(0, 2, 1), SSM_SLABS)],
                               axis=1).astype(BF16)
        are_row = abre[sl].reshape(1, SSM_N)
        aim_row = abim[sl].reshape(1, SSM_N)
        proj = [w.astype(BF16) for w in (w_ret_out[l], w_glu_a[l], w_glu_b[l], w_conv_out[l], w_mix_out[l])]
        wq = peer_wq[l].reshape(D_MODEL, PEER_HEADS, 2, hq).transpose(0, 2, 1, 3).reshape(D_MODEL, -1).astype(BF16)
        k1big, k2big = keys_block_diag(peer_k1[l]), keys_block_diag(peer_k2[l])

        for gi, batch in enumerate(batches):
            x = xs[gi]
            nblk = x.size // (ROWS * D_MODEL)
            if gi == 0:
                s0, s0_layer = zero_ret, 0
                h0re = jnp.zeros((batch, SSM_N), F32)
                h0im = jnp.zeros((batch, SSM_N), F32)
                buf0 = jnp.zeros(((CONV_K - 1) * batch, CONV_W), F32)
                bblk = batch
            else:
                s0, s0_layer = state_ret, l
                h0re = state_ssm_re[l].reshape(batch, SSM_N)
                h0im = state_ssm_im[l].reshape(batch, SSM_N)
                buf0 = state_conv[l].transpose(1, 0, 2).reshape((CONV_K - 1) * batch, CONV_W)
                bblk = 16
            z = _inproj(x, norm_mix[l][None, :], w_in_b, l)
            if x.ndim == 3:
                z, x = z
            oa, ret_stacked[gi] = _retention(z, nblk, batch, bblk, ret_tabs[gi], ret_norm[l][None, :], s0, s0_layer,
                                          ret_stacked[gi], l, depth)
            ys, xre, xim = _s5(z, nblk, batch, bmat, cmat, are_row, aim_row, ssm_d[l][None, :], h0re, h0im)
            st = states[gi]
            st[0].append(xre.reshape(batch, SSM_G, SSM_P))
            st[1].append(xim.reshape(batch, SSM_G, SSM_P))

            x1, xn, buf = _merge(x, z, oa, ys, batch, buf0, conv_w[l], conv_b[l][None, :], *proj, norm_ffn[l][None, :])
            st[2].append(buf.reshape(CONV_K - 1, batch, CONV_W).transpose(1, 0, 2))
            e1, e2, g = _peer_select(xn, wq, k1big, k2big)
            last = l == depth - 1
            xs[gi] = _peer_dense(xn, e1, e2, g, ut, vt, x1, l, norm_final[None, :], last,
                                 out_batch=bp if last and gi == 0 else 0)

    y_prompt = xs[0]
    y_sample = _batch_major(xs[1], bs, ss)
    (re_p, im_p, cv_p), (re_s, im_s, cv_s) = states
    return (y_prompt, y_sample,
            ret_stacked[0], ret_stacked[1],
            jnp.stack(re_p), jnp.stack(re_s),
            jnp.stack(im_p), jnp.stack(im_s),
            jnp.stack(cv_p), jnp.stack(cv_s))
```

```python
import functools
import math

import jax
import jax.numpy as jnp
from jax import lax
from jax.experimental import pallas as pl
from jax.experimental.pallas import tpu as pltpu

F32 = jnp.float32
BF16 = jnp.bfloat16

D_MODEL = 1024
PAST_LEN = 16384
RET_HEADS = 8
RET_DK = 64
RET_W = 512
RET_CHUNK = 128
ROPE_BASE = 10000.0
SSM_W = 512
SSM_GC = 16
SSM_G = 32
SSM_P = 64
SSM_N = SSM_G * SSM_P
SSM_SLABS = 4
SSM_SLAB_N = SSM_N // SSM_SLABS
CONV_W = 512
CONV_K = 3
PROJ_W = 7168
PEER_HEADS = 8
PEER_DQ = 256
PEER_NKEYS = 128
PEER_TOPK = 16
PEER_NEXP = PEER_NKEYS ** 2
PEER_SLOTS = PEER_HEADS * PEER_TOPK
TOPK_GROUP = 8
EPS = 1e-6
GELU_C = math.sqrt(2.0 / math.pi)
GELU_A = 0.044715

ROWS = 1024
LANES = 128
SUBLANES = 8
MIB = 1024 * 1024


def _params(sem, vmem_mib):
    return pltpu.CompilerParams(dimension_semantics=sem, vmem_limit_bytes=vmem_mib * MIB)


def _rms(x, g):
    return x * lax.rsqrt(jnp.mean(x * x, axis=-1, keepdims=True) + EPS) * g


def _dot(a, b):
    return jnp.dot(a, b, preferred_element_type=F32)


def _dot_nt(a, b):
    return lax.dot_general(a, b, (((1,), (1,)), ((), ())), preferred_element_type=F32)


def _inproj_kernel(x_ref, g_ref, w_ref, z_ref, h_scr):
    @pl.when(pl.program_id(1) == 0)
    def _():
        h_scr[...] = _rms(x_ref[...], g_ref[...]).astype(BF16)

    z_ref[...] = _dot(h_scr[...], w_ref[...])


def _inproj_bm_kernel(x_ref, g_ref, w_ref, z_ref, xtm_ref, h_scr):
    @pl.when(pl.program_id(1) == 0)
    def _():
        x = pltpu.einshape("btd->(tb)d", x_ref[...])
        xtm_ref[...] = x
        h_scr[...] = _rms(x, g_ref[...]).astype(BF16)

    z_ref[...] = _dot(h_scr[...], w_ref[...])


def _inproj(x, g, w, layer):
    batch_major = x.ndim == 3
    nb = 1024
    if batch_major:
        batch, seq, _ = x.shape
        t = batch * seq
        x_spec = pl.BlockSpec((batch, ROWS // batch, D_MODEL), lambda i, j: (0, i, 0))
    else:
        t = x.shape[0]
        x_spec = pl.BlockSpec((ROWS, D_MODEL), lambda i, j: (i, 0))
    z_spec = pl.BlockSpec((ROWS, nb), lambda i, j: (i, j))
    z_shape = jax.ShapeDtypeStruct((t, PROJ_W), F32)
    return pl.pallas_call(
        _inproj_bm_kernel if batch_major else _inproj_kernel,
        grid=(t // ROWS, PROJ_W // nb),
        in_specs=[x_spec,
                  pl.BlockSpec((1, D_MODEL), lambda i, j: (0, 0)),
                  pl.BlockSpec((None, D_MODEL, nb), lambda i, j: (layer, 0, j))],
        out_specs=[z_spec, pl.BlockSpec((ROWS, D_MODEL), lambda i, j: (i, 0))] if batch_major else z_spec,
        out_shape=[z_shape, jax.ShapeDtypeStruct((t, D_MODEL), F32)] if batch_major else z_shape,
        scratch_shapes=[pltpu.VMEM((ROWS, D_MODEL), BF16)],
        compiler_params=_params(("parallel", "arbitrary"), 48),
        name="inproj",
    )(x, g, w)


def _ret_kernel(batch, bblk, aliased, out_layer, *refs):
    refs = [r for i, r in enumerate(refs) if not (aliased and i == 9)]
    (z_ref, cos_ref, sa_ref, sb_ref, qdec_ref, kdec_ref, cdec_ref, gn_ref, s0_ref,
     o_ref, st_ref, qd_scr, kd_scr, v_scr, mask_scr, oacc_scr) = refs
    s_ref = st_ref if aliased else st_ref.at[out_layer]
    bb = pl.program_id(0)
    c = pl.program_id(1)
    steps = ROWS // batch
    nslab = RET_W // LANES

    def head_view(ref, h):
        return ref[h // 2, :, (h % 2) * RET_DK:(h % 2 + 1) * RET_DK]

    seq_local = steps >= LANES
    msize = steps if seq_local else ROWS

    @pl.when((bb == 0) & (c == 0))
    def _():
        r = lax.broadcasted_iota(jnp.int32, (msize, msize), 0)
        cc = lax.broadcasted_iota(jnp.int32, (msize, msize), 1)
        if seq_local:
            mask_scr[...] = (r >= cc).astype(F32)
        else:
            same = (r & (batch - 1)) == (cc & (batch - 1))
            mask_scr[...] = (same & (r >= cc)).astype(F32)

    @pl.when(c == 0)
    def _():
        s_ref[...] = s0_ref[...]
        if not aliased:
            for k in range(st_ref.shape[0]):
                if k != out_layer:
                    st_ref[k] = jnp.zeros(st_ref.shape[1:], F32)

    @pl.when(bb == 0)
    def _():
        def per_row(t_ref):
            return jnp.broadcast_to(t_ref[...][:, None, :], (steps, batch, LANES)).reshape(ROWS, LANES)

        cos, sa, sb = per_row(cos_ref), per_row(sa_ref), per_row(sb_ref)

        def rot(x):
            return x * cos + pltpu.roll(x, 32, 1) * sa + pltpu.roll(x, 96, 1) * sb

        for s in range(nslab):
            cols = slice(s * LANES, (s + 1) * LANES)
            qd_scr[s] = rot(z_ref[:, cols]) * qdec_ref[:, cols]
            kd_scr[s] = rot(z_ref[:, RET_W + s * LANES:RET_W + (s + 1) * LANES]) * kdec_ref[:, cols]
            v_scr[s] = z_ref[:, 2 * RET_W + s * LANES:2 * RET_W + (s + 1) * LANES]
        if not seq_local:
            for s in range(nslab):
                outs = []
                for h in (2 * s, 2 * s + 1):
                    qh = head_view(qd_scr, h).astype(BF16)
                    kh = head_view(kd_scr, h).astype(BF16)
                    vh = head_view(v_scr, h).astype(BF16)
                    p = (_dot_nt(qh, kh) * mask_scr[...]).astype(BF16)
                    outs.append(_dot(p, vh))
                oacc_scr[s] = jnp.concatenate(outs, axis=1)

    def per_seq(bl, carry):
        b = bb * bblk + bl
        rows = pl.ds(b, steps, stride=batch)
        for s in range(nslab):
            qb = qd_scr[s, rows, :]
            kb = kd_scr[s, rows, :]
            vb = v_scr[s, rows, :]
            outs = []
            for hh in range(2):
                h = 2 * s + hh
                hc = slice(hh * RET_DK, (hh + 1) * RET_DK)
                q16, k16, v16 = qb[:, hc].astype(BF16), kb[:, hc].astype(BF16), vb[:, hc].astype(BF16)
                st = s_ref[bl, h]
                o = _dot(q16, st.astype(BF16))
                if seq_local:
                    p = (_dot_nt(q16, k16) * mask_scr[...]).astype(BF16)
                    o = _dot(p, v16) + o
                outs.append(o)
                upd = lax.dot_general(k16, v16, (((0,), (0,)), ((), ())), preferred_element_type=F32)
                s_ref[bl, h] = (st + upd) * cdec_ref[:, h * RET_DK:(h + 1) * RET_DK]
            o2 = jnp.concatenate(outs, axis=1)
            oacc_scr[s, rows, :] = o2 if seq_local else oacc_scr[s, rows, :] + o2
        return carry

    lax.fori_loop(0, bblk, per_seq, 0, unroll=4)

    @pl.when(bb == pl.num_programs(0) - 1)
    def _():
        r = lax.broadcasted_iota(jnp.int32, (LANES, LANES), 0) // RET_DK
        cc = lax.broadcasted_iota(jnp.int32, (LANES, LANES), 1) // RET_DK
        avg = jnp.where(r == cc, 1.0 / RET_DK, 0.0).astype(BF16)

        def seg_mean(x):
            hi = x.astype(BF16)
            lo = (x - hi.astype(F32)).astype(BF16)
            return _dot(hi, avg) + _dot(lo, avg)

        normed = []
        for s in range(nslab):
            o2 = oacc_scr[s]
            dlt = o2 - seg_mean(o2)
            normed.append(dlt * lax.rsqrt(seg_mean(dlt * dlt) + EPS))
        o = jnp.concatenate(normed, axis=1) * gn_ref[...]
        o_ref[...] = jax.nn.silu(z_ref[:, 3 * RET_W:4 * RET_W]) * o


def _retention(z, nblk, batch, bblk, tabs, gn, s0, layer, stacked, out_layer, depth):
    cos, sa, sb, qdec, kdec, cdec = tabs
    nbb = batch // bblk
    steps = ROWS // batch
    msize = steps if steps >= LANES else ROWS
    aliased = stacked is not None
    if aliased:
        st_spec = pl.BlockSpec((None, bblk, RET_HEADS, RET_DK, RET_DK), lambda bb, c: (out_layer, bb, 0, 0, 0))
    else:
        st_spec = pl.BlockSpec((depth, bblk, RET_HEADS, RET_DK, RET_DK), lambda bb, c: (0, bb, 0, 0, 0))
    st_in = pl.BlockSpec((None, bblk, RET_HEADS, RET_DK, RET_DK), lambda bb, c: (layer, bb, 0, 0, 0))
    const = lambda bb, c: (0, 0)
    return pl.pallas_call(
        functools.partial(_ret_kernel, batch, bblk, aliased, out_layer),
        grid=(nbb, nblk),
        in_specs=[pl.BlockSpec((ROWS, 4 * RET_W), lambda bb, c: (c, 0)),
                  pl.BlockSpec((steps, LANES), lambda bb, c: (c, 0)),
                  pl.BlockSpec((steps, LANES), lambda bb, c: (c, 0)),
                  pl.BlockSpec((steps, LANES), lambda bb, c: (c, 0)),
                  pl.BlockSpec((ROWS, RET_W), const),
                  pl.BlockSpec((ROWS, RET_W), const),
                  pl.BlockSpec((1, RET_W), const),
                  pl.BlockSpec((1, RET_W), const),
                  st_in] + ([pl.BlockSpec(memory_space=pl.ANY)] if aliased else []),
        out_specs=[pl.BlockSpec((ROWS, RET_W), lambda bb, c: (c, 0)), st_spec],
        out_shape=[jax.ShapeDtypeStruct((nblk * ROWS, RET_W), F32),
                   jax.ShapeDtypeStruct((depth, batch, RET_HEADS, RET_DK, RET_DK), F32)],
        input_output_aliases={9: 1} if aliased else {},
        scratch_shapes=[pltpu.VMEM((RET_W // LANES, ROWS, LANES), F32)] * 3
        + [pltpu.VMEM((msize, msize), F32), pltpu.VMEM((RET_W // LANES, ROWS, LANES), F32)],
        compiler_params=_params(("arbitrary", "arbitrary"), 56),
        name="retention",
    )(z, cos, sa, sb, qdec, kdec, cdec, gn, s0, *([stacked] if aliased else []))


def _retention_tables(pos, batch):
    half = RET_DK // 2
    lane = jnp.arange(LANES)
    upper = (lane % RET_DK) >= half
    freqs = (ROPE_BASE ** (-jnp.arange(half, dtype=F32) / half))[lane % half]
    ang = pos[:, None] * freqs[None, :]
    cos_t, sin = jnp.cos(ang), jnp.sin(ang)
    sa_t = jnp.where(upper[None, :], sin, 0.0)
    sb_t = jnp.where(upper[None, :], 0.0, -sin)
    lg = jnp.repeat(jnp.log1p(-jnp.exp2(-5.0 - jnp.arange(RET_HEADS, dtype=F32))), RET_DK)
    steps = ROWS // batch
    i1 = (jnp.arange(ROWS) // batch).astype(F32) + 1.0
    qdec = jnp.exp(i1[:, None] * lg[None, :])
    kdec = jnp.exp(-i1[:, None] * lg[None, :]) * (RET_DK ** -0.5)
    cdec = jnp.exp(steps * lg)[None, :]
    return cos_t, sa_t, sb_t, qdec, kdec, cdec


def _s5_disc_kernel(are_ref, aim_ref, ldt_ref, bre_ref, bim_ref, abre_ref, abim_ref, bbre_ref, bbim_ref):
    ar, ai = are_ref[...], aim_ref[...]
    dt = jnp.exp(ldt_ref[...])
    dar, dai = dt * ar, dt * ai
    mag = jnp.exp(dar)
    abar_re, abar_im = mag * jnp.cos(dai), mag * jnp.sin(dai)
    den = ar * ar + ai * ai
    nr, ni = abar_re - 1.0, abar_im
    f_re = (nr * ar + ni * ai) / den
    f_im = (ni * ar - nr * ai) / den
    abre_ref[...] = abar_re
    abim_ref[...] = abar_im
    br, bi = bre_ref[...], bim_ref[...]
    bbre_ref[...] = f_re[:, None, :] * br - f_im[:, None, :] * bi
    bbim_ref[...] = f_re[:, None, :] * bi + f_im[:, None, :] * br


def _s5_discretise(a_re, a_im, log_dt, b_re_t, b_im_t):
    lg = a_re.shape[0]
    small = jax.ShapeDtypeStruct((lg, SSM_P), F32)
    big = jax.ShapeDtypeStruct((lg, SSM_GC, SSM_P), F32)
    return pl.pallas_call(_s5_disc_kernel, out_shape=[small, small, big, big], name="s5_disc")(
        a_re, a_im, log_dt, b_re_t, b_im_t)


def _s5_kernel(batch, u_ref, bmat_ref, cmat_ref, are_ref, aim_ref, d_ref, h0re_ref, h0im_ref,
               y_ref, xre_ref, xim_ref, x_scr):
    c = pl.program_id(0)
    steps = ROWS // batch
    half = SSM_SLAB_N

    @pl.when(c == 0)
    def _():
        xre_ref[...] = h0re_ref[...]
        xim_ref[...] = h0im_ref[...]

    u = u_ref[...]
    ub = u.astype(BF16)
    for s in range(SSM_SLABS):
        x_scr[:, 2 * half * s:2 * half * (s + 1)] = _dot(ub[:, s * LANES:(s + 1) * LANES], bmat_ref[s])

    for s in range(SSM_SLABS):
        re0 = 2 * half * s
        im0 = re0 + half
        sc = slice(half * s, half * (s + 1))
        ar = jnp.broadcast_to(are_ref[:, sc], (SUBLANES, half))
        ai = jnp.broadcast_to(aim_ref[:, sc], (SUBLANES, half))

        def row_tile(rt, carry, re0=re0, im0=im0, sc=sc, ar=ar, ai=ai):
            r0 = pl.multiple_of(rt * SUBLANES, SUBLANES)

            def step(t, x):
                xr, xi = x
                row = pl.multiple_of(t * batch + r0, SUBLANES)
                nr = ar * xr - ai * xi + x_scr[pl.ds(row, SUBLANES), re0:re0 + half]
                ni = ar * xi + ai * xr + x_scr[pl.ds(row, SUBLANES), im0:im0 + half]
                x_scr[pl.ds(row, SUBLANES), re0:re0 + half] = nr
                x_scr[pl.ds(row, SUBLANES), im0:im0 + half] = ni
                return nr, ni

            init = (xre_ref[pl.ds(r0, SUBLANES), sc], xim_ref[pl.ds(r0, SUBLANES), sc])
            xr, xi = lax.fori_loop(0, steps, step, init, unroll=8)
            xre_ref[pl.ds(r0, SUBLANES), sc] = xr
            xim_ref[pl.ds(r0, SUBLANES), sc] = xi
            return carry

        lax.fori_loop(0, batch // SUBLANES, row_tile, 0)

    ys = [_dot(x_scr[:, 2 * half * s:2 * half * (s + 1)].astype(BF16), cmat_ref[s]) for s in range(SSM_SLABS)]
    y = jnp.concatenate(ys, axis=1) + d_ref[...] * u
    y_ref[...] = jax.nn.gelu(y)


def _s5(z, nblk, batch, bmat, cmat, abre, abim, d, h0re, h0im):
    const2 = lambda c: (0, 0)
    const3 = lambda c: (0, 0, 0)
    st = pl.BlockSpec((batch, SSM_N), const2)
    return pl.pallas_call(
        functools.partial(_s5_kernel, batch),
        grid=(nblk,),
        in_specs=[pl.BlockSpec((ROWS, SSM_W), lambda c: (c, 4)),
                  pl.BlockSpec((SSM_SLABS, LANES, 2 * SSM_SLAB_N), const3),
                  pl.BlockSpec((SSM_SLABS, 2 * SSM_SLAB_N, LANES), const3),
                  pl.BlockSpec((1, SSM_N), const2),
                  pl.BlockSpec((1, SSM_N), const2),
                  pl.BlockSpec((1, SSM_W), const2),
                  st, st],
        out_specs=[pl.BlockSpec((ROWS, SSM_W), lambda c: (c, 0)), st, st],
        out_shape=[jax.ShapeDtypeStruct((nblk * ROWS, SSM_W), F32),
                   jax.ShapeDtypeStruct((batch, SSM_N), F32),
                   jax.ShapeDtypeStruct((batch, SSM_N), F32)],
        scratch_shapes=[pltpu.VMEM((ROWS, 2 * SSM_N), F32)],
        compiler_params=_params(("arbitrary",), 48),
        name="s5",
    )(z, bmat, cmat, abre, abim, d, h0re, h0im)


def _block_diag_slabs(w, nslab):
    gps = SSM_G // nslab
    eye = jnp.eye(gps, dtype=w.dtype)
    w4 = w.reshape(nslab, gps, w.shape[1], w.shape[2])
    out = w4[:, :, :, None, :] * eye[None, :, None, :, None]
    return out.reshape(nslab, gps * w.shape[1], gps * w.shape[2])


def _merge_kernel(batch, x_ref, oa_ref, ys_ref, bg_ref, cg_ref, hc_ref, buf0_ref, cw_ref, cb_ref,
                  ga_ref, gb_ref, gc_ref, wr_ref, wa_ref, wb_ref, wc_ref, wm_ref, gf_ref,
                  x1_ref, xn_ref, buf_ref, zp_scr):
    rb = x_ref.shape[0]
    pad = (CONV_K - 1) * batch

    @pl.when(pl.program_id(0) == 0)
    def _():
        zp_scr[0:pad, :] = buf0_ref[...]

    zc = cg_ref[...] * hc_ref[...]
    zp_scr[pad:pad + rb, :] = zc
    y = cb_ref[...]
    for j in range(CONV_K):
        y = y + cw_ref[j:j + 1, :] * zp_scr[j * batch:j * batch + rb, :]
    oc_pre = bg_ref[...] * y
    tail = zp_scr[rb:rb + pad, :]
    buf_ref[...] = tail
    zp_scr[0:pad, :] = tail

    oa = _dot(oa_ref[...].astype(BF16), wr_ref[...])
    ysb = ys_ref[...].astype(BF16)
    ob = _dot(ysb, wa_ref[...]) * jax.nn.sigmoid(_dot(ysb, wb_ref[...]))
    oc = _dot(oc_pre.astype(BF16), wc_ref[...])
    merged = (jax.nn.sigmoid(ga_ref[...]) * oa + jax.nn.sigmoid(gb_ref[...]) * ob
              + jax.nn.sigmoid(gc_ref[...]) * oc)
    x1 = x_ref[...] + _dot(merged.astype(BF16), wm_ref[...])
    x1_ref[...] = x1
    xn_ref[...] = _rms(x1, gf_ref[...]).astype(BF16)


def _merge(x, z, oa, ys, batch, buf0, conv_w, conv_b, wr, wa, wb, wc, wm, gf):
    t = x.shape[0]
    rb = 512
    pad = (CONV_K - 1) * batch
    assert pad <= rb
    row = lambda w: pl.BlockSpec((rb, w), lambda i: (i, 0))
    zcol = lambda w, j: pl.BlockSpec((rb, w), lambda i: (i, j))
    const = lambda r, w: pl.BlockSpec((r, w), lambda i: (0, 0))
    return pl.pallas_call(
        functools.partial(_merge_kernel, batch),
        grid=(t // rb,),
        in_specs=[row(D_MODEL), row(RET_W), row(SSM_W),
                  zcol(CONV_W, 5), zcol(CONV_W, 6), zcol(CONV_W, 7),
                  const(pad, CONV_W), const(CONV_K, CONV_W), const(1, CONV_W),
                  zcol(D_MODEL, 4), zcol(D_MODEL, 5), zcol(D_MODEL, 6),
                  const(RET_W, D_MODEL), const(SSM_W, D_MODEL), const(SSM_W, D_MODEL), const(CONV_W, D_MODEL),
                  const(D_MODEL, D_MODEL), const(1, D_MODEL)],
        out_specs=[row(D_MODEL), row(D_MODEL), const(pad, CONV_W)],
        out_shape=[jax.ShapeDtypeStruct((t, D_MODEL), F32), jax.ShapeDtypeStruct((t, D_MODEL), BF16),
                   jax.ShapeDtypeStruct((pad, CONV_W), F32)],
        scratch_shapes=[pltpu.VMEM((rb + pad, CONV_W), F32)],
        compiler_params=_params(("arbitrary",), 48),
        name="merge",
    )(x, oa, ys, z, z, z, buf0, conv_w, conv_b, z, z, z, wr, wa, wb, wc, wm, gf)


def _tree(items, combine):
    while len(items) > 1:
        nxt = [combine(items[i], items[i + 1]) for i in range(0, len(items) - 1, 2)]
        if len(items) % 2:
            nxt.append(items[-1])
        items = nxt
    return items[0]


def _first_max(x, y):
    (vx, ix), (vy, iy) = x, y
    return jnp.maximum(vx, vy), jnp.where(vx >= vy, ix, iy)


def _bits(x, n):
    out, rest = [], x
    for _ in range(n):
        half = jnp.floor(rest * 0.5)
        out.append(rest - 2.0 * half == 1.0)
        rest = half
    return out


def _mux(vals, bits):
    level = list(vals)
    for bit in bits:
        level = [jnp.where(bit, level[j + 1], level[j]) for j in range(0, len(level), 2)]
    return level[0]


def _top16_of_keys(s_scrs, gv_scrs, gi_scrs, v_scrs, i_scrs):
    grp = TOPK_GROUP
    ngrp = PEER_NKEYS // grp
    nbits = ngrp.bit_length() - 1

    for s_scr, gv, gi in zip(s_scrs, gv_scrs, gi_scrs):
        for g in range(ngrp):
            gv[g], gi[g] = _tree([(s_scr[g * grp + p], float(g * grp + p)) for p in range(grp)], _first_max)

    def body(r, carry):
        for s_scr, gv, gi, v_scr, i_scr in zip(s_scrs, gv_scrs, gi_scrs, v_scrs, i_scrs):
            m, idx = _tree([(gv[g], gi[g]) for g in range(ngrp)], _first_max)
            v_scr[r] = m
            i_scr[r] = idx
            gid = jnp.floor(idx * (1.0 / grp))
            rel = idx - gid * grp
            bits = _bits(gid, nbits)
            cands = []
            for p in range(grp):
                val = _mux([s_scr[g * grp + p] for g in range(ngrp)], bits)
                left = (val < m) | ((val == m) & (rel < float(p)))
                cands.append((jnp.where(left, val, -jnp.inf), float(p)))
            nv, npos = _tree(cands, _first_max)
            ni = gid * grp + npos
            for g in range(ngrp):
                hit = gid == float(g)
                gv[g] = jnp.where(hit, nv, gv[g])
                gi[g] = jnp.where(hit, ni, gi[g])
        return carry

    lax.fori_loop(0, PEER_TOPK, body, 0)


def _select_kernel(tb, xn_ref, wq_ref, k1_ref, k2_ref, e1_ref, e2_ref, g_ref,
                   s1_scr, s2_scr, gv1_scr, gi1_scr, gv2_scr, gi2_scr, v1_scr, i1_scr, v2_scr, i2_scr,
                   hv_scr, hb_scr, sc_scr, se1_scr, se2_scr):
    q = _dot(xn_ref[...], wq_ref[...]).astype(BF16)
    hq = PEER_HEADS * PEER_DQ // 2
    s1 = _dot_nt(k1_ref[...], q[:, :hq])
    s2 = _dot_nt(k2_ref[...], q[:, hq:])
    kbits = PEER_TOPK.bit_length() - 1
    for lt in range(tb // LANES):
        lanes = slice(lt * LANES, (lt + 1) * LANES)
        s1_scr[...] = s1[:, lanes].reshape(PEER_NKEYS, SUBLANES, LANES)
        s2_scr[...] = s2[:, lanes].reshape(PEER_NKEYS, SUBLANES, LANES)
        _top16_of_keys((s1_scr, s2_scr), (gv1_scr, gv2_scr), (gi1_scr, gi2_scr), (v1_scr, v2_scr), (i1_scr, i2_scr))

        for a in range(PEER_TOPK):
            hv_scr[a] = v1_scr[a] + v2_scr[0]
            hb_scr[a] = jnp.zeros((SUBLANES, LANES), F32)

        def body(r, carry):
            m, a_sel = _tree([(hv_scr[a], float(a)) for a in range(PEER_TOPK)], _first_max)
            abits = _bits(a_sel, kbits)
            b_sel = _mux([hb_scr[a] for a in range(PEER_TOPK)], abits)
            bbits = _bits(b_sel, kbits)
            sc_scr[r] = m
            se1_scr[r] = _mux([i1_scr[a] for a in range(PEER_TOPK)], abits)
            se2_scr[r] = _mux([i2_scr[b] for b in range(PEER_TOPK)], bbits)
            nb = b_sel + 1.0
            v2_next = _mux([v2_scr[(b + 1) % PEER_TOPK] for b in range(PEER_TOPK)], bbits)
            v1_sel = _mux([v1_scr[a] for a in range(PEER_TOPK)], abits)
            live = (a_sel + 1.0) * (nb + 1.0) <= float(PEER_TOPK)
            nv = jnp.where(live, v1_sel + v2_next, -jnp.inf)
            for a in range(PEER_TOPK):
                hit = a_sel == float(a)
                hv_scr[a] = jnp.where(hit, nv, hv_scr[a])
                hb_scr[a] = jnp.where(hit, nb, hb_scr[a])
            return carry

        lax.fori_loop(0, PEER_TOPK, body, 0)
        sc = sc_scr[...]
        ex = jnp.exp(sc - jnp.max(sc, axis=0, keepdims=True))
        gate = ex / jnp.sum(ex, axis=0, keepdims=True)
        rows = slice(lt * LANES, (lt + 1) * LANES)
        g_ref[rows, :] = gate.reshape(PEER_SLOTS, LANES).T
        e1_ref[rows, :] = se1_scr[...].reshape(PEER_SLOTS, LANES).T
        e2_ref[rows, :] = se2_scr[...].reshape(PEER_SLOTS, LANES).T


def _peer_select(xn, wq, k1big, k2big):
    t = xn.shape[0]
    tb = 256
    hq = PEER_HEADS * PEER_DQ // 2
    nk = PEER_NKEYS * PEER_HEADS
    const = lambda i: (0, 0)
    row = lambda dt: jax.ShapeDtypeStruct((t, PEER_SLOTS), dt)
    vec = lambda n: pltpu.VMEM((n, SUBLANES, LANES), F32)
    return pl.pallas_call(
        functools.partial(_select_kernel, tb),
        grid=(t // tb,),
        in_specs=[pl.BlockSpec((tb, D_MODEL), lambda i: (i, 0)),
                  pl.BlockSpec((D_MODEL, 2 * hq), const),
                  pl.BlockSpec((nk, hq), const),
                  pl.BlockSpec((nk, hq), const)],
        out_specs=[pl.BlockSpec((tb, PEER_SLOTS), lambda i: (i, 0))] * 3,
        out_shape=[row(F32), row(F32), row(F32)],
        scratch_shapes=[vec(PEER_NKEYS), vec(PEER_NKEYS)] + [vec(PEER_NKEYS // TOPK_GROUP)] * 4
        + [vec(PEER_TOPK), vec(PEER_TOPK), vec(PEER_TOPK), vec(PEER_TOPK),
                        vec(PEER_TOPK), vec(PEER_TOPK), vec(PEER_TOPK), vec(PEER_TOPK), vec(PEER_TOPK)],
        compiler_params=_params(("parallel",), 40),
        name="peer_select",
    )(xn, wq, k1big, k2big)


def _peer_kernel(tb, eb, stride, final_norm, out_batch, xn_ref, e1_ref, e2_ref, g_ref, ut_ref, v_ref, x1_ref,
                 gain_ref, out_ref, m_scr, *acc_scr):
    e = pl.program_id(1)
    nk1 = eb // PEER_NKEYS
    acc_ref = acc_scr[0] if out_batch else out_ref

    @pl.when(e == 0)
    def _():
        acc_ref[...] = x1_ref[...]

    @pl.when(e == 0)
    def _():
        key = lax.broadcasted_iota(jnp.int32, (PEER_NKEYS, PEER_SLOTS), 0).astype(F32)

        def token(t, carry):
            e1 = e1_ref[pl.ds(t, 1), :]
            e2 = e2_ref[pl.ds(t, 1), :]
            gt = 0.5 * g_ref[pl.ds(t, 1), :]
            a_t = jnp.where(key == e1, gt, 0.0).astype(BF16)
            b_t = jnp.where(key == e2, 1.0, 0.0).astype(BF16)
            m_scr[pl.ds(t, PEER_NKEYS, stride=stride), :] = _dot_nt(a_t, b_t)
            return carry

        lax.fori_loop(0, tb, token, 0, unroll=128)

    s = _dot(xn_ref[...], ut_ref[...])
    t = jnp.tanh(s * (GELU_C + (GELU_C * GELU_A) * (s * s)))
    k1 = e * nk1
    gates = [m_scr[pl.ds(pl.multiple_of((k1 + i) * stride, SUBLANES), tb), :] for i in range(nk1)]
    w = ((s + s * t) * jnp.concatenate(gates, axis=1)).astype(BF16)
    acc_ref[...] += _dot(w, v_ref[...])

    if final_norm or out_batch:
        @pl.when(e == pl.num_programs(1) - 1)
        def _():
            y = acc_ref[...]
            if final_norm:
                y = _rms(y, gain_ref[...])
            out_ref[...] = pltpu.einshape("(tb)d->btd", y, b=out_batch) if out_batch else y


def _peer_dense(xn, e1, e2, g, ut, v, x1, layer, gain, final_norm, out_batch=0):
    t = xn.shape[0]
    tb, eb = 512, 2048
    stride = tb + SUBLANES
    once = pl.Buffered(1)
    tok = lambda w: pl.BlockSpec((tb, w), lambda i, e: (i, 0), pipeline_mode=once)
    tab = pl.BlockSpec((None, eb, D_MODEL), lambda i, e: (layer, e, 0))
    if out_batch:
        out_spec = pl.BlockSpec((out_batch, tb // out_batch, D_MODEL), lambda i, e: (0, i, 0), pipeline_mode=once)
        out_shape = jax.ShapeDtypeStruct((out_batch, t // out_batch, D_MODEL), F32)
        acc = [pltpu.VMEM((tb, D_MODEL), F32)]
    else:
        out_spec = pl.BlockSpec((tb, D_MODEL), lambda i, e: (i, 0), pipeline_mode=once)
        out_shape = jax.ShapeDtypeStruct((t, D_MODEL), F32)
        acc = []
    return pl.pallas_call(
        functools.partial(_peer_kernel, tb, eb, stride, final_norm, out_batch),
        grid=(t // tb, PEER_NEXP // eb),
        in_specs=[tok(D_MODEL), tok(PEER_SLOTS), tok(PEER_SLOTS), tok(PEER_SLOTS),
                  pl.BlockSpec((None, D_MODEL, eb), lambda i, e: (layer, 0, e)), tab, tok(D_MODEL),
                  pl.BlockSpec((1, D_MODEL), lambda i, e: (0, 0))],
        out_specs=out_spec,
        out_shape=out_shape,
        scratch_shapes=[pltpu.VMEM((PEER_NKEYS * stride, PEER_NKEYS), F32)] + acc,
        compiler_params=_params(("parallel", "arbitrary"), 62),
        name="peer_dense",
    )(xn, e1, e2, g, ut, v, x1, gain)


def _time_major(x):
    b, s, d = x.shape
    return x.transpose(1, 0, 2).reshape(s * b, d)


def _batch_major(y, b, s):
    return y.reshape(s, b, y.shape[-1]).transpose(1, 0, 2)


def kernel(x_prompt, x_sample, state_ret, state_ssm_re, state_ssm_im, state_conv, norm_mix, w_in, ret_norm, w_ret_out, ssm_a_re, ssm_a_im, ssm_b_re, ssm_b_im, ssm_c_re, ssm_c_im, ssm_d, ssm_log_dt, w_glu_a, w_glu_b, conv_w, conv_b, w_conv_out, w_mix_out, norm_ffn, peer_wq, peer_k1, peer_k2, peer_u, peer_v, norm_final):
    bp, sp, _ = x_prompt.shape
    bs, ss, _ = x_sample.shape
    tp, ts = bp * sp, bs * ss
    depth = w_in.shape[0]
    assert tp % ROWS == 0 and ts == ROWS and ROWS % bp == 0 and ROWS // bp == math.gcd(sp, RET_CHUNK)

    xs = [x_prompt, _time_major(x_sample)]
    batches = (bp, bs)
    pos = (jnp.arange(sp, dtype=F32), PAST_LEN + jnp.arange(ss, dtype=F32))
    ret_tabs = [_retention_tables(p, b) for p, b in zip(pos, batches)]

    lg = depth * SSM_G
    abre, abim, bbre, bbim = _s5_discretise(
        ssm_a_re.reshape(lg, SSM_P), ssm_a_im.reshape(lg, SSM_P), ssm_log_dt.reshape(lg, 1),
        ssm_b_re.transpose(0, 1, 3, 2).reshape(lg, SSM_GC, SSM_P),
        ssm_b_im.transpose(0, 1, 3, 2).reshape(lg, SSM_GC, SSM_P))

    hq = PEER_DQ // 2
    eye = jnp.eye(PEER_HEADS, dtype=F32)

    def keys_block_diag(k):
        return (k.transpose(1, 0, 2)[:, :, None, :] * eye[None, :, :, None]).reshape(
            PEER_NKEYS * PEER_HEADS, PEER_HEADS * hq).astype(BF16)

    w_in_b = w_in.astype(BF16)
    ut = peer_u.astype(BF16).transpose(0, 2, 1)
    vt = peer_v.astype(BF16)
    zero_ret = jnp.zeros((1, bp, RET_HEADS, RET_DK, RET_DK), F32)

    states = [[[] for _ in range(3)] for _ in range(2)]
    ret_stacked = [None, None]
    for l in range(depth):
        sl = slice(l * SSM_G, (l + 1) * SSM_G)
        bmat = jnp.concatenate([_block_diag_slabs(bbre[sl], SSM_SLABS), _block_diag_slabs(bbim[sl], SSM_SLABS)],
                               axis=2).astype(BF16)
        cmat = jnp.concatenate([_block_diag_slabs(ssm_c_re[l].transpose(0, 2, 1), SSM_SLABS),
                                _block_diag_slabs(-ssm_c_im[l].transpose(0, 2, 1), SSM_SLABS)],
                               axis=1).astype(BF16)
        are_row = abre[sl].reshape(1, SSM_N)
        aim_row = abim[sl].reshape(1, SSM_N)
        proj = [w.astype(BF16) for w in (w_ret_out[l], w_glu_a[l], w_glu_b[l], w_conv_out[l], w_mix_out[l])]
        wq = peer_wq[l].reshape(D_MODEL, PEER_HEADS, 2, hq).transpose(0, 2, 1, 3).reshape(D_MODEL, -1).astype(BF16)
        k1big, k2big = keys_block_diag(peer_k1[l]), keys_block_diag(peer_k2[l])

        for gi, batch in enumerate(batches):
            x = xs[gi]
            nblk = x.size // (ROWS * D_MODEL)
            if gi == 0:
                s0, s0_layer = zero_ret, 0
                h0re = jnp.zeros((batch, SSM_N), F32)
                h0im = jnp.zeros((batch, SSM_N), F32)
                buf0 = jnp.zeros(((CONV_K - 1) * batch, CONV_W), F32)
                bblk = batch
            else:
                s0, s0_layer = state_ret, l
                h0re = state_ssm_re[l].reshape(batch, SSM_N)
                h0im = state_ssm_im[l].reshape(batch, SSM_N)
                buf0 = state_conv[l].transpose(1, 0, 2).reshape((CONV_K - 1) * batch, CONV_W)
                bblk = 16
            z = _inproj(x, norm_mix[l][None, :], w_in_b, l)
            if x.ndim == 3:
                z, x = z
            oa, ret_stacked[gi] = _retention(z, nblk, batch, bblk, ret_tabs[gi], ret_norm[l][None, :], s0, s0_layer,
                                          ret_stacked[gi], l, depth)
            ys, xre, xim = _s5(z, nblk, batch, bmat, cmat, are_row, aim_row, ssm_d[l][None, :], h0re, h0im)
            st = states[gi]
            st[0].append(xre.reshape(batch, SSM_G, SSM_P))
            st[1].append(xim.reshape(batch, SSM_G, SSM_P))

            x1, xn, buf = _merge(x, z, oa, ys, batch, buf0, conv_w[l], conv_b[l][None, :], *proj, norm_ffn[l][None, :])
            st[2].append(buf.reshape(CONV_K - 1, batch, CONV_W).transpose(1, 0, 2))
            e1, e2, g = _peer_select(xn, wq, k1big, k2big)
            last = l == depth - 1
            xs[gi] = _peer_dense(xn, e1, e2, g, ut, vt, x1, l, norm_final[None, :], last,
                                 out_batch=bp if last and gi == 0 else 0)

    y_prompt = xs[0]
    y_sample = _batch_major(xs[1], bs, ss)
    (re_p, im_p, cv_p), (re_s, im_s, cv_s) = states
    return (y_prompt, y_sample,
            ret_stacked[0], ret_stacked[1],
            jnp.stack(re_p), jnp.stack(re_s),
            jnp.stack(im_p), jnp.stack(im_s),
            jnp.stack(cv_p), jnp.stack(cv_s))
```

```python
import functools
import math

import jax
import jax.numpy as jnp
from jax import lax
from jax.experimental import pallas as pl
from jax.experimental.pallas import tpu as pltpu

F32 = jnp.float32
BF16 = jnp.bfloat16

D_MODEL = 1024
PAST_LEN = 16384
RET_HEADS = 8
RET_DK = 64
RET_W = 512
RET_CHUNK = 128
ROPE_BASE = 10000.0
SSM_W = 512
SSM_GC = 16
SSM_G = 32
SSM_P = 64
SSM_N = SSM_G * SSM_P
SSM_SLABS = 4
SSM_SLAB_N = SSM_N // SSM_SLABS
CONV_W = 512
CONV_K = 3
PROJ_W = 7168
PEER_HEADS = 8
PEER_DQ = 256
PEER_NKEYS = 128
PEER_TOPK = 16
PEER_NEXP = PEER_NKEYS ** 2
PEER_SLOTS = PEER_HEADS * PEER_TOPK
TOPK_GROUP = 8
EPS = 1e-6
GELU_C = math.sqrt(2.0 / math.pi)
GELU_A = 0.044715

ROWS = 1024
LANES = 128
SUBLANES = 8
MIB = 1024 * 1024


def _params(sem, vmem_mib):
    return pltpu.CompilerParams(dimension_semantics=sem, vmem_limit_bytes=vmem_mib * MIB)


def _rms(x, g):
    return x * lax.rsqrt(jnp.mean(x * x, axis=-1, keepdims=True) + EPS) * g


def _dot(a, b):
    return jnp.dot(a, b, preferred_element_type=F32)


def _dot_nt(a, b):
    return lax.dot_general(a, b, (((1,), (1,)), ((), ())), preferred_element_type=F32)


def _inproj_kernel(x_ref, g_ref, w_ref, z_ref, h_scr):
    @pl.when(pl.program_id(1) == 0)
    def _():
        h_scr[...] = _rms(x_ref[...], g_ref[...]).astype(BF16)

    z_ref[...] = _dot(h_scr[...], w_ref[...])


def _inproj_bm_kernel(x_ref, g_ref, w_ref, z_ref, xtm_ref, h_scr):
    @pl.when(pl.program_id(1) == 0)
    def _():
        x = pltpu.einshape("btd->(tb)d", x_ref[...])
        xtm_ref[...] = x
        h_scr[...] = _rms(x, g_ref[...]).astype(BF16)

    z_ref[...] = _dot(h_scr[...], w_ref[...])


def _inproj(x, g, w, layer):
    batch_major = x.ndim == 3
    nb = PROJ_W // 4
    if batch_major:
        batch, seq, _ = x.shape
        t = batch * seq
        x_spec = pl.BlockSpec((batch, ROWS // batch, D_MODEL), lambda i, j: (0, i, 0))
    else:
        t = x.shape[0]
        x_spec = pl.BlockSpec((ROWS, D_MODEL), lambda i, j: (i, 0))
    z_spec = pl.BlockSpec((ROWS, nb), lambda i, j: (i, j))
    z_shape = jax.ShapeDtypeStruct((t, PROJ_W), F32)
    return pl.pallas_call(
        _inproj_bm_kernel if batch_major else _inproj_kernel,
        grid=(t // ROWS, PROJ_W // nb),
        in_specs=[x_spec,
                  pl.BlockSpec((1, D_MODEL), lambda i, j: (0, 0)),
                  pl.BlockSpec((None, D_MODEL, nb), lambda i, j: (layer, 0, j))],
        out_specs=[z_spec, pl.BlockSpec((ROWS, D_MODEL), lambda i, j: (i, 0))] if batch_major else z_spec,
        out_shape=[z_shape, jax.ShapeDtypeStruct((t, D_MODEL), F32)] if batch_major else z_shape,
        scratch_shapes=[pltpu.VMEM((ROWS, D_MODEL), BF16)],
        compiler_params=_params(("parallel", "arbitrary"), 48),
        name="inproj",
    )(x, g, w)


def _ret_kernel(batch, bblk, aliased, out_layer, *refs):
    refs = [r for i, r in enumerate(refs) if not (aliased and i == 9)]
    (z_ref, cos_ref, sa_ref, sb_ref, qdec_ref, kdec_ref, cdec_ref, gn_ref, s0_ref,
     o_ref, st_ref, qd_scr, kd_scr, v_scr, mask_scr, oacc_scr) = refs
    s_ref = st_ref if aliased else st_ref.at[out_layer]
    bb = pl.program_id(0)
    c = pl.program_id(1)
    steps = ROWS // batch
    nslab = RET_W // LANES

    def head_view(ref, h):
        return ref[h // 2, :, (h % 2) * RET_DK:(h % 2 + 1) * RET_DK]

    seq_local = steps >= LANES
    msize = steps if seq_local else ROWS

    @pl.when((bb == 0) & (c == 0))
    def _():
        r = lax.broadcasted_iota(jnp.int32, (msize, msize), 0)
        cc = lax.broadcasted_iota(jnp.int32, (msize, msize), 1)
        if seq_local:
            mask_scr[...] = (r >= cc).astype(F32)
        else:
            same = (r & (batch - 1)) == (cc & (batch - 1))
            mask_scr[...] = (same & (r >= cc)).astype(F32)

    @pl.when(c == 0)
    def _():
        s_ref[...] = s0_ref[...]
        if not aliased:
            for k in range(st_ref.shape[0]):
                if k != out_layer:
                    st_ref[k] = jnp.zeros(st_ref.shape[1:], F32)

    @pl.when(bb == 0)
    def _():
        def per_row(t_ref):
            return jnp.broadcast_to(t_ref[...][:, None, :], (steps, batch, LANES)).reshape(ROWS, LANES)

        cos, sa, sb = per_row(cos_ref), per_row(sa_ref), per_row(sb_ref)

        def rot(x):
            return x * cos + pltpu.roll(x, 32, 1) * sa + pltpu.roll(x, 96, 1) * sb

        for s in range(nslab):
            cols = slice(s * LANES, (s + 1) * LANES)
            qd_scr[s] = rot(z_ref[:, cols]) * qdec_ref[:, cols]
            kd_scr[s] = rot(z_ref[:, RET_W + s * LANES:RET_W + (s + 1) * LANES]) * kdec_ref[:, cols]
            v_scr[s] = z_ref[:, 2 * RET_W + s * LANES:2 * RET_W + (s + 1) * LANES]
        if not seq_local:
            for s in range(nslab):
                outs = []
                for h in (2 * s, 2 * s + 1):
                    qh = head_view(qd_scr, h).astype(BF16)
                    kh = head_view(kd_scr, h).astype(BF16)
                    vh = head_view(v_scr, h).astype(BF16)
                    p = (_dot_nt(qh, kh) * mask_scr[...]).astype(BF16)
                    outs.append(_dot(p, vh))
                oacc_scr[s] = jnp.concatenate(outs, axis=1)

    def per_seq(bl, carry):
        b = bb * bblk + bl
        rows = pl.ds(b, steps, stride=batch)
        heads = range(RET_HEADS)

        def head_cols(scr):
            slabs = [scr[s, rows, :] for s in range(nslab)]
            return [slabs[h // 2][:, (h % 2) * RET_DK:(h % 2 + 1) * RET_DK].astype(BF16) for h in heads]

        q16, k16, v16 = head_cols(qd_scr), head_cols(kd_scr), head_cols(v_scr)
        st = [s_ref[bl, h] for h in heads]
        outs = [_dot(q16[h], st[h].astype(BF16)) for h in heads]
        if seq_local:
            sc = [_dot_nt(q16[h], k16[h]) for h in heads]
            p = [(sc[h] * mask_scr[...]).astype(BF16) for h in heads]
            outs = [_dot(p[h], v16[h]) + outs[h] for h in heads]
        upd = [lax.dot_general(k16[h], v16[h], (((0,), (0,)), ((), ())), preferred_element_type=F32) for h in heads]
        for h in heads:
            s_ref[bl, h] = (st[h] + upd[h]) * cdec_ref[:, h * RET_DK:(h + 1) * RET_DK]
        for s in range(nslab):
            o2 = jnp.concatenate(outs[2 * s:2 * s + 2], axis=1)
            oacc_scr[s, rows, :] = o2 if seq_local else oacc_scr[s, rows, :] + o2
        return carry

    lax.fori_loop(0, bblk, per_seq, 0, unroll=4)

    @pl.when(bb == pl.num_programs(0) - 1)
    def _():
        r = lax.broadcasted_iota(jnp.int32, (LANES, LANES), 0) // RET_DK
        cc = lax.broadcasted_iota(jnp.int32, (LANES, LANES), 1) // RET_DK
        avg = jnp.where(r == cc, 1.0 / RET_DK, 0.0).astype(BF16)

        def seg_mean(x):
            hi = x.astype(BF16)
            lo = (x - hi.astype(F32)).astype(BF16)
            return _dot(hi, avg) + _dot(lo, avg)

        normed = []
        for s in range(nslab):
            o2 = oacc_scr[s]
            dlt = o2 - seg_mean(o2)
            normed.append(dlt * lax.rsqrt(seg_mean(dlt * dlt) + EPS))
        o = jnp.concatenate(normed, axis=1) * gn_ref[...]
        o_ref[...] = jax.nn.silu(z_ref[:, 3 * RET_W:4 * RET_W]) * o


def _retention(z, nblk, batch, bblk, tabs, gn, s0, layer, stacked, out_layer, depth):
    cos, sa, sb, qdec, kdec, cdec = tabs
    nbb = batch // bblk
    steps = ROWS // batch
    msize = steps if steps >= LANES else ROWS
    aliased = stacked is not None
    if aliased:
        st_spec = pl.BlockSpec((None, bblk, RET_HEADS, RET_DK, RET_DK), lambda bb, c: (out_layer, bb, 0, 0, 0))
    else:
        st_spec = pl.BlockSpec((depth, bblk, RET_HEADS, RET_DK, RET_DK), lambda bb, c: (0, bb, 0, 0, 0))
    st_in = pl.BlockSpec((None, bblk, RET_HEADS, RET_DK, RET_DK), lambda bb, c: (layer, bb, 0, 0, 0))
    const = lambda bb, c: (0, 0)
    return pl.pallas_call(
        functools.partial(_ret_kernel, batch, bblk, aliased, out_layer),
        grid=(nbb, nblk),
        in_specs=[pl.BlockSpec((ROWS, 4 * RET_W), lambda bb, c: (c, 0)),
                  pl.BlockSpec((steps, LANES), lambda bb, c: (c, 0)),
                  pl.BlockSpec((steps, LANES), lambda bb, c: (c, 0)),
                  pl.BlockSpec((steps, LANES), lambda bb, c: (c, 0)),
                  pl.BlockSpec((ROWS, RET_W), const),
                  pl.BlockSpec((ROWS, RET_W), const),
                  pl.BlockSpec((1, RET_W), const),
                  pl.BlockSpec((1, RET_W), const),
                  st_in] + ([pl.BlockSpec(memory_space=pl.ANY)] if aliased else []),
        out_specs=[pl.BlockSpec((ROWS, RET_W), lambda bb, c: (c, 0)), st_spec],
        out_shape=[jax.ShapeDtypeStruct((nblk * ROWS, RET_W), F32),
                   jax.ShapeDtypeStruct((depth, batch, RET_HEADS, RET_DK, RET_DK), F32)],
        input_output_aliases={9: 1} if aliased else {},
        scratch_shapes=[pltpu.VMEM((RET_W // LANES, ROWS, LANES), F32)] * 3
        + [pltpu.VMEM((msize, msize), F32), pltpu.VMEM((RET_W // LANES, ROWS, LANES), F32)],
        compiler_params=_params(("arbitrary", "arbitrary"), 56),
        name="retention",
    )(z, cos, sa, sb, qdec, kdec, cdec, gn, s0, *([stacked] if aliased else []))


def _retention_tables(pos, batch):
    half = RET_DK // 2
    lane = jnp.arange(LANES)
    upper = (lane % RET_DK) >= half
    freqs = (ROPE_BASE ** (-jnp.arange(half, dtype=F32) / half))[lane % half]
    ang = pos[:, None] * freqs[None, :]
    cos_t, sin = jnp.cos(ang), jnp.sin(ang)
    sa_t = jnp.where(upper[None, :], sin, 0.0)
    sb_t = jnp.where(upper[None, :], 0.0, -sin)
    lg = jnp.repeat(jnp.log1p(-jnp.exp2(-5.0 - jnp.arange(RET_HEADS, dtype=F32))), RET_DK)
    steps = ROWS // batch
    i1 = (jnp.arange(ROWS) // batch).astype(F32) + 1.0
    qdec = jnp.exp(i1[:, None] * lg[None, :])
    kdec = jnp.exp(-i1[:, None] * lg[None, :]) * (RET_DK ** -0.5)
    cdec = jnp.exp(steps * lg)[None, :]
    return cos_t, sa_t, sb_t, qdec, kdec, cdec


def _s5_disc_kernel(are_ref, aim_ref, ldt_ref, bre_ref, bim_ref, abre_ref, abim_ref, bbre_ref, bbim_ref):
    ar, ai = are_ref[...], aim_ref[...]
    dt = jnp.exp(ldt_ref[...])
    dar, dai = dt * ar, dt * ai
    mag = jnp.exp(dar)
    abar_re, abar_im = mag * jnp.cos(dai), mag * jnp.sin(dai)
    den = ar * ar + ai * ai
    nr, ni = abar_re - 1.0, abar_im
    f_re = (nr * ar + ni * ai) / den
    f_im = (ni * ar - nr * ai) / den
    abre_ref[...] = abar_re
    abim_ref[...] = abar_im
    br, bi = bre_ref[...], bim_ref[...]
    bbre_ref[...] = f_re[:, None, :] * br - f_im[:, None, :] * bi
    bbim_ref[...] = f_re[:, None, :] * bi + f_im[:, None, :] * br


def _s5_discretise(a_re, a_im, log_dt, b_re_t, b_im_t):
    lg = a_re.shape[0]
    small = jax.ShapeDtypeStruct((lg, SSM_P), F32)
    big = jax.ShapeDtypeStruct((lg, SSM_GC, SSM_P), F32)
    return pl.pallas_call(_s5_disc_kernel, out_shape=[small, small, big, big], name="s5_disc")(
        a_re, a_im, log_dt, b_re_t, b_im_t)


def _s5_kernel(batch, u_ref, bmat_ref, cmat_ref, are_ref, aim_ref, d_ref, h0re_ref, h0im_ref,
               y_ref, xre_ref, xim_ref, x_scr):
    c = pl.program_id(0)
    steps = ROWS // batch
    half = SSM_SLAB_N

    @pl.when(c == 0)
    def _():
        xre_ref[...] = h0re_ref[...]
        xim_ref[...] = h0im_ref[...]

    u = u_ref[...]
    ub = u.astype(BF16)
    for s in range(SSM_SLABS):
        x_scr[:, 2 * half * s:2 * half * (s + 1)] = _dot(ub[:, s * LANES:(s + 1) * LANES], bmat_ref[s])

    for s in range(SSM_SLABS):
        re0 = 2 * half * s
        im0 = re0 + half
        sc = slice(half * s, half * (s + 1))
        ar = jnp.broadcast_to(are_ref[:, sc], (SUBLANES, half))
        ai = jnp.broadcast_to(aim_ref[:, sc], (SUBLANES, half))

        def row_tile(rt, carry, re0=re0, im0=im0, sc=sc, ar=ar, ai=ai):
            r0 = pl.multiple_of(rt * SUBLANES, SUBLANES)

            def step(t, x):
                xr, xi = x
                row = pl.multiple_of(t * batch + r0, SUBLANES)
                nr = ar * xr - ai * xi + x_scr[pl.ds(row, SUBLANES), re0:re0 + half]
                ni = ar * xi + ai * xr + x_scr[pl.ds(row, SUBLANES), im0:im0 + half]
                x_scr[pl.ds(row, SUBLANES), re0:re0 + half] = nr
                x_scr[pl.ds(row, SUBLANES), im0:im0 + half] = ni
                return nr, ni

            init = (xre_ref[pl.ds(r0, SUBLANES), sc], xim_ref[pl.ds(r0, SUBLANES), sc])
            xr, xi = lax.fori_loop(0, steps, step, init, unroll=8)
            xre_ref[pl.ds(r0, SUBLANES), sc] = xr
            xim_ref[pl.ds(r0, SUBLANES), sc] = xi
            return carry

        lax.fori_loop(0, batch // SUBLANES, row_tile, 0)

    ys = [_dot(x_scr[:, 2 * half * s:2 * half * (s + 1)].astype(BF16), cmat_ref[s]) for s in range(SSM_SLABS)]
    y = jnp.concatenate(ys, axis=1) + d_ref[...] * u
    y_ref[...] = jax.nn.gelu(y)


def _s5(z, nblk, batch, bmat, cmat, abre, abim, d, h0re, h0im):
    const2 = lambda c: (0, 0)
    const3 = lambda c: (0, 0, 0)
    st = pl.BlockSpec((batch, SSM_N), const2)
    return pl.pallas_call(
        functools.partial(_s5_kernel, batch),
        grid=(nblk,),
        in_specs=[pl.BlockSpec((ROWS, SSM_W), lambda c: (c, 4)),
                  pl.BlockSpec((SSM_SLABS, LANES, 2 * SSM_SLAB_N), const3),
                  pl.BlockSpec((SSM_SLABS, 2 * SSM_SLAB_N, LANES), const3),
                  pl.BlockSpec((1, SSM_N), const2),
                  pl.BlockSpec((1, SSM_N), const2),
                  pl.BlockSpec((1, SSM_W), const2),
                  st, st],
        out_specs=[pl.BlockSpec((ROWS, SSM_W), lambda c: (c, 0)), st, st],
        out_shape=[jax.ShapeDtypeStruct((nblk * ROWS, SSM_W), F32),
                   jax.ShapeDtypeStruct((batch, SSM_N), F32),
                   jax.ShapeDtypeStruct((batch, SSM_N), F32)],
        scratch_shapes=[pltpu.VMEM((ROWS, 2 * SSM_N), F32)],
        compiler_params=_params(("arbitrary",), 48),
        name="s5",
    )(z, bmat, cmat, abre, abim, d, h0re, h0im)


def _block_diag_slabs(w, nslab):
    gps = SSM_G // nslab
    eye = jnp.eye(gps, dtype=w.dtype)
    w4 = w.reshape(nslab, gps, w.shape[1], w.shape[2])
    out = w4[:, :, :, None, :] * eye[None, :, None, :, None]
    return out.reshape(nslab, gps * w.shape[1], gps * w.shape[2])


def _merge_kernel(batch, x_ref, oa_ref, ys_ref, bg_ref, cg_ref, hc_ref, buf0_ref, cw_ref, cb_ref,
                  ga_ref, gb_ref, gc_ref, wr_ref, wa_ref, wb_ref, wc_ref, wm_ref, gf_ref,
                  x1_ref, xn_ref, buf_ref, zp_scr):
    rb = x_ref.shape[0]
    pad = (CONV_K - 1) * batch

    @pl.when(pl.program_id(0) == 0)
    def _():
        zp_scr[0:pad, :] = buf0_ref[...]

    zc = cg_ref[...] * hc_ref[...]
    zp_scr[pad:pad + rb, :] = zc
    y = cb_ref[...]
    for j in range(CONV_K):
        y = y + cw_ref[j:j + 1, :] * zp_scr[j * batch:j * batch + rb, :]
    oc_pre = bg_ref[...] * y
    tail = zp_scr[rb:rb + pad, :]
    buf_ref[...] = tail
    zp_scr[0:pad, :] = tail

    oa = _dot(oa_ref[...].astype(BF16), wr_ref[...])
    ysb = ys_ref[...].astype(BF16)
    ob = _dot(ysb, wa_ref[...]) * jax.nn.sigmoid(_dot(ysb, wb_ref[...]))
    oc = _dot(oc_pre.astype(BF16), wc_ref[...])
    merged = (jax.nn.sigmoid(ga_ref[...]) * oa + jax.nn.sigmoid(gb_ref[...]) * ob
              + jax.nn.sigmoid(gc_ref[...]) * oc)
    x1 = x_ref[...] + _dot(merged.astype(BF16), wm_ref[...])
    x1_ref[...] = x1
    xn_ref[...] = _rms(x1, gf_ref[...]).astype(BF16)


def _merge(x, z, oa, ys, batch, buf0, conv_w, conv_b, wr, wa, wb, wc, wm, gf):
    t = x.shape[0]
    rb = 512
    pad = (CONV_K - 1) * batch
    assert pad <= rb
    row = lambda w: pl.BlockSpec((rb, w), lambda i: (i, 0))
    zcol = lambda w, j: pl.BlockSpec((rb, w), lambda i: (i, j))
    const = lambda r, w: pl.BlockSpec((r, w), lambda i: (0, 0))
    return pl.pallas_call(
        functools.partial(_merge_kernel, batch),
        grid=(t // rb,),
        in_specs=[row(D_MODEL), row(RET_W), row(SSM_W),
                  zcol(CONV_W, 5), zcol(CONV_W, 6), zcol(CONV_W, 7),
                  const(pad, CONV_W), const(CONV_K, CONV_W), const(1, CONV_W),
                  zcol(D_MODEL, 4), zcol(D_MODEL, 5), zcol(D_MODEL, 6),
                  const(RET_W, D_MODEL), const(SSM_W, D_MODEL), const(SSM_W, D_MODEL), const(CONV_W, D_MODEL),
                  const(D_MODEL, D_MODEL), const(1, D_MODEL)],
        out_specs=[row(D_MODEL), row(D_MODEL), const(pad, CONV_W)],
        out_shape=[jax.ShapeDtypeStruct((t, D_MODEL), F32), jax.ShapeDtypeStruct((t, D_MODEL), BF16),
                   jax.ShapeDtypeStruct((pad, CONV_W), F32)],
        scratch_shapes=[pltpu.VMEM((rb + pad, CONV_W), F32)],
        compiler_params=_params(("arbitrary",), 48),
        name="merge",
    )(x, oa, ys, z, z, z, buf0, conv_w, conv_b, z, z, z, wr, wa, wb, wc, wm, gf)


def _tree(items, combine):
    while len(items) > 1:
        nxt = [combine(items[i], items[i + 1]) for i in range(0, len(items) - 1, 2)]
        if len(items) % 2:
            nxt.append(items[-1])
        items = nxt
    return items[0]


def _first_max(x, y):
    (vx, ix), (vy, iy) = x, y
    return jnp.maximum(vx, vy), jnp.where(vx >= vy, ix, iy)


def _bits(x, n):
    out, rest = [], x
    for _ in range(n):
        half = jnp.floor(rest * 0.5)
        out.append(rest - 2.0 * half == 1.0)
        rest = half
    return out


def _mux(vals, bits):
    level = list(vals)
    for bit in bits:
        level = [jnp.where(bit, level[j + 1], level[j]) for j in range(0, len(level), 2)]
    return level[0]


def _top16_of_keys(s_scrs, gv_scrs, gi_scrs, v_scrs, i_scrs):
    grp = TOPK_GROUP
    ngrp = PEER_NKEYS // grp
    nbits = ngrp.bit_length() - 1

    for s_scr, gv, gi in zip(s_scrs, gv_scrs, gi_scrs):
        for g in range(ngrp):
            gv[g], gi[g] = _tree([(s_scr[g * grp + p], float(g * grp + p)) for p in range(grp)], _first_max)

    def body(r, carry):
        for s_scr, gv, gi, v_scr, i_scr in zip(s_scrs, gv_scrs, gi_scrs, v_scrs, i_scrs):
            m, idx = _tree([(gv[g], gi[g]) for g in range(ngrp)], _first_max)
            v_scr[r] = m
            i_scr[r] = idx
            gid = jnp.floor(idx * (1.0 / grp))
            rel = idx - gid * grp
            bits = _bits(gid, nbits)
            cands = []
            for p in range(grp):
                val = _mux([s_scr[g * grp + p] for g in range(ngrp)], bits)
                left = (val < m) | ((val == m) & (rel < float(p)))
                cands.append((jnp.where(left, val, -jnp.inf), float(p)))
            nv, npos = _tree(cands, _first_max)
            ni = gid * grp + npos
            for g in range(ngrp):
                hit = gid == float(g)
                gv[g] = jnp.where(hit, nv, gv[g])
                gi[g] = jnp.where(hit, ni, gi[g])
        return carry

    lax.fori_loop(0, PEER_TOPK, body, 0)


def _select_kernel(tb, xn_ref, wq_ref, k1_ref, k2_ref, e1_ref, e2_ref, g_ref,
                   s1_scr, s2_scr, gv1_scr, gi1_scr, gv2_scr, gi2_scr, v1_scr, i1_scr, v2_scr, i2_scr,
                   hv_scr, hb_scr, sc_scr, se1_scr, se2_scr):
    q = _dot(xn_ref[...], wq_ref[...]).astype(BF16)
    hq = PEER_HEADS * PEER_DQ // 2
    s1 = _dot_nt(k1_ref[...], q[:, :hq])
    s2 = _dot_nt(k2_ref[...], q[:, hq:])
    kbits = PEER_TOPK.bit_length() - 1
    for lt in range(tb // LANES):
        lanes = slice(lt * LANES, (lt + 1) * LANES)
        s1_scr[...] = s1[:, lanes].reshape(PEER_NKEYS, SUBLANES, LANES)
        s2_scr[...] = s2[:, lanes].reshape(PEER_NKEYS, SUBLANES, LANES)
        _top16_of_keys((s1_scr, s2_scr), (gv1_scr, gv2_scr), (gi1_scr, gi2_scr), (v1_scr, v2_scr), (i1_scr, i2_scr))

        for a in range(PEER_TOPK):
            hv_scr[a] = v1_scr[a] + v2_scr[0]
            hb_scr[a] = jnp.zeros((SUBLANES, LANES), F32)

        def body(r, carry):
            m, a_sel = _tree([(hv_scr[a], float(a)) for a in range(PEER_TOPK)], _first_max)
            abits = _bits(a_sel, kbits)
            b_sel = _mux([hb_scr[a] for a in range(PEER_TOPK)], abits)
            bbits = _bits(b_sel, kbits)
            sc_scr[r] = m
            se1_scr[r] = _mux([i1_scr[a] for a in range(PEER_TOPK)], abits)
            se2_scr[r] = _mux([i2_scr[b] for b in range(PEER_TOPK)], bbits)
            nb = b_sel + 1.0
            v2_next = _mux([v2_scr[(b + 1) % PEER_TOPK] for b in range(PEER_TOPK)], bbits)
            v1_sel = _mux([v1_scr[a] for a in range(PEER_TOPK)], abits)
            live = (a_sel + 1.0) * (nb + 1.0) <= float(PEER_TOPK)
            nv = jnp.where(live, v1_sel + v2_next, -jnp.inf)
            for a in range(PEER_TOPK):
                hit = a_sel == float(a)
                hv_scr[a] = jnp.where(hit, nv, hv_scr[a])
                hb_scr[a] = jnp.where(hit, nb, hb_scr[a])
            return carry

        lax.fori_loop(0, PEER_TOPK, body, 0)
        sc = sc_scr[...]
        ex = jnp.exp(sc - jnp.max(sc, axis=0, keepdims=True))
        gate = ex / jnp.sum(ex, axis=0, keepdims=True)
        rows = slice(lt * LANES, (lt + 1) * LANES)
        g_ref[rows, :] = gate.reshape(PEER_SLOTS, LANES).T
        e1_ref[rows, :] = se1_scr[...].reshape(PEER_SLOTS, LANES).T
        e2_ref[rows, :] = se2_scr[...].reshape(PEER_SLOTS, LANES).T


def _peer_select(xn, wq, k1big, k2big):
    t = xn.shape[0]
    tb = 256
    hq = PEER_HEADS * PEER_DQ // 2
    nk = PEER_NKEYS * PEER_HEADS
    const = lambda i: (0, 0)
    row = lambda dt: jax.ShapeDtypeStruct((t, PEER_SLOTS), dt)
    vec = lambda n: pltpu.VMEM((n, SUBLANES, LANES), F32)
    return pl.pallas_call(
        functools.partial(_select_kernel, tb),
        grid=(t // tb,),
        in_specs=[pl.BlockSpec((tb, D_MODEL), lambda i: (i, 0)),
                  pl.BlockSpec((D_MODEL, 2 * hq), const),
                  pl.BlockSpec((nk, hq), const),
                  pl.BlockSpec((nk, hq), const)],
        out_specs=[pl.BlockSpec((tb, PEER_SLOTS), lambda i: (i, 0))] * 3,
        out_shape=[row(F32), row(F32), row(F32)],
        scratch_shapes=[vec(PEER_NKEYS), vec(PEER_NKEYS)] + [vec(PEER_NKEYS // TOPK_GROUP)] * 4
        + [vec(PEER_TOPK), vec(PEER_TOPK), vec(PEER_TOPK), vec(PEER_TOPK),
                        vec(PEER_TOPK), vec(PEER_TOPK), vec(PEER_TOPK), vec(PEER_TOPK), vec(PEER_TOPK)],
        compiler_params=_params(("parallel",), 40),
        name="peer_select",
    )(xn, wq, k1big, k2big)


def _peer_kernel(tb, eb, stride, final_norm, out_batch, xn_ref, e1_ref, e2_ref, g_ref, ut_ref, v_ref, x1_ref,
                 gain_ref, out_ref, m_scr, *acc_scr):
    e = pl.program_id(1)
    nk1 = eb // PEER_NKEYS
    acc_ref = acc_scr[0] if out_batch else out_ref

    @pl.when(e == 0)
    def _():
        acc_ref[...] = x1_ref[...]

    @pl.when(e == 0)
    def _():
        key = lax.broadcasted_iota(jnp.int32, (PEER_NKEYS, PEER_SLOTS), 0).astype(F32)

        def token(t, carry):
            e1 = e1_ref[pl.ds(t, 1), :]
            e2 = e2_ref[pl.ds(t, 1), :]
            gt = 0.5 * g_ref[pl.ds(t, 1), :]
            a_t = jnp.where(key == e1, gt, 0.0).astype(BF16)
            b_t = jnp.where(key == e2, 1.0, 0.0).astype(BF16)
            m_scr[pl.ds(t, PEER_NKEYS, stride=stride), :] = _dot_nt(a_t, b_t)
            return carry

        lax.fori_loop(0, tb, token, 0, unroll=128)

    s = _dot(xn_ref[...], ut_ref[...])
    t = jnp.tanh(s * (GELU_C + (GELU_C * GELU_A) * (s * s)))
    k1 = e * nk1
    gates = [m_scr[pl.ds(pl.multiple_of((k1 + i) * stride, SUBLANES), tb), :] for i in range(nk1)]
    w = ((s + s * t) * jnp.concatenate(gates, axis=1)).astype(BF16)
    acc_ref[...] += _dot(w, v_ref[...])

    if final_norm or out_batch:
        @pl.when(e == pl.num_programs(1) - 1)
        def _():
            y = acc_ref[...]
            if final_norm:
                y = _rms(y, gain_ref[...])
            out_ref[...] = pltpu.einshape("(tb)d->btd", y, b=out_batch) if out_batch else y


def _peer_dense(xn, e1, e2, g, ut, v, x1, layer, gain, final_norm, out_batch=0):
    t = xn.shape[0]
    tb, eb = 512, 2048
    stride = tb + SUBLANES
    once = pl.Buffered(1)
    tok = lambda w: pl.BlockSpec((tb, w), lambda i, e: (i, 0), pipeline_mode=once)
    tab = pl.BlockSpec((None, eb, D_MODEL), lambda i, e: (layer, e, 0))
    if out_batch:
        out_spec = pl.BlockSpec((out_batch, tb // out_batch, D_MODEL), lambda i, e: (0, i, 0), pipeline_mode=once)
        out_shape = jax.ShapeDtypeStruct((out_batch, t // out_batch, D_MODEL), F32)
        acc = [pltpu.VMEM((tb, D_MODEL), F32)]
    else:
        out_spec = pl.BlockSpec((tb, D_MODEL), lambda i, e: (i, 0), pipeline_mode=once)
        out_shape = jax.ShapeDtypeStruct((t, D_MODEL), F32)
        acc = []
    return pl.pallas_call(
        functools.partial(_peer_kernel, tb, eb, stride, final_norm, out_batch),
        grid=(t // tb, PEER_NEXP // eb),
        in_specs=[tok(D_MODEL), tok(PEER_SLOTS), tok(PEER_SLOTS), tok(PEER_SLOTS),
                  pl.BlockSpec((None, D_MODEL, eb), lambda i, e: (layer, 0, e)), tab, tok(D_MODEL),
                  pl.BlockSpec((1, D_MODEL), lambda i, e: (0, 0))],
        out_specs=out_spec,
        out_shape=out_shape,
        scratch_shapes=[pltpu.VMEM((PEER_NKEYS * stride, PEER_NKEYS), F32)] + acc,
        compiler_params=_params(("parallel", "arbitrary"), 62),
        name="peer_dense",
    )(xn, e1, e2, g, ut, v, x1, gain)


def _time_major(x):
    b, s, d = x.shape
    return x.transpose(1, 0, 2).reshape(s * b, d)


def _batch_major(y, b, s):
    return y.reshape(s, b, y.shape[-1]).transpose(1, 0, 2)


def kernel(x_prompt, x_sample, state_ret, state_ssm_re, state_ssm_im, state_conv, norm_mix, w_in, ret_norm, w_ret_out, ssm_a_re, ssm_a_im, ssm_b_re, ssm_b_im, ssm_c_re, ssm_c_im, ssm_d, ssm_log_dt, w_glu_a, w_glu_b, conv_w, conv_b, w_conv_out, w_mix_out, norm_ffn, peer_wq, peer_k1, peer_k2, peer_u, peer_v, norm_final):
    bp, sp, _ = x_prompt.shape
    bs, ss, _ = x_sample.shape
    tp, ts = bp * sp, bs * ss
    depth = w_in.shape[0]
    assert tp % ROWS == 0 and ts == ROWS and ROWS % bp == 0 and ROWS // bp == math.gcd(sp, RET_CHUNK)

    xs = [x_prompt, _time_major(x_sample)]
    batches = (bp, bs)
    pos = (jnp.arange(sp, dtype=F32), PAST_LEN + jnp.arange(ss, dtype=F32))
    ret_tabs = [_retention_tables(p, b) for p, b in zip(pos, batches)]

    lg = depth * SSM_G
    abre, abim, bbre, bbim = _s5_discretise(
        ssm_a_re.reshape(lg, SSM_P), ssm_a_im.reshape(lg, SSM_P), ssm_log_dt.reshape(lg, 1),
        ssm_b_re.transpose(0, 1, 3, 2).reshape(lg, SSM_GC, SSM_P),
        ssm_b_im.transpose(0, 1, 3, 2).reshape(lg, SSM_GC, SSM_P))

    hq = PEER_DQ // 2
    eye = jnp.eye(PEER_HEADS, dtype=F32)

    def keys_block_diag(k):
        return (k.transpose(1, 0, 2)[:, :, None, :] * eye[None, :, :, None]).reshape(
            PEER_NKEYS * PEER_HEADS, PEER_HEADS * hq).astype(BF16)

    w_in_b = w_in.astype(BF16)
    ut = peer_u.astype(BF16).transpose(0, 2, 1)
    vt = peer_v.astype(BF16)
    zero_ret = jnp.zeros((1, bp, RET_HEADS, RET_DK, RET_DK), F32)

    states = [[[] for _ in range(3)] for _ in range(2)]
    ret_stacked = [None, None]
    for l in range(depth):
        sl = slice(l * SSM_G, (l + 1) * SSM_G)
        bmat = jnp.concatenate([_block_diag_slabs(bbre[sl], SSM_SLABS), _block_diag_slabs(bbim[sl], SSM_SLABS)],
                               axis=2).astype(BF16)
        cmat = jnp.concatenate([_block_diag_slabs(ssm_c_re[l].transpose(0, 2, 1), SSM_SLABS),
                                _block_diag_slabs(-ssm_c_im[l].transpose(0, 2, 1), SSM_SLABS)],
                               axis=1).astype(BF16)
        are_row = abre[sl].reshape(1, SSM_N)
        aim_row = abim[sl].reshape(1, SSM_N)
        proj = [w.astype(BF16) for w in (w_ret_out[l], w_glu_a[l], w_glu_b[l], w_conv_out[l], w_mix_out[l])]
        wq = peer_wq[l].reshape(D_MODEL, PEER_HEADS, 2, hq).transpose(0, 2, 1, 3).reshape(D_MODEL, -1).astype(BF16)
        k1big, k2big = keys_block_diag(peer_k1[l]), keys_block_diag(peer_k2[l])

        for gi, batch in enumerate(batches):
            x = xs[gi]
            nblk = x.size // (ROWS * D_MODEL)
            if gi == 0:
                s0, s0_layer = zero_ret, 0
                h0re = jnp.zeros((batch, SSM_N), F32)
                h0im = jnp.zeros((batch, SSM_N), F32)
                buf0 = jnp.zeros(((CONV_K - 1) * batch, CONV_W), F32)
                bblk = batch
            else:
                s0, s0_layer = state_ret, l
                h0re = state_ssm_re[l].reshape(batch, SSM_N)
                h0im = state_ssm_im[l].reshape(batch, SSM_N)
                buf0 = state_conv[l].transpose(1, 0, 2).reshape((CONV_K - 1) * batch, CONV_W)
                bblk = 16
            z = _inproj(x, norm_mix[l][None, :], w_in_b, l)
            if x.ndim == 3:
                z, x = z
            oa, ret_stacked[gi] = _retention(z, nblk, batch, bblk, ret_tabs[gi], ret_norm[l][None, :], s0, s0_layer,
                                          ret_stacked[gi], l, depth)
            ys, xre, xim = _s5(z, nblk, batch, bmat, cmat, are_row, aim_row, ssm_d[l][None, :], h0re, h0im)
            st = states[gi]
            st[0].append(xre.reshape(batch, SSM_G, SSM_P))
            st[1].append(xim.reshape(batch, SSM_G, SSM_P))

            x1, xn, buf = _merge(x, z, oa, ys, batch, buf0, conv_w[l], conv_b[l][None, :], *proj, norm_ffn[l][None, :])
            st[2].append(buf.reshape(CONV_K - 1, batch, CONV_W).transpose(1, 0, 2))
            e1, e2, g = _peer_select(xn, wq, k1big, k2big)
            last = l == depth - 1
            xs[gi] = _peer_dense(xn, e1, e2, g, ut, vt, x1, l, norm_final[None, :], last,
                                 out_batch=bp if last and gi == 0 else 0)

    y_prompt = xs[0]
    y_sample = _batch_major(xs[1], bs, ss)
    (re_p, im_p, cv_p), (re_s, im_s, cv_s) = states
    return (y_prompt, y_sample,
            ret_stacked[0], ret_stacked[1],
            jnp.stack(re_p), jnp.stack(re_s),
            jnp.stack(im_p), jnp.stack(im_s),
            jnp.stack(cv_p), jnp.stack(cv_s))
```

```python
import functools
import math

import jax
import jax.numpy as jnp
from jax import lax
from jax.experimental import pallas as pl
from jax.experimental.pallas import tpu as pltpu

F32 = jnp.float32
BF16 = jnp.bfloat16

D_MODEL = 1024
PAST_LEN = 16384
RET_HEADS = 8
RET_DK = 64
RET_W = 512
RET_CHUNK = 128
ROPE_BASE = 10000.0
SSM_W = 512
SSM_GC = 16
SSM_G = 32
SSM_P = 64
SSM_N = SSM_G * SSM_P
SSM_SLABS = 4
SSM_SLAB_N = SSM_N // SSM_SLABS
CONV_W = 512
CONV_K = 3
PROJ_W = 7168
PEER_HEADS = 8
PEER_DQ = 256
PEER_NKEYS = 128
PEER_TOPK = 16
PEER_NEXP = PEER_NKEYS ** 2
PEER_SLOTS = PEER_HEADS * PEER_TOPK
TOPK_GROUP = 8
EPS = 1e-6
GELU_C = math.sqrt(2.0 / math.pi)
GELU_A = 0.044715

ROWS = 1024
LANES = 128
SUBLANES = 8
MIB = 1024 * 1024


def _params(sem, vmem_mib):
    return pltpu.CompilerParams(dimension_semantics=sem, vmem_limit_bytes=vmem_mib * MIB)


def _rms(x, g):
    return x * lax.rsqrt(jnp.mean(x * x, axis=-1, keepdims=True) + EPS) * g


def _sigmoid(x):
    return 0.5 * jnp.tanh(0.5 * x) + 0.5


def _gelu_x2(x):
    return x + x * jnp.tanh(x * (GELU_C + (GELU_C * GELU_A) * (x * x)))


def _dot(a, b):
    return jnp.dot(a, b, preferred_element_type=F32)


def _dot_nt(a, b):
    return lax.dot_general(a, b, (((1,), (1,)), ((), ())), preferred_element_type=F32)


def _inproj_kernel(x_ref, g_ref, w_ref, z_ref, h_scr):
    @pl.when(pl.program_id(1) == 0)
    def _():
        h_scr[...] = _rms(x_ref[...], g_ref[...]).astype(BF16)

    z_ref[...] = _dot(h_scr[...], w_ref[...])


def _inproj_bm_kernel(x_ref, g_ref, w_ref, z_ref, xtm_ref, h_scr):
    @pl.when(pl.program_id(1) == 0)
    def _():
        x = pltpu.einshape("btd->(tb)d", x_ref[...])
        xtm_ref[...] = x
        h_scr[...] = _rms(x, g_ref[...]).astype(BF16)

    z_ref[...] = _dot(h_scr[...], w_ref[...])


def _inproj(x, g, w, layer):
    batch_major = x.ndim == 3
    nb = PROJ_W // 4
    if batch_major:
        batch, seq, _ = x.shape
        t = batch * seq
        x_spec = pl.BlockSpec((batch, ROWS // batch, D_MODEL), lambda i, j: (0, i, 0))
    else:
        t = x.shape[0]
        x_spec = pl.BlockSpec((ROWS, D_MODEL), lambda i, j: (i, 0))
    z_spec = pl.BlockSpec((ROWS, nb), lambda i, j: (i, j))
    z_shape = jax.ShapeDtypeStruct((t, PROJ_W), F32)
    return pl.pallas_call(
        _inproj_bm_kernel if batch_major else _inproj_kernel,
        grid=(t // ROWS, PROJ_W // nb),
        in_specs=[x_spec,
                  pl.BlockSpec((1, D_MODEL), lambda i, j: (0, 0)),
                  pl.BlockSpec((None, D_MODEL, nb), lambda i, j: (layer, 0, j))],
        out_specs=[z_spec, pl.BlockSpec((ROWS, D_MODEL), lambda i, j: (i, 0))] if batch_major else z_spec,
        out_shape=[z_shape, jax.ShapeDtypeStruct((t, D_MODEL), F32)] if batch_major else z_shape,
        scratch_shapes=[pltpu.VMEM((ROWS, D_MODEL), BF16)],
        compiler_params=_params(("parallel", "arbitrary"), 48),
        name="inproj",
    )(x, g, w)


def _ret_kernel(batch, bblk, aliased, out_layer, *refs):
    refs = [r for i, r in enumerate(refs) if not (aliased and i == 9)]
    (z_ref, cos_ref, sa_ref, sb_ref, qdec_ref, kdec_ref, cdec_ref, gn_ref, s0_ref,
     o_ref, st_ref, qd_scr, kd_scr, v_scr, mask_scr, oacc_scr) = refs
    s_ref = st_ref if aliased else st_ref.at[out_layer]
    bb = pl.program_id(0)
    c = pl.program_id(1)
    steps = ROWS // batch
    nslab = RET_W // LANES

    def head_view(ref, h):
        return ref[h // 2, :, (h % 2) * RET_DK:(h % 2 + 1) * RET_DK]

    seq_local = steps >= LANES
    msize = steps if seq_local else ROWS

    @pl.when((bb == 0) & (c == 0))
    def _():
        r = lax.broadcasted_iota(jnp.int32, (msize, msize), 0)
        cc = lax.broadcasted_iota(jnp.int32, (msize, msize), 1)
        if seq_local:
            mask_scr[...] = (r >= cc).astype(F32)
        else:
            same = (r & (batch - 1)) == (cc & (batch - 1))
            mask_scr[...] = (same & (r >= cc)).astype(F32)

    @pl.when(c == 0)
    def _():
        s_ref[...] = s0_ref[...]
        if not aliased:
            for k in range(st_ref.shape[0]):
                if k != out_layer:
                    st_ref[k] = jnp.zeros(st_ref.shape[1:], F32)

    @pl.when(bb == 0)
    def _():
        def per_row(t_ref):
            return jnp.broadcast_to(t_ref[...][:, None, :], (steps, batch, LANES)).reshape(ROWS, LANES)

        cos, sa, sb = per_row(cos_ref), per_row(sa_ref), per_row(sb_ref)

        def rot(x):
            return x * cos + pltpu.roll(x, 32, 1) * sa + pltpu.roll(x, 96, 1) * sb

        for s in range(nslab):
            cols = slice(s * LANES, (s + 1) * LANES)
            qd_scr[s] = rot(z_ref[:, cols]) * qdec_ref[:, cols]
            kd_scr[s] = rot(z_ref[:, RET_W + s * LANES:RET_W + (s + 1) * LANES]) * kdec_ref[:, cols]
            v_scr[s] = z_ref[:, 2 * RET_W + s * LANES:2 * RET_W + (s + 1) * LANES]
        if not seq_local:
            for s in range(nslab):
                outs = []
                for h in (2 * s, 2 * s + 1):
                    qh = head_view(qd_scr, h).astype(BF16)
                    kh = head_view(kd_scr, h).astype(BF16)
                    vh = head_view(v_scr, h).astype(BF16)
                    p = (_dot_nt(qh, kh) * mask_scr[...]).astype(BF16)
                    outs.append(_dot(p, vh))
                oacc_scr[s] = jnp.concatenate(outs, axis=1)

    def per_seq(bl, carry):
        b = bb * bblk + bl
        rows = pl.ds(b, steps, stride=batch)
        heads = range(RET_HEADS)

        def head_cols(scr):
            slabs = [scr[s, rows, :] for s in range(nslab)]
            return [slabs[h // 2][:, (h % 2) * RET_DK:(h % 2 + 1) * RET_DK].astype(BF16) for h in heads]

        q16, k16, v16 = head_cols(qd_scr), head_cols(kd_scr), head_cols(v_scr)
        st = [s_ref[bl, h] for h in heads]
        outs = [_dot(q16[h], st[h].astype(BF16)) for h in heads]
        if seq_local:
            sc = [_dot_nt(q16[h], k16[h]) for h in heads]
            p = [(sc[h] * mask_scr[...]).astype(BF16) for h in heads]
            outs = [_dot(p[h], v16[h]) + outs[h] for h in heads]
        upd = [lax.dot_general(k16[h], v16[h], (((0,), (0,)), ((), ())), preferred_element_type=F32) for h in heads]
        for h in heads:
            s_ref[bl, h] = (st[h] + upd[h]) * cdec_ref[:, h * RET_DK:(h + 1) * RET_DK]
        for s in range(nslab):
            o2 = jnp.concatenate(outs[2 * s:2 * s + 2], axis=1)
            oacc_scr[s, rows, :] = o2 if seq_local else oacc_scr[s, rows, :] + o2
        return carry

    lax.fori_loop(0, bblk, per_seq, 0, unroll=4)

    @pl.when(bb == pl.num_programs(0) - 1)
    def _():
        r = lax.broadcasted_iota(jnp.int32, (LANES, LANES), 0) // RET_DK
        cc = lax.broadcasted_iota(jnp.int32, (LANES, LANES), 1) // RET_DK
        avg = jnp.where(r == cc, 1.0 / RET_DK, 0.0).astype(BF16)

        def seg_mean(x):
            hi = x.astype(BF16)
            lo = (x - hi.astype(F32)).astype(BF16)
            return _dot(hi, avg) + _dot(lo, avg)

        normed = []
        for s in range(nslab):
            o2 = oacc_scr[s]
            dlt = o2 - seg_mean(o2)
            normed.append(dlt * lax.rsqrt(seg_mean(dlt * dlt) + EPS))
        o = jnp.concatenate(normed, axis=1) * gn_ref[...]
        g = z_ref[:, 3 * RET_W:4 * RET_W]
        o_ref[...] = g * _sigmoid(g) * o


def _retention(z, nblk, batch, bblk, tabs, gn, s0, layer, stacked, out_layer, depth):
    cos, sa, sb, qdec, kdec, cdec = tabs
    nbb = batch // bblk
    steps = ROWS // batch
    msize = steps if steps >= LANES else ROWS
    aliased = stacked is not None
    if aliased:
        st_spec = pl.BlockSpec((None, bblk, RET_HEADS, RET_DK, RET_DK), lambda bb, c: (out_layer, bb, 0, 0, 0))
    else:
        st_spec = pl.BlockSpec((depth, bblk, RET_HEADS, RET_DK, RET_DK), lambda bb, c: (0, bb, 0, 0, 0))
    st_in = pl.BlockSpec((None, bblk, RET_HEADS, RET_DK, RET_DK), lambda bb, c: (layer, bb, 0, 0, 0))
    const = lambda bb, c: (0, 0)
    return pl.pallas_call(
        functools.partial(_ret_kernel, batch, bblk, aliased, out_layer),
        grid=(nbb, nblk),
        in_specs=[pl.BlockSpec((ROWS, 4 * RET_W), lambda bb, c: (c, 0)),
                  pl.BlockSpec((steps, LANES), lambda bb, c: (c, 0)),
                  pl.BlockSpec((steps, LANES), lambda bb, c: (c, 0)),
                  pl.BlockSpec((steps, LANES), lambda bb, c: (c, 0)),
                  pl.BlockSpec((ROWS, RET_W), const),
                  pl.BlockSpec((ROWS, RET_W), const),
                  pl.BlockSpec((1, RET_W), const),
                  pl.BlockSpec((1, RET_W), const),
                  st_in] + ([pl.BlockSpec(memory_space=pl.ANY)] if aliased else []),
        out_specs=[pl.BlockSpec((ROWS, RET_W), lambda bb, c: (c, 0)), st_spec],
        out_shape=[jax.ShapeDtypeStruct((nblk * ROWS, RET_W), F32),
                   jax.ShapeDtypeStruct((depth, batch, RET_HEADS, RET_DK, RET_DK), F32)],
        input_output_aliases={9: 1} if aliased else {},
        scratch_shapes=[pltpu.VMEM((RET_W // LANES, ROWS, LANES), F32)] * 3
        + [pltpu.VMEM((msize, msize), F32), pltpu.VMEM((RET_W // LANES, ROWS, LANES), F32)],
        compiler_params=_params(("arbitrary", "arbitrary"), 56),
        name="retention",
    )(z, cos, sa, sb, qdec, kdec, cdec, gn, s0, *([stacked] if aliased else []))


def _retention_tables(pos, batch):
    half = RET_DK // 2
    lane = jnp.arange(LANES)
    upper = (lane % RET_DK) >= half
    freqs = (ROPE_BASE ** (-jnp.arange(half, dtype=F32) / half))[lane % half]
    ang = pos[:, None] * freqs[None, :]
    cos_t, sin = jnp.cos(ang), jnp.sin(ang)
    sa_t = jnp.where(upper[None, :], sin, 0.0)
    sb_t = jnp.where(upper[None, :], 0.0, -sin)
    lg = jnp.repeat(jnp.log1p(-jnp.exp2(-5.0 - jnp.arange(RET_HEADS, dtype=F32))), RET_DK)
    steps = ROWS // batch
    i1 = (jnp.arange(ROWS) // batch).astype(F32) + 1.0
    qdec = jnp.exp(i1[:, None] * lg[None, :])
    kdec = jnp.exp(-i1[:, None] * lg[None, :]) * (RET_DK ** -0.5)
    cdec = jnp.exp(steps * lg)[None, :]
    return cos_t, sa_t, sb_t, qdec, kdec, cdec


def _s5_disc_kernel(are_ref, aim_ref, ldt_ref, bre_ref, bim_ref, abre_ref, abim_ref, bbre_ref, bbim_ref):
    ar, ai = are_ref[...], aim_ref[...]
    dt = jnp.exp(ldt_ref[...])
    dar, dai = dt * ar, dt * ai
    mag = jnp.exp(dar)
    abar_re, abar_im = mag * jnp.cos(dai), mag * jnp.sin(dai)
    den = ar * ar + ai * ai
    nr, ni = abar_re - 1.0, abar_im
    f_re = (nr * ar + ni * ai) / den
    f_im = (ni * ar - nr * ai) / den
    abre_ref[...] = abar_re
    abim_ref[...] = abar_im
    br, bi = bre_ref[...], bim_ref[...]
    bbre_ref[...] = f_re[:, None, :] * br - f_im[:, None, :] * bi
    bbim_ref[...] = f_re[:, None, :] * bi + f_im[:, None, :] * br


def _s5_discretise(a_re, a_im, log_dt, b_re_t, b_im_t):
    lg = a_re.shape[0]
    small = jax.ShapeDtypeStruct((lg, SSM_P), F32)
    big = jax.ShapeDtypeStruct((lg, SSM_GC, SSM_P), F32)
    return pl.pallas_call(_s5_disc_kernel, out_shape=[small, small, big, big], name="s5_disc")(
        a_re, a_im, log_dt, b_re_t, b_im_t)


def _s5_kernel(batch, u_ref, bmat_ref, cmat_ref, are_ref, aim_ref, d_ref, h0re_ref, h0im_ref,
               y_ref, xre_ref, xim_ref, x_scr):
    c = pl.program_id(0)
    steps = ROWS // batch
    half = SSM_SLAB_N

    @pl.when(c == 0)
    def _():
        xre_ref[...] = h0re_ref[...]
        xim_ref[...] = h0im_ref[...]

    u = u_ref[...]
    ub = u.astype(BF16)
    for s in range(SSM_SLABS):
        x_scr[:, 2 * half * s:2 * half * (s + 1)] = _dot(ub[:, s * LANES:(s + 1) * LANES], bmat_ref[s])

    for s in range(SSM_SLABS):
        re0 = 2 * half * s
        im0 = re0 + half
        sc = slice(half * s, half * (s + 1))
        ar = jnp.broadcast_to(are_ref[:, sc], (SUBLANES, half))
        ai = jnp.broadcast_to(aim_ref[:, sc], (SUBLANES, half))

        def row_tile(rt, carry, re0=re0, im0=im0, sc=sc, ar=ar, ai=ai):
            r0 = pl.multiple_of(rt * SUBLANES, SUBLANES)

            def step(t, x):
                xr, xi = x
                row = pl.multiple_of(t * batch + r0, SUBLANES)
                nr = ar * xr - ai * xi + x_scr[pl.ds(row, SUBLANES), re0:re0 + half]
                ni = ar * xi + ai * xr + x_scr[pl.ds(row, SUBLANES), im0:im0 + half]
                x_scr[pl.ds(row, SUBLANES), re0:re0 + half] = nr
                x_scr[pl.ds(row, SUBLANES), im0:im0 + half] = ni
                return nr, ni

            init = (xre_ref[pl.ds(r0, SUBLANES), sc], xim_ref[pl.ds(r0, SUBLANES), sc])
            xr, xi = lax.fori_loop(0, steps, step, init, unroll=8)
            xre_ref[pl.ds(r0, SUBLANES), sc] = xr
            xim_ref[pl.ds(r0, SUBLANES), sc] = xi
            return carry

        lax.fori_loop(0, batch // SUBLANES, row_tile, 0)

    ys = [_dot(x_scr[:, 2 * half * s:2 * half * (s + 1)].astype(BF16), cmat_ref[s]) for s in range(SSM_SLABS)]
    y = jnp.concatenate(ys, axis=1) + d_ref[...] * u
    y_ref[...] = 0.5 * _gelu_x2(y)


def _s5(z, nblk, batch, bmat, cmat, abre, abim, d, h0re, h0im):
    const2 = lambda c: (0, 0)
    const3 = lambda c: (0, 0, 0)
    st = pl.BlockSpec((batch, SSM_N), const2)
    return pl.pallas_call(
        functools.partial(_s5_kernel, batch),
        grid=(nblk,),
        in_specs=[pl.BlockSpec((ROWS, SSM_W), lambda c: (c, 4)),
                  pl.BlockSpec((SSM_SLABS, LANES, 2 * SSM_SLAB_N), const3),
                  pl.BlockSpec((SSM_SLABS, 2 * SSM_SLAB_N, LANES), const3),
                  pl.BlockSpec((1, SSM_N), const2),
                  pl.BlockSpec((1, SSM_N), const2),
                  pl.BlockSpec((1, SSM_W), const2),
                  st, st],
        out_specs=[pl.BlockSpec((ROWS, SSM_W), lambda c: (c, 0)), st, st],
        out_shape=[jax.ShapeDtypeStruct((nblk * ROWS, SSM_W), F32),
                   jax.ShapeDtypeStruct((batch, SSM_N), F32),
                   jax.ShapeDtypeStruct((batch, SSM_N), F32)],
        scratch_shapes=[pltpu.VMEM((ROWS, 2 * SSM_N), F32)],
        compiler_params=_params(("arbitrary",), 48),
        name="s5",
    )(z, bmat, cmat, abre, abim, d, h0re, h0im)


def _block_diag_slabs(w, nslab):
    gps = SSM_G // nslab
    eye = jnp.eye(gps, dtype=w.dtype)
    w4 = w.reshape(nslab, gps, w.shape[1], w.shape[2])
    out = w4[:, :, :, None, :] * eye[None, :, None, :, None]
    return out.reshape(nslab, gps * w.shape[1], gps * w.shape[2])


def _merge_kernel(batch, x_ref, oa_ref, ys_ref, bg_ref, cg_ref, hc_ref, buf0_ref, cw_ref, cb_ref,
                  ga_ref, gb_ref, gc_ref, wr_ref, wa_ref, wb_ref, wc_ref, wm_ref, gf_ref,
                  x1_ref, xn_ref, buf_ref, zp_scr):
    rb = x_ref.shape[0]
    pad = (CONV_K - 1) * batch

    @pl.when(pl.program_id(0) == 0)
    def _():
        zp_scr[0:pad, :] = buf0_ref[...]

    zc = cg_ref[...] * hc_ref[...]
    zp_scr[pad:pad + rb, :] = zc
    y = cb_ref[...]
    for j in range(CONV_K):
        y = y + cw_ref[j:j + 1, :] * zp_scr[j * batch:j * batch + rb, :]
    oc_pre = bg_ref[...] * y
    tail = zp_scr[rb:rb + pad, :]
    buf_ref[...] = tail
    zp_scr[0:pad, :] = tail

    merged = _sigmoid(ga_ref[...]) * _dot(oa_ref[...].astype(BF16), wr_ref[...])
    ysb = ys_ref[...].astype(BF16)
    merged = merged + _sigmoid(gb_ref[...]) * (_dot(ysb, wa_ref[...]) * _sigmoid(_dot(ysb, wb_ref[...])))
    merged = merged + _sigmoid(gc_ref[...]) * _dot(oc_pre.astype(BF16), wc_ref[...])
    x1 = x_ref[...] + _dot(merged.astype(BF16), wm_ref[...])
    x1_ref[...] = x1
    xn_ref[...] = _rms(x1, gf_ref[...]).astype(BF16)


def _merge(x, z, oa, ys, batch, buf0, conv_w, conv_b, wr, wa, wb, wc, wm, gf):
    t = x.shape[0]
    rb = 512
    pad = (CONV_K - 1) * batch
    assert pad <= rb
    row = lambda w: pl.BlockSpec((rb, w), lambda i: (i, 0))
    zcol = lambda w, j: pl.BlockSpec((rb, w), lambda i: (i, j))
    const = lambda r, w: pl.BlockSpec((r, w), lambda i: (0, 0))
    return pl.pallas_call(
        functools.partial(_merge_kernel, batch),
        grid=(t // rb,),
        in_specs=[row(D_MODEL), row(RET_W), row(SSM_W),
                  zcol(CONV_W, 5), zcol(CONV_W, 6), zcol(CONV_W, 7),
                  const(pad, CONV_W), const(CONV_K, CONV_W), const(1, CONV_W),
                  zcol(D_MODEL, 4), zcol(D_MODEL, 5), zcol(D_MODEL, 6),
                  const(RET_W, D_MODEL), const(SSM_W, D_MODEL), const(SSM_W, D_MODEL), const(CONV_W, D_MODEL),
                  const(D_MODEL, D_MODEL), const(1, D_MODEL)],
        out_specs=[row(D_MODEL), row(D_MODEL), const(pad, CONV_W)],
        out_shape=[jax.ShapeDtypeStruct((t, D_MODEL), F32), jax.ShapeDtypeStruct((t, D_MODEL), BF16),
                   jax.ShapeDtypeStruct((pad, CONV_W), F32)],
        scratch_shapes=[pltpu.VMEM((rb + pad, CONV_W), F32)],
        compiler_params=_params(("arbitrary",), 48),
        name="merge",
    )(x, oa, ys, z, z, z, buf0, conv_w, conv_b, z, z, z, wr, wa, wb, wc, wm, gf)


def _tree(items, combine):
    while len(items) > 1:
        nxt = [combine(items[i], items[i + 1]) for i in range(0, len(items) - 1, 2)]
        if len(items) % 2:
            nxt.append(items[-1])
        items = nxt
    return items[0]


def _first_max(x, y):
    (vx, ix), (vy, iy) = x, y
    return jnp.maximum(vx, vy), jnp.where(vx >= vy, ix, iy)


def _bits(x, n):
    out, rest = [], x
    for _ in range(n):
        half = jnp.floor(rest * 0.5)
        out.append(rest - 2.0 * half == 1.0)
        rest = half
    return out


def _mux(vals, bits):
    level = list(vals)
    for bit in bits:
        level = [jnp.where(bit, level[j + 1], level[j]) for j in range(0, len(level), 2)]
    return level[0]


def _top16_of_keys(s_scrs, gv_scrs, gi_scrs, v_scrs, i_scrs):
    grp = TOPK_GROUP
    ngrp = PEER_NKEYS // grp
    nbits = ngrp.bit_length() - 1

    for s_scr, gv, gi in zip(s_scrs, gv_scrs, gi_scrs):
        for g in range(ngrp):
            gv[g], gi[g] = _tree([(s_scr[g * grp + p], float(g * grp + p)) for p in range(grp)], _first_max)

    def body(r, carry):
        for s_scr, gv, gi, v_scr, i_scr in zip(s_scrs, gv_scrs, gi_scrs, v_scrs, i_scrs):
            m, idx = _tree([(gv[g], gi[g]) for g in range(ngrp)], _first_max)
            v_scr[r] = m
            i_scr[r] = idx
            gid = jnp.floor(idx * (1.0 / grp))
            rel = idx - gid * grp
            bits = _bits(gid, nbits)
            cands = []
            for p in range(grp):
                val = _mux([s_scr[g * grp + p] for g in range(ngrp)], bits)
                left = (val < m) | ((val == m) & (rel < float(p)))
                cands.append((jnp.where(left, val, -jnp.inf), float(p)))
            nv, npos = _tree(cands, _first_max)
            ni = gid * grp + npos
            for g in range(ngrp):
                hit = gid == float(g)
                gv[g] = jnp.where(hit, nv, gv[g])
                gi[g] = jnp.where(hit, ni, gi[g])
        return carry

    lax.fori_loop(0, PEER_TOPK, body, 0)


def _select_kernel(tb, xn_ref, wq_ref, k1_ref, k2_ref, e1_ref, e2_ref, g_ref,
                   s1_scr, s2_scr, gv1_scr, gi1_scr, gv2_scr, gi2_scr, v1_scr, i1_scr, v2_scr, i2_scr,
                   hv_scr, hb_scr, sc_scr, se1_scr, se2_scr):
    q = _dot(xn_ref[...], wq_ref[...]).astype(BF16)
    hq = PEER_HEADS * PEER_DQ // 2
    s1 = _dot_nt(k1_ref[...], q[:, :hq])
    s2 = _dot_nt(k2_ref[...], q[:, hq:])
    kbits = PEER_TOPK.bit_length() - 1
    for lt in range(tb // LANES):
        lanes = slice(lt * LANES, (lt + 1) * LANES)
        s1_scr[...] = s1[:, lanes].reshape(PEER_NKEYS, SUBLANES, LANES)
        s2_scr[...] = s2[:, lanes].reshape(PEER_NKEYS, SUBLANES, LANES)
        _top16_of_keys((s1_scr, s2_scr), (gv1_scr, gv2_scr), (gi1_scr, gi2_scr), (v1_scr, v2_scr), (i1_scr, i2_scr))

        for a in range(PEER_TOPK):
            hv_scr[a] = v1_scr[a] + v2_scr[0]
            hb_scr[a] = jnp.zeros((SUBLANES, LANES), F32)

        def body(r, carry):
            m, a_sel = _tree([(hv_scr[a], float(a)) for a in range(PEER_TOPK)], _first_max)
            abits = _bits(a_sel, kbits)
            b_sel = _mux([hb_scr[a] for a in range(PEER_TOPK)], abits)
            bbits = _bits(b_sel, kbits)
            sc_scr[r] = m
            se1_scr[r] = _mux([i1_scr[a] for a in range(PEER_TOPK)], abits)
            se2_scr[r] = _mux([i2_scr[b] for b in range(PEER_TOPK)], bbits)
            nb = b_sel + 1.0
            v2_next = _mux([v2_scr[(b + 1) % PEER_TOPK] for b in range(PEER_TOPK)], bbits)
            v1_sel = _mux([v1_scr[a] for a in range(PEER_TOPK)], abits)
            live = (a_sel + 1.0) * (nb + 1.0) <= float(PEER_TOPK)
            nv = jnp.where(live, v1_sel + v2_next, -jnp.inf)
            for a in range(PEER_TOPK):
                hit = a_sel == float(a)
                hv_scr[a] = jnp.where(hit, nv, hv_scr[a])
                hb_scr[a] = jnp.where(hit, nb, hb_scr[a])
            return carry

        lax.fori_loop(0, PEER_TOPK, body, 0)
        sc = sc_scr[...]
        ex = jnp.exp(sc - jnp.max(sc, axis=0, keepdims=True))
        gate = ex / jnp.sum(ex, axis=0, keepdims=True)
        rows = slice(lt * LANES, (lt + 1) * LANES)
        g_ref[rows, :] = gate.reshape(PEER_SLOTS, LANES).T
        e1_ref[rows, :] = se1_scr[...].reshape(PEER_SLOTS, LANES).T
        e2_ref[rows, :] = se2_scr[...].reshape(PEER_SLOTS, LANES).T


def _peer_select(xn, wq, k1big, k2big):
    t = xn.shape[0]
    tb = 256
    hq = PEER_HEADS * PEER_DQ // 2
    nk = PEER_NKEYS * PEER_HEADS
    const = lambda i: (0, 0)
    row = lambda dt: jax.ShapeDtypeStruct((t, PEER_SLOTS), dt)
    vec = lambda n: pltpu.VMEM((n, SUBLANES, LANES), F32)
    return pl.pallas_call(
        functools.partial(_select_kernel, tb),
        grid=(t // tb,),
        in_specs=[pl.BlockSpec((tb, D_MODEL), lambda i: (i, 0)),
                  pl.BlockSpec((D_MODEL, 2 * hq), const),
                  pl.BlockSpec((nk, hq), const),
                  pl.BlockSpec((nk, hq), const)],
        out_specs=[pl.BlockSpec((tb, PEER_SLOTS), lambda i: (i, 0))] * 3,
        out_shape=[row(F32), row(F32), row(F32)],
        scratch_shapes=[vec(PEER_NKEYS), vec(PEER_NKEYS)] + [vec(PEER_NKEYS // TOPK_GROUP)] * 4
        + [vec(PEER_TOPK), vec(PEER_TOPK), vec(PEER_TOPK), vec(PEER_TOPK),
                        vec(PEER_TOPK), vec(PEER_TOPK), vec(PEER_TOPK), vec(PEER_TOPK), vec(PEER_TOPK)],
        compiler_params=_params(("parallel",), 40),
        name="peer_select",
    )(xn, wq, k1big, k2big)


def _peer_kernel(tb, eb, stride, final_norm, out_batch, xn_ref, e1_ref, e2_ref, g_ref, ut_ref, v_ref, x1_ref,
                 gain_ref, out_ref, m_scr, *acc_scr):
    e = pl.program_id(1)
    nk1 = eb // PEER_NKEYS
    acc_ref = acc_scr[0] if out_batch else out_ref

    @pl.when(e == 0)
    def _():
        acc_ref[...] = x1_ref[...]

    @pl.when(e == 0)
    def _():
        key = lax.broadcasted_iota(jnp.int32, (PEER_NKEYS, PEER_SLOTS), 0).astype(F32)

        def token(t, carry):
            e1 = e1_ref[pl.ds(t, 1), :]
            e2 = e2_ref[pl.ds(t, 1), :]
            gt = 0.5 * g_ref[pl.ds(t, 1), :]
            a_t = jnp.where(key == e1, gt, 0.0).astype(BF16)
            b_t = jnp.where(key == e2, 1.0, 0.0).astype(BF16)
            m_scr[pl.ds(t, PEER_NKEYS, stride=stride), :] = _dot_nt(a_t, b_t)
            return carry

        lax.fori_loop(0, tb, token, 0, unroll=128)

    s = _dot(xn_ref[...], ut_ref[...])
    k1 = e * nk1
    gates = [m_scr[pl.ds(pl.multiple_of((k1 + i) * stride, SUBLANES), tb), :] for i in range(nk1)]
    w = (_gelu_x2(s) * jnp.concatenate(gates, axis=1)).astype(BF16)
    acc_ref[...] += _dot(w, v_ref[...])

    if final_norm or out_batch:
        @pl.when(e == pl.num_programs(1) - 1)
        def _():
            y = acc_ref[...]
            if final_norm:
                y = _rms(y, gain_ref[...])
            out_ref[...] = pltpu.einshape("(tb)d->btd", y, b=out_batch) if out_batch else y


def _peer_dense(xn, e1, e2, g, ut, v, x1, layer, gain, final_norm, out_batch=0):
    t = xn.shape[0]
    tb, eb = 512, 2048
    stride = tb + SUBLANES
    once = pl.Buffered(1)
    tok = lambda w: pl.BlockSpec((tb, w), lambda i, e: (i, 0), pipeline_mode=once)
    tab = pl.BlockSpec((None, eb, D_MODEL), lambda i, e: (layer, e, 0))
    if out_batch:
        out_spec = pl.BlockSpec((out_batch, tb // out_batch, D_MODEL), lambda i, e: (0, i, 0), pipeline_mode=once)
        out_shape = jax.ShapeDtypeStruct((out_batch, t // out_batch, D_MODEL), F32)
        acc = [pltpu.VMEM((tb, D_MODEL), F32)]
    else:
        out_spec = pl.BlockSpec((tb, D_MODEL), lambda i, e: (i, 0), pipeline_mode=once)
        out_shape = jax.ShapeDtypeStruct((t, D_MODEL), F32)
        acc = []
    return pl.pallas_call(
        functools.partial(_peer_kernel, tb, eb, stride, final_norm, out_batch),
        grid=(t // tb, PEER_NEXP // eb),
        in_specs=[tok(D_MODEL), tok(PEER_SLOTS), tok(PEER_SLOTS), tok(PEER_SLOTS),
                  pl.BlockSpec((None, D_MODEL, eb), lambda i, e: (layer, 0, e)), tab, tok(D_MODEL),
                  pl.BlockSpec((1, D_MODEL), lambda i, e: (0, 0))],
        out_specs=out_spec,
        out_shape=out_shape,
        scratch_shapes=[pltpu.VMEM((PEER_NKEYS * stride, PEER_NKEYS), F32)] + acc,
        compiler_params=_params(("parallel", "arbitrary"), 62),
        name="peer_dense",
    )(xn, e1, e2, g, ut, v, x1, gain)


def _time_major(x):
    b, s, d = x.shape
    return x.transpose(1, 0, 2).reshape(s * b, d)


def _batch_major(y, b, s):
    return y.reshape(s, b, y.shape[-1]).transpose(1, 0, 2)


def kernel(x_prompt, x_sample, state_ret, state_ssm_re, state_ssm_im, state_conv, norm_mix, w_in, ret_norm, w_ret_out, ssm_a_re, ssm_a_im, ssm_b_re, ssm_b_im, ssm_c_re, ssm_c_im, ssm_d, ssm_log_dt, w_glu_a, w_glu_b, conv_w, conv_b, w_conv_out, w_mix_out, norm_ffn, peer_wq, peer_k1, peer_k2, peer_u, peer_v, norm_final):
    bp, sp, _ = x_prompt.shape
    bs, ss, _ = x_sample.shape
    tp, ts = bp * sp, bs * ss
    depth = w_in.shape[0]
    assert tp % ROWS == 0 and ts == ROWS and ROWS % bp == 0 and ROWS // bp == math.gcd(sp, RET_CHUNK)

    xs = [x_prompt, _time_major(x_sample)]
    batches = (bp, bs)
    pos = (jnp.arange(sp, dtype=F32), PAST_LEN + jnp.arange(ss, dtype=F32))
    ret_tabs = [_retention_tables(p, b) for p, b in zip(pos, batches)]

    lg = depth * SSM_G
    abre, abim, bbre, bbim = _s5_discretise(
        ssm_a_re.reshape(lg, SSM_P), ssm_a_im.reshape(lg, SSM_P), ssm_log_dt.reshape(lg, 1),
        ssm_b_re.transpose(0, 1, 3, 2).reshape(lg, SSM_GC, SSM_P),
        ssm_b_im.transpose(0, 1, 3, 2).reshape(lg, SSM_GC, SSM_P))

    hq = PEER_DQ // 2
    eye = jnp.eye(PEER_HEADS, dtype=F32)

    def keys_block_diag(k):
        return (k.transpose(1, 0, 2)[:, :, None, :] * eye[None, :, :, None]).reshape(
            PEER_NKEYS * PEER_HEADS, PEER_HEADS * hq).astype(BF16)

    w_in_b = w_in.astype(BF16)
    ut = peer_u.astype(BF16).transpose(0, 2, 1)
    vt = peer_v.astype(BF16)
    zero_ret = jnp.zeros((1, bp, RET_HEADS, RET_DK, RET_DK), F32)

    states = [[[] for _ in range(3)] for _ in range(2)]
    ret_stacked = [None, None]
    for l in range(depth):
        sl = slice(l * SSM_G, (l + 1) * SSM_G)
        bmat = jnp.concatenate([_block_diag_slabs(bbre[sl], SSM_SLABS), _block_diag_slabs(bbim[sl], SSM_SLABS)],
                               axis=2).astype(BF16)
        cmat = jnp.concatenate([_block_diag_slabs(ssm_c_re[l].transpose(0, 2, 1), SSM_SLABS),
                                _block_diag_slabs(-ssm_c_im[l].transpose(0, 2, 1), SSM_SLABS)],
                               axis=1).astype(BF16)
        are_row = abre[sl].reshape(1, SSM_N)
        aim_row = abim[sl].reshape(1, SSM_N)
        proj = [w.astype(BF16) for w in (w_ret_out[l], w_glu_a[l], w_glu_b[l], w_conv_out[l], w_mix_out[l])]
        wq = peer_wq[l].reshape(D_MODEL, PEER_HEADS, 2, hq).transpose(0, 2, 1, 3).reshape(D_MODEL, -1).astype(BF16)
        k1big, k2big = keys_block_diag(peer_k1[l]), keys_block_diag(peer_k2[l])

        for gi, batch in enumerate(batches):
            x = xs[gi]
            nblk = x.size // (ROWS * D_MODEL)
            if gi == 0:
                s0, s0_layer = zero_ret, 0
                h0re = jnp.zeros((batch, SSM_N), F32)
                h0im = jnp.zeros((batch, SSM_N), F32)
                buf0 = jnp.zeros(((CONV_K - 1) * batch, CONV_W), F32)
                bblk = batch
            else:
                s0, s0_layer = state_ret, l
                h0re = state_ssm_re[l].reshape(batch, SSM_N)
                h0im = state_ssm_im[l].reshape(batch, SSM_N)
                buf0 = state_conv[l].transpose(1, 0, 2).reshape((CONV_K - 1) * batch, CONV_W)
                bblk = 16
            z = _inproj(x, norm_mix[l][None, :], w_in_b, l)
            if x.ndim == 3:
                z, x = z
            oa, ret_stacked[gi] = _retention(z, nblk, batch, bblk, ret_tabs[gi], ret_norm[l][None, :], s0, s0_layer,
                                          ret_stacked[gi], l, depth)
            ys, xre, xim = _s5(z, nblk, batch, bmat, cmat, are_row, aim_row, ssm_d[l][None, :], h0re, h0im)
            st = states[gi]
            st[0].append(xre.reshape(batch, SSM_G, SSM_P))
            st[1].append(xim.reshape(batch, SSM_G, SSM_P))

            x1, xn, buf = _merge(x, z, oa, ys, batch, buf0, conv_w[l], conv_b[l][None, :], *proj, norm_ffn[l][None, :])
            st[2].append(buf.reshape(CONV_K - 1, batch, CONV_W).transpose(1, 0, 2))
            e1, e2, g = _peer_select(xn, wq, k1big, k2big)
            last = l == depth - 1
            xs[gi] = _peer_dense(xn, e1, e2, g, ut, vt, x1, l, norm_final[None, :], last,
                                 out_batch=bp if last and gi == 0 else 0)

    y_prompt = xs[0]
    y_sample = _batch_major(xs[1], bs, ss)
    (re_p, im_p, cv_p), (re_s, im_s, cv_s) = states
    return (y_prompt, y_sample,
            ret_stacked[0], ret_stacked[1],
            jnp.stack(re_p), jnp.stack(re_s),
            jnp.stack(im_p), jnp.stack(im_s),
            jnp.stack(cv_p), jnp.stack(cv_s))
```

```python
import functools
import math

import jax
import jax.numpy as jnp
from jax import lax
from jax.experimental import pallas as pl
from jax.experimental.pallas import tpu as pltpu

F32 = jnp.float32
BF16 = jnp.bfloat16

D_MODEL = 1024
PAST_LEN = 16384
RET_HEADS = 8
RET_DK = 64
RET_W = 512
RET_CHUNK = 128
ROPE_BASE = 10000.0
SSM_W = 512
SSM_GC = 16
SSM_G = 32
SSM_P = 64
SSM_N = SSM_G * SSM_P
SSM_SLABS = 4
SSM_SLAB_N = SSM_N // SSM_SLABS
CONV_W = 512
CONV_K = 3
PROJ_W = 7168
PEER_HEADS = 8
PEER_DQ = 256
PEER_NKEYS = 128
PEER_TOPK = 16
PEER_NEXP = PEER_NKEYS ** 2
PEER_SLOTS = PEER_HEADS * PEER_TOPK
TOPK_GROUP = 8
EPS = 1e-6
GELU_C = math.sqrt(2.0 / math.pi)
GELU_A = 0.044715

ROWS = 1024
LANES = 128
SUBLANES = 8
MIB = 1024 * 1024


def _params(sem, vmem_mib):
    return pltpu.CompilerParams(dimension_semantics=sem, vmem_limit_bytes=vmem_mib * MIB)


def _rms(x, g):
    return x * lax.rsqrt(jnp.mean(x * x, axis=-1, keepdims=True) + EPS) * g


def _dot(a, b):
    return jnp.dot(a, b, preferred_element_type=F32)


def _dot_nt(a, b):
    return lax.dot_general(a, b, (((1,), (1,)), ((), ())), preferred_element_type=F32)


def _inproj_kernel(x_ref, g_ref, w_ref, z_ref, h_scr):
    @pl.when(pl.program_id(1) == 0)
    def _():
        h_scr[...] = _rms(x_ref[...], g_ref[...]).astype(BF16)

    z_ref[...] = _dot(h_scr[...], w_ref[...])


def _inproj_bm_kernel(x_ref, g_ref, w_ref, z_ref, xtm_ref, h_scr):
    @pl.when(pl.program_id(1) == 0)
    def _():
        x = pltpu.einshape("btd->(tb)d", x_ref[...])
        xtm_ref[...] = x
        h_scr[...] = _rms(x, g_ref[...]).astype(BF16)

    z_ref[...] = _dot(h_scr[...], w_ref[...])


def _inproj(x, g, w, layer):
    batch_major = x.ndim == 3
    nb = PROJ_W // 4
    if batch_major:
        batch, seq, _ = x.shape
        t = batch * seq
        x_spec = pl.BlockSpec((batch, ROWS // batch, D_MODEL), lambda i, j: (0, i, 0))
    else:
        t = x.shape[0]
        x_spec = pl.BlockSpec((ROWS, D_MODEL), lambda i, j: (i, 0))
    z_spec = pl.BlockSpec((ROWS, nb), lambda i, j: (i, j))
    z_shape = jax.ShapeDtypeStruct((t, PROJ_W), F32)
    return pl.pallas_call(
        _inproj_bm_kernel if batch_major else _inproj_kernel,
        grid=(t // ROWS, PROJ_W // nb),
        in_specs=[x_spec,
                  pl.BlockSpec((1, D_MODEL), lambda i, j: (0, 0)),
                  pl.BlockSpec((None, D_MODEL, nb), lambda i, j: (layer, 0, j))],
        out_specs=[z_spec, pl.BlockSpec((ROWS, D_MODEL), lambda i, j: (i, 0))] if batch_major else z_spec,
        out_shape=[z_shape, jax.ShapeDtypeStruct((t, D_MODEL), F32)] if batch_major else z_shape,
        scratch_shapes=[pltpu.VMEM((ROWS, D_MODEL), BF16)],
        compiler_params=_params(("parallel", "arbitrary"), 48),
        name="inproj",
    )(x, g, w)


def _ret_kernel(batch, bblk, aliased, out_layer, *refs):
    refs = [r for i, r in enumerate(refs) if not (aliased and i == 9)]
    (z_ref, cos_ref, sa_ref, sb_ref, qdec_ref, kdec_ref, cdec_ref, gn_ref, s0_ref,
     o_ref, st_ref, qd_scr, kd_scr, v_scr, mask_scr, oacc_scr) = refs
    s_ref = st_ref if aliased else st_ref.at[out_layer]
    bb = pl.program_id(0)
    c = pl.program_id(1)
    steps = ROWS // batch
    nslab = RET_W // LANES

    def head_view(ref, h):
        return ref[h // 2, :, (h % 2) * RET_DK:(h % 2 + 1) * RET_DK]

    seq_local = steps >= LANES
    msize = steps if seq_local else ROWS

    @pl.when((bb == 0) & (c == 0))
    def _():
        r = lax.broadcasted_iota(jnp.int32, (msize, msize), 0)
        cc = lax.broadcasted_iota(jnp.int32, (msize, msize), 1)
        if seq_local:
            mask_scr[...] = (r >= cc).astype(F32)
        else:
            same = (r & (batch - 1)) == (cc & (batch - 1))
            mask_scr[...] = (same & (r >= cc)).astype(F32)

    @pl.when(c == 0)
    def _():
        s_ref[...] = s0_ref[...]
        if not aliased:
            for k in range(st_ref.shape[0]):
                if k != out_layer:
                    st_ref[k] = jnp.zeros(st_ref.shape[1:], F32)

    @pl.when(bb == 0)
    def _():
        def per_row(t_ref):
            return jnp.broadcast_to(t_ref[...][:, None, :], (steps, batch, LANES)).reshape(ROWS, LANES)

        cos, sa, sb = per_row(cos_ref), per_row(sa_ref), per_row(sb_ref)

        def rot(x):
            return x * cos + pltpu.roll(x, 32, 1) * sa + pltpu.roll(x, 96, 1) * sb

        for s in range(nslab):
            cols = slice(s * LANES, (s + 1) * LANES)
            qd_scr[s] = rot(z_ref[:, cols]) * qdec_ref[:, cols]
            kd_scr[s] = rot(z_ref[:, RET_W + s * LANES:RET_W + (s + 1) * LANES]) * kdec_ref[:, cols]
            v_scr[s] = z_ref[:, 2 * RET_W + s * LANES:2 * RET_W + (s + 1) * LANES]
        if not seq_local:
            for s in range(nslab):
                outs = []
                for h in (2 * s, 2 * s + 1):
                    qh = head_view(qd_scr, h).astype(BF16)
                    kh = head_view(kd_scr, h).astype(BF16)
                    vh = head_view(v_scr, h).astype(BF16)
                    p = (_dot_nt(qh, kh) * mask_scr[...]).astype(BF16)
                    outs.append(_dot(p, vh))
                oacc_scr[s] = jnp.concatenate(outs, axis=1)

    def per_seq(bl, carry):
        b = bb * bblk + bl
        rows = pl.ds(b, steps, stride=batch)
        heads = range(RET_HEADS)

        def head_cols(scr):
            slabs = [scr[s, rows, :] for s in range(nslab)]
            return [slabs[h // 2][:, (h % 2) * RET_DK:(h % 2 + 1) * RET_DK].astype(BF16) for h in heads]

        q16, k16, v16 = head_cols(qd_scr), head_cols(kd_scr), head_cols(v_scr)
        st = [s_ref[bl, h] for h in heads]
        outs = [_dot(q16[h], st[h].astype(BF16)) for h in heads]
        if seq_local:
            sc = [_dot_nt(q16[h], k16[h]) for h in heads]
            p = [(sc[h] * mask_scr[...]).astype(BF16) for h in heads]
            outs = [_dot(p[h], v16[h]) + outs[h] for h in heads]
        upd = [lax.dot_general(k16[h], v16[h], (((0,), (0,)), ((), ())), preferred_element_type=F32) for h in heads]
        for h in heads:
            s_ref[bl, h] = (st[h] + upd[h]) * cdec_ref[:, h * RET_DK:(h + 1) * RET_DK]
        for s in range(nslab):
            o2 = jnp.concatenate(outs[2 * s:2 * s + 2], axis=1)
            oacc_scr[s, rows, :] = o2 if seq_local else oacc_scr[s, rows, :] + o2
        return carry

    lax.fori_loop(0, bblk, per_seq, 0, unroll=4)

    @pl.when(bb == pl.num_programs(0) - 1)
    def _():
        r = lax.broadcasted_iota(jnp.int32, (LANES, LANES), 0) // RET_DK
        cc = lax.broadcasted_iota(jnp.int32, (LANES, LANES), 1) // RET_DK
        avg = jnp.where(r == cc, 1.0 / RET_DK, 0.0).astype(BF16)

        def seg_mean(x):
            hi = x.astype(BF16)
            lo = (x - hi.astype(F32)).astype(BF16)
            return _dot(hi, avg) + _dot(lo, avg)

        normed = []
        for s in range(nslab):
            o2 = oacc_scr[s]
            dlt = o2 - seg_mean(o2)
            normed.append(dlt * lax.rsqrt(seg_mean(dlt * dlt) + EPS))
        o = jnp.concatenate(normed, axis=1) * gn_ref[...]
        o_ref[...] = jax.nn.silu(z_ref[:, 3 * RET_W:4 * RET_W]) * o


def _retention(z, nblk, batch, bblk, tabs, gn, s0, layer, stacked, out_layer, depth):
    cos, sa, sb, qdec, kdec, cdec = tabs
    nbb = batch // bblk
    steps = ROWS // batch
    msize = steps if steps >= LANES else ROWS
    aliased = stacked is not None
    if aliased:
        st_spec = pl.BlockSpec((None, bblk, RET_HEADS, RET_DK, RET_DK), lambda bb, c: (out_layer, bb, 0, 0, 0))
    else:
        st_spec = pl.BlockSpec((depth, bblk, RET_HEADS, RET_DK, RET_DK), lambda bb, c: (0, bb, 0, 0, 0))
    st_in = pl.BlockSpec((None, bblk, RET_HEADS, RET_DK, RET_DK), lambda bb, c: (layer, bb, 0, 0, 0))
    const = lambda bb, c: (0, 0)
    return pl.pallas_call(
        functools.partial(_ret_kernel, batch, bblk, aliased, out_layer),
        grid=(nbb, nblk),
        in_specs=[pl.BlockSpec((ROWS, 4 * RET_W), lambda bb, c: (c, 0)),
                  pl.BlockSpec((steps, LANES), lambda bb, c: (c, 0)),
                  pl.BlockSpec((steps, LANES), lambda bb, c: (c, 0)),
                  pl.BlockSpec((steps, LANES), lambda bb, c: (c, 0)),
                  pl.BlockSpec((ROWS, RET_W), const),
                  pl.BlockSpec((ROWS, RET_W), const),
                  pl.BlockSpec((1, RET_W), const),
                  pl.BlockSpec((1, RET_W), const),
                  st_in] + ([pl.BlockSpec(memory_space=pl.ANY)] if aliased else []),
        out_specs=[pl.BlockSpec((ROWS, RET_W), lambda bb, c: (c, 0)), st_spec],
        out_shape=[jax.ShapeDtypeStruct((nblk * ROWS, RET_W), F32),
                   jax.ShapeDtypeStruct((depth, batch, RET_HEADS, RET_DK, RET_DK), F32)],
        input_output_aliases={9: 1} if aliased else {},
        scratch_shapes=[pltpu.VMEM((RET_W // LANES, ROWS, LANES), F32)] * 3
        + [pltpu.VMEM((msize, msize), F32), pltpu.VMEM((RET_W // LANES, ROWS, LANES), F32)],
        compiler_params=_params(("arbitrary", "arbitrary"), 56),
        name="retention",
    )(z, cos, sa, sb, qdec, kdec, cdec, gn, s0, *([stacked] if aliased else []))


def _retention_tables(pos, batch):
    half = RET_DK // 2
    lane = jnp.arange(LANES)
    upper = (lane % RET_DK) >= half
    freqs = (ROPE_BASE ** (-jnp.arange(half, dtype=F32) / half))[lane % half]
    ang = pos[:, None] * freqs[None, :]
    cos_t, sin = jnp.cos(ang), jnp.sin(ang)
    sa_t = jnp.where(upper[None, :], sin, 0.0)
    sb_t = jnp.where(upper[None, :], 0.0, -sin)
    lg = jnp.repeat(jnp.log1p(-jnp.exp2(-5.0 - jnp.arange(RET_HEADS, dtype=F32))), RET_DK)
    steps = ROWS // batch
    i1 = (jnp.arange(ROWS) // batch).astype(F32) + 1.0
    qdec = jnp.exp(i1[:, None] * lg[None, :])
    kdec = jnp.exp(-i1[:, None] * lg[None, :]) * (RET_DK ** -0.5)
    cdec = jnp.exp(steps * lg)[None, :]
    return cos_t, sa_t, sb_t, qdec, kdec, cdec


def _s5_disc_kernel(are_ref, aim_ref, ldt_ref, bre_ref, bim_ref, abre_ref, abim_ref, bbre_ref, bbim_ref):
    ar, ai = are_ref[...], aim_ref[...]
    dt = jnp.exp(ldt_ref[...])
    dar, dai = dt * ar, dt * ai
    mag = jnp.exp(dar)
    abar_re, abar_im = mag * jnp.cos(dai), mag * jnp.sin(dai)
    den = ar * ar + ai * ai
    nr, ni = abar_re - 1.0, abar_im
    f_re = (nr * ar + ni * ai) / den
    f_im = (ni * ar - nr * ai) / den
    abre_ref[...] = abar_re
    abim_ref[...] = abar_im
    br, bi = bre_ref[...], bim_ref[...]
    bbre_ref[...] = f_re[:, None, :] * br - f_im[:, None, :] * bi
    bbim_ref[...] = f_re[:, None, :] * bi + f_im[:, None, :] * br


def _s5_discretise(a_re, a_im, log_dt, b_re_t, b_im_t):
    lg = a_re.shape[0]
    small = jax.ShapeDtypeStruct((lg, SSM_P), F32)
    big = jax.ShapeDtypeStruct((lg, SSM_GC, SSM_P), F32)
    return pl.pallas_call(_s5_disc_kernel, out_shape=[small, small, big, big], name="s5_disc")(
        a_re, a_im, log_dt, b_re_t, b_im_t)


def _s5_kernel(batch, u_ref, bmat_ref, cmat_ref, are_ref, aim_ref, d_ref, h0re_ref, h0im_ref,
               y_ref, xre_ref, xim_ref, x_scr):
    c = pl.program_id(0)
    steps = ROWS // batch
    half = SSM_SLAB_N

    @pl.when(c == 0)
    def _():
        xre_ref[...] = h0re_ref[...]
        xim_ref[...] = h0im_ref[...]

    u = u_ref[...]
    ub = u.astype(BF16)
    for s in range(SSM_SLABS):
        x_scr[:, 2 * half * s:2 * half * (s + 1)] = _dot(ub[:, s * LANES:(s + 1) * LANES], bmat_ref[s])

    for s in range(SSM_SLABS):
        re0 = 2 * half * s
        im0 = re0 + half
        sc = slice(half * s, half * (s + 1))
        ar = jnp.broadcast_to(are_ref[:, sc], (SUBLANES, half))
        ai = jnp.broadcast_to(aim_ref[:, sc], (SUBLANES, half))

        def row_tile(rt, carry, re0=re0, im0=im0, sc=sc, ar=ar, ai=ai):
            r0 = pl.multiple_of(rt * SUBLANES, SUBLANES)

            def step(t, x):
                xr, xi = x
                row = pl.multiple_of(t * batch + r0, SUBLANES)
                nr = ar * xr - ai * xi + x_scr[pl.ds(row, SUBLANES), re0:re0 + half]
                ni = ar * xi + ai * xr + x_scr[pl.ds(row, SUBLANES), im0:im0 + half]
                x_scr[pl.ds(row, SUBLANES), re0:re0 + half] = nr
                x_scr[pl.ds(row, SUBLANES), im0:im0 + half] = ni
                return nr, ni

            init = (xre_ref[pl.ds(r0, SUBLANES), sc], xim_ref[pl.ds(r0, SUBLANES), sc])
            xr, xi = lax.fori_loop(0, steps, step, init, unroll=8)
            xre_ref[pl.ds(r0, SUBLANES), sc] = xr
            xim_ref[pl.ds(r0, SUBLANES), sc] = xi
            return carry

        lax.fori_loop(0, batch // SUBLANES, row_tile, 0)

    ys = [_dot(x_scr[:, 2 * half * s:2 * half * (s + 1)].astype(BF16), cmat_ref[s]) for s in range(SSM_SLABS)]
    y = jnp.concatenate(ys, axis=1) + d_ref[...] * u
    y_ref[...] = jax.nn.gelu(y)


def _s5(z, nblk, batch, bmat, cmat, abre, abim, d, h0re, h0im):
    const2 = lambda c: (0, 0)
    const3 = lambda c: (0, 0, 0)
    st = pl.BlockSpec((batch, SSM_N), const2)
    return pl.pallas_call(
        functools.partial(_s5_kernel, batch),
        grid=(nblk,),
        in_specs=[pl.BlockSpec((ROWS, SSM_W), lambda c: (c, 4)),
                  pl.BlockSpec((SSM_SLABS, LANES, 2 * SSM_SLAB_N), const3),
                  pl.BlockSpec((SSM_SLABS, 2 * SSM_SLAB_N, LANES), const3),
                  pl.BlockSpec((1, SSM_N), const2),
                  pl.BlockSpec((1, SSM_N), const2),
                  pl.BlockSpec((1, SSM_W), const2),
                  st, st],
        out_specs=[pl.BlockSpec((ROWS, SSM_W), lambda c: (c, 0)), st, st],
        out_shape=[jax.ShapeDtypeStruct((nblk * ROWS, SSM_W), F32),
                   jax.ShapeDtypeStruct((batch, SSM_N), F32),
                   jax.ShapeDtypeStruct((batch, SSM_N), F32)],
        scratch_shapes=[pltpu.VMEM((ROWS, 2 * SSM_N), F32)],
        compiler_params=_params(("arbitrary",), 48),
        name="s5",
    )(z, bmat, cmat, abre, abim, d, h0re, h0im)


def _block_diag_slabs(w, nslab):
    gps = SSM_G // nslab
    eye = jnp.eye(gps, dtype=w.dtype)
    w4 = w.reshape(nslab, gps, w.shape[1], w.shape[2])
    out = w4[:, :, :, None, :] * eye[None, :, None, :, None]
    return out.reshape(nslab, gps * w.shape[1], gps * w.shape[2])


def _merge_kernel(batch, x_ref, oa_ref, ys_ref, bg_ref, cg_ref, hc_ref, buf0_ref, cw_ref, cb_ref,
                  ga_ref, gb_ref, gc_ref, wr_ref, wa_ref, wb_ref, wc_ref, wm_ref, gf_ref,
                  x1_ref, xn_ref, buf_ref, zp_scr):
    rb = x_ref.shape[0]
    pad = (CONV_K - 1) * batch

    @pl.when(pl.program_id(0) == 0)
    def _():
        zp_scr[0:pad, :] = buf0_ref[...]

    zc = cg_ref[...] * hc_ref[...]
    zp_scr[pad:pad + rb, :] = zc
    y = cb_ref[...]
    for j in range(CONV_K):
        y = y + cw_ref[j:j + 1, :] * zp_scr[j * batch:j * batch + rb, :]
    oc_pre = bg_ref[...] * y
    tail = zp_scr[rb:rb + pad, :]
    buf_ref[...] = tail
    zp_scr[0:pad, :] = tail

    oa = _dot(oa_ref[...].astype(BF16), wr_ref[...])
    ysb = ys_ref[...].astype(BF16)
    ob = _dot(ysb, wa_ref[...]) * jax.nn.sigmoid(_dot(ysb, wb_ref[...]))
    oc = _dot(oc_pre.astype(BF16), wc_ref[...])
    merged = (jax.nn.sigmoid(ga_ref[...]) * oa + jax.nn.sigmoid(gb_ref[...]) * ob
              + jax.nn.sigmoid(gc_ref[...]) * oc)
    x1 = x_ref[...] + _dot(merged.astype(BF16), wm_ref[...])
    x1_ref[...] = x1
    xn_ref[...] = _rms(x1, gf_ref[...]).astype(BF16)


def _merge(x, z, oa, ys, batch, buf0, conv_w, conv_b, wr, wa, wb, wc, wm, gf):
    t = x.shape[0]
    rb = 512
    pad = (CONV_K - 1) * batch
    assert pad <= rb
    row = lambda w: pl.BlockSpec((rb, w), lambda i: (i, 0))
    zcol = lambda w, j: pl.BlockSpec((rb, w), lambda i: (i, j))
    const = lambda r, w: pl.BlockSpec((r, w), lambda i: (0, 0))
    return pl.pallas_call(
        functools.partial(_merge_kernel, batch),
        grid=(t // rb,),
        in_specs=[row(D_MODEL), row(RET_W), row(SSM_W),
                  zcol(CONV_W, 5), zcol(CONV_W, 6), zcol(CONV_W, 7),
                  const(pad, CONV_W), const(CONV_K, CONV_W), const(1, CONV_W),
                  zcol(D_MODEL, 4), zcol(D_MODEL, 5), zcol(D_MODEL, 6),
                  const(RET_W, D_MODEL), const(SSM_W, D_MODEL), const(SSM_W, D_MODEL), const(CONV_W, D_MODEL),
                  const(D_MODEL, D_MODEL), const(1, D_MODEL)],
        out_specs=[row(D_MODEL), row(D_MODEL), const(pad, CONV_W)],
        out_shape=[jax.ShapeDtypeStruct((t, D_MODEL), F32), jax.ShapeDtypeStruct((t, D_MODEL), BF16),
                   jax.ShapeDtypeStruct((pad, CONV_W), F32)],
        scratch_shapes=[pltpu.VMEM((rb + pad, CONV_W), F32)],
        compiler_params=_params(("arbitrary",), 48),
        name="merge",
    )(x, oa, ys, z, z, z, buf0, conv_w, conv_b, z, z, z, wr, wa, wb, wc, wm, gf)


def _tree(items, combine):
    while len(items) > 1:
        nxt = [combine(items[i], items[i + 1]) for i in range(0, len(items) - 1, 2)]
        if len(items) % 2:
            nxt.append(items[-1])
        items = nxt
    return items[0]


def _first_max(x, y):
    (vx, ix), (vy, iy) = x, y
    return jnp.maximum(vx, vy), jnp.where(vx >= vy, ix, iy)


def _bits(x, n):
    out, rest = [], x
    for _ in range(n):
        half = jnp.floor(rest * 0.5)
        out.append(rest - 2.0 * half == 1.0)
        rest = half
    return out


def _mux(vals, bits):
    level = list(vals)
    for bit in bits:
        level = [jnp.where(bit, level[j + 1], level[j]) for j in range(0, len(level), 2)]
    return level[0]


def _top16_of_keys(s_scrs, gv_scrs, gi_scrs, v_scrs, i_scrs):
    grp = TOPK_GROUP
    ngrp = PEER_NKEYS // grp
    nbits = ngrp.bit_length() - 1

    for s_scr, gv, gi in zip(s_scrs, gv_scrs, gi_scrs):
        for g in range(ngrp):
            gv[g], gi[g] = _tree([(s_scr[g * grp + p], float(g * grp + p)) for p in range(grp)], _first_max)

    def body(r, carry):
        for s_scr, gv, gi, v_scr, i_scr in zip(s_scrs, gv_scrs, gi_scrs, v_scrs, i_scrs):
            m, idx = _tree([(gv[g], gi[g]) for g in range(ngrp)], _first_max)
            v_scr[r] = m
            i_scr[r] = idx
            gid = jnp.floor(idx * (1.0 / grp))
            rel = idx - gid * grp
            bits = _bits(gid, nbits)
            cands = []
            for p in range(grp):
                val = _mux([s_scr[g * grp + p] for g in range(ngrp)], bits)
                left = (val < m) | ((val == m) & (rel < float(p)))
                cands.append((jnp.where(left, val, -jnp.inf), float(p)))
            nv, npos = _tree(cands, _first_max)
            ni = gid * grp + npos
            for g in range(ngrp):
                hit = gid == float(g)
                gv[g] = jnp.where(hit, nv, gv[g])
                gi[g] = jnp.where(hit, ni, gi[g])
        return carry

    lax.fori_loop(0, PEER_TOPK, body, 0)


def _select_kernel(tb, xn_ref, wq_ref, k1_ref, k2_ref, e1_ref, e2_ref, g_ref,
                   s1_scr, s2_scr, gv1_scr, gi1_scr, gv2_scr, gi2_scr, v1_scr, i1_scr, v2_scr, i2_scr,
                   hv_scr, hb_scr, sc_scr, se1_scr, se2_scr):
    q = _dot(xn_ref[...], wq_ref[...]).astype(BF16)
    hq = PEER_HEADS * PEER_DQ // 2
    s1 = _dot_nt(k1_ref[...], q[:, :hq])
    s2 = _dot_nt(k2_ref[...], q[:, hq:])
    kbits = PEER_TOPK.bit_length() - 1
    tiles = range(tb // LANES)
    for lt in tiles:
        lanes = slice(lt * LANES, (lt + 1) * LANES)
        s1_scr[lt] = s1[:, lanes].reshape(PEER_NKEYS, SUBLANES, LANES)
        s2_scr[lt] = s2[:, lanes].reshape(PEER_NKEYS, SUBLANES, LANES)
    both = lambda r1, r2: tuple(r1.at[lt] for lt in tiles) + tuple(r2.at[lt] for lt in tiles)
    _top16_of_keys(both(s1_scr, s2_scr), both(gv1_scr, gv2_scr), both(gi1_scr, gi2_scr),
                   both(v1_scr, v2_scr), both(i1_scr, i2_scr))

    for lt in tiles:
        for a in range(PEER_TOPK):
            hv_scr[lt, a] = v1_scr[lt, a] + v2_scr[lt, 0]
            hb_scr[lt, a] = jnp.zeros((SUBLANES, LANES), F32)

    def body(r, carry):
        for lt in tiles:
            hv, hb = hv_scr.at[lt], hb_scr.at[lt]
            v1, i1, v2, i2 = v1_scr.at[lt], i1_scr.at[lt], v2_scr.at[lt], i2_scr.at[lt]
            m, a_sel = _tree([(hv[a], float(a)) for a in range(PEER_TOPK)], _first_max)
            abits = _bits(a_sel, kbits)
            b_sel = _mux([hb[a] for a in range(PEER_TOPK)], abits)
            bbits = _bits(b_sel, kbits)
            sc_scr[lt, r] = m
            se1_scr[lt, r] = _mux([i1[a] for a in range(PEER_TOPK)], abits)
            se2_scr[lt, r] = _mux([i2[b] for b in range(PEER_TOPK)], bbits)
            nb = b_sel + 1.0
            v2_next = _mux([v2[(b + 1) % PEER_TOPK] for b in range(PEER_TOPK)], bbits)
            v1_sel = _mux([v1[a] for a in range(PEER_TOPK)], abits)
            live = (a_sel + 1.0) * (nb + 1.0) <= float(PEER_TOPK)
            nv = jnp.where(live, v1_sel + v2_next, -jnp.inf)
            for a in range(PEER_TOPK):
                hit = a_sel == float(a)
                hv[a] = jnp.where(hit, nv, hv[a])
                hb[a] = jnp.where(hit, nb, hb[a])
        return carry

    lax.fori_loop(0, PEER_TOPK, body, 0)
    for lt in tiles:
        sc = sc_scr[lt]
        ex = jnp.exp(sc - jnp.max(sc, axis=0, keepdims=True))
        gate = ex / jnp.sum(ex, axis=0, keepdims=True)
        rows = slice(lt * LANES, (lt + 1) * LANES)
        g_ref[rows, :] = gate.reshape(PEER_SLOTS, LANES).T
        e1_ref[rows, :] = se1_scr[lt].reshape(PEER_SLOTS, LANES).T
        e2_ref[rows, :] = se2_scr[lt].reshape(PEER_SLOTS, LANES).T


def _peer_select(xn, wq, k1big, k2big):
    t = xn.shape[0]
    tb = 256
    hq = PEER_HEADS * PEER_DQ // 2
    nk = PEER_NKEYS * PEER_HEADS
    const = lambda i: (0, 0)
    row = lambda dt: jax.ShapeDtypeStruct((t, PEER_SLOTS), dt)
    vec = lambda n: pltpu.VMEM((tb // LANES, n, SUBLANES, LANES), F32)
    return pl.pallas_call(
        functools.partial(_select_kernel, tb),
        grid=(t // tb,),
        in_specs=[pl.BlockSpec((tb, D_MODEL), lambda i: (i, 0)),
                  pl.BlockSpec((D_MODEL, 2 * hq), const),
                  pl.BlockSpec((nk, hq), const),
                  pl.BlockSpec((nk, hq), const)],
        out_specs=[pl.BlockSpec((tb, PEER_SLOTS), lambda i: (i, 0))] * 3,
        out_shape=[row(F32), row(F32), row(F32)],
        scratch_shapes=[vec(PEER_NKEYS), vec(PEER_NKEYS)] + [vec(PEER_NKEYS // TOPK_GROUP)] * 4
        + [vec(PEER_TOPK), vec(PEER_TOPK), vec(PEER_TOPK), vec(PEER_TOPK),
                        vec(PEER_TOPK), vec(PEER_TOPK), vec(PEER_TOPK), vec(PEER_TOPK), vec(PEER_TOPK)],
        compiler_params=_params(("parallel",), 40),
        name="peer_select",
    )(xn, wq, k1big, k2big)


def _peer_kernel(tb, eb, stride, final_norm, out_batch, xn_ref, e1_ref, e2_ref, g_ref, ut_ref, v_ref, x1_ref,
                 gain_ref, out_ref, m_scr, *acc_scr):
    e = pl.program_id(1)
    nk1 = eb // PEER_NKEYS
    acc_ref = acc_scr[0] if out_batch else out_ref

    @pl.when(e == 0)
    def _():
        acc_ref[...] = x1_ref[...]

    @pl.when(e == 0)
    def _():
        key = lax.broadcasted_iota(jnp.int32, (PEER_NKEYS, PEER_SLOTS), 0).astype(F32)

        def token(t, carry):
            e1 = e1_ref[pl.ds(t, 1), :]
            e2 = e2_ref[pl.ds(t, 1), :]
            gt = 0.5 * g_ref[pl.ds(t, 1), :]
            a_t = jnp.where(key == e1, gt, 0.0).astype(BF16)
            b_t = jnp.where(key == e2, 1.0, 0.0).astype(BF16)
            m_scr[pl.ds(t, PEER_NKEYS, stride=stride), :] = _dot_nt(a_t, b_t)
            return carry

        lax.fori_loop(0, tb, token, 0, unroll=128)

    s = _dot(xn_ref[...], ut_ref[...])
    t = jnp.tanh(s * (GELU_C + (GELU_C * GELU_A) * (s * s)))
    k1 = e * nk1
    gates = [m_scr[pl.ds(pl.multiple_of((k1 + i) * stride, SUBLANES), tb), :] for i in range(nk1)]
    w = ((s + s * t) * jnp.concatenate(gates, axis=1)).astype(BF16)
    acc_ref[...] += _dot(w, v_ref[...])

    if final_norm or out_batch:
        @pl.when(e == pl.num_programs(1) - 1)
        def _():
            y = acc_ref[...]
            if final_norm:
                y = _rms(y, gain_ref[...])
            out_ref[...] = pltpu.einshape("(tb)d->btd", y, b=out_batch) if out_batch else y


def _peer_dense(xn, e1, e2, g, ut, v, x1, layer, gain, final_norm, out_batch=0):
    t = xn.shape[0]
    tb, eb = 512, 2048
    stride = tb + SUBLANES
    once = pl.Buffered(1)
    tok = lambda w: pl.BlockSpec((tb, w), lambda i, e: (i, 0), pipeline_mode=once)
    tab = pl.BlockSpec((None, eb, D_MODEL), lambda i, e: (layer, e, 0))
    if out_batch:
        out_spec = pl.BlockSpec((out_batch, tb // out_batch, D_MODEL), lambda i, e: (0, i, 0), pipeline_mode=once)
        out_shape = jax.ShapeDtypeStruct((out_batch, t // out_batch, D_MODEL), F32)
        acc = [pltpu.VMEM((tb, D_MODEL), F32)]
    else:
        out_spec = pl.BlockSpec((tb, D_MODEL), lambda i, e: (i, 0), pipeline_mode=once)
        out_shape = jax.ShapeDtypeStruct((t, D_MODEL), F32)
        acc = []
    return pl.pallas_call(
        functools.partial(_peer_kernel, tb, eb, stride, final_norm, out_batch),
        grid=(t // tb, PEER_NEXP // eb),
        in_specs=[tok(D_MODEL), tok(PEER_SLOTS), tok(PEER_SLOTS), tok(PEER_SLOTS),
                  pl.BlockSpec((None, D_MODEL, eb), lambda i, e: (layer, 0, e)), tab, tok(D_MODEL),
                  pl.BlockSpec((1, D_MODEL), lambda i, e: (0, 0))],
        out_specs=out_spec,
        out_shape=out_shape,
        scratch_shapes=[pltpu.VMEM((PEER_NKEYS * stride, PEER_NKEYS), F32)] + acc,
        compiler_params=_params(("parallel", "arbitrary"), 62),
        name="peer_dense",
    )(xn, e1, e2, g, ut, v, x1, gain)


def _time_major(x):
    b, s, d = x.shape
    return x.transpose(1, 0, 2).reshape(s * b, d)


def _batch_major(y, b, s):
    return y.reshape(s, b, y.shape[-1]).transpose(1, 0, 2)


def kernel(x_prompt, x_sample, state_ret, state_ssm_re, state_ssm_im, state_conv, norm_mix, w_in, ret_norm, w_ret_out, ssm_a_re, ssm_a_im, ssm_b_re, ssm_b_im, ssm_c_re, ssm_c_im, ssm_d, ssm_log_dt, w_glu_a, w_glu_b, conv_w, conv_b, w_conv_out, w_mix_out, norm_ffn, peer_wq, peer_k1, peer_k2, peer_u, peer_v, norm_final):
    bp, sp, _ = x_prompt.shape
    bs, ss, _ = x_sample.shape
    tp, ts = bp * sp, bs * ss
    depth = w_in.shape[0]
    assert tp % ROWS == 0 and ts == ROWS and ROWS % bp == 0 and ROWS // bp == math.gcd(sp, RET_CHUNK)

    xs = [x_prompt, _time_major(x_sample)]
    batches = (bp, bs)
    pos = (jnp.arange(sp, dtype=F32), PAST_LEN + jnp.arange(ss, dtype=F32))
    ret_tabs = [_retention_tables(p, b) for p, b in zip(pos, batches)]

    lg = depth * SSM_G
    abre, abim, bbre, bbim = _s5_discretise(
        ssm_a_re.reshape(lg, SSM_P), ssm_a_im.reshape(lg, SSM_P), ssm_log_dt.reshape(lg, 1),
        ssm_b_re.transpose(0, 1, 3, 2).reshape(lg, SSM_GC, SSM_P),
        ssm_b_im.transpose(0, 1, 3, 2).reshape(lg, SSM_GC, SSM_P))

    hq = PEER_DQ // 2
    eye = jnp.eye(PEER_HEADS, dtype=F32)

    def keys_block_diag(k):
        return (k.transpose(1, 0, 2)[:, :, None, :] * eye[None, :, :, None]).reshape(
            PEER_NKEYS * PEER_HEADS, PEER_HEADS * hq).astype(BF16)

    w_in_b = w_in.astype(BF16)
    ut = peer_u.astype(BF16).transpose(0, 2, 1)
    vt = peer_v.astype(BF16)
    zero_ret = jnp.zeros((1, bp, RET_HEADS, RET_DK, RET_DK), F32)

    states = [[[] for _ in range(3)] for _ in range(2)]
    ret_stacked = [None, None]
    for l in range(depth):
        sl = slice(l * SSM_G, (l + 1) * SSM_G)
        bmat = jnp.concatenate([_block_diag_slabs(bbre[sl], SSM_SLABS), _block_diag_slabs(bbim[sl], SSM_SLABS)],
                               axis=2).astype(BF16)
        cmat = jnp.concatenate([_block_diag_slabs(ssm_c_re[l].transpose(0, 2, 1), SSM_SLABS),
                                _block_diag_slabs(-ssm_c_im[l].transpose(0, 2, 1), SSM_SLABS)],
                               axis=1).astype(BF16)
        are_row = abre[sl].reshape(1, SSM_N)
        aim_row = abim[sl].reshape(1, SSM_N)
        proj = [w.astype(BF16) for w in (w_ret_out[l], w_glu_a[l], w_glu_b[l], w_conv_out[l], w_mix_out[l])]
        wq = peer_wq[l].reshape(D_MODEL, PEER_HEADS, 2, hq).transpose(0, 2, 1, 3).reshape(D_MODEL, -1).astype(BF16)
        k1big, k2big = keys_block_diag(peer_k1[l]), keys_block_diag(peer_k2[l])

        for gi, batch in enumerate(batches):
            x = xs[gi]
            nblk = x.size // (ROWS * D_MODEL)
            if gi == 0:
                s0, s0_layer = zero_ret, 0
                h0re = jnp.zeros((batch, SSM_N), F32)
                h0im = jnp.zeros((batch, SSM_N), F32)
                buf0 = jnp.zeros(((CONV_K - 1) * batch, CONV_W), F32)
                bblk = batch
            else:
                s0, s0_layer = state_ret, l
                h0re = state_ssm_re[l].reshape(batch, SSM_N)
                h0im = state_ssm_im[l].reshape(batch, SSM_N)
                buf0 = state_conv[l].transpose(1, 0, 2).reshape((CONV_K - 1) * batch, CONV_W)
                bblk = 16
            z = _inproj(x, norm_mix[l][None, :], w_in_b, l)
            if x.ndim == 3:
                z, x = z
            oa, ret_stacked[gi] = _retention(z, nblk, batch, bblk, ret_tabs[gi], ret_norm[l][None, :], s0, s0_layer,
                                          ret_stacked[gi], l, depth)
            ys, xre, xim = _s5(z, nblk, batch, bmat, cmat, are_row, aim_row, ssm_d[l][None, :], h0re, h0im)
            st = states[gi]
            st[0].append(xre.reshape(batch, SSM_G, SSM_P))
            st[1].append(xim.reshape(batch, SSM_G, SSM_P))

            x1, xn, buf = _merge(x, z, oa, ys, batch, buf0, conv_w[l], conv_b[l][None, :], *proj, norm_ffn[l][None, :])
            st[2].append(buf.reshape(CONV_K - 1, batch, CONV_W).transpose(1, 0, 2))
            e1, e2, g = _peer_select(xn, wq, k1big, k2big)
            last = l == depth - 1
            xs[gi] = _peer_dense(xn, e1, e2, g, ut, vt, x1, l, norm_final[None, :], last,
                                 out_batch=bp if last and gi == 0 else 0)

    y_prompt = xs[0]
    y_sample = _batch_major(xs[1], bs, ss)
    (re_p, im_p, cv_p), (re_s, im_s, cv_s) = states
    return (y_prompt, y_sample,
            ret_stacked[0], ret_stacked[1],
            jnp.stack(re_p), jnp.stack(re_s),
            jnp.stack(im_p), jnp.stack(im_s),
            jnp.stack(cv_p), jnp.stack(cv_s))
```

```python
import functools
import math

import jax
import jax.numpy as jnp
from jax import lax
from jax.experimental import pallas as pl
from jax.experimental.pallas import tpu as pltpu

F32 = jnp.float32
BF16 = jnp.bfloat16

D_MODEL = 1024
PAST_LEN = 16384
RET_HEADS = 8
RET_DK = 64
RET_W = 512
RET_CHUNK = 128
ROPE_BASE = 10000.0
SSM_W = 512
SSM_GC = 16
SSM_G = 32
SSM_P = 64
SSM_N = SSM_G * SSM_P
SSM_SLABS = 4
SSM_SLAB_N = SSM_N // SSM_SLABS
CONV_W = 512
CONV_K = 3
PROJ_W = 7168
PEER_HEADS = 8
PEER_DQ = 256
PEER_NKEYS = 128
PEER_TOPK = 16
PEER_NEXP = PEER_NKEYS ** 2
PEER_SLOTS = PEER_HEADS * PEER_TOPK
TOPK_GROUP = 8
EPS = 1e-6
GELU_C = math.sqrt(2.0 / math.pi)
GELU_A = 0.044715

ROWS = 1024
LANES = 128
SUBLANES = 8
MIB = 1024 * 1024


def _params(sem, vmem_mib):
    return pltpu.CompilerParams(dimension_semantics=sem, vmem_limit_bytes=vmem_mib * MIB)


def _rms(x, g):
    return x * lax.rsqrt(jnp.mean(x * x, axis=-1, keepdims=True) + EPS) * g


def _dot(a, b):
    return jnp.dot(a, b, preferred_element_type=F32)


def _dot_nt(a, b):
    return lax.dot_general(a, b, (((1,), (1,)), ((), ())), preferred_element_type=F32)


def _inproj_kernel(x_ref, g_ref, w_ref, z_ref, h_scr):
    @pl.when(pl.program_id(1) == 0)
    def _():
        h_scr[...] = _rms(x_ref[...], g_ref[...]).astype(BF16)

    z_ref[...] = _dot(h_scr[...], w_ref[...])


def _inproj_bm_kernel(x_ref, g_ref, w_ref, z_ref, xtm_ref, h_scr):
    @pl.when(pl.program_id(1) == 0)
    def _():
        x = pltpu.einshape("btd->(tb)d", x_ref[...])
        xtm_ref[...] = x
        h_scr[...] = _rms(x, g_ref[...]).astype(BF16)

    z_ref[...] = _dot(h_scr[...], w_ref[...])


def _inproj(x, g, w, layer):
    batch_major = x.ndim == 3
    nb = PROJ_W // 4
    if batch_major:
        batch, seq, _ = x.shape
        t = batch * seq
        x_spec = pl.BlockSpec((batch, ROWS // batch, D_MODEL), lambda i, j: (0, i, 0))
    else:
        t = x.shape[0]
        x_spec = pl.BlockSpec((ROWS, D_MODEL), lambda i, j: (i, 0))
    z_spec = pl.BlockSpec((ROWS, nb), lambda i, j: (i, j))
    z_shape = jax.ShapeDtypeStruct((t, PROJ_W), F32)
    return pl.pallas_call(
        _inproj_bm_kernel if batch_major else _inproj_kernel,
        grid=(t // ROWS, PROJ_W // nb),
        in_specs=[x_spec,
                  pl.BlockSpec((1, D_MODEL), lambda i, j: (0, 0)),
                  pl.BlockSpec((None, D_MODEL, nb), lambda i, j: (layer, 0, j))],
        out_specs=[z_spec, pl.BlockSpec((ROWS, D_MODEL), lambda i, j: (i, 0))] if batch_major else z_spec,
        out_shape=[z_shape, jax.ShapeDtypeStruct((t, D_MODEL), F32)] if batch_major else z_shape,
        scratch_shapes=[pltpu.VMEM((ROWS, D_MODEL), BF16)],
        compiler_params=_params(("parallel", "arbitrary"), 48),
        name="inproj",
    )(x, g, w)


def _ret_kernel(batch, bblk, aliased, out_layer, *refs):
    refs = [r for i, r in enumerate(refs) if not (aliased and i == 9)]
    (z_ref, cos_ref, sa_ref, sb_ref, qdec_ref, kdec_ref, cdec_ref, gn_ref, s0_ref,
     o_ref, st_ref, qd_scr, kd_scr, v_scr, mask_scr, oacc_scr) = refs
    s_ref = st_ref if aliased else st_ref.at[out_layer]
    bb = pl.program_id(0)
    c = pl.program_id(1)
    steps = ROWS // batch
    nslab = RET_W // LANES

    def head_view(ref, h):
        return ref[h // 2, :, (h % 2) * RET_DK:(h % 2 + 1) * RET_DK]

    seq_local = steps >= LANES
    msize = steps if seq_local else ROWS

    @pl.when((bb == 0) & (c == 0))
    def _():
        r = lax.broadcasted_iota(jnp.int32, (msize, msize), 0)
        cc = lax.broadcasted_iota(jnp.int32, (msize, msize), 1)
        if seq_local:
            mask_scr[...] = (r >= cc).astype(F32)
        else:
            same = (r & (batch - 1)) == (cc & (batch - 1))
            mask_scr[...] = (same & (r >= cc)).astype(F32)

    @pl.when(c == 0)
    def _():
        s_ref[...] = s0_ref[...]
        if not aliased:
            for k in range(st_ref.shape[0]):
                if k != out_layer:
                    st_ref[k] = jnp.zeros(st_ref.shape[1:], F32)

    @pl.when(bb == 0)
    def _():
        def per_row(t_ref):
            return jnp.broadcast_to(t_ref[...][:, None, :], (steps, batch, LANES)).reshape(ROWS, LANES)

        cos, sa, sb = per_row(cos_ref), per_row(sa_ref), per_row(sb_ref)

        def rot(x):
            return x * cos + pltpu.roll(x, 32, 1) * sa + pltpu.roll(x, 96, 1) * sb

        for s in range(nslab):
            cols = slice(s * LANES, (s + 1) * LANES)
            qd_scr[s] = rot(z_ref[:, cols]) * qdec_ref[:, cols]
            kd_scr[s] = rot(z_ref[:, RET_W + s * LANES:RET_W + (s + 1) * LANES]) * kdec_ref[:, cols]
            v_scr[s] = z_ref[:, 2 * RET_W + s * LANES:2 * RET_W + (s + 1) * LANES]
        if not seq_local:
            for s in range(nslab):
                outs = []
                for h in (2 * s, 2 * s + 1):
                    qh = head_view(qd_scr, h).astype(BF16)
                    kh = head_view(kd_scr, h).astype(BF16)
                    vh = head_view(v_scr, h).astype(BF16)
                    p = (_dot_nt(qh, kh) * mask_scr[...]).astype(BF16)
                    outs.append(_dot(p, vh))
                oacc_scr[s] = jnp.concatenate(outs, axis=1)

    def per_seq(bl, carry):
        b = bb * bblk + bl
        rows = pl.ds(b, steps, stride=batch)
        heads = range(RET_HEADS)

        def head_cols(scr):
            slabs = [scr[s, rows, :] for s in range(nslab)]
            return [slabs[h // 2][:, (h % 2) * RET_DK:(h % 2 + 1) * RET_DK].astype(BF16) for h in heads]

        q16, k16, v16 = head_cols(qd_scr), head_cols(kd_scr), head_cols(v_scr)
        st = [s_ref[bl, h] for h in heads]
        outs = [_dot(q16[h], st[h].astype(BF16)) for h in heads]
        if seq_local:
            sc = [_dot_nt(q16[h], k16[h]) for h in heads]
            p = [(sc[h] * mask_scr[...]).astype(BF16) for h in heads]
            outs = [_dot(p[h], v16[h]) + outs[h] for h in heads]
        upd = [lax.dot_general(k16[h], v16[h], (((0,), (0,)), ((), ())), preferred_element_type=F32) for h in heads]
        for h in heads:
            s_ref[bl, h] = (st[h] + upd[h]) * cdec_ref[:, h * RET_DK:(h + 1) * RET_DK]
        for s in range(nslab):
            o2 = jnp.concatenate(outs[2 * s:2 * s + 2], axis=1)
            oacc_scr[s, rows, :] = o2 if seq_local else oacc_scr[s, rows, :] + o2
        return carry

    lax.fori_loop(0, bblk, per_seq, 0, unroll=4)

    @pl.when(bb == pl.num_programs(0) - 1)
    def _():
        r = lax.broadcasted_iota(jnp.int32, (LANES, LANES), 0) // RET_DK
        cc = lax.broadcasted_iota(jnp.int32, (LANES, LANES), 1) // RET_DK
        avg = jnp.where(r == cc, 1.0 / RET_DK, 0.0).astype(BF16)

        def seg_mean(x):
            hi = x.astype(BF16)
            lo = (x - hi.astype(F32)).astype(BF16)
            return _dot(hi, avg) + _dot(lo, avg)

        normed = []
        for s in range(nslab):
            o2 = oacc_scr[s]
            dlt = o2 - seg_mean(o2)
            normed.append(dlt * lax.rsqrt(seg_mean(dlt * dlt) + EPS))
        o = jnp.concatenate(normed, axis=1) * gn_ref[...]
        o_ref[...] = jax.nn.silu(z_ref[:, 3 * RET_W:4 * RET_W]) * o


def _retention(z, nblk, batch, bblk, tabs, gn, s0, layer, stacked, out_layer, depth):
    cos, sa, sb, qdec, kdec, cdec = tabs
    nbb = batch // bblk
    steps = ROWS // batch
    msize = steps if steps >= LANES else ROWS
    aliased = stacked is not None
    if aliased:
        st_spec = pl.BlockSpec((None, bblk, RET_HEADS, RET_DK, RET_DK), lambda bb, c: (out_layer, bb, 0, 0, 0))
    else:
        st_spec = pl.BlockSpec((depth, bblk, RET_HEADS, RET_DK, RET_DK), lambda bb, c: (0, bb, 0, 0, 0))
    st_in = pl.BlockSpec((None, bblk, RET_HEADS, RET_DK, RET_DK), lambda bb, c: (layer, bb, 0, 0, 0))
    const = lambda bb, c: (0, 0)
    return pl.pallas_call(
        functools.partial(_ret_kernel, batch, bblk, aliased, out_layer),
        grid=(nbb, nblk),
        in_specs=[pl.BlockSpec((ROWS, 4 * RET_W), lambda bb, c: (c, 0)),
                  pl.BlockSpec((steps, LANES), lambda bb, c: (c, 0)),
                  pl.BlockSpec((steps, LANES), lambda bb, c: (c, 0)),
                  pl.BlockSpec((steps, LANES), lambda bb, c: (c, 0)),
                  pl.BlockSpec((ROWS, RET_W), const),
                  pl.BlockSpec((ROWS, RET_W), const),
                  pl.BlockSpec((1, RET_W), const),
                  pl.BlockSpec((1, RET_W), const),
                  st_in] + ([pl.BlockSpec(memory_space=pl.ANY)] if aliased else []),
        out_specs=[pl.BlockSpec((ROWS, RET_W), lambda bb, c: (c, 0)), st_spec],
        out_shape=[jax.ShapeDtypeStruct((nblk * ROWS, RET_W), F32),
                   jax.ShapeDtypeStruct((depth, batch, RET_HEADS, RET_DK, RET_DK), F32)],
        input_output_aliases={9: 1} if aliased else {},
        scratch_shapes=[pltpu.VMEM((RET_W // LANES, ROWS, LANES), F32)] * 3
        + [pltpu.VMEM((msize, msize), F32), pltpu.VMEM((RET_W // LANES, ROWS, LANES), F32)],
        compiler_params=_params(("arbitrary", "arbitrary"), 56),
        name="retention",
    )(z, cos, sa, sb, qdec, kdec, cdec, gn, s0, *([stacked] if aliased else []))


def _retention_tables(pos, batch):
    half = RET_DK // 2
    lane = jnp.arange(LANES)
    upper = (lane % RET_DK) >= half
    freqs = (ROPE_BASE ** (-jnp.arange(half, dtype=F32) / half))[lane % half]
    ang = pos[:, None] * freqs[None, :]
    cos_t, sin = jnp.cos(ang), jnp.sin(ang)
    sa_t = jnp.where(upper[None, :], sin, 0.0)
    sb_t = jnp.where(upper[None, :], 0.0, -sin)
    lg = jnp.repeat(jnp.log1p(-jnp.exp2(-5.0 - jnp.arange(RET_HEADS, dtype=F32))), RET_DK)
    steps = ROWS // batch
    i1 = (jnp.arange(ROWS) // batch).astype(F32) + 1.0
    qdec = jnp.exp(i1[:, None] * lg[None, :])
    kdec = jnp.exp(-i1[:, None] * lg[None, :]) * (RET_DK ** -0.5)
    cdec = jnp.exp(steps * lg)[None, :]
    return cos_t, sa_t, sb_t, qdec, kdec, cdec


def _s5_disc_kernel(are_ref, aim_ref, ldt_ref, bre_ref, bim_ref, abre_ref, abim_ref, bbre_ref, bbim_ref):
    ar, ai = are_ref[...], aim_ref[...]
    dt = jnp.exp(ldt_ref[...])
    dar, dai = dt * ar, dt * ai
    mag = jnp.exp(dar)
    abar_re, abar_im = mag * jnp.cos(dai), mag * jnp.sin(dai)
    den = ar * ar + ai * ai
    nr, ni = abar_re - 1.0, abar_im
    f_re = (nr * ar + ni * ai) / den
    f_im = (ni * ar - nr * ai) / den
    abre_ref[...] = abar_re
    abim_ref[...] = abar_im
    br, bi = bre_ref[...], bim_ref[...]
    bbre_ref[...] = f_re[:, None, :] * br - f_im[:, None, :] * bi
    bbim_ref[...] = f_re[:, None, :] * bi + f_im[:, None, :] * br


def _s5_discretise(a_re, a_im, log_dt, b_re_t, b_im_t):
    lg = a_re.shape[0]
    small = jax.ShapeDtypeStruct((lg, SSM_P), F32)
    big = jax.ShapeDtypeStruct((lg, SSM_GC, SSM_P), F32)
    return pl.pallas_call(_s5_disc_kernel, out_shape=[small, small, big, big], name="s5_disc")(
        a_re, a_im, log_dt, b_re_t, b_im_t)


def _s5_kernel(batch, u_ref, bmat_ref, cmat_ref, are_ref, aim_ref, d_ref, h0re_ref, h0im_ref,
               y_ref, xre_ref, xim_ref, x_scr):
    c = pl.program_id(0)
    steps = ROWS // batch
    half = SSM_SLAB_N

    @pl.when(c == 0)
    def _():
        xre_ref[...] = h0re_ref[...]
        xim_ref[...] = h0im_ref[...]

    u = u_ref[...]
    ub = u.astype(BF16)
    for s in range(SSM_SLABS):
        x_scr[:, 2 * half * s:2 * half * (s + 1)] = _dot(ub[:, s * LANES:(s + 1) * LANES], bmat_ref[s])

    for s0 in range(0, SSM_SLABS, 2):
        pair = (s0, s0 + 1)
        re0 = [2 * half * s for s in pair]
        im0 = [r + half for r in re0]
        sc = [slice(half * s, half * (s + 1)) for s in pair]
        ar = [jnp.broadcast_to(are_ref[:, c], (SUBLANES, half)) for c in sc]
        ai = [jnp.broadcast_to(aim_ref[:, c], (SUBLANES, half)) for c in sc]

        def row_tile(rt, carry, re0=re0, im0=im0, sc=sc, ar=ar, ai=ai):
            r0 = pl.multiple_of(rt * SUBLANES, SUBLANES)

            def step(t, x):
                row = pl.multiple_of(t * batch + r0, SUBLANES)
                out = []
                for k in range(2):
                    xr, xi = x[2 * k], x[2 * k + 1]
                    nr = ar[k] * xr - ai[k] * xi + x_scr[pl.ds(row, SUBLANES), re0[k]:re0[k] + half]
                    ni = ar[k] * xi + ai[k] * xr + x_scr[pl.ds(row, SUBLANES), im0[k]:im0[k] + half]
                    x_scr[pl.ds(row, SUBLANES), re0[k]:re0[k] + half] = nr
                    x_scr[pl.ds(row, SUBLANES), im0[k]:im0[k] + half] = ni
                    out += [nr, ni]
                return tuple(out)

            init = tuple(ref[pl.ds(r0, SUBLANES), sc[k]] for k in range(2) for ref in (xre_ref, xim_ref))
            x = lax.fori_loop(0, steps, step, init, unroll=8)
            for k in range(2):
                xre_ref[pl.ds(r0, SUBLANES), sc[k]] = x[2 * k]
                xim_ref[pl.ds(r0, SUBLANES), sc[k]] = x[2 * k + 1]
            return carry

        lax.fori_loop(0, batch // SUBLANES, row_tile, 0)

    ys = [_dot(x_scr[:, 2 * half * s:2 * half * (s + 1)].astype(BF16), cmat_ref[s]) for s in range(SSM_SLABS)]
    y = jnp.concatenate(ys, axis=1) + d_ref[...] * u
    y_ref[...] = jax.nn.gelu(y)


def _s5(z, nblk, batch, bmat, cmat, abre, abim, d, h0re, h0im):
    const2 = lambda c: (0, 0)
    const3 = lambda c: (0, 0, 0)
    st = pl.BlockSpec((batch, SSM_N), const2)
    return pl.pallas_call(
        functools.partial(_s5_kernel, batch),
        grid=(nblk,),
        in_specs=[pl.BlockSpec((ROWS, SSM_W), lambda c: (c, 4)),
                  pl.BlockSpec((SSM_SLABS, LANES, 2 * SSM_SLAB_N), const3),
                  pl.BlockSpec((SSM_SLABS, 2 * SSM_SLAB_N, LANES), const3),
                  pl.BlockSpec((1, SSM_N), const2),
                  pl.BlockSpec((1, SSM_N), const2),
                  pl.BlockSpec((1, SSM_W), const2),
                  st, st],
        out_specs=[pl.BlockSpec((ROWS, SSM_W), lambda c: (c, 0)), st, st],
        out_shape=[jax.ShapeDtypeStruct((nblk * ROWS, SSM_W), F32),
                   jax.ShapeDtypeStruct((batch, SSM_N), F32),
                   jax.ShapeDtypeStruct((batch, SSM_N), F32)],
        scratch_shapes=[pltpu.VMEM((ROWS, 2 * SSM_N), F32)],
        compiler_params=_params(("arbitrary",), 48),
        name="s5",
    )(z, bmat, cmat, abre, abim, d, h0re, h0im)


def _block_diag_slabs(w, nslab):
    gps = SSM_G // nslab
    eye = jnp.eye(gps, dtype=w.dtype)
    w4 = w.reshape(nslab, gps, w.shape[1], w.shape[2])
    out = w4[:, :, :, None, :] * eye[None, :, None, :, None]
    return out.reshape(nslab, gps * w.shape[1], gps * w.shape[2])


def _merge_kernel(batch, x_ref, oa_ref, ys_ref, bg_ref, cg_ref, hc_ref, buf0_ref, cw_ref, cb_ref,
                  ga_ref, gb_ref, gc_ref, wr_ref, wa_ref, wb_ref, wc_ref, wm_ref, gf_ref,
                  x1_ref, xn_ref, buf_ref, zp_scr):
    rb = x_ref.shape[0]
    pad = (CONV_K - 1) * batch

    @pl.when(pl.program_id(0) == 0)
    def _():
        zp_scr[0:pad, :] = buf0_ref[...]

    zc = cg_ref[...] * hc_ref[...]
    zp_scr[pad:pad + rb, :] = zc
    y = cb_ref[...]
    for j in range(CONV_K):
        y = y + cw_ref[j:j + 1, :] * zp_scr[j * batch:j * batch + rb, :]
    oc_pre = bg_ref[...] * y
    tail = zp_scr[rb:rb + pad, :]
    buf_ref[...] = tail
    zp_scr[0:pad, :] = tail

    oa = _dot(oa_ref[...].astype(BF16), wr_ref[...])
    ysb = ys_ref[...].astype(BF16)
    ob = _dot(ysb, wa_ref[...]) * jax.nn.sigmoid(_dot(ysb, wb_ref[...]))
    oc = _dot(oc_pre.astype(BF16), wc_ref[...])
    merged = (jax.nn.sigmoid(ga_ref[...]) * oa + jax.nn.sigmoid(gb_ref[...]) * ob
              + jax.nn.sigmoid(gc_ref[...]) * oc)
    x1 = x_ref[...] + _dot(merged.astype(BF16), wm_ref[...])
    x1_ref[...] = x1
    xn_ref[...] = _rms(x1, gf_ref[...]).astype(BF16)


def _merge(x, z, oa, ys, batch, buf0, conv_w, conv_b, wr, wa, wb, wc, wm, gf):
    t = x.shape[0]
    rb = 512
    pad = (CONV_K - 1) * batch
    assert pad <= rb
    row = lambda w: pl.BlockSpec((rb, w), lambda i: (i, 0))
    zcol = lambda w, j: pl.BlockSpec((rb, w), lambda i: (i, j))
    const = lambda r, w: pl.BlockSpec((r, w), lambda i: (0, 0))
    return pl.pallas_call(
        functools.partial(_merge_kernel, batch),
        grid=(t // rb,),
        in_specs=[row(D_MODEL), row(RET_W), row(SSM_W),
                  zcol(CONV_W, 5), zcol(CONV_W, 6), zcol(CONV_W, 7),
                  const(pad, CONV_W), const(CONV_K, CONV_W), const(1, CONV_W),
                  zcol(D_MODEL, 4), zcol(D_MODEL, 5), zcol(D_MODEL, 6),
                  const(RET_W, D_MODEL), const(SSM_W, D_MODEL), const(SSM_W, D_MODEL), const(CONV_W, D_MODEL),
                  const(D_MODEL, D_MODEL), const(1, D_MODEL)],
        out_specs=[row(D_MODEL), row(D_MODEL), const(pad, CONV_W)],
        out_shape=[jax.ShapeDtypeStruct((t, D_MODEL), F32), jax.ShapeDtypeStruct((t, D_MODEL), BF16),
                   jax.ShapeDtypeStruct((pad, CONV_W), F32)],
        scratch_shapes=[pltpu.VMEM((rb + pad, CONV_W), F32)],
        compiler_params=_params(("arbitrary",), 48),
        name="merge",
    )(x, oa, ys, z, z, z, buf0, conv_w, conv_b, z, z, z, wr, wa, wb, wc, wm, gf)


def _tree(items, combine):
    while len(items) > 1:
        nxt = [combine(items[i], items[i + 1]) for i in range(0, len(items) - 1, 2)]
        if len(items) % 2:
            nxt.append(items[-1])
        items = nxt
    return items[0]


def _first_max(x, y):
    (vx, ix), (vy, iy) = x, y
    return jnp.maximum(vx, vy), jnp.where(vx >= vy, ix, iy)


def _bits(x, n):
    out, rest = [], x
    for _ in range(n):
        half = jnp.floor(rest * 0.5)
        out.append(rest - 2.0 * half == 1.0)
        rest = half
    return out


def _mux(vals, bits):
    level = list(vals)
    for bit in bits:
        level = [jnp.where(bit, level[j + 1], level[j]) for j in range(0, len(level), 2)]
    return level[0]


def _top16_of_keys(s_scrs, gv_scrs, gi_scrs, v_scrs, i_scrs):
    grp = TOPK_GROUP
    ngrp = PEER_NKEYS // grp
    nbits = ngrp.bit_length() - 1

    for s_scr, gv, gi in zip(s_scrs, gv_scrs, gi_scrs):
        for g in range(ngrp):
            gv[g], gi[g] = _tree([(s_scr[g * grp + p], float(g * grp + p)) for p in range(grp)], _first_max)

    def body(r, carry):
        for s_scr, gv, gi, v_scr, i_scr in zip(s_scrs, gv_scrs, gi_scrs, v_scrs, i_scrs):
            m, idx = _tree([(gv[g], gi[g]) for g in range(ngrp)], _first_max)
            v_scr[r] = m
            i_scr[r] = idx
            gid = jnp.floor(idx * (1.0 / grp))
            rel = idx - gid * grp
            bits = _bits(gid, nbits)
            cands = []
            for p in range(grp):
                val = _mux([s_scr[g * grp + p] for g in range(ngrp)], bits)
                left = (val < m) | ((val == m) & (rel < float(p)))
                cands.append((jnp.where(left, val, -jnp.inf), float(p)))
            nv, npos = _tree(cands, _first_max)
            ni = gid * grp + npos
            for g in range(ngrp):
                hit = gid == float(g)
                gv[g] = jnp.where(hit, nv, gv[g])
                gi[g] = jnp.where(hit, ni, gi[g])
        return carry

    lax.fori_loop(0, PEER_TOPK, body, 0)


def _select_kernel(tb, xn_ref, wq_ref, k1_ref, k2_ref, e1_ref, e2_ref, g_ref,
                   s1_scr, s2_scr, gv1_scr, gi1_scr, gv2_scr, gi2_scr, v1_scr, i1_scr, v2_scr, i2_scr,
                   hv_scr, hb_scr, sc_scr, se1_scr, se2_scr):
    q = _dot(xn_ref[...], wq_ref[...]).astype(BF16)
    hq = PEER_HEADS * PEER_DQ // 2
    s1 = _dot_nt(k1_ref[...], q[:, :hq])
    s2 = _dot_nt(k2_ref[...], q[:, hq:])
    kbits = PEER_TOPK.bit_length() - 1
    tiles = range(tb // LANES)
    for lt in tiles:
        lanes = slice(lt * LANES, (lt + 1) * LANES)
        s1_scr[lt] = s1[:, lanes].reshape(PEER_NKEYS, SUBLANES, LANES)
        s2_scr[lt] = s2[:, lanes].reshape(PEER_NKEYS, SUBLANES, LANES)
    both = lambda r1, r2: tuple(r1.at[lt] for lt in tiles) + tuple(r2.at[lt] for lt in tiles)
    _top16_of_keys(both(s1_scr, s2_scr), both(gv1_scr, gv2_scr), both(gi1_scr, gi2_scr),
                   both(v1_scr, v2_scr), both(i1_scr, i2_scr))

    for lt in tiles:
        for a in range(PEER_TOPK):
            hv_scr[lt, a] = v1_scr[lt, a] + v2_scr[lt, 0]
            hb_scr[lt, a] = jnp.zeros((SUBLANES, LANES), F32)

    def body(r, carry):
        for lt in tiles:
            hv, hb = hv_scr.at[lt], hb_scr.at[lt]
            v1, i1, v2, i2 = v1_scr.at[lt], i1_scr.at[lt], v2_scr.at[lt], i2_scr.at[lt]
            m, a_sel = _tree([(hv[a], float(a)) for a in range(PEER_TOPK)], _first_max)
            abits = _bits(a_sel, kbits)
            b_sel = _mux([hb[a] for a in range(PEER_TOPK)], abits)
            bbits = _bits(b_sel, kbits)
            sc_scr[lt, r] = m
            se1_scr[lt, r] = _mux([i1[a] for a in range(PEER_TOPK)], abits)
            se2_scr[lt, r] = _mux([i2[b] for b in range(PEER_TOPK)], bbits)
            nb = b_sel + 1.0
            v2_next = _mux([v2[(b + 1) % PEER_TOPK] for b in range(PEER_TOPK)], bbits)
            v1_sel = _mux([v1[a] for a in range(PEER_TOPK)], abits)
            live = (a_sel + 1.0) * (nb + 1.0) <= float(PEER_TOPK)
            nv = jnp.where(live, v1_sel + v2_next, -jnp.inf)
            for a in range(PEER_TOPK):
                hit = a_sel == float(a)
                hv[a] = jnp.where(hit, nv, hv[a])
                hb[a] = jnp.where(hit, nb, hb[a])
        return carry

    lax.fori_loop(0, PEER_TOPK, body, 0)
    for lt in tiles:
        sc = sc_scr[lt]
        ex = jnp.exp(sc - jnp.max(sc, axis=0, keepdims=True))
        gate = ex / jnp.sum(ex, axis=0, keepdims=True)
        rows = slice(lt * LANES, (lt + 1) * LANES)
        g_ref[rows, :] = gate.reshape(PEER_SLOTS, LANES).T
        e1_ref[rows, :] = se1_scr[lt].reshape(PEER_SLOTS, LANES).T
        e2_ref[rows, :] = se2_scr[lt].reshape(PEER_SLOTS, LANES).T


def _peer_select(xn, wq, k1big, k2big):
    t = xn.shape[0]
    tb = 256
    hq = PEER_HEADS * PEER_DQ // 2
    nk = PEER_NKEYS * PEER_HEADS
    const = lambda i: (0, 0)
    row = lambda dt: jax.ShapeDtypeStruct((t, PEER_SLOTS), dt)
    vec = lambda n: pltpu.VMEM((tb // LANES, n, SUBLANES, LANES), F32)
    return pl.pallas_call(
        functools.partial(_select_kernel, tb),
        grid=(t // tb,),
        in_specs=[pl.BlockSpec((tb, D_MODEL), lambda i: (i, 0)),
                  pl.BlockSpec((D_MODEL, 2 * hq), const),
                  pl.BlockSpec((nk, hq), const),
                  pl.BlockSpec((nk, hq), const)],
        out_specs=[pl.BlockSpec((tb, PEER_SLOTS), lambda i: (i, 0))] * 3,
        out_shape=[row(F32), row(F32), row(F32)],
        scratch_shapes=[vec(PEER_NKEYS), vec(PEER_NKEYS)] + [vec(PEER_NKEYS // TOPK_GROUP)] * 4
        + [vec(PEER_TOPK), vec(PEER_TOPK), vec(PEER_TOPK), vec(PEER_TOPK),
                        vec(PEER_TOPK), vec(PEER_TOPK), vec(PEER_TOPK), vec(PEER_TOPK), vec(PEER_TOPK)],
        compiler_params=_params(("parallel",), 40),
        name="peer_select",
    )(xn, wq, k1big, k2big)


def _peer_kernel(tb, eb, stride, final_norm, out_batch, xn_ref, e1_ref, e2_ref, g_ref, ut_ref, v_ref, x1_ref,
                 gain_ref, out_ref, m_scr, *acc_scr):
    e = pl.program_id(1)
    nk1 = eb // PEER_NKEYS
    acc_ref = acc_scr[0] if out_batch else out_ref

    @pl.when(e == 0)
    def _():
        acc_ref[...] = x1_ref[...]

    @pl.when(e == 0)
    def _():
        key = lax.broadcasted_iota(jnp.int32, (PEER_NKEYS, PEER_SLOTS), 0).astype(F32)

        def token(t, carry):
            e1 = e1_ref[pl.ds(t, 1), :]
            e2 = e2_ref[pl.ds(t, 1), :]
            gt = 0.5 * g_ref[pl.ds(t, 1), :]
            a_t = jnp.where(key == e1, gt, 0.0).astype(BF16)
            b_t = jnp.where(key == e2, 1.0, 0.0).astype(BF16)
            m_scr[pl.ds(t, PEER_NKEYS, stride=stride), :] = _dot_nt(a_t, b_t)
            return carry

        lax.fori_loop(0, tb, token, 0, unroll=128)

    s = _dot(xn_ref[...], ut_ref[...])
    t = jnp.tanh(s * (GELU_C + (GELU_C * GELU_A) * (s * s)))
    k1 = e * nk1
    gates = [m_scr[pl.ds(pl.multiple_of((k1 + i) * stride, SUBLANES), tb), :] for i in range(nk1)]
    w = ((s + s * t) * jnp.concatenate(gates, axis=1)).astype(BF16)
    acc_ref[...] += _dot(w, v_ref[...])

    if final_norm or out_batch:
        @pl.when(e == pl.num_programs(1) - 1)
        def _():
            y = acc_ref[...]
            if final_norm:
                y = _rms(y, gain_ref[...])
            out_ref[...] = pltpu.einshape("(tb)d->btd", y, b=out_batch) if out_batch else y


def _peer_dense(xn, e1, e2, g, ut, v, x1, layer, gain, final_norm, out_batch=0):
    t = xn.shape[0]
    tb, eb = 512, 2048
    stride = tb + SUBLANES
    once = pl.Buffered(1)
    tok = lambda w: pl.BlockSpec((tb, w), lambda i, e: (i, 0), pipeline_mode=once)
    tab = pl.BlockSpec((None, eb, D_MODEL), lambda i, e: (layer, e, 0))
    if out_batch:
        out_spec = pl.BlockSpec((out_batch, tb // out_batch, D_MODEL), lambda i, e: (0, i, 0), pipeline_mode=once)
        out_shape = jax.ShapeDtypeStruct((out_batch, t // out_batch, D_MODEL), F32)
        acc = [pltpu.VMEM((tb, D_MODEL), F32)]
    else:
        out_spec = pl.BlockSpec((tb, D_MODEL), lambda i, e: (i, 0), pipeline_mode=once)
        out_shape = jax.ShapeDtypeStruct((t, D_MODEL), F32)
        acc = []
    return pl.pallas_call(
        functools.partial(_peer_kernel, tb, eb, stride, final_norm, out_batch),
        grid=(t // tb, PEER_NEXP // eb),
        in_specs=[tok(D_MODEL), tok(PEER_SLOTS), tok(PEER_SLOTS), tok(PEER_SLOTS),
                  pl.BlockSpec((None, D_MODEL, eb), lambda i, e: (layer, 0, e)), tab, tok(D_MODEL),
                  pl.BlockSpec((1, D_MODEL), lambda i, e: (0, 0))],
        out_specs=out_spec,
        out_shape=out_shape,
        scratch_shapes=[pltpu.VMEM((PEER_NKEYS * stride, PEER_NKEYS), F32)] + acc,
        compiler_params=_params(("parallel", "arbitrary"), 62),
        name="peer_dense",
    )(xn, e1, e2, g, ut, v, x1, gain)


def _time_major(x):
    b, s, d = x.shape
    return x.transpose(1, 0, 2).reshape(s * b, d)


def _batch_major(y, b, s):
    return y.reshape(s, b, y.shape[-1]).transpose(1, 0, 2)


def kernel(x_prompt, x_sample, state_ret, state_ssm_re, state_ssm_im, state_conv, norm_mix, w_in, ret_norm, w_ret_out, ssm_a_re, ssm_a_im, ssm_b_re, ssm_b_im, ssm_c_re, ssm_c_im, ssm_d, ssm_log_dt, w_glu_a, w_glu_b, conv_w, conv_b, w_conv_out, w_mix_out, norm_ffn, peer_wq, peer_k1, peer_k2, peer_u, peer_v, norm_final):
    bp, sp, _ = x_prompt.shape
    bs, ss, _ = x_sample.shape
    tp, ts = bp * sp, bs * ss
    depth = w_in.shape[0]
    assert tp % ROWS == 0 and ts == ROWS and ROWS % bp == 0 and ROWS // bp == math.gcd(sp, RET_CHUNK)

    xs = [x_prompt, _time_major(x_sample)]
    batches = (bp, bs)
    pos = (jnp.arange(sp, dtype=F32), PAST_LEN + jnp.arange(ss, dtype=F32))
    ret_tabs = [_retention_tables(p, b) for p, b in zip(pos, batches)]

    lg = depth * SSM_G
    abre, abim, bbre, bbim = _s5_discretise(
        ssm_a_re.reshape(lg, SSM_P), ssm_a_im.reshape(lg, SSM_P), ssm_log_dt.reshape(lg, 1),
        ssm_b_re.transpose(0, 1, 3, 2).reshape(lg, SSM_GC, SSM_P),
        ssm_b_im.transpose(0, 1, 3, 2).reshape(lg, SSM_GC, SSM_P))

    hq = PEER_DQ // 2
    eye = jnp.eye(PEER_HEADS, dtype=F32)

    def keys_block_diag(k):
        return (k.transpose(1, 0, 2)[:, :, None, :] * eye[None, :, :, None]).reshape(
            PEER_NKEYS * PEER_HEADS, PEER_HEADS * hq).astype(BF16)

    w_in_b = w_in.astype(BF16)
    ut = peer_u.astype(BF16).transpose(0, 2, 1)
    vt = peer_v.astype(BF16)
    zero_ret = jnp.zeros((1, bp, RET_HEADS, RET_DK, RET_DK), F32)

    states = [[[] for _ in range(3)] for _ in range(2)]
    ret_stacked = [None, None]
    for l in range(depth):
        sl = slice(l * SSM_G, (l + 1) * SSM_G)
        bmat = jnp.concatenate([_block_diag_slabs(bbre[sl], SSM_SLABS), _block_diag_slabs(bbim[sl], SSM_SLABS)],
                               axis=2).astype(BF16)
        cmat = jnp.concatenate([_block_diag_slabs(ssm_c_re[l].transpose(0, 2, 1), SSM_SLABS),
                                _block_diag_slabs(-ssm_c_im[l].transpose(0, 2, 1), SSM_SLABS)],
                               axis=1).astype(BF16)
        are_row = abre[sl].reshape(1, SSM_N)
        aim_row = abim[sl].reshape(1, SSM_N)
        proj = [w.astype(BF16) for w in (w_ret_out[l], w_glu_a[l], w_glu_b[l], w_conv_out[l], w_mix_out[l])]
        wq = peer_wq[l].reshape(D_MODEL, PEER_HEADS, 2, hq).transpose(0, 2, 1, 3).reshape(D_MODEL, -1).astype(BF16)
        k1big, k2big = keys_block_diag(peer_k1[l]), keys_block_diag(peer_k2[l])

        for gi, batch in enumerate(batches):
            x = xs[gi]
            nblk = x.size // (ROWS * D_MODEL)
            if gi == 0:
                s0, s0_layer = zero_ret, 0
                h0re = jnp.zeros((batch, SSM_N), F32)
                h0im = jnp.zeros((batch, SSM_N), F32)
                buf0 = jnp.zeros(((CONV_K - 1) * batch, CONV_W), F32)
                bblk = batch
            else:
                s0, s0_layer = state_ret, l
                h0re = state_ssm_re[l].reshape(batch, SSM_N)
                h0im = state_ssm_im[l].reshape(batch, SSM_N)
                buf0 = state_conv[l].transpose(1, 0, 2).reshape((CONV_K - 1) * batch, CONV_W)
                bblk = 16
            z = _inproj(x, norm_mix[l][None, :], w_in_b, l)
            if x.ndim == 3:
                z, x = z
            oa, ret_stacked[gi] = _retention(z, nblk, batch, bblk, ret_tabs[gi], ret_norm[l][None, :], s0, s0_layer,
                                          ret_stacked[gi], l, depth)
            ys, xre, xim = _s5(z, nblk, batch, bmat, cmat, are_row, aim_row, ssm_d[l][None, :], h0re, h0im)
            st = states[gi]
            st[0].append(xre.reshape(batch, SSM_G, SSM_P))
            st[1].append(xim.reshape(batch, SSM_G, SSM_P))

            x1, xn, buf = _merge(x, z, oa, ys, batch, buf0, conv_w[l], conv_b[l][None, :], *proj, norm_ffn[l][None, :])
            st[2].append(buf.reshape(CONV_K - 1, batch, CONV_W).transpose(1, 0, 2))
            e1, e2, g = _peer_select(xn, wq, k1big, k2big)
            last = l == depth - 1
            xs[gi] = _peer_dense(xn, e1, e2, g, ut, vt, x1, l, norm_final[None, :], last,
                                 out_batch=bp if last and gi == 0 else 0)

    y_prompt = xs[0]
    y_sample = _batch_major(xs[1], bs, ss)
    (re_p, im_p, cv_p), (re_s, im_s, cv_s) = states
    return (y_prompt, y_sample,
            ret_stacked[0], ret_stacked[1],
            jnp.stack(re_p), jnp.stack(re_s),
            jnp.stack(im_p), jnp.stack(im_s),
            jnp.stack(cv_p), jnp.stack(cv_s))
```
